```python
import math
import jax
import jax.numpy as jnp
from jax import lax
import numpy as np


D_MODEL = 2048
BATCH = 2
SEQ = 16384
DEPTH = 4

GRID_W = 64
CTX_LEN = 256
N_MIXERS = 3
N_HYENA = len(range(0, DEPTH, N_MIXERS))
N_MLA = len(range(1, DEPTH, N_MIXERS))
N_POOL = len(range(2, DEPTH, N_MIXERS))

D_FF = -(-(8 * D_MODEL) // (3 * 256)) * 256

ALPHA = (2.0 * DEPTH) ** 0.25
BETA = (8.0 * DEPTH) ** -0.25
LN_EPS = 1e-6
RMS_EPS = 1e-6

HY_EMB = 33
HY_BANDS = (HY_EMB - 1) // 2
HY_FILT = 64
HY_TARGET = 1e-2
HY_FAST = 0.3
HY_SLOW = 1.5
HY_MIN_DECAY = math.log(HY_TARGET) / HY_SLOW
HY_MAX_DECAY = math.log(HY_TARGET) / HY_FAST

MLA_HEADS = D_MODEL // 128
QK_NOPE = 128
QK_ROPE = 64
V_HEAD = 128
Q_RANK = D_MODEL // 4
KV_RANK = D_MODEL // 4
ROPE_PAIRS = QK_ROPE // 4
ROPE_THETA = 10000.0
Q_BLOCK = 128
ATTN_SCALE = (QK_NOPE + QK_ROPE) ** -0.5

POOL_WINDOWS = (2, 4, 8, 16)
POOL_GROUPS = len(POOL_WINDOWS)
POOL_CH = D_MODEL // POOL_GROUPS

kernel_name = 'hybrid_dit_hyena_mla_pool'


def layer_norm(x, g, b):
    xf = x.astype(jnp.float32)
    mu = jnp.mean(xf, axis=-1, keepdims=True)
    var = jnp.mean(jnp.square(xf - mu), axis=-1, keepdims=True)
    return ((xf - mu) * lax.rsqrt(var + LN_EPS) * g + b).astype(x.dtype)


def rms_norm(x, g):
    xf = x.astype(jnp.float32)
    return (xf * lax.rsqrt(jnp.mean(jnp.square(xf), axis=-1, keepdims=True) + RMS_EPS) * g).astype(x.dtype)


def modulate(x, shift, scale):
    return x * (1 + scale) + shift


def post_norm(x, y, g, b):
    return layer_norm(ALPHA * x + y, g, b)


def swiglu(u, w_gate, w_up, w_down):
    return (jax.nn.silu(u @ w_gate) * (u @ w_up)) @ w_down


def axial_rope_tables(L):
    rows = L // GRID_W
    row = jnp.repeat(jnp.arange(rows, dtype=jnp.float32), GRID_W)
    col = jnp.tile(jnp.arange(GRID_W, dtype=jnp.float32), rows)
    inv = ROPE_THETA ** (-jnp.arange(ROPE_PAIRS, dtype=jnp.float32) / ROPE_PAIRS)
    ang = jnp.stack([row[:, None] * inv, col[:, None] * inv], axis=1)
    ang = jnp.broadcast_to(ang[:, :, None, :], (L, 2, 2, ROPE_PAIRS)).reshape(L, QK_ROPE)
    return jnp.cos(ang), jnp.sin(ang)


def apply_axial_rope(x, cos, sin):
    xs = x.reshape(x.shape[:-1] + (2, 2, ROPE_PAIRS))
    rot = jnp.stack([-xs[..., 1, :], xs[..., 0, :]], axis=-2).reshape(x.shape)
    return (x * cos + rot * sin).astype(x.dtype)


def short_conv3(u, w, b):
    L = u.shape[1]
    up = jnp.pad(u, ((0, 0), (1, 1), (0, 0)))
    return up[:, :L] * w[0] + up[:, 1:L + 1] * w[1] + up[:, 2:] * w[2] + b


def implicit_filter(L, f_w_in, f_w_hid, f_b, f_freq, f_w_out):
    f32 = jnp.float32
    pos = jnp.arange(L, dtype=f32)
    t = pos / (L - 1)
    bands = jnp.linspace(1e-4, HY_BANDS - 1, HY_BANDS, dtype=f32)
    ang = (2.0 * math.pi / L) * pos[:, None] * bands[None, :]
    z = jnp.concatenate([t[:, None], jnp.cos(ang), -jnp.sin(ang)], axis=-1)
    f_w_in, f_w_hid, f_b, f_freq, f_w_out = [a.astype(f32) for a in (f_w_in, f_w_hid, f_b, f_freq, f_w_out)]
    g = jnp.sin(f_freq[0] * (z @ f_w_in + f_b[0]))
    g = jnp.sin(f_freq[1] * (g @ f_w_hid[0] + f_b[1]))
    g = jnp.sin(f_freq[2] * (g @ f_w_hid[1] + f_b[2]))
    filt = g @ f_w_out
    dist = jnp.abs(pos - L // 2) / (L // 2)
    deltas = jnp.abs(jnp.linspace(HY_MIN_DECAY, HY_MAX_DECAY, filt.shape[-1], dtype=f32))
    filt = filt * jnp.exp(-dist[:, None] * deltas[None, :])
    return filt / jnp.sum(jnp.abs(filt), axis=0, keepdims=True)


def centred_long_conv(u, h):
    L = u.shape[1]
    n = 2 * L
    U = jnp.fft.rfft(u.astype(jnp.float32), n=n, axis=1)
    H = jnp.fft.rfft(h.astype(jnp.float32), n=n, axis=0)
    y = jnp.fft.irfft(U * H[None], n=n, axis=1)[:, L // 2:L // 2 + L]
    return y.astype(u.dtype)


def hyena_mixer(u, w_in, b_in, conv_w, conv_b, f_w_in, f_w_hid, f_b, f_freq, f_w_out, bias, w_out, b_out):
    L = u.shape[1]
    proj = short_conv3(u @ w_in + b_in, conv_w, conv_b)
    x0, x1, v = jnp.split(proj, 3, axis=-1)
    h = implicit_filter(L, f_w_in, f_w_hid, f_b, f_freq, f_w_out)
    v = v * x1
    y = x0 * (centred_long_conv(v, h) + v * bias)
    return y @ w_out + b_out


def mla_queries(cq, q_norm, wq_b, cos, sin):
    B, L, _ = cq.shape
    q = (rms_norm(cq, q_norm) @ wq_b).reshape(B, L, MLA_HEADS, QK_NOPE + QK_ROPE)
    q_nope, q_rope = q[..., :QK_NOPE], q[..., QK_NOPE:]
    if cos is not None:
        q_rope = apply_axial_rope(q_rope, cos[None, :, None], sin[None, :, None])
    return q_nope, q_rope


def mla_keys_values(ckv, k_rope, kv_norm, wkv_b, cos, sin):
    B, L, _ = ckv.shape
    kv = (rms_norm(ckv, kv_norm) @ wkv_b).reshape(B, L, MLA_HEADS, QK_NOPE + V_HEAD)
    k_nope, v = kv[..., :QK_NOPE], kv[..., QK_NOPE:]
    if cos is not None:
        k_rope = apply_axial_rope(k_rope, cos[None], sin[None])
    return k_nope, k_rope, v


def mla_attend(q_nope, q_rope, k_nope, k_rope, v):
    B, Lq = q_nope.shape[:2]
    nb = Lq // Q_BLOCK
    qn = q_nope.reshape(B, nb, Q_BLOCK, MLA_HEADS, QK_NOPE).transpose(1, 0, 2, 3, 4)
    qr = q_rope.reshape(B, nb, Q_BLOCK, MLA_HEADS, QK_ROPE).transpose(1, 0, 2, 3, 4)

    def block(args):
        qn_b, qr_b = args
        s = jnp.einsum('bqhd,bkhd->bhqk', qn_b, k_nope) + jnp.einsum('bqhr,bkr->bhqk', qr_b, k_rope)
        p = jax.nn.softmax(s.astype(jnp.float32) * ATTN_SCALE, axis=-1).astype(v.dtype)
        return jnp.einsum('bhqk,bkhd->bqhd', p, v)

    o = lax.map(block, (qn, qr))
    return o.transpose(1, 0, 2, 3, 4).reshape(B, Lq, MLA_HEADS * V_HEAD)


def pool_mixer(u, w_grp, scale):
    B, L, D = u.shape
    uf = u.astype(jnp.float32).reshape(B, L, POOL_GROUPS, POOL_CH)
    cs = jnp.concatenate([jnp.zeros((B, 1, POOL_GROUPS, POOL_CH), jnp.float32), jnp.cumsum(uf, axis=1)], axis=1)
    t = jnp.arange(L)
    outs = []
    for g, w in enumerate(POOL_WINDOWS):
        lo = jnp.clip(t - w // 2, 0, L)
        hi = jnp.clip(t + w // 2, 0, L)
        csg = cs[:, :, g]
        s = jnp.take(csg, hi, axis=1) - jnp.take(csg, lo, axis=1)
        mean = s / (hi - lo).astype(jnp.float32)[None, :, None]
        outs.append(mean - uf[:, :, g])
    d = jnp.stack(outs, axis=2).astype(u.dtype)
    y = jnp.einsum('blgc,gce->blge', d, w_grp).reshape(B, L, D)
    return y * scale


def setup_inputs(seed: int = 0) -> dict:
    key = jax.random.key(seed)
    ks = iter(jax.random.split(key, 40))

    def nrm(shape, s):
        return jax.random.normal(next(ks), shape, jnp.float32) * s

    D = D_MODEL
    inp = {}
    inp['x'] = nrm((BATCH, SEQ, D), 1.0)
    inp['c'] = nrm((BATCH, D), 1.0)
    inp['ctx'] = nrm((BATCH, CTX_LEN, D), 1.0)
    inp['c_ctx'] = nrm((D,), 1.0)
    inp['ada_w'] = nrm((DEPTH, D, 6 * D), 0.5 * D ** -0.5)
    inp['ada_b'] = nrm((DEPTH, 6 * D), 0.02)
    inp['ln_g'] = 1.0 + nrm((DEPTH, 2, D), 0.02)
    inp['ln_b'] = nrm((DEPTH, 2, D), 0.02)
    inp['ffn_w_gate'] = nrm((DEPTH, D, D_FF), D ** -0.5)
    inp['ffn_w_up'] = nrm((DEPTH, D, D_FF), D ** -0.5)
    inp['ffn_w_down'] = nrm((DEPTH, D_FF, D), BETA * D_FF ** -0.5)
    inp['hy_w_in'] = nrm((N_HYENA, D, 3 * D), D ** -0.5)
    inp['hy_b_in'] = nrm((N_HYENA, 3 * D), 0.02)
    inp['hy_conv_w'] = nrm((N_HYENA, 3, 3 * D), 3 ** -0.5)
    inp['hy_conv_b'] = nrm((N_HYENA, 3 * D), 0.02)
    inp['hy_f_w_in'] = nrm((N_HYENA, HY_EMB, HY_FILT), HY_EMB ** -0.5)
    inp['hy_f_w_hid'] = nrm((N_HYENA, 2, HY_FILT, HY_FILT), HY_FILT ** -0.5)
    inp['hy_f_b'] = nrm((N_HYENA, 3, HY_FILT), 0.02)
    inp['hy_f_freq'] = 1.0 + nrm((N_HYENA, 3, HY_FILT), 0.1)
    inp['hy_f_w_out'] = nrm((N_HYENA, HY_FILT, D), HY_FILT ** -0.5)
    inp['hy_bias'] = nrm((N_HYENA, D), 1.0)
    inp['hy_w_out'] = nrm((N_HYENA, D, D), BETA * D ** -0.5)
    inp['hy_b_out'] = nrm((N_HYENA, D), 0.02)
    inp['mla_w_in'] = nrm((N_MLA, D, Q_RANK + KV_RANK + QK_ROPE), D ** -0.5)
    inp['mla_q_norm'] = 1.0 + nrm((N_MLA, Q_RANK), 0.02)
    inp['mla_kv_norm'] = 1.0 + nrm((N_MLA, KV_RANK), 0.02)
    inp['mla_wq_b'] = nrm((N_MLA, Q_RANK, MLA_HEADS * (QK_NOPE + QK_ROPE)), Q_RANK ** -0.5)
    inp['mla_wkv_b'] = nrm((N_MLA, KV_RANK, MLA_HEADS * (QK_NOPE + V_HEAD)), KV_RANK ** -0.5)
    inp['mla_w_out'] = nrm((N_MLA, MLA_HEADS * V_HEAD, D), BETA * (MLA_HEADS * V_HEAD) ** -0.5)
    inp['pool_w'] = nrm((N_POOL, POOL_GROUPS, POOL_CH, POOL_CH), BETA * POOL_CH ** -0.5)
    inp['pool_scale'] = 1.0 + nrm((N_POOL, D), 0.1)
    return inp


def reference(x, c, ctx, c_ctx, ada_w, ada_b, ln_g, ln_b, ffn_w_gate, ffn_w_up, ffn_w_down,
              hy_w_in, hy_b_in, hy_conv_w, hy_conv_b, hy_f_w_in, hy_f_w_hid, hy_f_b, hy_f_freq,
              hy_f_w_out, hy_bias, hy_w_out, hy_b_out,
              mla_w_in, mla_q_norm, mla_kv_norm, mla_wq_b, mla_wkv_b, mla_w_out,
              pool_w, pool_scale):
    L = x.shape[1]
    cos, sin = axial_rope_tables(L)
    mla_layers = [i for i in range(DEPTH) if i % N_MIXERS == 1]
    last_ctx_read = mla_layers[-1] if mla_layers else -1
    silu_c = jax.nn.silu(c)
    silu_cc = jax.nn.silu(c_ctx)
    h, hc = x, ctx
    for i in range(DEPTH):
        kind, j = i % N_MIXERS, i // N_MIXERS
        ctx_update = i < last_ctx_read
        sh1, sc1, g1, sh2, sc2, g2 = [t[:, None, :] for t in jnp.split(silu_c @ ada_w[i] + ada_b[i], 6, axis=-1)]
        if kind == 1 or ctx_update:
            csh1, csc1, cg1, csh2, csc2, cg2 = jnp.split(silu_cc @ ada_w[i] + ada_b[i], 6, axis=-1)
            uc = modulate(hc, csh1, csc1)
        u = modulate(h, sh1, sc1)
        if kind == 0:
            hy = (hy_w_in[j], hy_b_in[j], hy_conv_w[j], hy_conv_b[j], hy_f_w_in[j], hy_f_w_hid[j],
                  hy_f_b[j], hy_f_freq[j], hy_f_w_out[j], hy_bias[j], hy_w_out[j], hy_b_out[j])
            y = hyena_mixer(u, *hy)
            if ctx_update:
                yc = hyena_mixer(uc, *hy)
        elif kind == 1:
            w_in = mla_w_in[j]
            if ctx_update:
                cq_c, ckv_c, kr_c = jnp.split(uc @ w_in, [Q_RANK, Q_RANK + KV_RANK], axis=-1)
            else:
                ckv_c, kr_c = jnp.split(uc @ w_in[:, Q_RANK:], [KV_RANK], axis=-1)
            kn_c, kr_c, v_c = mla_keys_values(ckv_c, kr_c, mla_kv_norm[j], mla_wkv_b[j], None, None)
            cq, ckv, kr = jnp.split(u @ w_in, [Q_RANK, Q_RANK + KV_RANK], axis=-1)
            qn, qr = mla_queries(cq, mla_q_norm[j], mla_wq_b[j], cos, sin)
            kn, kr, v = mla_keys_values(ckv, kr, mla_kv_norm[j], mla_wkv_b[j], cos, sin)
            o = mla_attend(qn, qr, jnp.concatenate([kn, kn_c], axis=1),
                           jnp.concatenate([kr, kr_c], axis=1), jnp.concatenate([v, v_c], axis=1))
            y = o @ mla_w_out[j]
            if ctx_update:
                qn_c, qr_c = mla_queries(cq_c, mla_q_norm[j], mla_wq_b[j], None, None)
                yc = mla_attend(qn_c, qr_c, kn_c, kr_c, v_c) @ mla_w_out[j]
        else:
            y = pool_mixer(u, pool_w[j], pool_scale[j])
            if ctx_update:
                yc = pool_mixer(uc, pool_w[j], pool_scale[j])
        ffn = (ffn_w_gate[i], ffn_w_up[i], ffn_w_down[i])
        h = post_norm(h, g1 * y, ln_g[i, 0], ln_b[i, 0])
        h = post_norm(h, g2 * swiglu(modulate(h, sh2, sc2), *ffn), ln_g[i, 1], ln_b[i, 1])
        if ctx_update:
            hc = post_norm(hc, cg1 * yc, ln_g[i, 0], ln_b[i, 0])
            hc = post_norm(hc, cg2 * swiglu(modulate(hc, csh2, csc2), *ffn), ln_g[i, 1], ln_b[i, 1])
    return h
```

```python
import functools
import math

import jax
import jax.numpy as jnp
from jax import lax
from jax.experimental import pallas as pl
from jax.experimental.pallas import tpu as pltpu

F32 = jnp.float32
MXU_DTYPE = jnp.bfloat16

V7X_VMEM_BYTES = 64 * 1024 * 1024
VMEM_LIMIT = V7X_VMEM_BYTES - 8 * 1024 * 1024
LANES = 128
SUBLANES = 8

N_MIXERS = 3
GRID_W = 64
LN_EPS = 1e-6
RMS_EPS = 1e-6
HY_TARGET = 1e-2
HY_FAST = 0.3
HY_SLOW = 1.5
HY_MIN_DECAY = math.log(HY_TARGET) / HY_SLOW
HY_MAX_DECAY = math.log(HY_TARGET) / HY_FAST
QK_NOPE = 128
QK_ROPE = 64
V_HEAD = 128
ROPE_PAIRS = QK_ROPE // 4
ROPE_THETA = 10000.0
ATTN_SCALE = (QK_NOPE + QK_ROPE) ** -0.5
POOL_WINDOWS = (2, 4, 8, 16)
POOL_HALO = 8

DFT_N2 = 128
DFT_SMALL_MAX_L = 512
DFT_PASSES = 3


def _params(sem, vmem=VMEM_LIMIT):
    return pltpu.CompilerParams(dimension_semantics=sem, vmem_limit_bytes=vmem)


def _tile(n, t):
    t = min(n, t)
    assert n % t == 0, (n, t)
    return t


def _dot(a, b):
    return jnp.dot(a, b, preferred_element_type=F32)


def _split(x):
    hi = x.astype(MXU_DTYPE)
    lo = (x - hi.astype(F32)).astype(MXU_DTYPE)
    return hi, lo


def _dot_split(a, b):
    out = _dot(a[0], b[0])
    if DFT_PASSES >= 3:
        out = out + _dot(a[1], b[0]) + _dot(a[0], b[1])
    return out


def _layer_norm_rows(x, g, b):
    mu = jnp.mean(x, axis=-1, keepdims=True)
    xc = x - mu
    var = jnp.mean(xc * xc, axis=-1, keepdims=True)
    return xc * lax.rsqrt(var + LN_EPS) * g + b


def _silu(x):
    return x * (1.0 / (1.0 + jnp.exp(-x)))


def _ada_kernel(c_ref, w_ref, b_ref, o_ref):
    c = c_ref[...]
    a = _silu(c).astype(MXU_DTYPE)
    o_ref[0] = _dot(a, w_ref[0].astype(MXU_DTYPE)) + b_ref[0]


def _ada_mods(cond, ada_w, ada_b):
    depth, d, n = ada_w.shape
    rows = cond.shape[0]
    tn = _tile(n, 1024)
    return pl.pallas_call(
        _ada_kernel,
        out_shape=jax.ShapeDtypeStruct((depth, rows, n), F32),
        grid=(depth, n // tn),
        in_specs=[
            pl.BlockSpec((rows, d), lambda i, j: (0, 0)),
            pl.BlockSpec((1, d, tn), lambda i, j: (i, 0, j)),
            pl.BlockSpec((1, 1, tn), lambda i, j: (i, 0, j)),
        ],
        out_specs=pl.BlockSpec((1, rows, tn), lambda i, j: (i, 0, j)),
        compiler_params=_params(("parallel", "parallel")),
        name="ada_mods",
    )(cond, ada_w, ada_b.reshape(depth, 1, n))


def _mod_proj_kernel(x_ref, sh_ref, sc_ref, w_ref, b_ref, o_ref, u_scr):
    @pl.when(pl.program_id(2) == 0)
    def _():
        u_scr[...] = (x_ref[0] * (1.0 + sc_ref[0]) + sh_ref[0]).astype(MXU_DTYPE)

    o_ref[0, 0] = (_dot(u_scr[...], w_ref[...]) + b_ref[...]).astype(o_ref.dtype)


def _mod_proj(x, shift, scale, w, b, slabs, tm, tn):
    bsz, seq, k = x.shape
    n = w.shape[1]
    n_slab = n // slabs
    tm = _tile(seq, tm)
    tn = _tile(n_slab, tn)
    per = n_slab // tn
    return pl.pallas_call(
        _mod_proj_kernel,
        out_shape=jax.ShapeDtypeStruct((slabs, bsz, seq, n_slab), F32),
        grid=(bsz, seq // tm, n // tn),
        in_specs=[
            pl.BlockSpec((1, tm, k), lambda bi, i, j: (bi, i, 0)),
            pl.BlockSpec((1, 1, k), lambda bi, i, j: (bi, 0, 0)),
            pl.BlockSpec((1, 1, k), lambda bi, i, j: (bi, 0, 0)),
            pl.BlockSpec((k, tn), lambda bi, i, j: (0, j)),
            pl.BlockSpec((1, tn), lambda bi, i, j: (0, j)),
        ],
        out_specs=pl.BlockSpec((1, 1, tm, tn), lambda bi, i, j: (j // per, bi, i, j % per)),
        scratch_shapes=[pltpu.VMEM((tm, k), MXU_DTYPE)],
        compiler_params=_params(("parallel", "parallel", "arbitrary")),
        name="mod_proj",
    )(x, shift, scale, w.astype(MXU_DTYPE), b.reshape(1, n))


def _conv3_kernel(p_ref, prev_ref, next_ref, w_ref, b_ref, x0_ref, v_ref):
    i = pl.program_id(1)
    last = pl.num_programs(1) - 1
    tl = p_ref.shape[2]
    row = lax.broadcasted_iota(jnp.int32, (tl, 1), 0)
    outs = []
    for s in range(3):
        cur = p_ref[s, 0]
        before = jnp.where(i == 0, 0.0, prev_ref[s, 0][SUBLANES - 1:SUBLANES, :])
        after = jnp.where(i == last, 0.0, next_ref[s, 0][0:1, :])
        dn = jnp.where(row == 0, before, pltpu.roll(cur, 1, 0))
        up = jnp.where(row == tl - 1, after, pltpu.roll(cur, tl - 1, 0))
        w = w_ref[s]
        outs.append(dn * w[0:1, :] + cur * w[1:2, :] + up * w[2:3, :] + b_ref[s])
    x0_ref[0] = outs[0]
    v_ref[0] = outs[2] * outs[1]


def _conv3_gate(proj, conv_w, conv_b):
    _, bsz, seq, d = proj.shape
    tl = _tile(seq, 512)
    td = _tile(d, 512)
    hb = tl // SUBLANES
    nhb = seq // SUBLANES
    w = conv_w.reshape(3, 3, d).transpose(1, 0, 2)
    b = conv_b.reshape(3, 1, d)
    out = jax.ShapeDtypeStruct((bsz, seq, d), F32)
    return pl.pallas_call(
        _conv3_kernel,
        out_shape=(out, out),
        grid=(bsz, seq // tl, d // td),
        in_specs=[
            pl.BlockSpec((3, 1, tl, td), lambda bi, i, j: (0, bi, i, j)),
            pl.BlockSpec((3, 1, SUBLANES, td), lambda bi, i, j: (0, bi, jnp.maximum(i * hb - 1, 0), j)),
            pl.BlockSpec((3, 1, SUBLANES, td), lambda bi, i, j: (0, bi, jnp.minimum((i + 1) * hb, nhb - 1), j)),
            pl.BlockSpec((3, 3, td), lambda bi, i, j: (0, 0, j)),
            pl.BlockSpec((3, 1, td), lambda bi, i, j: (0, 0, j)),
        ],
        out_specs=(
            pl.BlockSpec((1, tl, td), lambda bi, i, j: (bi, i, j)),
            pl.BlockSpec((1, tl, td), lambda bi, i, j: (bi, i, j)),
        ),
        compiler_params=_params(("parallel", "parallel", "parallel")),
        name="hyena_conv3_gate",
    )(proj, proj, proj, w, b)


def _filter_kernel(z_ref, dist_ref, w_in_ref, w_hid_ref, b_ref, fr_ref, w_out_ref, delta_ref, f_ref, norm_ref):
    hp = lax.Precision.HIGHEST
    b = b_ref[...]
    fr = fr_ref[...]
    g = jnp.sin(fr[0:1] * (jnp.dot(z_ref[...], w_in_ref[...], precision=hp, preferred_element_type=F32) + b[0:1]))
    g = jnp.sin(fr[1:2] * (jnp.dot(g, w_hid_ref[0], precision=hp, preferred_element_type=F32) + b[1:2]))
    g = jnp.sin(fr[2:3] * (jnp.dot(g, w_hid_ref[1], precision=hp, preferred_element_type=F32) + b[2:3]))
    filt = jnp.dot(g, w_out_ref[...], precision=hp, preferred_element_type=F32)
    filt = filt * jnp.exp(-dist_ref[...] * delta_ref[...])
    f_ref[...] = filt

    @pl.when(pl.program_id(0) == 0)
    def _():
        norm_ref[...] = jnp.zeros_like(norm_ref)

    norm_ref[...] += jnp.sum(jnp.abs(filt), axis=0, keepdims=True)


def _implicit_filter(seq, f_w_in, f_w_hid, f_b, f_freq, f_w_out):
    emb, width = f_w_in.shape
    bands_n = (emb - 1) // 2
    d = f_w_out.shape[1]
    emb_pad = -(-emb // LANES) * LANES
    filt_w = -(-width // LANES) * LANES
    pad_to = lambda a, shape: jnp.pad(a.astype(F32), [(0, s - n) for s, n in zip(shape, a.shape)])
    pos = jnp.arange(seq, dtype=F32)
    t = pos / (seq - 1)
    bands = jnp.linspace(1e-4, bands_n - 1, bands_n, dtype=F32)
    ang = (2.0 * math.pi / seq) * pos[:, None] * bands[None, :]
    z = pad_to(jnp.concatenate([t[:, None], jnp.cos(ang), -jnp.sin(ang)], axis=-1), (seq, emb_pad))
    w_in = pad_to(f_w_in, (emb_pad, filt_w))
    f_w_hid = pad_to(f_w_hid, (2, filt_w, filt_w))
    f_b = pad_to(f_b, (3, filt_w))
    f_freq = pad_to(f_freq, (3, filt_w))
    f_w_out = pad_to(f_w_out, (filt_w, d))
    dist = (jnp.abs(pos - seq // 2) / (seq // 2))[:, None]
    deltas = jnp.abs(jnp.linspace(HY_MIN_DECAY, HY_MAX_DECAY, d, dtype=F32))[None, :]
    tl = _tile(seq, 512)
    full = lambda shape: pl.BlockSpec(shape, lambda i: (0,) * len(shape))
    return pl.pallas_call(
        _filter_kernel,
        out_shape=(jax.ShapeDtypeStruct((seq, d), F32), jax.ShapeDtypeStruct((1, d), F32)),
        grid=(seq // tl,),
        in_specs=[
            pl.BlockSpec((tl, emb_pad), lambda i: (i, 0)),
            pl.BlockSpec((tl, 1), lambda i: (i, 0)),
            full((emb_pad, filt_w)),
            full((2, filt_w, filt_w)),
            full((3, filt_w)),
            full((3, filt_w)),
            full((filt_w, d)),
            full((1, d)),
        ],
        out_specs=(pl.BlockSpec((tl, d), lambda i: (i, 0)), full((1, d))),
        compiler_params=_params(("arbitrary",)),
        name="hyena_filter",
    )(z, dist, w_in, f_w_hid.astype(F32), f_b.astype(F32), f_freq.astype(F32), f_w_out.astype(F32), deltas)


def _real_form(mr, mi):
    top = jnp.concatenate([mr, -mi], axis=-1)
    bot = jnp.concatenate([mi, mr], axis=-1)
    return jnp.concatenate([top, bot], axis=-2)


def _unit_roots(idx, n):
    ang = (2.0 * math.pi / n) * (idx % n).astype(F32)
    return jnp.cos(ang), -jnp.sin(ang)


def _split_table(t):
    hi = t.astype(MXU_DTYPE)
    return hi, (t - hi.astype(F32)).astype(MXU_DTYPE)


def _dft_tables(seq):
    n = 2 * seq
    n2 = DFT_N2
    n1 = n // n2
    i32 = jnp.int32
    p = jnp.arange(n1, dtype=i32)
    c = jnp.arange(n2, dtype=i32)
    tr, ti = _unit_roots(c[:, None] * p[None, :], n)
    a_in = jnp.arange(n1 // 2, dtype=i32)
    fr, fi = _unit_roots(p[:, None] * a_in[None, :], n1)
    mr = tr[:, :, None] * fr[None] - ti[:, :, None] * fi[None]
    mi = tr[:, :, None] * fi[None] + ti[:, :, None] * fr[None]
    first = _real_form(mr, mi)
    a_out = jnp.arange(n1 // 4, n1 // 4 + n1 // 2, dtype=i32)
    gr, gi = _unit_roots(p[:, None] * a_out[None, :], n1)
    cr = (tr[:, :, None] * gr[None] - ti[:, :, None] * gi[None]) / n
    ci = -(tr[:, :, None] * gi[None] + ti[:, :, None] * gr[None]) / n
    last = _real_form(jnp.swapaxes(cr, 1, 2), jnp.swapaxes(ci, 1, 2))
    q = jnp.arange(n2, dtype=i32)
    hr, hi_ = _unit_roots(q[:, None] * c[None, :], n2)
    mid_f = _real_form(hr, hi_)
    mid_i = _real_form(hr, -hi_)
    return tuple(_split_table(t) for t in (first, mid_f, mid_i, last))


def _dft_first_kernel(x_ref, mh_ref, ml_ref, o_ref):
    o_ref[...] = _dot_split((mh_ref[0], ml_ref[0]), _split(x_ref[...]))


def _dft_first(x, table, d):
    n1, cols = x.shape
    bd = _tile(d, 2048)
    per = d // bd
    mh, ml = table
    return pl.pallas_call(
        _dft_first_kernel,
        out_shape=jax.ShapeDtypeStruct((2 * n1, cols), F32),
        grid=(cols // bd,),
        in_specs=[
            pl.BlockSpec((n1, bd), lambda j: (0, j)),
            pl.BlockSpec((1, 2 * n1, n1), lambda j: (j // per, 0, 0)),
            pl.BlockSpec((1, 2 * n1, n1), lambda j: (j // per, 0, 0)),
        ],
        out_specs=pl.BlockSpec((2 * n1, bd), lambda j: (0, j)),
        compiler_params=_params(("parallel",)),
        name="hyena_dft_first",
    )(x, mh, ml)


def _dft_mid_filter_kernel(a_ref, fh_ref, fl_ref, norm_ref, o_ref):
    n2 = a_ref.shape[2]
    a = a_ref[...].reshape(2 * n2, a_ref.shape[3])
    z = _dot_split((fh_ref[...], fl_ref[...]), _split(a))
    o_ref[...] = (z / norm_ref[...]).reshape(o_ref.shape)


def _dft_mid_conv_kernel(a_ref, h_ref, fh_ref, fl_ref, gh_ref, gl_ref, o_ref):
    n2 = a_ref.shape[2]
    d = a_ref.shape[3]
    a = a_ref[...].reshape(2 * n2, d)
    z = _dot_split((fh_ref[...], fl_ref[...]), _split(a))
    zr, zi = z[:n2], z[n2:]
    hr, hi_ = h_ref[0, 0], h_ref[1, 0]
    w = jnp.concatenate([zr * hr - zi * hi_, zr * hi_ + zi * hr], axis=0)
    y = _dot_split((gh_ref[...], gl_ref[...]), _split(w))
    o_ref[...] = y.reshape(o_ref.shape)


def _dft_mid(a, tables, d, *, spectrum=None, norm=None):
    n2 = DFT_N2
    n1 = a.shape[0] // 2
    a4 = a.reshape(2, n1, n2, d)
    bd = _tile(d, 2048)
    blk = pl.BlockSpec((2, 1, n2, bd), lambda p, j: (0, p, 0, j))
    mat = pl.BlockSpec((2 * n2, 2 * n2), lambda p, j: (0, 0))
    (fh, fl), (gh, gl) = tables
    if spectrum is None:
        kern, ins, specs = _dft_mid_filter_kernel, (a4, fh, fl, norm), [blk, mat, mat, pl.BlockSpec((1, bd), lambda p, j: (0, j))]
    else:
        kern, ins, specs = _dft_mid_conv_kernel, (a4, spectrum, fh, fl, gh, gl), [blk, blk, mat, mat, mat, mat]
    out = pl.pallas_call(
        kern,
        out_shape=jax.ShapeDtypeStruct((2, n1, n2, d), F32),
        grid=(n1, d // bd),
        in_specs=specs,
        out_specs=blk,
        compiler_params=_params(("parallel", "parallel")),
        name="hyena_dft_mid",
    )(*ins)
    return out


def _dft_last_kernel(x_ref, mh_ref, ml_ref, o_ref):
    o_ref[...] = _dot_split((mh_ref[0], ml_ref[0]), _split(x_ref[...]))


def _dft_last(b, table, d):
    rows, cols = b.shape
    n1 = rows // 2
    bd = _tile(d, 2048)
    per = d // bd
    mh, ml = table
    return pl.pallas_call(
        _dft_last_kernel,
        out_shape=jax.ShapeDtypeStruct((n1, cols), F32),
        grid=(cols // bd,),
        in_specs=[
            pl.BlockSpec((rows, bd), lambda j: (0, j)),
            pl.BlockSpec((1, n1, rows), lambda j: (j // per, 0, 0)),
            pl.BlockSpec((1, n1, rows), lambda j: (j // per, 0, 0)),
        ],
        out_specs=pl.BlockSpec((n1, bd), lambda j: (0, j)),
        compiler_params=_params(("parallel",)),
        name="hyena_dft_last",
    )(b, mh, ml)


def _long_conv_two_stage(v, filt, norm, tables):
    bsz, seq, d = v.shape
    n1 = 2 * seq // DFT_N2
    first, mid_f, mid_i, last = tables
    cols = DFT_N2 * d
    hx = jnp.concatenate([filt.reshape(n1 // 2, cols), jnp.zeros((n1 // 2, cols), F32)], axis=0)
    spectrum = _dft_mid(_dft_first(hx, first, d), (mid_f, mid_i), d, norm=norm)
    a = _dft_first(v.reshape(n1, cols), first, d)
    bmat = _dft_mid(a, (mid_f, mid_i), d, spectrum=spectrum)
    y = _dft_last(bmat.reshape(2 * n1, cols), last, d)
    return y.reshape(bsz, seq, d)


def _dft_small_kernel(v_ref, h_ref, norm_ref, fh_ref, fl_ref, gh_ref, gl_ref, o_ref):
    seq = v_ref.shape[1]
    n = 2 * seq
    fwd = (fh_ref[...], fl_ref[...])
    x = jnp.concatenate([v_ref[0], v_ref[1]], axis=0)
    z = _dot_split(fwd, _split(x))
    hx = jnp.concatenate([h_ref[...] / norm_ref[...], jnp.zeros_like(h_ref)], axis=0)
    hs = _dot_split(fwd, _split(hx))
    zr, zi, hr, hi_ = z[:n], z[n:], hs[:n], hs[n:]
    w = jnp.concatenate([zr * hr - zi * hi_, zr * hi_ + zi * hr], axis=0)
    y = _dot_split((gh_ref[...], gl_ref[...]), _split(w))
    o_ref[0] = y[:seq]
    o_ref[1] = y[seq:]


def _long_conv_small(v, filt, norm):
    bsz, seq, d = v.shape
    n = 2 * seq
    k = jnp.arange(n, dtype=jnp.int32)
    t_in = jnp.arange(seq, dtype=jnp.int32)
    fr, fi = _unit_roots(k[:, None] * t_in[None, :], n)
    fwd = _real_form(fr, fi)
    t_out = jnp.arange(seq // 2, seq // 2 + seq, dtype=jnp.int32)
    gr, gi = _unit_roots(t_out[:, None] * k[None, :], n)
    inv = _real_form(gr / n, -gi / n)
    (fh, fl), (gh, gl) = _split_table(fwd), _split_table(inv)
    bd = _tile(d, 512)
    full = lambda shape: pl.BlockSpec(shape, lambda j: (0,) * len(shape))
    return pl.pallas_call(
        _dft_small_kernel,
        out_shape=jax.ShapeDtypeStruct((bsz, seq, d), F32),
        grid=(d // bd,),
        in_specs=[
            pl.BlockSpec((bsz, seq, bd), lambda j: (0, 0, j)),
            pl.BlockSpec((seq, bd), lambda j: (0, j)),
            pl.BlockSpec((1, bd), lambda j: (0, j)),
            full(fwd.shape), full(fwd.shape), full(inv.shape), full(inv.shape),
        ],
        out_specs=pl.BlockSpec((bsz, seq, bd), lambda j: (0, 0, j)),
        compiler_params=_params(("parallel",)),
        name="hyena_dft_small",
    )(v, filt, norm, fh, fl, gh, gl)


def _hyena_out_kernel(alpha, x0_ref, cv_ref, v_ref, bias_ref, w_ref, b_ref, h_ref, g_ref, lg_ref, lb_ref, o_ref):
    y = x0_ref[0] * (cv_ref[0] + v_ref[0] * bias_ref[...])
    y = _dot(y.astype(MXU_DTYPE), w_ref[...]) + b_ref[...]
    o_ref[0] = _layer_norm_rows(alpha * h_ref[0] + g_ref[0] * y, lg_ref[...], lb_ref[...])


def _attn_out_kernel(alpha, o_in_ref, w_ref, h_ref, g_ref, lg_ref, lb_ref, o_ref):
    y = _dot(o_in_ref[0], w_ref[...])
    o_ref[0] = _layer_norm_rows(alpha * h_ref[0] + g_ref[0] * y, lg_ref[...], lb_ref[...])


def _row_spec(tm, d):
    return pl.BlockSpec((1, tm, d), lambda bi, i: (bi, i, 0))


def _vec_spec(d):
    return pl.BlockSpec((1, d), lambda bi, i: (0, 0))


def _bvec_spec(d):
    return pl.BlockSpec((1, 1, d), lambda bi, i: (bi, 0, 0))


def _const_spec(shape):
    return pl.BlockSpec(shape, lambda bi, i: (0,) * len(shape), pipeline_mode=pl.Buffered(1))


def _hyena_out(alpha, x0, cv, v, bias, w, b, h, gate, ln_g, ln_b):
    bsz, seq, d = h.shape
    tm = _tile(seq, 256)
    return pl.pallas_call(
        functools.partial(_hyena_out_kernel, alpha),
        out_shape=jax.ShapeDtypeStruct(h.shape, F32),
        grid=(bsz, seq // tm),
        in_specs=[_row_spec(tm, d), _row_spec(tm, d), _row_spec(tm, d), _vec_spec(d), _const_spec((d, d)),
                  _vec_spec(d), _row_spec(tm, d), _bvec_spec(d), _vec_spec(d), _vec_spec(d)],
        out_specs=_row_spec(tm, d),
        compiler_params=_params(("parallel", "parallel")),
        name="hyena_out_norm",
    )(x0, cv, v, bias.reshape(1, d), w.astype(MXU_DTYPE), b.reshape(1, d), h, gate, ln_g.reshape(1, d), ln_b.reshape(1, d))


def _attn_out(alpha, o, w, h, gate, ln_g, ln_b):
    bsz, seq, d = h.shape
    k = o.shape[2]
    tm = _tile(seq, 512)
    return pl.pallas_call(
        functools.partial(_attn_out_kernel, alpha),
        out_shape=jax.ShapeDtypeStruct(h.shape, F32),
        grid=(bsz, seq // tm),
        in_specs=[_row_spec(tm, k), _const_spec((k, d)), _row_spec(tm, d), _bvec_spec(d), _vec_spec(d), _vec_spec(d)],
        out_specs=_row_spec(tm, d),
        compiler_params=_params(("parallel", "parallel")),
        name="attn_out_norm",
    )(o, w.astype(MXU_DTYPE), h, gate, ln_g.reshape(1, d), ln_b.reshape(1, d))


def _ffn_kernel(alpha, h_ref, sh_ref, sc_ref, g_ref, wg_ref, wu_ref, wd_ref, lg_ref, lb_ref, o_ref, u_scr, acc_scr):
    f = pl.program_id(2)

    @pl.when(f == 0)
    def _():
        u_scr[...] = (h_ref[0] * (1.0 + sc_ref[0]) + sh_ref[0]).astype(MXU_DTYPE)
        acc_scr[...] = jnp.zeros_like(acc_scr)

    u = u_scr[...]
    gate = _dot(u, wg_ref[...])
    up = _dot(u, wu_ref[...])
    act = (_silu(gate) * up).astype(MXU_DTYPE)
    acc_scr[...] += _dot(act, wd_ref[...])

    @pl.when(f == pl.num_programs(2) - 1)
    def _():
        o_ref[0] = _layer_norm_rows(alpha * h_ref[0] + g_ref[0] * acc_scr[...], lg_ref[...], lb_ref[...])


def _ffn(alpha, h, shift, scale, gate, w_gate, w_up, w_down, ln_g, ln_b):
    bsz, seq, d = h.shape
    ff = w_gate.shape[1]
    tm = _tile(seq, 512)
    tf = _tile(ff, 512)
    return pl.pallas_call(
        functools.partial(_ffn_kernel, alpha),
        out_shape=jax.ShapeDtypeStruct(h.shape, F32),
        grid=(bsz, seq // tm, ff // tf),
        in_specs=[
            pl.BlockSpec((1, tm, d), lambda bi, i, f: (bi, i, 0)),
            pl.BlockSpec((1, 1, d), lambda bi, i, f: (bi, 0, 0)),
            pl.BlockSpec((1, 1, d), lambda bi, i, f: (bi, 0, 0)),
            pl.BlockSpec((1, 1, d), lambda bi, i, f: (bi, 0, 0)),
            pl.BlockSpec((d, tf), lambda bi, i, f: (0, f)),
            pl.BlockSpec((d, tf), lambda bi, i, f: (0, f)),
            pl.BlockSpec((tf, d), lambda bi, i, f: (f, 0)),
            pl.BlockSpec((1, d), lambda bi, i, f: (0, 0)),
            pl.BlockSpec((1, d), lambda bi, i, f: (0, 0)),
        ],
        out_specs=pl.BlockSpec((1, tm, d), lambda bi, i, f: (bi, i, 0)),
        scratch_shapes=[pltpu.VMEM((tm, d), MXU_DTYPE), pltpu.VMEM((tm, d), F32)],
        compiler_params=_params(("parallel", "parallel", "arbitrary")),
        name="ffn_swiglu_norm",
    )(h, shift, scale, gate, w_gate.astype(MXU_DTYPE), w_up.astype(MXU_DTYPE), w_down.astype(MXU_DTYPE),
      ln_g.reshape(1, d), ln_b.reshape(1, d))


def _rotate_half_cols(w):
    ws = w.reshape(w.shape[:-1] + (2, 2, ROPE_PAIRS))
    return jnp.stack([-ws[..., 1, :], ws[..., 0, :]], axis=-2).reshape(w.shape)


def _rope_table(seq):
    rows = seq // GRID_W
    row = jnp.repeat(jnp.arange(rows, dtype=F32), GRID_W)
    col = jnp.tile(jnp.arange(GRID_W, dtype=F32), rows)
    inv = ROPE_THETA ** (-jnp.arange(ROPE_PAIRS, dtype=F32) / ROPE_PAIRS)
    ang = jnp.stack([row[:, None] * inv, col[:, None] * inv], axis=1)
    ang = jnp.broadcast_to(ang[:, :, None, :], (seq, 2, 2, ROPE_PAIRS)).reshape(seq, QK_ROPE)
    return jnp.concatenate([jnp.cos(ang), jnp.sin(ang)], axis=-1)


def _rms_rows(x, g):
    return x * lax.rsqrt(jnp.mean(x * x, axis=-1, keepdims=True) + RMS_EPS) * g


def _rope_pair(x, cs):
    t = x * cs
    return t + pltpu.roll(t, QK_ROPE, 1)


def _mla_q_kernel(cq_ref, g_ref, w_ref, cs_ref, q_ref, xn_scr):
    @pl.when(pl.program_id(2) == 0)
    def _():
        xn_scr[...] = _rms_rows(cq_ref[0], g_ref[...]).astype(MXU_DTYPE)

    a = _dot(xn_scr[...], w_ref[0])
    r = _rope_pair(a[:, QK_NOPE:], cs_ref[...])
    q_ref[0, 0] = (jnp.concatenate([a[:, :QK_NOPE], r], axis=1) * ATTN_SCALE).astype(q_ref.dtype)


def _mla_queries(t, q_norm, wq_b, cs, heads):
    bsz, seq, _ = t.shape
    rank = q_norm.shape[0]
    tm = _tile(seq, 512)
    w = wq_b.reshape(rank, heads, QK_NOPE + QK_ROPE)
    w = jnp.concatenate([w, _rotate_half_cols(w[..., QK_NOPE:])], axis=-1).transpose(1, 0, 2).astype(MXU_DTYPE)
    return pl.pallas_call(
        _mla_q_kernel,
        out_shape=jax.ShapeDtypeStruct((bsz, heads, seq, 2 * LANES), MXU_DTYPE),
        grid=(bsz, seq // tm, heads),
        in_specs=[
            pl.BlockSpec((1, tm, rank), lambda bi, i, h: (bi, i, 0)),
            pl.BlockSpec((1, rank), lambda bi, i, h: (0, 0)),
            pl.BlockSpec((1, rank, 2 * LANES), lambda bi, i, h: (h, 0, 0)),
            pl.BlockSpec((tm, LANES), lambda bi, i, h: (i, 0)),
        ],
        out_specs=pl.BlockSpec((1, 1, tm, 2 * LANES), lambda bi, i, h: (bi, h, i, 0)),
        scratch_shapes=[pltpu.VMEM((tm, rank), MXU_DTYPE)],
        compiler_params=_params(("parallel", "parallel", "arbitrary")),
        name="mla_queries",
    )(t, q_norm.reshape(1, rank), w, cs)


def _mla_kv_kernel(use_rope, ckv_ref, kr_ref, g_ref, w_ref, cs_ref, k_ref, v_ref, xn_scr):
    @pl.when(pl.program_id(2) == 0)
    def _():
        xn_scr[...] = _rms_rows(ckv_ref[0], g_ref[...]).astype(MXU_DTYPE)

    kv = _dot(xn_scr[...], w_ref[0])
    kr = kr_ref[0]
    if use_rope:
        kr = _rope_pair(kr, cs_ref[...])
    lane = lax.broadcasted_iota(jnp.int32, kr.shape, 1)
    kr = jnp.where(lane < QK_ROPE, kr, 0.0)
    k_ref[0, 0] = jnp.concatenate([kv[:, :QK_NOPE], kr], axis=1).astype(k_ref.dtype)
    v_ref[0, 0] = kv[:, QK_NOPE:].astype(v_ref.dtype)


def _mla_keys_values(t, kv_norm, wkv_b, cs, heads, use_rope):
    bsz, seq, _ = t.shape
    rank = kv_norm.shape[0]
    tm = _tile(seq, 512)
    w = wkv_b.reshape(rank, heads, QK_NOPE + V_HEAD).transpose(1, 0, 2).astype(MXU_DTYPE)
    return pl.pallas_call(
        functools.partial(_mla_kv_kernel, use_rope),
        out_shape=(jax.ShapeDtypeStruct((bsz, heads, seq, 2 * LANES), MXU_DTYPE),
                   jax.ShapeDtypeStruct((bsz, heads, seq, V_HEAD), MXU_DTYPE)),
        grid=(bsz, seq // tm, heads),
        in_specs=[
            pl.BlockSpec((1, tm, rank), lambda bi, i, h: (bi, i, 1)),
            pl.BlockSpec((1, tm, LANES), lambda bi, i, h: (bi, i, 2 * rank // LANES)),
            pl.BlockSpec((1, rank), lambda bi, i, h: (0, 0)),
            pl.BlockSpec((1, rank, QK_NOPE + V_HEAD), lambda bi, i, h: (h, 0, 0)),
            pl.BlockSpec((tm, LANES), lambda bi, i, h: (i, 0)),
        ],
        out_specs=(pl.BlockSpec((1, 1, tm, 2 * LANES), lambda bi, i, h: (bi, h, i, 0)),
                   pl.BlockSpec((1, 1, tm, V_HEAD), lambda bi, i, h: (bi, h, i, 0))),
        scratch_shapes=[pltpu.VMEM((tm, rank), MXU_DTYPE)],
        compiler_params=_params(("parallel", "parallel", "arbitrary")),
        name="mla_keys_values",
    )(t, t, kv_norm.reshape(1, rank), w, cs)


def _attn_kernel(q_ref, k_ref, v_ref, kc_ref, vc_ref, o_ref, m_scr, l_scr, acc_scr):
    kv = pl.program_id(3)
    n_main = pl.num_programs(3) - 1

    @pl.when(kv == 0)
    def _():
        m_scr[...] = jnp.full_like(m_scr, -jnp.inf)
        l_scr[...] = jnp.zeros_like(l_scr)
        acc_scr[...] = jnp.zeros_like(acc_scr)

    def step(k, v):
        s = lax.dot_general(q_ref[0, 0], k, (((1,), (1,)), ((), ())), preferred_element_type=F32)
        m_prev = m_scr[...]
        m_new = jnp.maximum(m_prev, jnp.max(s, axis=1, keepdims=True))
        a = jnp.exp(m_prev - m_new)
        p = jnp.exp(s - m_new)
        l_scr[...] = a * l_scr[...] + jnp.sum(p, axis=1, keepdims=True)
        acc_scr[...] = a * acc_scr[...] + _dot(p.astype(MXU_DTYPE), v)
        m_scr[...] = m_new

    @pl.when(kv < n_main)
    def _():
        step(k_ref[0, 0], v_ref[0, 0])

    @pl.when(kv == n_main)
    def _():
        step(kc_ref[0, 0], vc_ref[0, 0])
        o_ref[0] = (acc_scr[...] / l_scr[...]).astype(o_ref.dtype)


def _mla_attend(q, k, v, kc, vc):
    bsz, heads, seq, dq = q.shape
    lc = kc.shape[2]
    tq = _tile(seq, 1024)
    tk = _tile(seq, 512)
    n_main = seq // tk
    kv_idx = lambda bi, h, i, j: (bi, h, jnp.minimum(j, n_main - 1), 0)
    return pl.pallas_call(
        _attn_kernel,
        out_shape=jax.ShapeDtypeStruct((bsz, seq, heads * V_HEAD), MXU_DTYPE),
        grid=(bsz, heads, seq // tq, n_main + 1),
        in_specs=[
            pl.BlockSpec((1, 1, tq, dq), lambda bi, h, i, j: (bi, h, i, 0)),
            pl.BlockSpec((1, 1, tk, dq), kv_idx),
            pl.BlockSpec((1, 1, tk, V_HEAD), kv_idx),
            pl.BlockSpec((1, 1, lc, dq), lambda bi, h, i, j: (bi, h, 0, 0)),
            pl.BlockSpec((1, 1, lc, V_HEAD), lambda bi, h, i, j: (bi, h, 0, 0)),
        ],
        out_specs=pl.BlockSpec((1, tq, V_HEAD), lambda bi, h, i, j: (bi, i, h)),
        scratch_shapes=[pltpu.VMEM((tq, 1), F32), pltpu.VMEM((tq, 1), F32), pltpu.VMEM((tq, V_HEAD), F32)],
        compiler_params=_params(("parallel", "parallel", "parallel", "arbitrary")),
        name="mla_attention",
    )(q, k, v, kc, vc)


def _pool_kernel(alpha, seq, h_ref, prev_ref, next_ref, sh_ref, sc_ref, g_ref, w_ref, ps_ref, lg_ref, lb_ref, o_ref):
    i = pl.program_id(1)
    last = pl.num_programs(1) - 1
    tm = h_ref.shape[1]
    d = h_ref.shape[2]
    groups = len(POOL_WINDOWS)
    ch = d // groups
    ext = tm + 2 * POOL_HALO
    shift, scale = sh_ref[0], sc_ref[0]
    h = h_ref[0]
    u = h * (1.0 + scale) + shift
    u_prev = jnp.where(i == 0, 0.0, prev_ref[0] * (1.0 + scale) + shift)
    u_next = jnp.where(i == last, 0.0, next_ref[0] * (1.0 + scale) + shift)
    e = jnp.concatenate([u_prev, u, u_next], axis=0)
    t = i * tm + lax.broadcasted_iota(jnp.int32, (tm, 1), 0)
    ys = []
    for g, win in enumerate(POOL_WINDOWS):
        a = e[:, g * ch:(g + 1) * ch]
        span = 1
        while span < win:
            a = a + pltpu.roll(a, ext - span, 0)
            span *= 2
        half = win // 2
        a = pltpu.roll(a, half, 0)
        s = a[POOL_HALO:POOL_HALO + tm]
        cnt = (jnp.minimum(t + half, seq) - jnp.maximum(t - half, 0)).astype(F32)
        dg = s / cnt - u[:, g * ch:(g + 1) * ch]
        ys.append(_dot(dg.astype(MXU_DTYPE), w_ref[g]))
    y = jnp.concatenate(ys, axis=1) * ps_ref[...]
    o_ref[0] = _layer_norm_rows(alpha * h + g_ref[0] * y, lg_ref[...], lb_ref[...])


def _pool_mixer(alpha, h, shift, scale, gate, w_grp, pool_scale, ln_g, ln_b):
    bsz, seq, d = h.shape
    groups, ch, _ = w_grp.shape
    tm = _tile(seq, 512)
    hb = tm // POOL_HALO
    nhb = seq // POOL_HALO
    return pl.pallas_call(
        functools.partial(_pool_kernel, alpha, seq),
        out_shape=jax.ShapeDtypeStruct(h.shape, F32),
        grid=(bsz, seq // tm),
        in_specs=[
            _row_spec(tm, d),
            pl.BlockSpec((1, POOL_HALO, d), lambda bi, i: (bi, jnp.maximum(i * hb - 1, 0), 0)),
            pl.BlockSpec((1, POOL_HALO, d), lambda bi, i: (bi, jnp.minimum((i + 1) * hb, nhb - 1), 0)),
            _bvec_spec(d), _bvec_spec(d), _bvec_spec(d),
            pl.BlockSpec((groups, ch, ch), lambda bi, i: (0, 0, 0)),
            _vec_spec(d), _vec_spec(d), _vec_spec(d),
        ],
        out_specs=_row_spec(tm, d),
        compiler_params=_params(("parallel", "parallel")),
        name="pool_mixer_norm",
    )(h, h, h, shift, scale, gate, w_grp.astype(MXU_DTYPE), pool_scale.reshape(1, d), ln_g.reshape(1, d), ln_b.reshape(1, d))


def _hyena_mixer_norm(alpha, h, mod, hy, filt, norm, tables, ln_g, ln_b):
    shift, scale, gate = mod
    w_in, b_in, conv_w, conv_b, bias, w_out, b_out = hy
    proj = _mod_proj(h, shift, scale, w_in, b_in, slabs=3, tm=512, tn=1024)
    x0, v = _conv3_gate(proj, conv_w, conv_b)
    if h.shape[1] <= DFT_SMALL_MAX_L:
        cv = _long_conv_small(v, filt, norm)
    else:
        cv = _long_conv_two_stage(v, filt, norm, tables)
    return _hyena_out(alpha, x0, cv, v, bias, w_out, b_out, h, gate, ln_g, ln_b)


def kernel(x, c, ctx, c_ctx, ada_w, ada_b, ln_g, ln_b, ffn_w_gate, ffn_w_up, ffn_w_down, hy_w_in, hy_b_in, hy_conv_w, hy_conv_b, hy_f_w_in, hy_f_w_hid, hy_f_b, hy_f_freq, hy_f_w_out, hy_bias, hy_w_out, hy_b_out, mla_w_in, mla_q_norm, mla_kv_norm, mla_wq_b, mla_wkv_b, mla_w_out, pool_w, pool_scale):
    bsz, seq, d = x.shape
    depth = ada_w.shape[0]
    assert bsz == 2, "the long convolution packs exactly two batch rows into one complex signal"
    assert ctx.shape[0] == bsz and seq % GRID_W == 0
    alpha = (2.0 * depth) ** 0.25
    heads = d // V_HEAD
    mla_layers = [i for i in range(depth) if i % N_MIXERS == 1]
    last_ctx_read = mla_layers[-1] if mla_layers else -1

    cond = jnp.concatenate([c, jnp.broadcast_to(c_ctx[None], (SUBLANES - bsz, d))], axis=0)
    mods = _ada_mods(cond, ada_w, ada_b)

    def mod_vecs(i, ctx_stream):
        m = jnp.broadcast_to(mods[i, bsz][None], (bsz, 6 * d)) if ctx_stream else mods[i, :bsz]
        return [m[:, None, k * d:(k + 1) * d] for k in range(6)]

    cs = _rope_table(seq)
    tables = _dft_tables(seq) if seq > DFT_SMALL_MAX_L else None

    h, hc = x, ctx
    for i in range(depth):
        kind, j = i % N_MIXERS, i // N_MIXERS
        ctx_update = i < last_ctx_read
        sh1, sc1, g1, sh2, sc2, g2 = mod_vecs(i, False)
        if kind == 1 or ctx_update:
            csh1, csc1, cg1, csh2, csc2, cg2 = mod_vecs(i, True)
        lg, lb = ln_g[i], ln_b[i]
        if kind == 0:
            hy = (hy_w_in[j], hy_b_in[j], hy_conv_w[j], hy_conv_b[j], hy_bias[j], hy_w_out[j], hy_b_out[j])
            fp = (hy_f_w_in[j], hy_f_w_hid[j], hy_f_b[j], hy_f_freq[j], hy_f_w_out[j])
            filt, norm = _implicit_filter(seq, *fp)
            h_mid = _hyena_mixer_norm(alpha, h, (sh1, sc1, g1), hy, filt, norm, tables, lg[0], lb[0])
            if ctx_update:
                filt_c, norm_c = _implicit_filter(hc.shape[1], *fp)
                hc_mid = _hyena_mixer_norm(alpha, hc, (csh1, csc1, cg1), hy, filt_c, norm_c, None, lg[0], lb[0])
        elif kind == 1:
            assert not ctx_update, "context queries are only needed when a later layer reads the context"
            rank = mla_q_norm.shape[1]
            w_in = mla_w_in[j]
            w_in = jnp.concatenate([w_in, _rotate_half_cols(w_in[:, 2 * rank:])], axis=1)
            zeros = jnp.zeros((w_in.shape[1],), F32)
            t = _mod_proj(h, sh1, sc1, w_in, zeros, slabs=1, tm=512, tn=w_in.shape[1])[0]
            tc = _mod_proj(hc, csh1, csc1, w_in, zeros, slabs=1, tm=512, tn=w_in.shape[1])[0]
            q = _mla_queries(t, mla_q_norm[j], mla_wq_b[j], cs, heads)
            k, v = _mla_keys_values(t, mla_kv_norm[j], mla_wkv_b[j], cs, heads, True)
            kc, vc = _mla_keys_values(tc, mla_kv_norm[j], mla_wkv_b[j], cs, heads, False)
            o = _mla_attend(q, k, v, kc, vc)
            h_mid = _attn_out(alpha, o, mla_w_out[j], h, g1, lg[0], lb[0])
        else:
            h_mid = _pool_mixer(alpha, h, sh1, sc1, g1, pool_w[j], pool_scale[j], lg[0], lb[0])
            if ctx_update:
                hc_mid = _pool_mixer(alpha, hc, csh1, csc1, cg1, pool_w[j], pool_scale[j], lg[0], lb[0])
        ffn = (ffn_w_gate[i], ffn_w_up[i], ffn_w_down[i])
        h = _ffn(alpha, h_mid, sh2, sc2, g2, *ffn, lg[1], lb[1])
        if ctx_update:
            hc = _ffn(alpha, hc_mid, csh2, csc2, cg2, *ffn, lg[1], lb[1])
    return h
```

```python
import functools
import math

import jax
import jax.numpy as jnp
from jax import lax
from jax.experimental import pallas as pl
from jax.experimental.pallas import tpu as pltpu

F32 = jnp.float32
MXU_DTYPE = jnp.bfloat16

V7X_VMEM_BYTES = 64 * 1024 * 1024
VMEM_LIMIT = V7X_VMEM_BYTES - 8 * 1024 * 1024
LANES = 128
SUBLANES = 8

N_MIXERS = 3
GRID_W = 64
LN_EPS = 1e-6
RMS_EPS = 1e-6
HY_TARGET = 1e-2
HY_FAST = 0.3
HY_SLOW = 1.5
HY_MIN_DECAY = math.log(HY_TARGET) / HY_SLOW
HY_MAX_DECAY = math.log(HY_TARGET) / HY_FAST
QK_NOPE = 128
QK_ROPE = 64
V_HEAD = 128
ROPE_PAIRS = QK_ROPE // 4
ROPE_THETA = 10000.0
ATTN_SCALE = (QK_NOPE + QK_ROPE) ** -0.5
QUERY_SCALE = ATTN_SCALE * math.log2(math.e)
ATTN_CHUNK = 256
V_ROWS = V_HEAD + 16
POOL_WINDOWS = (2, 4, 8, 16)
POOL_HALO = 8

DFT_N2 = 128
DFT_SMALL_MAX_L = 512
DFT_PASSES = 3


def _params(sem, vmem=VMEM_LIMIT):
    return pltpu.CompilerParams(dimension_semantics=sem, vmem_limit_bytes=vmem)


def _tile(n, t):
    t = min(n, t)
    assert n % t == 0, (n, t)
    return t


def _dot(a, b):
    return jnp.dot(a, b, preferred_element_type=F32)


def _split(x):
    hi = x.astype(MXU_DTYPE)
    lo = (x - hi.astype(F32)).astype(MXU_DTYPE)
    return hi, lo


def _dot_split(a, b):
    out = _dot(a[0], b[0])
    if DFT_PASSES >= 3:
        out = out + _dot(a[1], b[0]) + _dot(a[0], b[1])
    return out


def _layer_norm_rows(x, g, b):
    mu = jnp.mean(x, axis=-1, keepdims=True)
    xc = x - mu
    var = jnp.mean(xc * xc, axis=-1, keepdims=True)
    return xc * lax.rsqrt(var + LN_EPS) * g + b


def _silu(x):
    return x * (1.0 / (1.0 + jnp.exp(-x)))


def _ada_kernel(c_ref, w_ref, b_ref, o_ref):
    c = c_ref[...]
    a = _silu(c).astype(MXU_DTYPE)
    o_ref[0] = _dot(a, w_ref[0].astype(MXU_DTYPE)) + b_ref[0]


def _ada_mods(cond, ada_w, ada_b):
    depth, d, n = ada_w.shape
    rows = cond.shape[0]
    tn = _tile(n, 1024)
    return pl.pallas_call(
        _ada_kernel,
        out_shape=jax.ShapeDtypeStruct((depth, rows, n), F32),
        grid=(depth, n // tn),
        in_specs=[
            pl.BlockSpec((rows, d), lambda i, j: (0, 0)),
            pl.BlockSpec((1, d, tn), lambda i, j: (i, 0, j)),
            pl.BlockSpec((1, 1, tn), lambda i, j: (i, 0, j)),
        ],
        out_specs=pl.BlockSpec((1, rows, tn), lambda i, j: (i, 0, j)),
        compiler_params=_params(("parallel", "parallel")),
        name="ada_mods",
    )(cond, ada_w, ada_b.reshape(depth, 1, n))


def _mod_proj_kernel(x_ref, sh_ref, sc_ref, w_ref, b_ref, o_ref, u_scr):
    @pl.when(pl.program_id(2) == 0)
    def _():
        u_scr[...] = (x_ref[0] * (1.0 + sc_ref[0]) + sh_ref[0]).astype(MXU_DTYPE)

    o_ref[0, 0] = (_dot(u_scr[...], w_ref[...]) + b_ref[...]).astype(o_ref.dtype)


def _mod_proj(x, shift, scale, w, b, slabs, tm, tn):
    bsz, seq, k = x.shape
    n = w.shape[1]
    n_slab = n // slabs
    tm = _tile(seq, tm)
    tn = _tile(n_slab, tn)
    per = n_slab // tn
    return pl.pallas_call(
        _mod_proj_kernel,
        out_shape=jax.ShapeDtypeStruct((slabs, bsz, seq, n_slab), F32),
        grid=(bsz, seq // tm, n // tn),
        in_specs=[
            pl.BlockSpec((1, tm, k), lambda bi, i, j: (bi, i, 0)),
            pl.BlockSpec((1, 1, k), lambda bi, i, j: (bi, 0, 0)),
            pl.BlockSpec((1, 1, k), lambda bi, i, j: (bi, 0, 0)),
            pl.BlockSpec((k, tn), lambda bi, i, j: (0, j)),
            pl.BlockSpec((1, tn), lambda bi, i, j: (0, j)),
        ],
        out_specs=pl.BlockSpec((1, 1, tm, tn), lambda bi, i, j: (j // per, bi, i, j % per)),
        scratch_shapes=[pltpu.VMEM((tm, k), MXU_DTYPE)],
        compiler_params=_params(("parallel", "parallel", "arbitrary")),
        name="mod_proj",
    )(x, shift, scale, w.astype(MXU_DTYPE), b.reshape(1, n))


def _conv3_kernel(p_ref, prev_ref, next_ref, w_ref, b_ref, x0_ref, v_ref):
    i = pl.program_id(1)
    last = pl.num_programs(1) - 1
    tl = p_ref.shape[2]
    row = lax.broadcasted_iota(jnp.int32, (tl, 1), 0)
    outs = []
    for s in range(3):
        cur = p_ref[s, 0]
        before = jnp.where(i == 0, 0.0, prev_ref[s, 0][SUBLANES - 1:SUBLANES, :])
        after = jnp.where(i == last, 0.0, next_ref[s, 0][0:1, :])
        dn = jnp.where(row == 0, before, pltpu.roll(cur, 1, 0))
        up = jnp.where(row == tl - 1, after, pltpu.roll(cur, tl - 1, 0))
        w = w_ref[s]
        outs.append(dn * w[0:1, :] + cur * w[1:2, :] + up * w[2:3, :] + b_ref[s])
    x0_ref[0] = outs[0]
    v_ref[0] = outs[2] * outs[1]


def _conv3_gate(proj, conv_w, conv_b):
    _, bsz, seq, d = proj.shape
    tl = _tile(seq, 512)
    td = _tile(d, 512)
    hb = tl // SUBLANES
    nhb = seq // SUBLANES
    w = conv_w.reshape(3, 3, d).transpose(1, 0, 2)
    b = conv_b.reshape(3, 1, d)
    out = jax.ShapeDtypeStruct((bsz, seq, d), F32)
    return pl.pallas_call(
        _conv3_kernel,
        out_shape=(out, out),
        grid=(bsz, seq // tl, d // td),
        in_specs=[
            pl.BlockSpec((3, 1, tl, td), lambda bi, i, j: (0, bi, i, j)),
            pl.BlockSpec((3, 1, SUBLANES, td), lambda bi, i, j: (0, bi, jnp.maximum(i * hb - 1, 0), j)),
            pl.BlockSpec((3, 1, SUBLANES, td), lambda bi, i, j: (0, bi, jnp.minimum((i + 1) * hb, nhb - 1), j)),
            pl.BlockSpec((3, 3, td), lambda bi, i, j: (0, 0, j)),
            pl.BlockSpec((3, 1, td), lambda bi, i, j: (0, 0, j)),
        ],
        out_specs=(
            pl.BlockSpec((1, tl, td), lambda bi, i, j: (bi, i, j)),
            pl.BlockSpec((1, tl, td), lambda bi, i, j: (bi, i, j)),
        ),
        compiler_params=_params(("parallel", "parallel", "parallel")),
        name="hyena_conv3_gate",
    )(proj, proj, proj, w, b)


def _filter_kernel(z_ref, dist_ref, w_in_ref, w_hid_ref, b_ref, fr_ref, w_out_ref, delta_ref, f_ref, norm_ref):
    hp = lax.Precision.HIGHEST
    b = b_ref[...]
    fr = fr_ref[...]
    g = jnp.sin(fr[0:1] * (jnp.dot(z_ref[...], w_in_ref[...], precision=hp, preferred_element_type=F32) + b[0:1]))
    g = jnp.sin(fr[1:2] * (jnp.dot(g, w_hid_ref[0], precision=hp, preferred_element_type=F32) + b[1:2]))
    g = jnp.sin(fr[2:3] * (jnp.dot(g, w_hid_ref[1], precision=hp, preferred_element_type=F32) + b[2:3]))
    filt = jnp.dot(g, w_out_ref[...], precision=hp, preferred_element_type=F32)
    filt = filt * jnp.exp(-dist_ref[...] * delta_ref[...])
    f_ref[...] = filt

    @pl.when(pl.program_id(0) == 0)
    def _():
        norm_ref[...] = jnp.zeros_like(norm_ref)

    norm_ref[...] += jnp.sum(jnp.abs(filt), axis=0, keepdims=True)


def _implicit_filter(seq, f_w_in, f_w_hid, f_b, f_freq, f_w_out):
    emb, width = f_w_in.shape
    bands_n = (emb - 1) // 2
    d = f_w_out.shape[1]
    emb_pad = -(-emb // LANES) * LANES
    filt_w = -(-width // LANES) * LANES
    pad_to = lambda a, shape: jnp.pad(a.astype(F32), [(0, s - n) for s, n in zip(shape, a.shape)])
    pos = jnp.arange(seq, dtype=F32)
    t = pos / (seq - 1)
    bands = jnp.linspace(1e-4, bands_n - 1, bands_n, dtype=F32)
    ang = (2.0 * math.pi / seq) * pos[:, None] * bands[None, :]
    z = pad_to(jnp.concatenate([t[:, None], jnp.cos(ang), -jnp.sin(ang)], axis=-1), (seq, emb_pad))
    w_in = pad_to(f_w_in, (emb_pad, filt_w))
    f_w_hid = pad_to(f_w_hid, (2, filt_w, filt_w))
    f_b = pad_to(f_b, (3, filt_w))
    f_freq = pad_to(f_freq, (3, filt_w))
    f_w_out = pad_to(f_w_out, (filt_w, d))
    dist = (jnp.abs(pos - seq // 2) / (seq // 2))[:, None]
    deltas = jnp.abs(jnp.linspace(HY_MIN_DECAY, HY_MAX_DECAY, d, dtype=F32))[None, :]
    tl = _tile(seq, 512)
    full = lambda shape: pl.BlockSpec(shape, lambda i: (0,) * len(shape))
    return pl.pallas_call(
        _filter_kernel,
        out_shape=(jax.ShapeDtypeStruct((seq, d), F32), jax.ShapeDtypeStruct((1, d), F32)),
        grid=(seq // tl,),
        in_specs=[
            pl.BlockSpec((tl, emb_pad), lambda i: (i, 0)),
            pl.BlockSpec((tl, 1), lambda i: (i, 0)),
            full((emb_pad, filt_w)),
            full((2, filt_w, filt_w)),
            full((3, filt_w)),
            full((3, filt_w)),
            full((filt_w, d)),
            full((1, d)),
        ],
        out_specs=(pl.BlockSpec((tl, d), lambda i: (i, 0)), full((1, d))),
        compiler_params=_params(("arbitrary",)),
        name="hyena_filter",
    )(z, dist, w_in, f_w_hid.astype(F32), f_b.astype(F32), f_freq.astype(F32), f_w_out.astype(F32), deltas)


def _real_form(mr, mi):
    top = jnp.concatenate([mr, -mi], axis=-1)
    bot = jnp.concatenate([mi, mr], axis=-1)
    return jnp.concatenate([top, bot], axis=-2)


def _unit_roots(idx, n):
    ang = (2.0 * math.pi / n) * (idx % n).astype(F32)
    return jnp.cos(ang), -jnp.sin(ang)


def _split_table(t):
    hi = t.astype(MXU_DTYPE)
    return hi, (t - hi.astype(F32)).astype(MXU_DTYPE)


def _dft_tables(seq):
    n = 2 * seq
    n2 = DFT_N2
    n1 = n // n2
    i32 = jnp.int32
    p = jnp.arange(n1, dtype=i32)
    c = jnp.arange(n2, dtype=i32)
    tr, ti = _unit_roots(c[:, None] * p[None, :], n)
    a_in = jnp.arange(n1 // 2, dtype=i32)
    fr, fi = _unit_roots(p[:, None] * a_in[None, :], n1)
    mr = tr[:, :, None] * fr[None] - ti[:, :, None] * fi[None]
    mi = tr[:, :, None] * fi[None] + ti[:, :, None] * fr[None]
    first = _real_form(mr, mi)
    a_out = jnp.arange(n1 // 4, n1 // 4 + n1 // 2, dtype=i32)
    gr, gi = _unit_roots(p[:, None] * a_out[None, :], n1)
    cr = (tr[:, :, None] * gr[None] - ti[:, :, None] * gi[None]) / n
    ci = -(tr[:, :, None] * gi[None] + ti[:, :, None] * gr[None]) / n
    last = _real_form(jnp.swapaxes(cr, 1, 2), jnp.swapaxes(ci, 1, 2))
    q = jnp.arange(n2, dtype=i32)
    hr, hi_ = _unit_roots(q[:, None] * c[None, :], n2)
    mid_f = _real_form(hr, hi_)
    mid_i = _real_form(hr, -hi_)
    return tuple(_split_table(t) for t in (first, mid_f, mid_i, last))


def _dft_first_kernel(x_ref, mh_ref, ml_ref, o_ref):
    o_ref[...] = _dot_split((mh_ref[0], ml_ref[0]), _split(x_ref[...]))


def _dft_first(x, table, d):
    n1, cols = x.shape
    bd = _tile(d, 2048)
    per = d // bd
    mh, ml = table
    return pl.pallas_call(
        _dft_first_kernel,
        out_shape=jax.ShapeDtypeStruct((2 * n1, cols), F32),
        grid=(cols // bd,),
        in_specs=[
            pl.BlockSpec((n1, bd), lambda j: (0, j)),
            pl.BlockSpec((1, 2 * n1, n1), lambda j: (j // per, 0, 0)),
            pl.BlockSpec((1, 2 * n1, n1), lambda j: (j // per, 0, 0)),
        ],
        out_specs=pl.BlockSpec((2 * n1, bd), lambda j: (0, j)),
        compiler_params=_params(("parallel",)),
        name="hyena_dft_first",
    )(x, mh, ml)


def _dft_mid_filter_kernel(a_ref, fh_ref, fl_ref, norm_ref, o_ref):
    n2 = a_ref.shape[2]
    a = a_ref[...].reshape(2 * n2, a_ref.shape[3])
    z = _dot_split((fh_ref[...], fl_ref[...]), _split(a))
    o_ref[...] = (z / norm_ref[...]).reshape(o_ref.shape)


def _dft_mid_conv_kernel(a_ref, h_ref, fh_ref, fl_ref, gh_ref, gl_ref, o_ref):
    n2 = a_ref.shape[2]
    d = a_ref.shape[3]
    a = a_ref[...].reshape(2 * n2, d)
    z = _dot_split((fh_ref[...], fl_ref[...]), _split(a))
    zr, zi = z[:n2], z[n2:]
    hr, hi_ = h_ref[0, 0], h_ref[1, 0]
    w = jnp.concatenate([zr * hr - zi * hi_, zr * hi_ + zi * hr], axis=0)
    y = _dot_split((gh_ref[...], gl_ref[...]), _split(w))
    o_ref[...] = y.reshape(o_ref.shape)


def _dft_mid(a, tables, d, *, spectrum=None, norm=None):
    n2 = DFT_N2
    n1 = a.shape[0] // 2
    a4 = a.reshape(2, n1, n2, d)
    bd = _tile(d, 2048)
    blk = pl.BlockSpec((2, 1, n2, bd), lambda p, j: (0, p, 0, j))
    mat = pl.BlockSpec((2 * n2, 2 * n2), lambda p, j: (0, 0))
    (fh, fl), (gh, gl) = tables
    if spectrum is None:
        kern, ins, specs = _dft_mid_filter_kernel, (a4, fh, fl, norm), [blk, mat, mat, pl.BlockSpec((1, bd), lambda p, j: (0, j))]
    else:
        kern, ins, specs = _dft_mid_conv_kernel, (a4, spectrum, fh, fl, gh, gl), [blk, blk, mat, mat, mat, mat]
    out = pl.pallas_call(
        kern,
        out_shape=jax.ShapeDtypeStruct((2, n1, n2, d), F32),
        grid=(n1, d // bd),
        in_specs=specs,
        out_specs=blk,
        compiler_params=_params(("parallel", "parallel")),
        name="hyena_dft_mid",
    )(*ins)
    return out


def _dft_last_kernel(x_ref, mh_ref, ml_ref, o_ref):
    o_ref[...] = _dot_split((mh_ref[0], ml_ref[0]), _split(x_ref[...]))


def _dft_last(b, table, d):
    rows, cols = b.shape
    n1 = rows // 2
    bd = _tile(d, 2048)
    per = d // bd
    mh, ml = table
    return pl.pallas_call(
        _dft_last_kernel,
        out_shape=jax.ShapeDtypeStruct((n1, cols), F32),
        grid=(cols // bd,),
        in_specs=[
            pl.BlockSpec((rows, bd), lambda j: (0, j)),
            pl.BlockSpec((1, n1, rows), lambda j: (j // per, 0, 0)),
            pl.BlockSpec((1, n1, rows), lambda j: (j // per, 0, 0)),
        ],
        out_specs=pl.BlockSpec((n1, bd), lambda j: (0, j)),
        compiler_params=_params(("parallel",)),
        name="hyena_dft_last",
    )(b, mh, ml)


def _long_conv_two_stage(v, filt, norm, tables):
    bsz, seq, d = v.shape
    n1 = 2 * seq // DFT_N2
    first, mid_f, mid_i, last = tables
    cols = DFT_N2 * d
    hx = jnp.concatenate([filt.reshape(n1 // 2, cols), jnp.zeros((n1 // 2, cols), F32)], axis=0)
    spectrum = _dft_mid(_dft_first(hx, first, d), (mid_f, mid_i), d, norm=norm)
    a = _dft_first(v.reshape(n1, cols), first, d)
    bmat = _dft_mid(a, (mid_f, mid_i), d, spectrum=spectrum)
    y = _dft_last(bmat.reshape(2 * n1, cols), last, d)
    return y.reshape(bsz, seq, d)


def _dft_small_kernel(v_ref, h_ref, norm_ref, fh_ref, fl_ref, gh_ref, gl_ref, o_ref):
    seq = v_ref.shape[1]
    n = 2 * seq
    fwd = (fh_ref[...], fl_ref[...])
    x = jnp.concatenate([v_ref[0], v_ref[1]], axis=0)
    z = _dot_split(fwd, _split(x))
    hx = jnp.concatenate([h_ref[...] / norm_ref[...], jnp.zeros_like(h_ref)], axis=0)
    hs = _dot_split(fwd, _split(hx))
    zr, zi, hr, hi_ = z[:n], z[n:], hs[:n], hs[n:]
    w = jnp.concatenate([zr * hr - zi * hi_, zr * hi_ + zi * hr], axis=0)
    y = _dot_split((gh_ref[...], gl_ref[...]), _split(w))
    o_ref[0] = y[:seq]
    o_ref[1] = y[seq:]


def _long_conv_small(v, filt, norm):
    bsz, seq, d = v.shape
    n = 2 * seq
    k = jnp.arange(n, dtype=jnp.int32)
    t_in = jnp.arange(seq, dtype=jnp.int32)
    fr, fi = _unit_roots(k[:, None] * t_in[None, :], n)
    fwd = _real_form(fr, fi)
    t_out = jnp.arange(seq // 2, seq // 2 + seq, dtype=jnp.int32)
    gr, gi = _unit_roots(t_out[:, None] * k[None, :], n)
    inv = _real_form(gr / n, -gi / n)
    (fh, fl), (gh, gl) = _split_table(fwd), _split_table(inv)
    bd = _tile(d, 512)
    full = lambda shape: pl.BlockSpec(shape, lambda j: (0,) * len(shape))
    return pl.pallas_call(
        _dft_small_kernel,
        out_shape=jax.ShapeDtypeStruct((bsz, seq, d), F32),
        grid=(d // bd,),
        in_specs=[
            pl.BlockSpec((bsz, seq, bd), lambda j: (0, 0, j)),
            pl.BlockSpec((seq, bd), lambda j: (0, j)),
            pl.BlockSpec((1, bd), lambda j: (0, j)),
            full(fwd.shape), full(fwd.shape), full(inv.shape), full(inv.shape),
        ],
        out_specs=pl.BlockSpec((bsz, seq, bd), lambda j: (0, 0, j)),
        compiler_params=_params(("parallel",)),
        name="hyena_dft_small",
    )(v, filt, norm, fh, fl, gh, gl)


def _hyena_out_kernel(alpha, x0_ref, cv_ref, v_ref, bias_ref, w_ref, b_ref, h_ref, g_ref, lg_ref, lb_ref, o_ref):
    y = x0_ref[0] * (cv_ref[0] + v_ref[0] * bias_ref[...])
    y = _dot(y.astype(MXU_DTYPE), w_ref[...]) + b_ref[...]
    o_ref[0] = _layer_norm_rows(alpha * h_ref[0] + g_ref[0] * y, lg_ref[...], lb_ref[...])


def _attn_out_kernel(alpha, o_in_ref, w_ref, h_ref, g_ref, lg_ref, lb_ref, o_ref):
    y = _dot(o_in_ref[0], w_ref[...])
    o_ref[0] = _layer_norm_rows(alpha * h_ref[0] + g_ref[0] * y, lg_ref[...], lb_ref[...])


def _row_spec(tm, d):
    return pl.BlockSpec((1, tm, d), lambda bi, i: (bi, i, 0))


def _vec_spec(d):
    return pl.BlockSpec((1, d), lambda bi, i: (0, 0))


def _bvec_spec(d):
    return pl.BlockSpec((1, 1, d), lambda bi, i: (bi, 0, 0))


def _const_spec(shape):
    return pl.BlockSpec(shape, lambda bi, i: (0,) * len(shape), pipeline_mode=pl.Buffered(1))


def _hyena_out(alpha, x0, cv, v, bias, w, b, h, gate, ln_g, ln_b):
    bsz, seq, d = h.shape
    tm = _tile(seq, 256)
    return pl.pallas_call(
        functools.partial(_hyena_out_kernel, alpha),
        out_shape=jax.ShapeDtypeStruct(h.shape, F32),
        grid=(bsz, seq // tm),
        in_specs=[_row_spec(tm, d), _row_spec(tm, d), _row_spec(tm, d), _vec_spec(d), _const_spec((d, d)),
                  _vec_spec(d), _row_spec(tm, d), _bvec_spec(d), _vec_spec(d), _vec_spec(d)],
        out_specs=_row_spec(tm, d),
        compiler_params=_params(("parallel", "parallel")),
        name="hyena_out_norm",
    )(x0, cv, v, bias.reshape(1, d), w.astype(MXU_DTYPE), b.reshape(1, d), h, gate, ln_g.reshape(1, d), ln_b.reshape(1, d))


def _attn_out(alpha, o, w, h, gate, ln_g, ln_b):
    bsz, seq, d = h.shape
    k = o.shape[2]
    tm = _tile(seq, 512)
    return pl.pallas_call(
        functools.partial(_attn_out_kernel, alpha),
        out_shape=jax.ShapeDtypeStruct(h.shape, F32),
        grid=(bsz, seq // tm),
        in_specs=[_row_spec(tm, k), _const_spec((k, d)), _row_spec(tm, d), _bvec_spec(d), _vec_spec(d), _vec_spec(d)],
        out_specs=_row_spec(tm, d),
        compiler_params=_params(("parallel", "parallel")),
        name="attn_out_norm",
    )(o, w.astype(MXU_DTYPE), h, gate, ln_g.reshape(1, d), ln_b.reshape(1, d))


def _ffn_kernel(alpha, h_ref, sh_ref, sc_ref, g_ref, wg_ref, wu_ref, wd_ref, lg_ref, lb_ref, o_ref, u_scr, acc_scr):
    f = pl.program_id(2)

    @pl.when(f == 0)
    def _():
        u_scr[...] = (h_ref[0] * (1.0 + sc_ref[0]) + sh_ref[0]).astype(MXU_DTYPE)
        acc_scr[...] = jnp.zeros_like(acc_scr)

    u = u_scr[...]
    gate = _dot(u, wg_ref[...])
    up = _dot(u, wu_ref[...])
    act = (_silu(gate) * up).astype(MXU_DTYPE)
    acc_scr[...] += _dot(act, wd_ref[...])

    @pl.when(f == pl.num_programs(2) - 1)
    def _():
        o_ref[0] = _layer_norm_rows(alpha * h_ref[0] + g_ref[0] * acc_scr[...], lg_ref[...], lb_ref[...])


def _ffn(alpha, h, shift, scale, gate, w_gate, w_up, w_down, ln_g, ln_b):
    bsz, seq, d = h.shape
    ff = w_gate.shape[1]
    tm = _tile(seq, 512)
    tf = _tile(ff, 512)
    return pl.pallas_call(
        functools.partial(_ffn_kernel, alpha),
        out_shape=jax.ShapeDtypeStruct(h.shape, F32),
        grid=(bsz, seq // tm, ff // tf),
        in_specs=[
            pl.BlockSpec((1, tm, d), lambda bi, i, f: (bi, i, 0)),
            pl.BlockSpec((1, 1, d), lambda bi, i, f: (bi, 0, 0)),
            pl.BlockSpec((1, 1, d), lambda bi, i, f: (bi, 0, 0)),
            pl.BlockSpec((1, 1, d), lambda bi, i, f: (bi, 0, 0)),
            pl.BlockSpec((d, tf), lambda bi, i, f: (0, f)),
            pl.BlockSpec((d, tf), lambda bi, i, f: (0, f)),
            pl.BlockSpec((tf, d), lambda bi, i, f: (f, 0)),
            pl.BlockSpec((1, d), lambda bi, i, f: (0, 0)),
            pl.BlockSpec((1, d), lambda bi, i, f: (0, 0)),
        ],
        out_specs=pl.BlockSpec((1, tm, d), lambda bi, i, f: (bi, i, 0)),
        scratch_shapes=[pltpu.VMEM((tm, d), MXU_DTYPE), pltpu.VMEM((tm, d), F32)],
        compiler_params=_params(("parallel", "parallel", "arbitrary")),
        name="ffn_swiglu_norm",
    )(h, shift, scale, gate, w_gate.astype(MXU_DTYPE), w_up.astype(MXU_DTYPE), w_down.astype(MXU_DTYPE),
      ln_g.reshape(1, d), ln_b.reshape(1, d))


def _rotate_half_cols(w):
    ws = w.reshape(w.shape[:-1] + (2, 2, ROPE_PAIRS))
    return jnp.stack([-ws[..., 1, :], ws[..., 0, :]], axis=-2).reshape(w.shape)


def _rope_table(seq):
    rows = seq // GRID_W
    row = jnp.repeat(jnp.arange(rows, dtype=F32), GRID_W)
    col = jnp.tile(jnp.arange(GRID_W, dtype=F32), rows)
    inv = ROPE_THETA ** (-jnp.arange(ROPE_PAIRS, dtype=F32) / ROPE_PAIRS)
    ang = jnp.stack([row[:, None] * inv, col[:, None] * inv], axis=1)
    ang = jnp.broadcast_to(ang[:, :, None, :], (seq, 2, 2, ROPE_PAIRS)).reshape(seq, QK_ROPE)
    return jnp.concatenate([jnp.cos(ang), jnp.sin(ang)], axis=-1)


def _rms_rows(x, g):
    return x * lax.rsqrt(jnp.mean(x * x, axis=-1, keepdims=True) + RMS_EPS) * g


def _rope_pair(x, cs):
    t = x * cs
    return t + pltpu.roll(t, QK_ROPE, 1)


def _mla_q_kernel(cq_ref, g_ref, w_ref, cs_ref, q_ref, xn_scr):
    @pl.when(pl.program_id(2) == 0)
    def _():
        xn_scr[...] = _rms_rows(cq_ref[0], g_ref[...]).astype(MXU_DTYPE)

    a = _dot(xn_scr[...], w_ref[0])
    r = _rope_pair(a[:, QK_NOPE:], cs_ref[...])
    q_ref[0, 0] = (jnp.concatenate([a[:, :QK_NOPE], r], axis=1) * QUERY_SCALE).astype(q_ref.dtype)


def _mla_queries(t, q_norm, wq_b, cs, heads):
    bsz, seq, _ = t.shape
    rank = q_norm.shape[0]
    tm = _tile(seq, 512)
    w = wq_b.reshape(rank, heads, QK_NOPE + QK_ROPE)
    w = jnp.concatenate([w, _rotate_half_cols(w[..., QK_NOPE:])], axis=-1).transpose(1, 0, 2).astype(MXU_DTYPE)
    return pl.pallas_call(
        _mla_q_kernel,
        out_shape=jax.ShapeDtypeStruct((bsz, heads, seq, 2 * LANES), MXU_DTYPE),
        grid=(bsz, seq // tm, heads),
        in_specs=[
            pl.BlockSpec((1, tm, rank), lambda bi, i, h: (bi, i, 0)),
            pl.BlockSpec((1, rank), lambda bi, i, h: (0, 0)),
            pl.BlockSpec((1, rank, 2 * LANES), lambda bi, i, h: (h, 0, 0)),
            pl.BlockSpec((tm, LANES), lambda bi, i, h: (i, 0)),
        ],
        out_specs=pl.BlockSpec((1, 1, tm, 2 * LANES), lambda bi, i, h: (bi, h, i, 0)),
        scratch_shapes=[pltpu.VMEM((tm, rank), MXU_DTYPE)],
        compiler_params=_params(("parallel", "parallel", "arbitrary")),
        name="mla_queries",
    )(t, q_norm.reshape(1, rank), w, cs)


def _mla_kv_kernel(use_rope, ckv_ref, kr_ref, g_ref, w_ref, cs_ref, k_ref, vt_ref, xn_scr):
    @pl.when(pl.program_id(2) == 0)
    def _():
        xn_scr[...] = _rms_rows(ckv_ref[0], g_ref[...]).astype(MXU_DTYPE)

    kv = _dot(xn_scr[...], w_ref[0])
    kr = kr_ref[0]
    if use_rope:
        kr = _rope_pair(kr, cs_ref[...])
    lane = lax.broadcasted_iota(jnp.int32, kr.shape, 1)
    kr = jnp.where(lane < QK_ROPE, kr, 0.0)
    k_ref[0, 0] = jnp.concatenate([kv[:, :QK_NOPE], kr], axis=1).astype(k_ref.dtype)
    ones = jnp.ones((V_ROWS - V_HEAD, kv.shape[0]), F32)
    vt_ref[0, 0] = jnp.concatenate([kv[:, QK_NOPE:].T, ones], axis=0).astype(vt_ref.dtype)


def _mla_keys_values(t, kv_norm, wkv_b, cs, heads, use_rope):
    bsz, seq, _ = t.shape
    rank = kv_norm.shape[0]
    tm = _tile(seq, 512)
    w = wkv_b.reshape(rank, heads, QK_NOPE + V_HEAD).transpose(1, 0, 2).astype(MXU_DTYPE)
    return pl.pallas_call(
        functools.partial(_mla_kv_kernel, use_rope),
        out_shape=(jax.ShapeDtypeStruct((bsz, heads, seq, 2 * LANES), MXU_DTYPE),
                   jax.ShapeDtypeStruct((bsz, heads, V_ROWS, seq), MXU_DTYPE)),
        grid=(bsz, seq // tm, heads),
        in_specs=[
            pl.BlockSpec((1, tm, rank), lambda bi, i, h: (bi, i, 1)),
            pl.BlockSpec((1, tm, LANES), lambda bi, i, h: (bi, i, 2 * rank // LANES)),
            pl.BlockSpec((1, rank), lambda bi, i, h: (0, 0)),
            pl.BlockSpec((1, rank, QK_NOPE + V_HEAD), lambda bi, i, h: (h, 0, 0)),
            pl.BlockSpec((tm, LANES), lambda bi, i, h: (i, 0)),
        ],
        out_specs=(pl.BlockSpec((1, 1, tm, 2 * LANES), lambda bi, i, h: (bi, h, i, 0)),
                   pl.BlockSpec((1, 1, V_ROWS, tm), lambda bi, i, h: (bi, h, 0, i))),
        scratch_shapes=[pltpu.VMEM((tm, rank), MXU_DTYPE)],
        compiler_params=_params(("parallel", "parallel", "arbitrary")),
        name="mla_keys_values",
    )(t, t, kv_norm.reshape(1, rank), w, cs)


def _attn_kernel(q_ref, k_ref, vt_ref, kc_ref, vtc_ref, o_ref, qt_scr, m_scr, acc_scr):
    kv = pl.program_id(3)
    n_main = pl.num_programs(3) - 1

    @pl.when(kv == 0)
    def _():
        qt_scr[...] = q_ref[0, 0].T
        m_scr[...] = jnp.full_like(m_scr, -jnp.inf)
        acc_scr[...] = jnp.zeros_like(acc_scr)

    def step(k, vt):
        tq = qt_scr.shape[1]
        cw = min(tq, ATTN_CHUNK)
        cols = [pl.ds(c * cw, cw) for c in range(tq // cw)]
        s_next = _dot(k, qt_scr[:, cols[0]])
        for c, col in enumerate(cols):
            s = s_next
            if c + 1 < len(cols):
                s_next = _dot(k, qt_scr[:, cols[c + 1]])
            m_prev = m_scr[:, col]
            m_new = jnp.maximum(m_prev, jnp.max(s, axis=0, keepdims=True))
            a = jnp.exp2(m_prev - m_new)
            p = jnp.exp2((s - m_new).astype(MXU_DTYPE))
            acc_scr[:, col] = a * acc_scr[:, col] + _dot(vt, p)
            m_scr[:, col] = m_new

    @pl.when(kv < n_main)
    def _():
        step(k_ref[0, 0], vt_ref[0, 0])

    @pl.when(kv == n_main)
    def _():
        step(kc_ref[0, 0], vtc_ref[0, 0])
        acc = acc_scr[...]
        o_ref[0] = (acc[:V_HEAD] / acc[V_HEAD:V_HEAD + 1]).T.astype(o_ref.dtype)


def _mla_attend(q, k, vt, kc, vtc):
    bsz, heads, seq, dq = q.shape
    lc = kc.shape[2]
    tq = _tile(seq, 1024)
    tk = _tile(seq, 1024)
    n_main = seq // tk
    return pl.pallas_call(
        _attn_kernel,
        out_shape=jax.ShapeDtypeStruct((bsz, seq, heads * V_HEAD), MXU_DTYPE),
        grid=(bsz, heads, seq // tq, n_main + 1),
        in_specs=[
            pl.BlockSpec((1, 1, tq, dq), lambda bi, h, i, j: (bi, h, i, 0)),
            pl.BlockSpec((1, 1, tk, dq), lambda bi, h, i, j: (bi, h, jnp.minimum(j, n_main - 1), 0)),
            pl.BlockSpec((1, 1, V_ROWS, tk), lambda bi, h, i, j: (bi, h, 0, jnp.minimum(j, n_main - 1))),
            pl.BlockSpec((1, 1, lc, dq), lambda bi, h, i, j: (bi, h, 0, 0)),
            pl.BlockSpec((1, 1, V_ROWS, lc), lambda bi, h, i, j: (bi, h, 0, 0)),
        ],
        out_specs=pl.BlockSpec((1, tq, V_HEAD), lambda bi, h, i, j: (bi, i, h)),
        scratch_shapes=[pltpu.VMEM((dq, tq), MXU_DTYPE), pltpu.VMEM((1, tq), F32), pltpu.VMEM((V_ROWS, tq), F32)],
        compiler_params=_params(("parallel", "parallel", "parallel", "arbitrary")),
        name="mla_attention",
    )(q, k, vt, kc, vtc)


def _pool_kernel(alpha, seq, h_ref, prev_ref, next_ref, sh_ref, sc_ref, g_ref, w_ref, ps_ref, lg_ref, lb_ref, o_ref):
    i = pl.program_id(1)
    last = pl.num_programs(1) - 1
    tm = h_ref.shape[1]
    d = h_ref.shape[2]
    groups = len(POOL_WINDOWS)
    ch = d // groups
    ext = tm + 2 * POOL_HALO
    shift, scale = sh_ref[0], sc_ref[0]
    h = h_ref[0]
    u = h * (1.0 + scale) + shift
    u_prev = jnp.where(i == 0, 0.0, prev_ref[0] * (1.0 + scale) + shift)
    u_next = jnp.where(i == last, 0.0, next_ref[0] * (1.0 + scale) + shift)
    e = jnp.concatenate([u_prev, u, u_next], axis=0)
    t = i * tm + lax.broadcasted_iota(jnp.int32, (tm, 1), 0)
    ys = []
    for g, win in enumerate(POOL_WINDOWS):
        a = e[:, g * ch:(g + 1) * ch]
        span = 1
        while span < win:
            a = a + pltpu.roll(a, ext - span, 0)
            span *= 2
        half = win // 2
        a = pltpu.roll(a, half, 0)
        s = a[POOL_HALO:POOL_HALO + tm]
        cnt = (jnp.minimum(t + half, seq) - jnp.maximum(t - half, 0)).astype(F32)
        dg = s / cnt - u[:, g * ch:(g + 1) * ch]
        ys.append(_dot(dg.astype(MXU_DTYPE), w_ref[g]))
    y = jnp.concatenate(ys, axis=1) * ps_ref[...]
    o_ref[0] = _layer_norm_rows(alpha * h + g_ref[0] * y, lg_ref[...], lb_ref[...])


def _pool_mixer(alpha, h, shift, scale, gate, w_grp, pool_scale, ln_g, ln_b):
    bsz, seq, d = h.shape
    groups, ch, _ = w_grp.shape
    tm = _tile(seq, 512)
    hb = tm // POOL_HALO
    nhb = seq // POOL_HALO
    return pl.pallas_call(
        functools.partial(_pool_kernel, alpha, seq),
        out_shape=jax.ShapeDtypeStruct(h.shape, F32),
        grid=(bsz, seq // tm),
        in_specs=[
            _row_spec(tm, d),
            pl.BlockSpec((1, POOL_HALO, d), lambda bi, i: (bi, jnp.maximum(i * hb - 1, 0), 0)),
            pl.BlockSpec((1, POOL_HALO, d), lambda bi, i: (bi, jnp.minimum((i + 1) * hb, nhb - 1), 0)),
            _bvec_spec(d), _bvec_spec(d), _bvec_spec(d),
            pl.BlockSpec((groups, ch, ch), lambda bi, i: (0, 0, 0)),
            _vec_spec(d), _vec_spec(d), _vec_spec(d),
        ],
        out_specs=_row_spec(tm, d),
        compiler_params=_params(("parallel", "parallel")),
        name="pool_mixer_norm",
    )(h, h, h, shift, scale, gate, w_grp.astype(MXU_DTYPE), pool_scale.reshape(1, d), ln_g.reshape(1, d), ln_b.reshape(1, d))


def _hyena_mixer_norm(alpha, h, mod, hy, filt, norm, tables, ln_g, ln_b):
    shift, scale, gate = mod
    w_in, b_in, conv_w, conv_b, bias, w_out, b_out = hy
    proj = _mod_proj(h, shift, scale, w_in, b_in, slabs=3, tm=512, tn=1024)
    x0, v = _conv3_gate(proj, conv_w, conv_b)
    if h.shape[1] <= DFT_SMALL_MAX_L:
        cv = _long_conv_small(v, filt, norm)
    else:
        cv = _long_conv_two_stage(v, filt, norm, tables)
    return _hyena_out(alpha, x0, cv, v, bias, w_out, b_out, h, gate, ln_g, ln_b)


def kernel(x, c, ctx, c_ctx, ada_w, ada_b, ln_g, ln_b, ffn_w_gate, ffn_w_up, ffn_w_down, hy_w_in, hy_b_in, hy_conv_w, hy_conv_b, hy_f_w_in, hy_f_w_hid, hy_f_b, hy_f_freq, hy_f_w_out, hy_bias, hy_w_out, hy_b_out, mla_w_in, mla_q_norm, mla_kv_norm, mla_wq_b, mla_wkv_b, mla_w_out, pool_w, pool_scale):
    bsz, seq, d = x.shape
    depth = ada_w.shape[0]
    assert bsz == 2, "the long convolution packs exactly two batch rows into one complex signal"
    assert ctx.shape[0] == bsz and seq % GRID_W == 0
    alpha = (2.0 * depth) ** 0.25
    heads = d // V_HEAD
    mla_layers = [i for i in range(depth) if i % N_MIXERS == 1]
    last_ctx_read = mla_layers[-1] if mla_layers else -1

    cond = jnp.concatenate([c, jnp.broadcast_to(c_ctx[None], (SUBLANES - bsz, d))], axis=0)
    mods = _ada_mods(cond, ada_w, ada_b)

    def mod_vecs(i, ctx_stream):
        m = jnp.broadcast_to(mods[i, bsz][None], (bsz, 6 * d)) if ctx_stream else mods[i, :bsz]
        return [m[:, None, k * d:(k + 1) * d] for k in range(6)]

    cs = _rope_table(seq)
    tables = _dft_tables(seq) if seq > DFT_SMALL_MAX_L else None

    h, hc = x, ctx
    for i in range(depth):
        kind, j = i % N_MIXERS, i // N_MIXERS
        ctx_update = i < last_ctx_read
        sh1, sc1, g1, sh2, sc2, g2 = mod_vecs(i, False)
        if kind == 1 or ctx_update:
            csh1, csc1, cg1, csh2, csc2, cg2 = mod_vecs(i, True)
        lg, lb = ln_g[i], ln_b[i]
        if kind == 0:
            hy = (hy_w_in[j], hy_b_in[j], hy_conv_w[j], hy_conv_b[j], hy_bias[j], hy_w_out[j], hy_b_out[j])
            fp = (hy_f_w_in[j], hy_f_w_hid[j], hy_f_b[j], hy_f_freq[j], hy_f_w_out[j])
            filt, norm = _implicit_filter(seq, *fp)
            h_mid = _hyena_mixer_norm(alpha, h, (sh1, sc1, g1), hy, filt, norm, tables, lg[0], lb[0])
            if ctx_update:
                filt_c, norm_c = _implicit_filter(hc.shape[1], *fp)
                hc_mid = _hyena_mixer_norm(alpha, hc, (csh1, csc1, cg1), hy, filt_c, norm_c, None, lg[0], lb[0])
        elif kind == 1:
            assert not ctx_update, "context queries are only needed when a later layer reads the context"
            rank = mla_q_norm.shape[1]
            w_in = mla_w_in[j]
            w_in = jnp.concatenate([w_in, _rotate_half_cols(w_in[:, 2 * rank:])], axis=1)
            zeros = jnp.zeros((w_in.shape[1],), F32)
            t = _mod_proj(h, sh1, sc1, w_in, zeros, slabs=1, tm=512, tn=w_in.shape[1])[0]
            tc = _mod_proj(hc, csh1, csc1, w_in, zeros, slabs=1, tm=512, tn=w_in.shape[1])[0]
            q = _mla_queries(t, mla_q_norm[j], mla_wq_b[j], cs, heads)
            k, v = _mla_keys_values(t, mla_kv_norm[j], mla_wkv_b[j], cs, heads, True)
            kc, vc = _mla_keys_values(tc, mla_kv_norm[j], mla_wkv_b[j], cs, heads, False)
            o = _mla_attend(q, k, v, kc, vc)
            h_mid = _attn_out(alpha, o, mla_w_out[j], h, g1, lg[0], lb[0])
        else:
            h_mid = _pool_mixer(alpha, h, sh1, sc1, g1, pool_w[j], pool_scale[j], lg[0], lb[0])
            if ctx_update:
                hc_mid = _pool_mixer(alpha, hc, csh1, csc1, cg1, pool_w[j], pool_scale[j], lg[0], lb[0])
        ffn = (ffn_w_gate[i], ffn_w_up[i], ffn_w_down[i])
        h = _ffn(alpha, h_mid, sh2, sc2, g2, *ffn, lg[1], lb[1])
        if ctx_update:
            hc = _ffn(alpha, hc_mid, csh2, csc2, cg2, *ffn, lg[1], lb[1])
    return h
```

```python
import functools
import math

import jax
import jax.numpy as jnp
from jax import lax
from jax.experimental import pallas as pl
from jax.experimental.pallas import tpu as pltpu

F32 = jnp.float32
MXU_DTYPE = jnp.bfloat16

V7X_VMEM_BYTES = 64 * 1024 * 1024
VMEM_LIMIT = V7X_VMEM_BYTES - 8 * 1024 * 1024
LANES = 128
SUBLANES = 8

N_MIXERS = 3
GRID_W = 64
LN_EPS = 1e-6
RMS_EPS = 1e-6
HY_TARGET = 1e-2
HY_FAST = 0.3
HY_SLOW = 1.5
HY_MIN_DECAY = math.log(HY_TARGET) / HY_SLOW
HY_MAX_DECAY = math.log(HY_TARGET) / HY_FAST
QK_NOPE = 128
QK_ROPE = 64
V_HEAD = 128
ROPE_PAIRS = QK_ROPE // 4
ROPE_THETA = 10000.0
ATTN_SCALE = (QK_NOPE + QK_ROPE) ** -0.5
QUERY_SCALE = ATTN_SCALE * math.log2(math.e)
ATTN_CHUNK = 512
V_ROWS = V_HEAD + 16
POOL_WINDOWS = (2, 4, 8, 16)
POOL_HALO = 8

DFT_N2 = 128
DFT_SMALL_MAX_L = 512


def _params(sem, vmem=VMEM_LIMIT):
    return pltpu.CompilerParams(dimension_semantics=sem, vmem_limit_bytes=vmem)


def _tile(n, t):
    t = min(n, t)
    assert n % t == 0, (n, t)
    return t


def _dot(a, b):
    return jnp.dot(a, b, preferred_element_type=F32)


def _layer_norm_rows(x, g, b):
    mu = jnp.mean(x, axis=-1, keepdims=True)
    xc = x - mu
    var = jnp.mean(xc * xc, axis=-1, keepdims=True)
    return xc * lax.rsqrt(var + LN_EPS) * g + b


def _silu(x):
    return x * (1.0 / (1.0 + jnp.exp(-x)))


def _ada_kernel(c_ref, w_ref, b_ref, o_ref):
    c = c_ref[...]
    a = _silu(c).astype(MXU_DTYPE)
    o_ref[0] = _dot(a, w_ref[0].astype(MXU_DTYPE)) + b_ref[0]


def _ada_mods(cond, ada_w, ada_b):
    depth, d, n = ada_w.shape
    rows = cond.shape[0]
    tn = _tile(n, 1024)
    return pl.pallas_call(
        _ada_kernel,
        out_shape=jax.ShapeDtypeStruct((depth, rows, n), F32),
        grid=(depth, n // tn),
        in_specs=[
            pl.BlockSpec((rows, d), lambda i, j: (0, 0)),
            pl.BlockSpec((1, d, tn), lambda i, j: (i, 0, j)),
            pl.BlockSpec((1, 1, tn), lambda i, j: (i, 0, j)),
        ],
        out_specs=pl.BlockSpec((1, rows, tn), lambda i, j: (i, 0, j)),
        compiler_params=_params(("parallel", "parallel")),
        name="ada_mods",
    )(cond, ada_w, ada_b.reshape(depth, 1, n))


def _mod_proj_kernel(x_ref, sh_ref, sc_ref, w_ref, b_ref, o_ref, u_scr):
    @pl.when(pl.program_id(2) == 0)
    def _():
        u_scr[...] = (x_ref[0] * (1.0 + sc_ref[0]) + sh_ref[0]).astype(MXU_DTYPE)

    o_ref[0, 0] = (_dot(u_scr[...], w_ref[...]) + b_ref[...]).astype(o_ref.dtype)


def _mod_proj(x, shift, scale, w, b, slabs, tm, tn):
    bsz, seq, k = x.shape
    n = w.shape[1]
    n_slab = n // slabs
    tm = _tile(seq, tm)
    tn = _tile(n_slab, tn)
    per = n_slab // tn
    return pl.pallas_call(
        _mod_proj_kernel,
        out_shape=jax.ShapeDtypeStruct((slabs, bsz, seq, n_slab), F32),
        grid=(bsz, seq // tm, n // tn),
        in_specs=[
            pl.BlockSpec((1, tm, k), lambda bi, i, j: (bi, i, 0)),
            pl.BlockSpec((1, 1, k), lambda bi, i, j: (bi, 0, 0)),
            pl.BlockSpec((1, 1, k), lambda bi, i, j: (bi, 0, 0)),
            pl.BlockSpec((k, tn), lambda bi, i, j: (0, j)),
            pl.BlockSpec((1, tn), lambda bi, i, j: (0, j)),
        ],
        out_specs=pl.BlockSpec((1, 1, tm, tn), lambda bi, i, j: (j // per, bi, i, j % per)),
        scratch_shapes=[pltpu.VMEM((tm, k), MXU_DTYPE)],
        compiler_params=_params(("parallel", "parallel", "arbitrary")),
        name="mod_proj",
    )(x, shift, scale, w.astype(MXU_DTYPE), b.reshape(1, n))


def _conv3_kernel(p_ref, prev_ref, next_ref, w_ref, b_ref, x0_ref, v_ref):
    i = pl.program_id(1)
    last = pl.num_programs(1) - 1
    tl = p_ref.shape[2]
    row = lax.broadcasted_iota(jnp.int32, (tl, 1), 0)
    outs = []
    for s in range(3):
        cur = p_ref[s, 0]
        before = jnp.where(i == 0, 0.0, prev_ref[s, 0][SUBLANES - 1:SUBLANES, :])
        after = jnp.where(i == last, 0.0, next_ref[s, 0][0:1, :])
        dn = jnp.where(row == 0, before, pltpu.roll(cur, 1, 0))
        up = jnp.where(row == tl - 1, after, pltpu.roll(cur, tl - 1, 0))
        w = w_ref[s]
        outs.append(dn * w[0:1, :] + cur * w[1:2, :] + up * w[2:3, :] + b_ref[s])
    x0_ref[0] = outs[0]
    v_ref[0] = outs[2] * outs[1]


def _conv3_gate(proj, conv_w, conv_b):
    _, bsz, seq, d = proj.shape
    tl = _tile(seq, 512)
    td = _tile(d, 512)
    hb = tl // SUBLANES
    nhb = seq // SUBLANES
    w = conv_w.reshape(3, 3, d).transpose(1, 0, 2)
    b = conv_b.reshape(3, 1, d)
    out = jax.ShapeDtypeStruct((bsz, seq, d), F32)
    return pl.pallas_call(
        _conv3_kernel,
        out_shape=(out, out),
        grid=(bsz, seq // tl, d // td),
        in_specs=[
            pl.BlockSpec((3, 1, tl, td), lambda bi, i, j: (0, bi, i, j)),
            pl.BlockSpec((3, 1, SUBLANES, td), lambda bi, i, j: (0, bi, jnp.maximum(i * hb - 1, 0), j)),
            pl.BlockSpec((3, 1, SUBLANES, td), lambda bi, i, j: (0, bi, jnp.minimum((i + 1) * hb, nhb - 1), j)),
            pl.BlockSpec((3, 3, td), lambda bi, i, j: (0, 0, j)),
            pl.BlockSpec((3, 1, td), lambda bi, i, j: (0, 0, j)),
        ],
        out_specs=(
            pl.BlockSpec((1, tl, td), lambda bi, i, j: (bi, i, j)),
            pl.BlockSpec((1, tl, td), lambda bi, i, j: (bi, i, j)),
        ),
        compiler_params=_params(("parallel", "parallel", "parallel")),
        name="hyena_conv3_gate",
    )(proj, proj, proj, w, b)


def _filter_kernel(z_ref, dist_ref, w_in_ref, w_hid_ref, b_ref, fr_ref, w_out_ref, delta_ref, f_ref, norm_ref):
    hp = lax.Precision.HIGHEST
    b = b_ref[...]
    fr = fr_ref[...]
    g = jnp.sin(fr[0:1] * (jnp.dot(z_ref[...], w_in_ref[...], precision=hp, preferred_element_type=F32) + b[0:1]))
    g = jnp.sin(fr[1:2] * (jnp.dot(g, w_hid_ref[0], precision=hp, preferred_element_type=F32) + b[1:2]))
    g = jnp.sin(fr[2:3] * (jnp.dot(g, w_hid_ref[1], precision=hp, preferred_element_type=F32) + b[2:3]))
    filt = jnp.dot(g, w_out_ref[...], precision=hp, preferred_element_type=F32)
    filt = filt * jnp.exp(-dist_ref[...] * delta_ref[...])
    f_ref[...] = filt

    @pl.when(pl.program_id(0) == 0)
    def _():
        norm_ref[...] = jnp.zeros_like(norm_ref)

    norm_ref[...] += jnp.sum(jnp.abs(filt), axis=0, keepdims=True)


def _implicit_filter(seq, f_w_in, f_w_hid, f_b, f_freq, f_w_out):
    emb, width = f_w_in.shape
    bands_n = (emb - 1) // 2
    d = f_w_out.shape[1]
    emb_pad = -(-emb // LANES) * LANES
    filt_w = -(-width // LANES) * LANES
    pad_to = lambda a, shape: jnp.pad(a.astype(F32), [(0, s - n) for s, n in zip(shape, a.shape)])
    pos = jnp.arange(seq, dtype=F32)
    t = pos / (seq - 1)
    bands = jnp.linspace(1e-4, bands_n - 1, bands_n, dtype=F32)
    ang = (2.0 * math.pi / seq) * pos[:, None] * bands[None, :]
    z = pad_to(jnp.concatenate([t[:, None], jnp.cos(ang), -jnp.sin(ang)], axis=-1), (seq, emb_pad))
    w_in = pad_to(f_w_in, (emb_pad, filt_w))
    f_w_hid = pad_to(f_w_hid, (2, filt_w, filt_w))
    f_b = pad_to(f_b, (3, filt_w))
    f_freq = pad_to(f_freq, (3, filt_w))
    f_w_out = pad_to(f_w_out, (filt_w, d))
    dist = (jnp.abs(pos - seq // 2) / (seq // 2))[:, None]
    deltas = jnp.abs(jnp.linspace(HY_MIN_DECAY, HY_MAX_DECAY, d, dtype=F32))[None, :]
    tl = _tile(seq, 512)
    full = lambda shape: pl.BlockSpec(shape, lambda i: (0,) * len(shape))
    return pl.pallas_call(
        _filter_kernel,
        out_shape=(jax.ShapeDtypeStruct((seq, d), F32), jax.ShapeDtypeStruct((1, d), F32)),
        grid=(seq // tl,),
        in_specs=[
            pl.BlockSpec((tl, emb_pad), lambda i: (i, 0)),
            pl.BlockSpec((tl, 1), lambda i: (i, 0)),
            full((emb_pad, filt_w)),
            full((2, filt_w, filt_w)),
            full((3, filt_w)),
            full((3, filt_w)),
            full((filt_w, d)),
            full((1, d)),
        ],
        out_specs=(pl.BlockSpec((tl, d), lambda i: (i, 0)), full((1, d))),
        compiler_params=_params(("arbitrary",)),
        name="hyena_filter",
    )(z, dist, w_in, f_w_hid.astype(F32), f_b.astype(F32), f_freq.astype(F32), f_w_out.astype(F32), deltas)


def _real_form(mr, mi):
    top = jnp.concatenate([mr, -mi], axis=-1)
    bot = jnp.concatenate([mi, mr], axis=-1)
    return jnp.concatenate([top, bot], axis=-2)


def _unit_roots(idx, n):
    ang = (2.0 * math.pi / n) * (idx % n).astype(F32)
    return jnp.cos(ang), -jnp.sin(ang)


def _dft_tables(seq):
    n = 2 * seq
    n2 = DFT_N2
    n1 = n // n2
    i32 = jnp.int32
    p = jnp.arange(n1, dtype=i32)
    c = jnp.arange(n2, dtype=i32)
    tr, ti = _unit_roots(c[:, None] * p[None, :], n)
    a_in = jnp.arange(n1 // 2, dtype=i32)
    fr, fi = _unit_roots(p[:, None] * a_in[None, :], n1)
    mr = tr[:, :, None] * fr[None] - ti[:, :, None] * fi[None]
    mi = tr[:, :, None] * fi[None] + ti[:, :, None] * fr[None]
    first = _real_form(mr, mi)
    a_out = jnp.arange(n1 // 4, n1 // 4 + n1 // 2, dtype=i32)
    gr, gi = _unit_roots(p[:, None] * a_out[None, :], n1)
    cr = (tr[:, :, None] * gr[None] - ti[:, :, None] * gi[None]) / n
    ci = -(tr[:, :, None] * gi[None] + ti[:, :, None] * gr[None]) / n
    last = _real_form(jnp.swapaxes(cr, 1, 2), jnp.swapaxes(ci, 1, 2))
    q = jnp.arange(n2, dtype=i32)
    hr, hi_ = _unit_roots(q[:, None] * c[None, :], n2)
    mid_f = _real_form(hr, hi_)
    mid_i = _real_form(hr, -hi_)
    return tuple(t.astype(MXU_DTYPE) for t in (first, mid_f, mid_i, last))


def _dft_rows_kernel(x_ref, m_ref, o_ref):
    o_ref[...] = _dot(m_ref[0], x_ref[...].astype(MXU_DTYPE)).astype(o_ref.dtype)


def _dft_first(x, table, d):
    rows, cols = x.shape
    n_out = table.shape[1]
    bd = _tile(d, 2048)
    per = d // bd
    return pl.pallas_call(
        _dft_rows_kernel,
        out_shape=jax.ShapeDtypeStruct((n_out, cols), MXU_DTYPE),
        grid=(cols // bd,),
        in_specs=[
            pl.BlockSpec((rows, bd), lambda j: (0, j)),
            pl.BlockSpec((1, n_out, rows), lambda j: (j // per, 0, 0)),
        ],
        out_specs=pl.BlockSpec((n_out, bd), lambda j: (0, j)),
        compiler_params=_params(("parallel",)),
        name="hyena_dft_first",
    )(x, table)


def _dft_mid_filter_kernel(a_ref, f_ref, norm_ref, o_ref):
    n2 = a_ref.shape[2]
    a = a_ref[...].reshape(2 * n2, a_ref.shape[3])
    z = _dot(f_ref[...], a)
    o_ref[...] = (z / norm_ref[...]).reshape(o_ref.shape).astype(o_ref.dtype)


def _dft_mid_conv_kernel(a_ref, h_ref, f_ref, g_ref, o_ref):
    n2 = a_ref.shape[2]
    d = a_ref.shape[3]
    a = a_ref[...].reshape(2 * n2, d)
    z = _dot(f_ref[...], a)
    zr, zi = z[:n2], z[n2:]
    hr, hi_ = h_ref[0, 0].astype(F32), h_ref[1, 0].astype(F32)
    w = jnp.concatenate([zr * hr - zi * hi_, zr * hi_ + zi * hr], axis=0)
    y = _dot(g_ref[...], w.astype(MXU_DTYPE))
    o_ref[...] = y.reshape(o_ref.shape).astype(o_ref.dtype)


def _dft_mid(a, tables, d, *, spectrum=None, norm=None):
    n2 = DFT_N2
    n1 = a.shape[0] // 2
    a4 = a.reshape(2, n1, n2, d)
    bd = _tile(d, 2048)
    blk = pl.BlockSpec((2, 1, n2, bd), lambda p, j: (0, p, 0, j))
    mat = pl.BlockSpec((2 * n2, 2 * n2), lambda p, j: (0, 0))
    fwd, inv = tables
    if spectrum is None:
        kern, ins, specs = _dft_mid_filter_kernel, (a4, fwd, norm), [blk, mat, pl.BlockSpec((1, bd), lambda p, j: (0, j))]
    else:
        kern, ins, specs = _dft_mid_conv_kernel, (a4, spectrum, fwd, inv), [blk, blk, mat, mat]
    out = pl.pallas_call(
        kern,
        out_shape=jax.ShapeDtypeStruct((2, n1, n2, d), MXU_DTYPE),
        grid=(n1, d // bd),
        in_specs=specs,
        out_specs=blk,
        compiler_params=_params(("parallel", "parallel")),
        name="hyena_dft_mid",
    )(*ins)
    return out


def _dft_last(b, table, d):
    rows, cols = b.shape
    n1 = rows // 2
    bd = _tile(d, 2048)
    per = d // bd
    return pl.pallas_call(
        _dft_rows_kernel,
        out_shape=jax.ShapeDtypeStruct((n1, cols), F32),
        grid=(cols // bd,),
        in_specs=[
            pl.BlockSpec((rows, bd), lambda j: (0, j)),
            pl.BlockSpec((1, n1, rows), lambda j: (j // per, 0, 0)),
        ],
        out_specs=pl.BlockSpec((n1, bd), lambda j: (0, j)),
        compiler_params=_params(("parallel",)),
        name="hyena_dft_last",
    )(b, table)


def _long_conv_two_stage(v, filt, norm, tables):
    bsz, seq, d = v.shape
    n1 = 2 * seq // DFT_N2
    first, mid_f, mid_i, last = tables
    cols = DFT_N2 * d
    h_first = _dft_first(filt.reshape(n1 // 2, cols), first[:, :, :n1 // 2], d)
    spectrum = _dft_mid(h_first, (mid_f, mid_i), d, norm=norm)
    a = _dft_first(v.reshape(n1, cols), first, d)
    bmat = _dft_mid(a, (mid_f, mid_i), d, spectrum=spectrum)
    y = _dft_last(bmat.reshape(2 * n1, cols), last, d)
    return y.reshape(bsz, seq, d)


def _dft_small_kernel(v_ref, h_ref, norm_ref, f_ref, g_ref, o_ref):
    seq = v_ref.shape[1]
    n = 2 * seq
    fwd = f_ref[...]
    x = jnp.concatenate([v_ref[0], v_ref[1]], axis=0)
    z = _dot(fwd, x.astype(MXU_DTYPE))
    hx = jnp.concatenate([h_ref[...] / norm_ref[...], jnp.zeros_like(h_ref)], axis=0)
    hs = _dot(fwd, hx.astype(MXU_DTYPE))
    zr, zi, hr, hi_ = z[:n], z[n:], hs[:n], hs[n:]
    w = jnp.concatenate([zr * hr - zi * hi_, zr * hi_ + zi * hr], axis=0)
    y = _dot(g_ref[...], w.astype(MXU_DTYPE))
    o_ref[0] = y[:seq]
    o_ref[1] = y[seq:]


def _long_conv_small(v, filt, norm):
    bsz, seq, d = v.shape
    n = 2 * seq
    k = jnp.arange(n, dtype=jnp.int32)
    t_in = jnp.arange(seq, dtype=jnp.int32)
    fr, fi = _unit_roots(k[:, None] * t_in[None, :], n)
    fwd = _real_form(fr, fi)
    t_out = jnp.arange(seq // 2, seq // 2 + seq, dtype=jnp.int32)
    gr, gi = _unit_roots(t_out[:, None] * k[None, :], n)
    inv = _real_form(gr / n, -gi / n)
    bd = _tile(d, 512)
    full = lambda shape: pl.BlockSpec(shape, lambda j: (0,) * len(shape))
    return pl.pallas_call(
        _dft_small_kernel,
        out_shape=jax.ShapeDtypeStruct((bsz, seq, d), F32),
        grid=(d // bd,),
        in_specs=[
            pl.BlockSpec((bsz, seq, bd), lambda j: (0, 0, j)),
            pl.BlockSpec((seq, bd), lambda j: (0, j)),
            pl.BlockSpec((1, bd), lambda j: (0, j)),
            full(fwd.shape), full(inv.shape),
        ],
        out_specs=pl.BlockSpec((bsz, seq, bd), lambda j: (0, 0, j)),
        compiler_params=_params(("parallel",)),
        name="hyena_dft_small",
    )(v, filt, norm, fwd.astype(MXU_DTYPE), inv.astype(MXU_DTYPE))


def _hyena_out_kernel(alpha, x0_ref, cv_ref, v_ref, bias_ref, w_ref, b_ref, h_ref, g_ref, lg_ref, lb_ref, o_ref):
    y = x0_ref[0] * (cv_ref[0] + v_ref[0] * bias_ref[...])
    y = _dot(y.astype(MXU_DTYPE), w_ref[...]) + b_ref[...]
    o_ref[0] = _layer_norm_rows(alpha * h_ref[0] + g_ref[0] * y, lg_ref[...], lb_ref[...])


def _attn_out_kernel(alpha, o_in_ref, w_ref, h_ref, g_ref, lg_ref, lb_ref, o_ref):
    y = _dot(o_in_ref[0], w_ref[...])
    o_ref[0] = _layer_norm_rows(alpha * h_ref[0] + g_ref[0] * y, lg_ref[...], lb_ref[...])


def _row_spec(tm, d):
    return pl.BlockSpec((1, tm, d), lambda bi, i: (bi, i, 0))


def _vec_spec(d):
    return pl.BlockSpec((1, d), lambda bi, i: (0, 0))


def _bvec_spec(d):
    return pl.BlockSpec((1, 1, d), lambda bi, i: (bi, 0, 0))


def _const_spec(shape):
    return pl.BlockSpec(shape, lambda bi, i: (0,) * len(shape), pipeline_mode=pl.Buffered(1))


def _hyena_out(alpha, x0, cv, v, bias, w, b, h, gate, ln_g, ln_b):
    bsz, seq, d = h.shape
    tm = _tile(seq, 256)
    return pl.pallas_call(
        functools.partial(_hyena_out_kernel, alpha),
        out_shape=jax.ShapeDtypeStruct(h.shape, F32),
        grid=(bsz, seq // tm),
        in_specs=[_row_spec(tm, d), _row_spec(tm, d), _row_spec(tm, d), _vec_spec(d), _const_spec((d, d)),
                  _vec_spec(d), _row_spec(tm, d), _bvec_spec(d), _vec_spec(d), _vec_spec(d)],
        out_specs=_row_spec(tm, d),
        compiler_params=_params(("parallel", "parallel")),
        name="hyena_out_norm",
    )(x0, cv, v, bias.reshape(1, d), w.astype(MXU_DTYPE), b.reshape(1, d), h, gate, ln_g.reshape(1, d), ln_b.reshape(1, d))


def _attn_out(alpha, o, w, h, gate, ln_g, ln_b):
    bsz, seq, d = h.shape
    k = o.shape[2]
    tm = _tile(seq, 512)
    return pl.pallas_call(
        functools.partial(_attn_out_kernel, alpha),
        out_shape=jax.ShapeDtypeStruct(h.shape, F32),
        grid=(bsz, seq // tm),
        in_specs=[_row_spec(tm, k), _const_spec((k, d)), _row_spec(tm, d), _bvec_spec(d), _vec_spec(d), _vec_spec(d)],
        out_specs=_row_spec(tm, d),
        compiler_params=_params(("parallel", "parallel")),
        name="attn_out_norm",
    )(o, w.astype(MXU_DTYPE), h, gate, ln_g.reshape(1, d), ln_b.reshape(1, d))


def _ffn_kernel(alpha, h_ref, sh_ref, sc_ref, g_ref, wg_ref, wu_ref, wd_ref, lg_ref, lb_ref, o_ref, u_scr, acc_scr):
    f = pl.program_id(2)

    @pl.when(f == 0)
    def _():
        u_scr[...] = (h_ref[0] * (1.0 + sc_ref[0]) + sh_ref[0]).astype(MXU_DTYPE)
        acc_scr[...] = jnp.zeros_like(acc_scr)

    u = u_scr[...]
    gate = _dot(u, wg_ref[...])
    up = _dot(u, wu_ref[...])
    act = (_silu(gate) * up).astype(MXU_DTYPE)
    acc_scr[...] += _dot(act, wd_ref[...])

    @pl.when(f == pl.num_programs(2) - 1)
    def _():
        o_ref[0] = _layer_norm_rows(alpha * h_ref[0] + g_ref[0] * acc_scr[...], lg_ref[...], lb_ref[...])


def _ffn(alpha, h, shift, scale, gate, w_gate, w_up, w_down, ln_g, ln_b):
    bsz, seq, d = h.shape
    ff = w_gate.shape[1]
    tm = _tile(seq, 512)
    tf = _tile(ff, 512)
    return pl.pallas_call(
        functools.partial(_ffn_kernel, alpha),
        out_shape=jax.ShapeDtypeStruct(h.shape, F32),
        grid=(bsz, seq // tm, ff // tf),
        in_specs=[
            pl.BlockSpec((1, tm, d), lambda bi, i, f: (bi, i, 0)),
            pl.BlockSpec((1, 1, d), lambda bi, i, f: (bi, 0, 0)),
            pl.BlockSpec((1, 1, d), lambda bi, i, f: (bi, 0, 0)),
            pl.BlockSpec((1, 1, d), lambda bi, i, f: (bi, 0, 0)),
            pl.BlockSpec((d, tf), lambda bi, i, f: (0, f)),
            pl.BlockSpec((d, tf), lambda bi, i, f: (0, f)),
            pl.BlockSpec((tf, d), lambda bi, i, f: (f, 0)),
            pl.BlockSpec((1, d), lambda bi, i, f: (0, 0)),
            pl.BlockSpec((1, d), lambda bi, i, f: (0, 0)),
        ],
        out_specs=pl.BlockSpec((1, tm, d), lambda bi, i, f: (bi, i, 0)),
        scratch_shapes=[pltpu.VMEM((tm, d), MXU_DTYPE), pltpu.VMEM((tm, d), F32)],
        compiler_params=_params(("parallel", "parallel", "arbitrary")),
        name="ffn_swiglu_norm",
    )(h, shift, scale, gate, w_gate.astype(MXU_DTYPE), w_up.astype(MXU_DTYPE), w_down.astype(MXU_DTYPE),
      ln_g.reshape(1, d), ln_b.reshape(1, d))


def _rotate_half_cols(w):
    ws = w.reshape(w.shape[:-1] + (2, 2, ROPE_PAIRS))
    return jnp.stack([-ws[..., 1, :], ws[..., 0, :]], axis=-2).reshape(w.shape)


def _rope_table(seq):
    rows = seq // GRID_W
    row = jnp.repeat(jnp.arange(rows, dtype=F32), GRID_W)
    col = jnp.tile(jnp.arange(GRID_W, dtype=F32), rows)
    inv = ROPE_THETA ** (-jnp.arange(ROPE_PAIRS, dtype=F32) / ROPE_PAIRS)
    ang = jnp.stack([row[:, None] * inv, col[:, None] * inv], axis=1)
    ang = jnp.broadcast_to(ang[:, :, None, :], (seq, 2, 2, ROPE_PAIRS)).reshape(seq, QK_ROPE)
    return jnp.concatenate([jnp.cos(ang), jnp.sin(ang)], axis=-1)


def _rms_rows(x, g):
    return x * lax.rsqrt(jnp.mean(x * x, axis=-1, keepdims=True) + RMS_EPS) * g


def _rope_pair(x, cs):
    t = x * cs
    return t + pltpu.roll(t, QK_ROPE, 1)


def _mla_q_kernel(cq_ref, g_ref, w_ref, cs_ref, q_ref):
    xn = _rms_rows(cq_ref[0], g_ref[...]).astype(MXU_DTYPE)
    cs = cs_ref[...]

    def head(h, carry):
        a = _dot(xn, w_ref[h])
        r = _rope_pair(a[:, QK_NOPE:], cs)
        q_ref[0, h] = (jnp.concatenate([a[:, :QK_NOPE], r], axis=1) * QUERY_SCALE).astype(q_ref.dtype)
        return carry

    lax.fori_loop(0, w_ref.shape[0], head, 0, unroll=2)


def _mla_queries(t, q_norm, wq_b, cs, heads):
    bsz, seq, _ = t.shape
    rank = q_norm.shape[0]
    tm = _tile(seq, 512)
    w = wq_b.reshape(rank, heads, QK_NOPE + QK_ROPE)
    w = jnp.concatenate([w, _rotate_half_cols(w[..., QK_NOPE:])], axis=-1).transpose(1, 0, 2).astype(MXU_DTYPE)
    return pl.pallas_call(
        _mla_q_kernel,
        out_shape=jax.ShapeDtypeStruct((bsz, heads, seq, 2 * LANES), MXU_DTYPE),
        grid=(bsz, seq // tm),
        in_specs=[
            pl.BlockSpec((1, tm, rank), lambda bi, i: (bi, i, 0)),
            pl.BlockSpec((1, rank), lambda bi, i: (0, 0)),
            pl.BlockSpec((heads, rank, 2 * LANES), lambda bi, i: (0, 0, 0)),
            pl.BlockSpec((tm, LANES), lambda bi, i: (i, 0)),
        ],
        out_specs=pl.BlockSpec((1, heads, tm, 2 * LANES), lambda bi, i: (bi, 0, i, 0)),
        compiler_params=_params(("parallel", "parallel")),
        name="mla_queries",
    )(t, q_norm.reshape(1, rank), w, cs)


def _mla_kv_kernel(use_rope, ckv_ref, kr_ref, g_ref, w_ref, cs_ref, k_ref, vt_ref):
    xn = _rms_rows(ckv_ref[0], g_ref[...]).astype(MXU_DTYPE)
    kr = kr_ref[0]
    if use_rope:
        kr = _rope_pair(kr, cs_ref[...])
    lane = lax.broadcasted_iota(jnp.int32, kr.shape, 1)
    kr = jnp.where(lane < QK_ROPE, kr, 0.0).astype(k_ref.dtype)
    ones = jnp.ones((V_ROWS - V_HEAD, xn.shape[0]), vt_ref.dtype)

    def head(h, carry):
        kv = _dot(xn, w_ref[h])
        k_ref[0, h] = jnp.concatenate([kv[:, :QK_NOPE].astype(k_ref.dtype), kr], axis=1)
        vt_ref[0, h] = jnp.concatenate([kv[:, QK_NOPE:].T.astype(vt_ref.dtype), ones], axis=0)
        return carry

    lax.fori_loop(0, w_ref.shape[0], head, 0, unroll=2)


def _mla_keys_values(t, kv_norm, wkv_b, cs, heads, use_rope):
    bsz, seq, _ = t.shape
    rank = kv_norm.shape[0]
    tm = _tile(seq, 512)
    w = wkv_b.reshape(rank, heads, QK_NOPE + V_HEAD).transpose(1, 0, 2).astype(MXU_DTYPE)
    return pl.pallas_call(
        functools.partial(_mla_kv_kernel, use_rope),
        out_shape=(jax.ShapeDtypeStruct((bsz, heads, seq, 2 * LANES), MXU_DTYPE),
                   jax.ShapeDtypeStruct((bsz, heads, V_ROWS, seq), MXU_DTYPE)),
        grid=(bsz, seq // tm),
        in_specs=[
            pl.BlockSpec((1, tm, rank), lambda bi, i: (bi, i, 1)),
            pl.BlockSpec((1, tm, LANES), lambda bi, i: (bi, i, 2 * rank // LANES)),
            pl.BlockSpec((1, rank), lambda bi, i: (0, 0)),
            pl.BlockSpec((heads, rank, QK_NOPE + V_HEAD), lambda bi, i: (0, 0, 0)),
            pl.BlockSpec((tm, LANES), lambda bi, i: (i, 0)),
        ],
        out_specs=(pl.BlockSpec((1, heads, tm, 2 * LANES), lambda bi, i: (bi, 0, i, 0)),
                   pl.BlockSpec((1, heads, V_ROWS, tm), lambda bi, i: (bi, 0, 0, i))),
        compiler_params=_params(("parallel", "parallel")),
        name="mla_keys_values",
    )(t, t, kv_norm.reshape(1, rank), w, cs)


def _attn_kernel(q_ref, k_ref, vt_ref, kc_ref, vtc_ref, o_ref, qt_scr, m_scr, acc_scr):
    kv = pl.program_id(3)
    n_main = pl.num_programs(3) - 1

    @pl.when(kv == 0)
    def _():
        qt_scr[...] = q_ref[0, 0].T
        m_scr[...] = jnp.full_like(m_scr, -jnp.inf)
        acc_scr[...] = jnp.zeros_like(acc_scr)

    def step(k, vt):
        tq = qt_scr.shape[1]
        cw = min(tq, ATTN_CHUNK)
        cols = [pl.ds(c * cw, cw) for c in range(tq // cw)]
        s_next = _dot(k, qt_scr[:, cols[0]])
        for c, col in enumerate(cols):
            s = s_next
            if c + 1 < len(cols):
                s_next = _dot(k, qt_scr[:, cols[c + 1]])
            m_prev = m_scr[:, col]
            m_new = jnp.maximum(m_prev, jnp.max(s, axis=0, keepdims=True))
            a = jnp.exp2(m_prev - m_new)
            p = jnp.exp2((s - m_new).astype(MXU_DTYPE))
            acc_scr[:, col] = a * acc_scr[:, col] + _dot(vt, p)
            m_scr[:, col] = m_new

    @pl.when(kv < n_main)
    def _():
        step(k_ref[0, 0], vt_ref[0, 0])

    @pl.when(kv == n_main)
    def _():
        step(kc_ref[0, 0], vtc_ref[0, 0])
        acc = acc_scr[...]
        o_ref[0] = (acc[:V_HEAD] / acc[V_HEAD:V_HEAD + 1]).T.astype(o_ref.dtype)


def _mla_attend(q, k, vt, kc, vtc):
    bsz, heads, seq, dq = q.shape
    lc = kc.shape[2]
    tq = _tile(seq, 2048)
    tk = _tile(seq, 1024)
    n_main = seq // tk
    return pl.pallas_call(
        _attn_kernel,
        out_shape=jax.ShapeDtypeStruct((bsz, seq, heads * V_HEAD), MXU_DTYPE),
        grid=(bsz, heads, seq // tq, n_main + 1),
        in_specs=[
            pl.BlockSpec((1, 1, tq, dq), lambda bi, h, i, j: (bi, h, i, 0)),
            pl.BlockSpec((1, 1, tk, dq), lambda bi, h, i, j: (bi, h, jnp.minimum(j, n_main - 1), 0)),
            pl.BlockSpec((1, 1, V_ROWS, tk), lambda bi, h, i, j: (bi, h, 0, jnp.minimum(j, n_main - 1))),
            pl.BlockSpec((1, 1, lc, dq), lambda bi, h, i, j: (bi, h, 0, 0)),
            pl.BlockSpec((1, 1, V_ROWS, lc), lambda bi, h, i, j: (bi, h, 0, 0)),
        ],
        out_specs=pl.BlockSpec((1, tq, V_HEAD), lambda bi, h, i, j: (bi, i, h)),
        scratch_shapes=[pltpu.VMEM((dq, tq), MXU_DTYPE), pltpu.VMEM((1, tq), F32), pltpu.VMEM((V_ROWS, tq), F32)],
        compiler_params=_params(("parallel", "parallel", "parallel", "arbitrary")),
        name="mla_attention",
    )(q, k, vt, kc, vtc)


def _pool_kernel(alpha, seq, h_ref, prev_ref, next_ref, sh_ref, sc_ref, g_ref, w_ref, ps_ref, lg_ref, lb_ref, o_ref):
    i = pl.program_id(1)
    last = pl.num_programs(1) - 1
    tm = h_ref.shape[1]
    d = h_ref.shape[2]
    groups = len(POOL_WINDOWS)
    ch = d // groups
    ext = tm + 2 * POOL_HALO
    shift, scale = sh_ref[0], sc_ref[0]
    h = h_ref[0]
    u = h * (1.0 + scale) + shift
    u_prev = jnp.where(i == 0, 0.0, prev_ref[0] * (1.0 + scale) + shift)
    u_next = jnp.where(i == last, 0.0, next_ref[0] * (1.0 + scale) + shift)
    e = jnp.concatenate([u_prev, u, u_next], axis=0)
    t = i * tm + lax.broadcasted_iota(jnp.int32, (tm, 1), 0)
    ys = []
    for g, win in enumerate(POOL_WINDOWS):
        a = e[:, g * ch:(g + 1) * ch]
        span = 1
        while span < win:
            a = a + pltpu.roll(a, ext - span, 0)
            span *= 2
        half = win // 2
        a = pltpu.roll(a, half, 0)
        s = a[POOL_HALO:POOL_HALO + tm]
        cnt = (jnp.minimum(t + half, seq) - jnp.maximum(t - half, 0)).astype(F32)
        dg = s / cnt - u[:, g * ch:(g + 1) * ch]
        ys.append(_dot(dg.astype(MXU_DTYPE), w_ref[g]))
    y = jnp.concatenate(ys, axis=1) * ps_ref[...]
    o_ref[0] = _layer_norm_rows(alpha * h + g_ref[0] * y, lg_ref[...], lb_ref[...])


def _pool_mixer(alpha, h, shift, scale, gate, w_grp, pool_scale, ln_g, ln_b):
    bsz, seq, d = h.shape
    groups, ch, _ = w_grp.shape
    tm = _tile(seq, 512)
    hb = tm // POOL_HALO
    nhb = seq // POOL_HALO
    return pl.pallas_call(
        functools.partial(_pool_kernel, alpha, seq),
        out_shape=jax.ShapeDtypeStruct(h.shape, F32),
        grid=(bsz, seq // tm),
        in_specs=[
            _row_spec(tm, d),
            pl.BlockSpec((1, POOL_HALO, d), lambda bi, i: (bi, jnp.maximum(i * hb - 1, 0), 0)),
            pl.BlockSpec((1, POOL_HALO, d), lambda bi, i: (bi, jnp.minimum((i + 1) * hb, nhb - 1), 0)),
            _bvec_spec(d), _bvec_spec(d), _bvec_spec(d),
            pl.BlockSpec((groups, ch, ch), lambda bi, i: (0, 0, 0)),
            _vec_spec(d), _vec_spec(d), _vec_spec(d),
        ],
        out_specs=_row_spec(tm, d),
        compiler_params=_params(("parallel", "parallel")),
        name="pool_mixer_norm",
    )(h, h, h, shift, scale, gate, w_grp.astype(MXU_DTYPE), pool_scale.reshape(1, d), ln_g.reshape(1, d), ln_b.reshape(1, d))


def _hyena_mixer_norm(alpha, h, mod, hy, filt, norm, tables, ln_g, ln_b):
    shift, scale, gate = mod
    w_in, b_in, conv_w, conv_b, bias, w_out, b_out = hy
    proj = _mod_proj(h, shift, scale, w_in, b_in, slabs=3, tm=512, tn=1024)
    x0, v = _conv3_gate(proj, conv_w, conv_b)
    if h.shape[1] <= DFT_SMALL_MAX_L:
        cv = _long_conv_small(v, filt, norm)
    else:
        cv = _long_conv_two_stage(v, filt, norm, tables)
    return _hyena_out(alpha, x0, cv, v, bias, w_out, b_out, h, gate, ln_g, ln_b)


def kernel(x, c, ctx, c_ctx, ada_w, ada_b, ln_g, ln_b, ffn_w_gate, ffn_w_up, ffn_w_down, hy_w_in, hy_b_in, hy_conv_w, hy_conv_b, hy_f_w_in, hy_f_w_hid, hy_f_b, hy_f_freq, hy_f_w_out, hy_bias, hy_w_out, hy_b_out, mla_w_in, mla_q_norm, mla_kv_norm, mla_wq_b, mla_wkv_b, mla_w_out, pool_w, pool_scale):
    bsz, seq, d = x.shape
    depth = ada_w.shape[0]
    assert bsz == 2, "the long convolution packs exactly two batch rows into one complex signal"
    assert ctx.shape[0] == bsz and seq % GRID_W == 0
    alpha = (2.0 * depth) ** 0.25
    heads = d // V_HEAD
    mla_layers = [i for i in range(depth) if i % N_MIXERS == 1]
    last_ctx_read = mla_layers[-1] if mla_layers else -1

    cond = jnp.concatenate([c, jnp.broadcast_to(c_ctx[None], (SUBLANES - bsz, d))], axis=0)
    mods = _ada_mods(cond, ada_w, ada_b)

    def mod_vecs(i, ctx_stream):
        m = jnp.broadcast_to(mods[i, bsz][None], (bsz, 6 * d)) if ctx_stream else mods[i, :bsz]
        return [m[:, None, k * d:(k + 1) * d] for k in range(6)]

    cs = _rope_table(seq)
    tables = _dft_tables(seq) if seq > DFT_SMALL_MAX_L else None

    h, hc = x, ctx
    for i in range(depth):
        kind, j = i % N_MIXERS, i // N_MIXERS
        ctx_update = i < last_ctx_read
        sh1, sc1, g1, sh2, sc2, g2 = mod_vecs(i, False)
        if kind == 1 or ctx_update:
            csh1, csc1, cg1, csh2, csc2, cg2 = mod_vecs(i, True)
        lg, lb = ln_g[i], ln_b[i]
        if kind == 0:
            hy = (hy_w_in[j], hy_b_in[j], hy_conv_w[j], hy_conv_b[j], hy_bias[j], hy_w_out[j], hy_b_out[j])
            fp = (hy_f_w_in[j], hy_f_w_hid[j], hy_f_b[j], hy_f_freq[j], hy_f_w_out[j])
            filt, norm = _implicit_filter(seq, *fp)
            h_mid = _hyena_mixer_norm(alpha, h, (sh1, sc1, g1), hy, filt, norm, tables, lg[0], lb[0])
            if ctx_update:
                filt_c, norm_c = _implicit_filter(hc.shape[1], *fp)
                hc_mid = _hyena_mixer_norm(alpha, hc, (csh1, csc1, cg1), hy, filt_c, norm_c, None, lg[0], lb[0])
        elif kind == 1:
            assert not ctx_update, "context queries are only needed when a later layer reads the context"
            rank = mla_q_norm.shape[1]
            w_in = mla_w_in[j]
            w_in = jnp.concatenate([w_in, _rotate_half_cols(w_in[:, 2 * rank:])], axis=1)
            zeros = jnp.zeros((w_in.shape[1],), F32)
            t = _mod_proj(h, sh1, sc1, w_in, zeros, slabs=1, tm=512, tn=w_in.shape[1])[0]
            tc = _mod_proj(hc, csh1, csc1, w_in, zeros, slabs=1, tm=512, tn=w_in.shape[1])[0]
            q = _mla_queries(t, mla_q_norm[j], mla_wq_b[j], cs, heads)
            k, v = _mla_keys_values(t, mla_kv_norm[j], mla_wkv_b[j], cs, heads, True)
            kc, vc = _mla_keys_values(tc, mla_kv_norm[j], mla_wkv_b[j], cs, heads, False)
            o = _mla_attend(q, k, v, kc, vc)
            h_mid = _attn_out(alpha, o, mla_w_out[j], h, g1, lg[0], lb[0])
        else:
            h_mid = _pool_mixer(alpha, h, sh1, sc1, g1, pool_w[j], pool_scale[j], lg[0], lb[0])
            if ctx_update:
                hc_mid = _pool_mixer(alpha, hc, csh1, csc1, cg1, pool_w[j], pool_scale[j], lg[0], lb[0])
        ffn = (ffn_w_gate[i], ffn_w_up[i], ffn_w_down[i])
        h = _ffn(alpha, h_mid, sh2, sc2, g2, *ffn, lg[1], lb[1])
        if ctx_update:
            hc = _ffn(alpha, hc_mid, csh2, csc2, cg2, *ffn, lg[1], lb[1])
    return h
```

```python
import functools
import math

import jax
import jax.numpy as jnp
from jax import lax
from jax.experimental import pallas as pl
from jax.experimental.pallas import tpu as pltpu

F32 = jnp.float32
MXU_DTYPE = jnp.bfloat16

V7X_VMEM_BYTES = 64 * 1024 * 1024
VMEM_LIMIT = V7X_VMEM_BYTES - 8 * 1024 * 1024
LANES = 128
SUBLANES = 8

N_MIXERS = 3
GRID_W = 64
LN_EPS = 1e-6
RMS_EPS = 1e-6
HY_TARGET = 1e-2
HY_FAST = 0.3
HY_SLOW = 1.5
HY_MIN_DECAY = math.log(HY_TARGET) / HY_SLOW
HY_MAX_DECAY = math.log(HY_TARGET) / HY_FAST
QK_NOPE = 128
QK_ROPE = 64
V_HEAD = 128
ROPE_PAIRS = QK_ROPE // 4
ROPE_THETA = 10000.0
ATTN_SCALE = (QK_NOPE + QK_ROPE) ** -0.5
QUERY_SCALE = ATTN_SCALE * math.log2(math.e)
ATTN_CHUNK = 512
V_ROWS = V_HEAD + 16
POOL_WINDOWS = (2, 4, 8, 16)
POOL_HALO = 8

DFT_N2 = 128
DFT_SMALL_MAX_L = 512
DFT_MID_ROWS = 4


def _params(sem, vmem=VMEM_LIMIT, flags=None):
    return pltpu.CompilerParams(dimension_semantics=sem, vmem_limit_bytes=vmem, flags=flags)


def _tile(n, t):
    t = min(n, t)
    assert n % t == 0, (n, t)
    return t


def _dot(a, b):
    return jnp.dot(a, b, preferred_element_type=F32)


def _layer_norm_rows(x, g, b):
    mu = jnp.mean(x, axis=-1, keepdims=True)
    xc = x - mu
    var = jnp.mean(xc * xc, axis=-1, keepdims=True)
    return xc * lax.rsqrt(var + LN_EPS) * g + b


def _silu(x):
    return x * (1.0 / (1.0 + jnp.exp(-x)))


def _ada_kernel(c_ref, w_ref, b_ref, o_ref):
    c = c_ref[...]
    a = _silu(c).astype(MXU_DTYPE)
    o_ref[0] = _dot(a, w_ref[0].astype(MXU_DTYPE)) + b_ref[0]


def _ada_mods(cond, ada_w, ada_b):
    depth, d, n = ada_w.shape
    rows = cond.shape[0]
    tn = _tile(n, 1024)
    return pl.pallas_call(
        _ada_kernel,
        out_shape=jax.ShapeDtypeStruct((depth, rows, n), F32),
        grid=(depth, n // tn),
        in_specs=[
            pl.BlockSpec((rows, d), lambda i, j: (0, 0)),
            pl.BlockSpec((1, d, tn), lambda i, j: (i, 0, j)),
            pl.BlockSpec((1, 1, tn), lambda i, j: (i, 0, j)),
        ],
        out_specs=pl.BlockSpec((1, rows, tn), lambda i, j: (i, 0, j)),
        compiler_params=_params(("parallel", "parallel")),
        name="ada_mods",
    )(cond, ada_w, ada_b.reshape(depth, 1, n))


def _mod_proj_kernel(x_ref, sh_ref, sc_ref, w_ref, b_ref, o_ref, u_scr):
    @pl.when(pl.program_id(2) == 0)
    def _():
        u_scr[...] = (x_ref[0] * (1.0 + sc_ref[0]) + sh_ref[0]).astype(MXU_DTYPE)

    o_ref[0, 0] = (_dot(u_scr[...], w_ref[...]) + b_ref[...]).astype(o_ref.dtype)


def _mod_proj(x, shift, scale, w, b, slabs, tm, tn):
    bsz, seq, k = x.shape
    n = w.shape[1]
    n_slab = n // slabs
    tm = _tile(seq, tm)
    tn = _tile(n_slab, tn)
    per = n_slab // tn
    return pl.pallas_call(
        _mod_proj_kernel,
        out_shape=jax.ShapeDtypeStruct((slabs, bsz, seq, n_slab), F32),
        grid=(bsz, seq // tm, n // tn),
        in_specs=[
            pl.BlockSpec((1, tm, k), lambda bi, i, j: (bi, i, 0)),
            pl.BlockSpec((1, 1, k), lambda bi, i, j: (bi, 0, 0)),
            pl.BlockSpec((1, 1, k), lambda bi, i, j: (bi, 0, 0)),
            pl.BlockSpec((k, tn), lambda bi, i, j: (0, j)),
            pl.BlockSpec((1, tn), lambda bi, i, j: (0, j)),
        ],
        out_specs=pl.BlockSpec((1, 1, tm, tn), lambda bi, i, j: (j // per, bi, i, j % per)),
        scratch_shapes=[pltpu.VMEM((tm, k), MXU_DTYPE)],
        compiler_params=_params(("parallel", "parallel", "arbitrary")),
        name="mod_proj",
    )(x, shift, scale, w.astype(MXU_DTYPE), b.reshape(1, n))


def _hyena_in_kernel(x_ref, prev_ref, next_ref, sh_ref, sc_ref, w0_ref, w1_ref, w2_ref, b_ref, cw_ref, cb_ref,
                     x0_ref, v_ref, u_scr):
    i = pl.program_id(1)
    last = pl.num_programs(1) - 1
    tm = x_ref.shape[1]
    ext = tm + 2 * SUBLANES

    @pl.when(pl.program_id(2) == 0)
    def _():
        shift, scale = sh_ref[0], sc_ref[0]
        rows = jnp.concatenate([prev_ref[0], x_ref[0], next_ref[0]], axis=0)
        u_scr[...] = (rows * (1.0 + scale) + shift).astype(MXU_DTYPE)

    r = lax.broadcasted_iota(jnp.int32, (ext, 1), 0)
    inside = jnp.logical_and(jnp.logical_or(i > 0, r >= SUBLANES), jnp.logical_or(i < last, r < SUBLANES + tm))
    u = u_scr[...]
    outs = []
    for s, w_ref in enumerate((w0_ref, w1_ref, w2_ref)):
        proj = jnp.where(inside, _dot(u, w_ref[...]) + b_ref[s], 0.0)
        taps = cw_ref[s]
        conv = pltpu.roll(proj, 1, 0) * taps[0:1] + proj * taps[1:2] + pltpu.roll(proj, ext - 1, 0) * taps[2:3]
        outs.append(conv[SUBLANES:SUBLANES + tm] + cb_ref[s])
    x0_ref[0] = outs[0]
    v_ref[0] = outs[2] * outs[1]


def _hyena_in(x, shift, scale, w_in, b_in, conv_w, conv_b):
    bsz, seq, k = x.shape
    d = w_in.shape[1] // 3
    tm = _tile(seq, 512)
    tn = _tile(d, 512)
    per = d // tn
    hb = tm // SUBLANES
    nhb = seq // SUBLANES
    w = w_in.astype(MXU_DTYPE)
    cw = conv_w.reshape(3, 3, d).transpose(1, 0, 2)
    out = jax.ShapeDtypeStruct((bsz, seq, d), F32)
    slab = lambda s: pl.BlockSpec((k, tn), lambda bi, i, j: (0, s * per + j))
    return pl.pallas_call(
        _hyena_in_kernel,
        out_shape=(out, out),
        grid=(bsz, seq // tm, per),
        in_specs=[
            pl.BlockSpec((1, tm, k), lambda bi, i, j: (bi, i, 0)),
            pl.BlockSpec((1, SUBLANES, k), lambda bi, i, j: (bi, jnp.maximum(i * hb - 1, 0), 0)),
            pl.BlockSpec((1, SUBLANES, k), lambda bi, i, j: (bi, jnp.minimum((i + 1) * hb, nhb - 1), 0)),
            pl.BlockSpec((1, 1, k), lambda bi, i, j: (bi, 0, 0)),
            pl.BlockSpec((1, 1, k), lambda bi, i, j: (bi, 0, 0)),
            slab(0), slab(1), slab(2),
            pl.BlockSpec((3, 1, tn), lambda bi, i, j: (0, 0, j)),
            pl.BlockSpec((3, 3, tn), lambda bi, i, j: (0, 0, j)),
            pl.BlockSpec((3, 1, tn), lambda bi, i, j: (0, 0, j)),
        ],
        out_specs=(
            pl.BlockSpec((1, tm, tn), lambda bi, i, j: (bi, i, j)),
            pl.BlockSpec((1, tm, tn), lambda bi, i, j: (bi, i, j)),
        ),
        scratch_shapes=[pltpu.VMEM((tm + 2 * SUBLANES, k), MXU_DTYPE)],
        compiler_params=_params(("parallel", "parallel", "arbitrary")),
        name="hyena_in_conv_gate",
    )(x, x, x, shift, scale, w, w, w, b_in.reshape(3, 1, d), cw, conv_b.reshape(3, 1, d))


def _filter_kernel(z_ref, dist_ref, w_in_ref, w_hid_ref, b_ref, fr_ref, w_out_ref, delta_ref, f_ref, norm_ref):
    hp = lax.Precision.HIGHEST
    b = b_ref[...]
    fr = fr_ref[...]
    g = jnp.sin(fr[0:1] * (jnp.dot(z_ref[...], w_in_ref[...], precision=hp, preferred_element_type=F32) + b[0:1]))
    g = jnp.sin(fr[1:2] * (jnp.dot(g, w_hid_ref[0], precision=hp, preferred_element_type=F32) + b[1:2]))
    g = jnp.sin(fr[2:3] * (jnp.dot(g, w_hid_ref[1], precision=hp, preferred_element_type=F32) + b[2:3]))
    filt = jnp.dot(g, w_out_ref[...], precision=hp, preferred_element_type=F32)
    filt = filt * jnp.exp(-dist_ref[...] * delta_ref[...])
    f_ref[...] = filt

    @pl.when(pl.program_id(0) == 0)
    def _():
        norm_ref[...] = jnp.zeros_like(norm_ref)

    norm_ref[...] += jnp.sum(jnp.abs(filt), axis=0, keepdims=True)


def _implicit_filter(seq, f_w_in, f_w_hid, f_b, f_freq, f_w_out):
    emb, width = f_w_in.shape
    bands_n = (emb - 1) // 2
    d = f_w_out.shape[1]
    emb_pad = -(-emb // LANES) * LANES
    filt_w = -(-width // LANES) * LANES
    pad_to = lambda a, shape: jnp.pad(a.astype(F32), [(0, s - n) for s, n in zip(shape, a.shape)])
    pos = jnp.arange(seq, dtype=F32)
    t = pos / (seq - 1)
    bands = jnp.linspace(1e-4, bands_n - 1, bands_n, dtype=F32)
    ang = (2.0 * math.pi / seq) * pos[:, None] * bands[None, :]
    z = pad_to(jnp.concatenate([t[:, None], jnp.cos(ang), -jnp.sin(ang)], axis=-1), (seq, emb_pad))
    w_in = pad_to(f_w_in, (emb_pad, filt_w))
    f_w_hid = pad_to(f_w_hid, (2, filt_w, filt_w))
    f_b = pad_to(f_b, (3, filt_w))
    f_freq = pad_to(f_freq, (3, filt_w))
    f_w_out = pad_to(f_w_out, (filt_w, d))
    dist = (jnp.abs(pos - seq // 2) / (seq // 2))[:, None]
    deltas = jnp.abs(jnp.linspace(HY_MIN_DECAY, HY_MAX_DECAY, d, dtype=F32))[None, :]
    tl = _tile(seq, 512)
    full = lambda shape: pl.BlockSpec(shape, lambda i: (0,) * len(shape))
    return pl.pallas_call(
        _filter_kernel,
        out_shape=(jax.ShapeDtypeStruct((seq, d), F32), jax.ShapeDtypeStruct((1, d), F32)),
        grid=(seq // tl,),
        in_specs=[
            pl.BlockSpec((tl, emb_pad), lambda i: (i, 0)),
            pl.BlockSpec((tl, 1), lambda i: (i, 0)),
            full((emb_pad, filt_w)),
            full((2, filt_w, filt_w)),
            full((3, filt_w)),
            full((3, filt_w)),
            full((filt_w, d)),
            full((1, d)),
        ],
        out_specs=(pl.BlockSpec((tl, d), lambda i: (i, 0)), full((1, d))),
        compiler_params=_params(("arbitrary",)),
        name="hyena_filter",
    )(z, dist, w_in, f_w_hid.astype(F32), f_b.astype(F32), f_freq.astype(F32), f_w_out.astype(F32), deltas)


def _real_form(mr, mi):
    top = jnp.concatenate([mr, -mi], axis=-1)
    bot = jnp.concatenate([mi, mr], axis=-1)
    return jnp.concatenate([top, bot], axis=-2)


def _unit_roots(idx, n):
    ang = (2.0 * math.pi / n) * (idx % n).astype(F32)
    return jnp.cos(ang), -jnp.sin(ang)


def _dft_tables(seq):
    n = 2 * seq
    n2 = DFT_N2
    n1 = n // n2
    i32 = jnp.int32
    p = jnp.arange(n1, dtype=i32)
    c = jnp.arange(n2, dtype=i32)
    tr, ti = _unit_roots(c[:, None] * p[None, :], n)
    a_in = jnp.arange(n1 // 2, dtype=i32)
    fr, fi = _unit_roots(p[:, None] * a_in[None, :], n1)
    mr = tr[:, :, None] * fr[None] - ti[:, :, None] * fi[None]
    mi = tr[:, :, None] * fi[None] + ti[:, :, None] * fr[None]
    first = _real_form(mr, mi)
    a_out = jnp.arange(n1 // 4, n1 // 4 + n1 // 2, dtype=i32)
    gr, gi = _unit_roots(p[:, None] * a_out[None, :], n1)
    cr = (tr[:, :, None] * gr[None] - ti[:, :, None] * gi[None]) / n
    ci = -(tr[:, :, None] * gi[None] + ti[:, :, None] * gr[None]) / n
    last = _real_form(jnp.swapaxes(cr, 1, 2), jnp.swapaxes(ci, 1, 2))
    q = jnp.arange(n2, dtype=i32)
    hr, hi_ = _unit_roots(q[:, None] * c[None, :], n2)
    mid_f = _real_form(hr, hi_)
    mid_i = _real_form(hr, -hi_)
    return tuple(t.astype(MXU_DTYPE) for t in (first, mid_f, mid_i, last))


def _dft_rows_kernel(x_ref, m_ref, o_ref):
    o_ref[...] = _dot(m_ref[0], x_ref[...].astype(MXU_DTYPE)).astype(o_ref.dtype)


def _dft_first(x, table, d):
    rows, cols = x.shape
    n_out = table.shape[1]
    bd = _tile(d, 2048)
    per = d // bd
    return pl.pallas_call(
        _dft_rows_kernel,
        out_shape=jax.ShapeDtypeStruct((n_out, cols), MXU_DTYPE),
        grid=(cols // bd,),
        in_specs=[
            pl.BlockSpec((rows, bd), lambda j: (0, j)),
            pl.BlockSpec((1, n_out, rows), lambda j: (j // per, 0, 0)),
        ],
        out_specs=pl.BlockSpec((n_out, bd), lambda j: (0, j)),
        compiler_params=_params(("parallel",)),
        name="hyena_dft_first",
    )(x, table)


def _dft_mid_filter_kernel(a_ref, f_ref, norm_ref, o_ref):
    n2 = a_ref.shape[2]
    for p in range(a_ref.shape[1]):
        a = jnp.concatenate([a_ref[0, p], a_ref[1, p]], axis=0)
        z = _dot(f_ref[...], a) / norm_ref[...]
        o_ref[0, p] = z[:n2].astype(o_ref.dtype)
        o_ref[1, p] = z[n2:].astype(o_ref.dtype)


def _dft_mid_conv_kernel(a_ref, h_ref, f_ref, g_ref, o_ref):
    n2 = a_ref.shape[2]
    for p in range(a_ref.shape[1]):
        a = jnp.concatenate([a_ref[0, p], a_ref[1, p]], axis=0)
        z = _dot(f_ref[...], a)
        zr, zi = z[:n2], z[n2:]
        hr, hi_ = h_ref[0, p].astype(F32), h_ref[1, p].astype(F32)
        w = jnp.concatenate([zr * hr - zi * hi_, zr * hi_ + zi * hr], axis=0)
        y = _dot(g_ref[...], w.astype(MXU_DTYPE))
        o_ref[0, p] = y[:n2].astype(o_ref.dtype)
        o_ref[1, p] = y[n2:].astype(o_ref.dtype)


def _dft_mid(a, tables, d, *, spectrum=None, norm=None):
    n2 = DFT_N2
    n1 = a.shape[0] // 2
    a4 = a.reshape(2, n1, n2, d)
    bd = _tile(d, 2048)
    pb = _tile(n1, DFT_MID_ROWS)
    blk = pl.BlockSpec((2, pb, n2, bd), lambda p, j: (0, p, 0, j))
    mat = pl.BlockSpec((2 * n2, 2 * n2), lambda p, j: (0, 0))
    fwd, inv = tables
    if spectrum is None:
        kern, ins, specs = _dft_mid_filter_kernel, (a4, fwd, norm), [blk, mat, pl.BlockSpec((1, bd), lambda p, j: (0, j))]
    else:
        kern, ins, specs = _dft_mid_conv_kernel, (a4, spectrum, fwd, inv), [blk, blk, mat, mat]
    out = pl.pallas_call(
        kern,
        out_shape=jax.ShapeDtypeStruct((2, n1, n2, d), MXU_DTYPE),
        grid=(n1 // pb, d // bd),
        in_specs=specs,
        out_specs=blk,
        compiler_params=_params(("parallel", "parallel")),
        name="hyena_dft_mid",
    )(*ins)
    return out


def _dft_last(b, table, d):
    rows, cols = b.shape
    n1 = rows // 2
    bd = _tile(d, 2048)
    per = d // bd
    return pl.pallas_call(
        _dft_rows_kernel,
        out_shape=jax.ShapeDtypeStruct((n1, cols), F32),
        grid=(cols // bd,),
        in_specs=[
            pl.BlockSpec((rows, bd), lambda j: (0, j)),
            pl.BlockSpec((1, n1, rows), lambda j: (j // per, 0, 0)),
        ],
        out_specs=pl.BlockSpec((n1, bd), lambda j: (0, j)),
        compiler_params=_params(("parallel",)),
        name="hyena_dft_last",
    )(b, table)


def _long_conv_two_stage(v, filt, norm, tables):
    bsz, seq, d = v.shape
    n1 = 2 * seq // DFT_N2
    first, mid_f, mid_i, last = tables
    cols = DFT_N2 * d
    h_first = _dft_first(filt.reshape(n1 // 2, cols), first[:, :, :n1 // 2], d)
    spectrum = _dft_mid(h_first, (mid_f, mid_i), d, norm=norm)
    a = _dft_first(v.reshape(n1, cols), first, d)
    bmat = _dft_mid(a, (mid_f, mid_i), d, spectrum=spectrum)
    y = _dft_last(bmat.reshape(2 * n1, cols), last, d)
    return y.reshape(bsz, seq, d)


def _dft_small_kernel(v_ref, h_ref, norm_ref, f_ref, g_ref, o_ref):
    seq = v_ref.shape[1]
    n = 2 * seq
    fwd = f_ref[...]
    x = jnp.concatenate([v_ref[0], v_ref[1]], axis=0)
    z = _dot(fwd, x.astype(MXU_DTYPE))
    hx = jnp.concatenate([h_ref[...] / norm_ref[...], jnp.zeros_like(h_ref)], axis=0)
    hs = _dot(fwd, hx.astype(MXU_DTYPE))
    zr, zi, hr, hi_ = z[:n], z[n:], hs[:n], hs[n:]
    w = jnp.concatenate([zr * hr - zi * hi_, zr * hi_ + zi * hr], axis=0)
    y = _dot(g_ref[...], w.astype(MXU_DTYPE))
    o_ref[0] = y[:seq]
    o_ref[1] = y[seq:]


def _long_conv_small(v, filt, norm):
    bsz, seq, d = v.shape
    n = 2 * seq
    k = jnp.arange(n, dtype=jnp.int32)
    t_in = jnp.arange(seq, dtype=jnp.int32)
    fr, fi = _unit_roots(k[:, None] * t_in[None, :], n)
    fwd = _real_form(fr, fi)
    t_out = jnp.arange(seq // 2, seq // 2 + seq, dtype=jnp.int32)
    gr, gi = _unit_roots(t_out[:, None] * k[None, :], n)
    inv = _real_form(gr / n, -gi / n)
    bd = _tile(d, 512)
    full = lambda shape: pl.BlockSpec(shape, lambda j: (0,) * len(shape))
    return pl.pallas_call(
        _dft_small_kernel,
        out_shape=jax.ShapeDtypeStruct((bsz, seq, d), F32),
        grid=(d // bd,),
        in_specs=[
            pl.BlockSpec((bsz, seq, bd), lambda j: (0, 0, j)),
            pl.BlockSpec((seq, bd), lambda j: (0, j)),
            pl.BlockSpec((1, bd), lambda j: (0, j)),
            full(fwd.shape), full(inv.shape),
        ],
        out_specs=pl.BlockSpec((bsz, seq, bd), lambda j: (0, 0, j)),
        compiler_params=_params(("parallel",)),
        name="hyena_dft_small",
    )(v, filt, norm, fwd.astype(MXU_DTYPE), inv.astype(MXU_DTYPE))


def _hyena_out_kernel(alpha, x0_ref, cv_ref, v_ref, bias_ref, w_ref, b_ref, h_ref, g_ref, lg_ref, lb_ref, o_ref):
    y = x0_ref[0] * (cv_ref[0] + v_ref[0] * bias_ref[...])
    y = _dot(y.astype(MXU_DTYPE), w_ref[...]) + b_ref[...]
    o_ref[0] = _layer_norm_rows(alpha * h_ref[0] + g_ref[0] * y, lg_ref[...], lb_ref[...])


def _attn_out_kernel(alpha, o_in_ref, w_ref, h_ref, g_ref, lg_ref, lb_ref, o_ref):
    y = _dot(o_in_ref[0], w_ref[...])
    o_ref[0] = _layer_norm_rows(alpha * h_ref[0] + g_ref[0] * y, lg_ref[...], lb_ref[...])


def _row_spec(tm, d):
    return pl.BlockSpec((1, tm, d), lambda bi, i: (bi, i, 0))


def _vec_spec(d):
    return pl.BlockSpec((1, d), lambda bi, i: (0, 0))


def _bvec_spec(d):
    return pl.BlockSpec((1, 1, d), lambda bi, i: (bi, 0, 0))


def _const_spec(shape):
    return pl.BlockSpec(shape, lambda bi, i: (0,) * len(shape), pipeline_mode=pl.Buffered(1))


def _hyena_out(alpha, x0, cv, v, bias, w, b, h, gate, ln_g, ln_b):
    bsz, seq, d = h.shape
    tm = _tile(seq, 256)
    return pl.pallas_call(
        functools.partial(_hyena_out_kernel, alpha),
        out_shape=jax.ShapeDtypeStruct(h.shape, F32),
        grid=(bsz, seq // tm),
        in_specs=[_row_spec(tm, d), _row_spec(tm, d), _row_spec(tm, d), _vec_spec(d), _const_spec((d, d)),
                  _vec_spec(d), _row_spec(tm, d), _bvec_spec(d), _vec_spec(d), _vec_spec(d)],
        out_specs=_row_spec(tm, d),
        compiler_params=_params(("parallel", "parallel")),
        name="hyena_out_norm",
    )(x0, cv, v, bias.reshape(1, d), w.astype(MXU_DTYPE), b.reshape(1, d), h, gate, ln_g.reshape(1, d), ln_b.reshape(1, d))


def _attn_out(alpha, o, w, h, gate, ln_g, ln_b):
    bsz, seq, d = h.shape
    k = o.shape[2]
    tm = _tile(seq, 512)
    return pl.pallas_call(
        functools.partial(_attn_out_kernel, alpha),
        out_shape=jax.ShapeDtypeStruct(h.shape, F32),
        grid=(bsz, seq // tm),
        in_specs=[_row_spec(tm, k), _const_spec((k, d)), _row_spec(tm, d), _bvec_spec(d), _vec_spec(d), _vec_spec(d)],
        out_specs=_row_spec(tm, d),
        compiler_params=_params(("parallel", "parallel")),
        name="attn_out_norm",
    )(o, w.astype(MXU_DTYPE), h, gate, ln_g.reshape(1, d), ln_b.reshape(1, d))


def _ffn_kernel(alpha, h_ref, sh_ref, sc_ref, g_ref, wg_ref, wu_ref, wd_ref, lg_ref, lb_ref, o_ref, u_scr, acc_scr):
    f = pl.program_id(2)

    @pl.when(f == 0)
    def _():
        u_scr[...] = (h_ref[0] * (1.0 + sc_ref[0]) + sh_ref[0]).astype(MXU_DTYPE)
        acc_scr[...] = jnp.zeros_like(acc_scr)

    u = u_scr[...]
    gate = _dot(u, wg_ref[...])
    up = _dot(u, wu_ref[...])
    act = (_silu(gate) * up).astype(MXU_DTYPE)
    acc_scr[...] += _dot(act, wd_ref[...])

    @pl.when(f == pl.num_programs(2) - 1)
    def _():
        o_ref[0] = _layer_norm_rows(alpha * h_ref[0] + g_ref[0] * acc_scr[...], lg_ref[...], lb_ref[...])


def _ffn(alpha, h, shift, scale, gate, w_gate, w_up, w_down, ln_g, ln_b):
    bsz, seq, d = h.shape
    ff = w_gate.shape[1]
    tm = _tile(seq, 512)
    tf = _tile(ff, 512)
    return pl.pallas_call(
        functools.partial(_ffn_kernel, alpha),
        out_shape=jax.ShapeDtypeStruct(h.shape, F32),
        grid=(bsz, seq // tm, ff // tf),
        in_specs=[
            pl.BlockSpec((1, tm, d), lambda bi, i, f: (bi, i, 0)),
            pl.BlockSpec((1, 1, d), lambda bi, i, f: (bi, 0, 0)),
            pl.BlockSpec((1, 1, d), lambda bi, i, f: (bi, 0, 0)),
            pl.BlockSpec((1, 1, d), lambda bi, i, f: (bi, 0, 0)),
            pl.BlockSpec((d, tf), lambda bi, i, f: (0, f)),
            pl.BlockSpec((d, tf), lambda bi, i, f: (0, f)),
            pl.BlockSpec((tf, d), lambda bi, i, f: (f, 0)),
            pl.BlockSpec((1, d), lambda bi, i, f: (0, 0)),
            pl.BlockSpec((1, d), lambda bi, i, f: (0, 0)),
        ],
        out_specs=pl.BlockSpec((1, tm, d), lambda bi, i, f: (bi, i, 0)),
        scratch_shapes=[pltpu.VMEM((tm, d), MXU_DTYPE), pltpu.VMEM((tm, d), F32)],
        compiler_params=_params(("parallel", "parallel", "arbitrary")),
        name="ffn_swiglu_norm",
    )(h, shift, scale, gate, w_gate.astype(MXU_DTYPE), w_up.astype(MXU_DTYPE), w_down.astype(MXU_DTYPE),
      ln_g.reshape(1, d), ln_b.reshape(1, d))


def _rotate_half_cols(w):
    ws = w.reshape(w.shape[:-1] + (2, 2, ROPE_PAIRS))
    return jnp.stack([-ws[..., 1, :], ws[..., 0, :]], axis=-2).reshape(w.shape)


def _rope_table(seq):
    rows = seq // GRID_W
    row = jnp.repeat(jnp.arange(rows, dtype=F32), GRID_W)
    col = jnp.tile(jnp.arange(GRID_W, dtype=F32), rows)
    inv = ROPE_THETA ** (-jnp.arange(ROPE_PAIRS, dtype=F32) / ROPE_PAIRS)
    ang = jnp.stack([row[:, None] * inv, col[:, None] * inv], axis=1)
    ang = jnp.broadcast_to(ang[:, :, None, :], (seq, 2, 2, ROPE_PAIRS)).reshape(seq, QK_ROPE)
    return jnp.concatenate([jnp.cos(ang), jnp.sin(ang)], axis=-1)


def _rms_rows(x, g):
    return x * lax.rsqrt(jnp.mean(x * x, axis=-1, keepdims=True) + RMS_EPS) * g


def _rope_pair(x, cs):
    t = x * cs
    return t + pltpu.roll(t, QK_ROPE, 1)


def _mla_q_kernel(cq_ref, g_ref, w_ref, cs_ref, q_ref):
    xn = _rms_rows(cq_ref[0], g_ref[...]).astype(MXU_DTYPE)
    cs = cs_ref[...]

    def head(h, carry):
        a = _dot(xn, w_ref[h])
        r = _rope_pair(a[:, QK_NOPE:], cs)
        q_ref[0, h] = (jnp.concatenate([a[:, :QK_NOPE], r], axis=1) * QUERY_SCALE).astype(q_ref.dtype)
        return carry

    lax.fori_loop(0, w_ref.shape[0], head, 0, unroll=2)


def _mla_queries(t, q_norm, wq_b, cs, heads):
    bsz, seq, _ = t.shape
    rank = q_norm.shape[0]
    tm = _tile(seq, 512)
    w = wq_b.reshape(rank, heads, QK_NOPE + QK_ROPE)
    w = jnp.concatenate([w, _rotate_half_cols(w[..., QK_NOPE:])], axis=-1).transpose(1, 0, 2).astype(MXU_DTYPE)
    return pl.pallas_call(
        _mla_q_kernel,
        out_shape=jax.ShapeDtypeStruct((bsz, heads, seq, 2 * LANES), MXU_DTYPE),
        grid=(bsz, seq // tm),
        in_specs=[
            pl.BlockSpec((1, tm, rank), lambda bi, i: (bi, i, 0)),
            pl.BlockSpec((1, rank), lambda bi, i: (0, 0)),
            pl.BlockSpec((heads, rank, 2 * LANES), lambda bi, i: (0, 0, 0)),
            pl.BlockSpec((tm, LANES), lambda bi, i: (i, 0)),
        ],
        out_specs=pl.BlockSpec((1, heads, tm, 2 * LANES), lambda bi, i: (bi, 0, i, 0)),
        compiler_params=_params(("parallel", "parallel")),
        name="mla_queries",
    )(t, q_norm.reshape(1, rank), w, cs)


def _mla_kv_kernel(use_rope, ckv_ref, kr_ref, g_ref, w_ref, cs_ref, k_ref, vt_ref):
    xn = _rms_rows(ckv_ref[0], g_ref[...]).astype(MXU_DTYPE)
    kr = kr_ref[0]
    if use_rope:
        kr = _rope_pair(kr, cs_ref[...])
    lane = lax.broadcasted_iota(jnp.int32, kr.shape, 1)
    kr = jnp.where(lane < QK_ROPE, kr, 0.0).astype(k_ref.dtype)
    ones = jnp.ones((V_ROWS - V_HEAD, xn.shape[0]), vt_ref.dtype)

    def head(h, carry):
        kv = _dot(xn, w_ref[h])
        k_ref[0, h] = jnp.concatenate([kv[:, :QK_NOPE].astype(k_ref.dtype), kr], axis=1)
        vt_ref[0, h] = jnp.concatenate([kv[:, QK_NOPE:].T.astype(vt_ref.dtype), ones], axis=0)
        return carry

    lax.fori_loop(0, w_ref.shape[0], head, 0, unroll=2)


def _mla_keys_values(t, kv_norm, wkv_b, cs, heads, use_rope):
    bsz, seq, _ = t.shape
    rank = kv_norm.shape[0]
    tm = _tile(seq, 512)
    w = wkv_b.reshape(rank, heads, QK_NOPE + V_HEAD).transpose(1, 0, 2).astype(MXU_DTYPE)
    return pl.pallas_call(
        functools.partial(_mla_kv_kernel, use_rope),
        out_shape=(jax.ShapeDtypeStruct((bsz, heads, seq, 2 * LANES), MXU_DTYPE),
                   jax.ShapeDtypeStruct((bsz, heads, V_ROWS, seq), MXU_DTYPE)),
        grid=(bsz, seq // tm),
        in_specs=[
            pl.BlockSpec((1, tm, rank), lambda bi, i: (bi, i, 1)),
            pl.BlockSpec((1, tm, LANES), lambda bi, i: (bi, i, 2 * rank // LANES)),
            pl.BlockSpec((1, rank), lambda bi, i: (0, 0)),
            pl.BlockSpec((heads, rank, QK_NOPE + V_HEAD), lambda bi, i: (0, 0, 0)),
            pl.BlockSpec((tm, LANES), lambda bi, i: (i, 0)),
        ],
        out_specs=(pl.BlockSpec((1, heads, tm, 2 * LANES), lambda bi, i: (bi, 0, i, 0)),
                   pl.BlockSpec((1, heads, V_ROWS, tm), lambda bi, i: (bi, 0, 0, i))),
        compiler_params=_params(("parallel", "parallel")),
        name="mla_keys_values",
    )(t, t, kv_norm.reshape(1, rank), w, cs)


def _attn_kernel(q_ref, k_ref, vt_ref, kc_ref, vtc_ref, o_ref, qt_scr, s_scr, p_scr, smax_scr, m_scr, alpha_scr, acc_scr):
    kv = pl.program_id(3)
    n_main = pl.num_programs(3) - 1

    n_chunks, _, cw = qt_scr.shape

    @pl.when(kv == 0)
    def _():
        for c in range(n_chunks):
            qt_scr[c] = q_ref[0, 0, c * cw:(c + 1) * cw, :].T
        m_scr[...] = jnp.full_like(m_scr, -jnp.inf)
        acc_scr[...] = jnp.zeros_like(acc_scr)

    def step(k, vt):
        nk = k.shape[0]

        def scores(c, slot):
            s = _dot(k, qt_scr[c])
            s_scr[slot, :nk, :] = s
            smax_scr[slot] = jnp.max(s, axis=0, keepdims=True)

        def exponent(c, slot):
            m_prev = m_scr[c]
            m_new = jnp.maximum(m_prev, smax_scr[slot])
            alpha_scr[c] = jnp.exp2(m_prev - m_new)
            p_scr[slot, :nk, :] = jnp.exp2((s_scr[slot, :nk, :] - m_new).astype(MXU_DTYPE))
            m_scr[c] = m_new

        def values(c, slot):
            acc_scr[c] = alpha_scr[c] * acc_scr[c] + _dot(vt, p_scr[slot, :nk, :])

        for i in range(n_chunks + 2):
            if i < n_chunks:
                scores(i, i % 2)
            if 1 <= i <= n_chunks:
                exponent(i - 1, (i - 1) % 2)
            if i >= 2:
                values(i - 2, i % 2)

    @pl.when(kv < n_main)
    def _():
        step(k_ref[0, 0], vt_ref[0, 0])

    @pl.when(kv == n_main)
    def _():
        step(kc_ref[0, 0], vtc_ref[0, 0])
        for c in range(n_chunks):
            acc = acc_scr[c]
            o_ref[0, c * cw:(c + 1) * cw, :] = (acc[:V_HEAD] / acc[V_HEAD:V_HEAD + 1]).T.astype(o_ref.dtype)


def _mla_attend(q, k, vt, kc, vtc):
    bsz, heads, seq, dq = q.shape
    lc = kc.shape[2]
    tq = _tile(seq, 2048)
    tk = _tile(seq, 2048)
    cw = min(tq, ATTN_CHUNK)
    assert lc <= tk
    n_main = seq // tk
    return pl.pallas_call(
        _attn_kernel,
        out_shape=jax.ShapeDtypeStruct((bsz, seq, heads * V_HEAD), MXU_DTYPE),
        grid=(bsz, heads, seq // tq, n_main + 1),
        in_specs=[
            pl.BlockSpec((1, 1, tq, dq), lambda bi, h, i, j: (bi, h, i, 0)),
            pl.BlockSpec((1, 1, tk, dq), lambda bi, h, i, j: (bi, h, jnp.minimum(j, n_main - 1), 0)),
            pl.BlockSpec((1, 1, V_ROWS, tk), lambda bi, h, i, j: (bi, h, 0, jnp.minimum(j, n_main - 1))),
            pl.BlockSpec((1, 1, lc, dq), lambda bi, h, i, j: (bi, h, 0, 0)),
            pl.BlockSpec((1, 1, V_ROWS, lc), lambda bi, h, i, j: (bi, h, 0, 0)),
        ],
        out_specs=pl.BlockSpec((1, tq, V_HEAD), lambda bi, h, i, j: (bi, i, h)),
        scratch_shapes=[pltpu.VMEM((tq // cw, dq, cw), MXU_DTYPE), pltpu.VMEM((2, tk, cw), F32),
                        pltpu.VMEM((2, tk, cw), MXU_DTYPE), pltpu.VMEM((2, 1, cw), F32),
                        pltpu.VMEM((tq // cw, 1, cw), F32), pltpu.VMEM((tq // cw, 1, cw), F32), pltpu.VMEM((tq // cw, V_ROWS, cw), F32)],
        compiler_params=_params(("parallel", "parallel", "parallel", "arbitrary")),
        name="mla_attention",
    )(q, k, vt, kc, vtc)


def _pool_kernel(alpha, seq, h_ref, prev_ref, next_ref, sh_ref, sc_ref, g_ref, w_ref, ps_ref, lg_ref, lb_ref, o_ref):
    i = pl.program_id(1)
    last = pl.num_programs(1) - 1
    tm = h_ref.shape[1]
    d = h_ref.shape[2]
    groups = len(POOL_WINDOWS)
    ch = d // groups
    ext = tm + 2 * POOL_HALO
    shift, scale = sh_ref[0], sc_ref[0]
    h = h_ref[0]
    u = h * (1.0 + scale) + shift
    u_prev = jnp.where(i == 0, 0.0, prev_ref[0] * (1.0 + scale) + shift)
    u_next = jnp.where(i == last, 0.0, next_ref[0] * (1.0 + scale) + shift)
    e = jnp.concatenate([u_prev, u, u_next], axis=0)
    t = i * tm + lax.broadcasted_iota(jnp.int32, (tm, 1), 0)
    ys = []
    for g, win in enumerate(POOL_WINDOWS):
        a = e[:, g * ch:(g + 1) * ch]
        span = 1
        while span < win:
            a = a + pltpu.roll(a, ext - span, 0)
            span *= 2
        half = win // 2
        a = pltpu.roll(a, half, 0)
        s = a[POOL_HALO:POOL_HALO + tm]
        cnt = (jnp.minimum(t + half, seq) - jnp.maximum(t - half, 0)).astype(F32)
        dg = s / cnt - u[:, g * ch:(g + 1) * ch]
        ys.append(_dot(dg.astype(MXU_DTYPE), w_ref[g]))
    y = jnp.concatenate(ys, axis=1) * ps_ref[...]
    o_ref[0] = _layer_norm_rows(alpha * h + g_ref[0] * y, lg_ref[...], lb_ref[...])


def _pool_mixer(alpha, h, shift, scale, gate, w_grp, pool_scale, ln_g, ln_b):
    bsz, seq, d = h.shape
    groups, ch, _ = w_grp.shape
    tm = _tile(seq, 512)
    hb = tm // POOL_HALO
    nhb = seq // POOL_HALO
    return pl.pallas_call(
        functools.partial(_pool_kernel, alpha, seq),
        out_shape=jax.ShapeDtypeStruct(h.shape, F32),
        grid=(bsz, seq // tm),
        in_specs=[
            _row_spec(tm, d),
            pl.BlockSpec((1, POOL_HALO, d), lambda bi, i: (bi, jnp.maximum(i * hb - 1, 0), 0)),
            pl.BlockSpec((1, POOL_HALO, d), lambda bi, i: (bi, jnp.minimum((i + 1) * hb, nhb - 1), 0)),
            _bvec_spec(d), _bvec_spec(d), _bvec_spec(d),
            pl.BlockSpec((groups, ch, ch), lambda bi, i: (0, 0, 0)),
            _vec_spec(d), _vec_spec(d), _vec_spec(d),
        ],
        out_specs=_row_spec(tm, d),
        compiler_params=_params(("parallel", "parallel")),
        name="pool_mixer_norm",
    )(h, h, h, shift, scale, gate, w_grp.astype(MXU_DTYPE), pool_scale.reshape(1, d), ln_g.reshape(1, d), ln_b.reshape(1, d))


def _hyena_mixer_norm(alpha, h, mod, hy, filt, norm, tables, ln_g, ln_b):
    shift, scale, gate = mod
    w_in, b_in, conv_w, conv_b, bias, w_out, b_out = hy
    x0, v = _hyena_in(h, shift, scale, w_in, b_in, conv_w, conv_b)
    if h.shape[1] <= DFT_SMALL_MAX_L:
        cv = _long_conv_small(v, filt, norm)
    else:
        cv = _long_conv_two_stage(v, filt, norm, tables)
    return _hyena_out(alpha, x0, cv, v, bias, w_out, b_out, h, gate, ln_g, ln_b)


def kernel(x, c, ctx, c_ctx, ada_w, ada_b, ln_g, ln_b, ffn_w_gate, ffn_w_up, ffn_w_down, hy_w_in, hy_b_in, hy_conv_w, hy_conv_b, hy_f_w_in, hy_f_w_hid, hy_f_b, hy_f_freq, hy_f_w_out, hy_bias, hy_w_out, hy_b_out, mla_w_in, mla_q_norm, mla_kv_norm, mla_wq_b, mla_wkv_b, mla_w_out, pool_w, pool_scale):
    bsz, seq, d = x.shape
    depth = ada_w.shape[0]
    assert bsz == 2, "the long convolution packs exactly two batch rows into one complex signal"
    assert ctx.shape[0] == bsz and seq % GRID_W == 0
    alpha = (2.0 * depth) ** 0.25
    heads = d // V_HEAD
    mla_layers = [i for i in range(depth) if i % N_MIXERS == 1]
    last_ctx_read = mla_layers[-1] if mla_layers else -1

    cond = jnp.concatenate([c, jnp.broadcast_to(c_ctx[None], (SUBLANES - bsz, d))], axis=0)
    mods = _ada_mods(cond, ada_w, ada_b)

    def mod_vecs(i, ctx_stream):
        m = jnp.broadcast_to(mods[i, bsz][None], (bsz, 6 * d)) if ctx_stream else mods[i, :bsz]
        return [m[:, None, k * d:(k + 1) * d] for k in range(6)]

    cs = _rope_table(seq)
    tables = _dft_tables(seq) if seq > DFT_SMALL_MAX_L else None

    h, hc = x, ctx
    for i in range(depth):
        kind, j = i % N_MIXERS, i // N_MIXERS
        ctx_update = i < last_ctx_read
        sh1, sc1, g1, sh2, sc2, g2 = mod_vecs(i, False)
        if kind == 1 or ctx_update:
            csh1, csc1, cg1, csh2, csc2, cg2 = mod_vecs(i, True)
        lg, lb = ln_g[i], ln_b[i]
        if kind == 0:
            hy = (hy_w_in[j], hy_b_in[j], hy_conv_w[j], hy_conv_b[j], hy_bias[j], hy_w_out[j], hy_b_out[j])
            fp = (hy_f_w_in[j], hy_f_w_hid[j], hy_f_b[j], hy_f_freq[j], hy_f_w_out[j])
            filt, norm = _implicit_filter(seq, *fp)
            h_mid = _hyena_mixer_norm(alpha, h, (sh1, sc1, g1), hy, filt, norm, tables, lg[0], lb[0])
            if ctx_update:
                filt_c, norm_c = _implicit_filter(hc.shape[1], *fp)
                hc_mid = _hyena_mixer_norm(alpha, hc, (csh1, csc1, cg1), hy, filt_c, norm_c, None, lg[0], lb[0])
        elif kind == 1:
            assert not ctx_update, "context queries are only needed when a later layer reads the context"
            rank = mla_q_norm.shape[1]
            w_in = mla_w_in[j]
            w_in = jnp.concatenate([w_in, _rotate_half_cols(w_in[:, 2 * rank:])], axis=1)
            zeros = jnp.zeros((w_in.shape[1],), F32)
            t = _mod_proj(h, sh1, sc1, w_in, zeros, slabs=1, tm=512, tn=w_in.shape[1])[0]
            tc = _mod_proj(hc, csh1, csc1, w_in, zeros, slabs=1, tm=512, tn=w_in.shape[1])[0]
            q = _mla_queries(t, mla_q_norm[j], mla_wq_b[j], cs, heads)
            k, v = _mla_keys_values(t, mla_kv_norm[j], mla_wkv_b[j], cs, heads, True)
            kc, vc = _mla_keys_values(tc, mla_kv_norm[j], mla_wkv_b[j], cs, heads, False)
            o = _mla_attend(q, k, v, kc, vc)
            h_mid = _attn_out(alpha, o, mla_w_out[j], h, g1, lg[0], lb[0])
        else:
            h_mid = _pool_mixer(alpha, h, sh1, sc1, g1, pool_w[j], pool_scale[j], lg[0], lb[0])
            if ctx_update:
                hc_mid = _pool_mixer(alpha, hc, csh1, csc1, cg1, pool_w[j], pool_scale[j], lg[0], lb[0])
        ffn = (ffn_w_gate[i], ffn_w_up[i], ffn_w_down[i])
        h = _ffn(alpha, h_mid, sh2, sc2, g2, *ffn, lg[1], lb[1])
        if ctx_update:
            hc = _ffn(alpha, hc_mid, csh2, csc2, cg2, *ffn, lg[1], lb[1])
    return h
```

```python
import functools
import math

import jax
import jax.numpy as jnp
from jax import lax
from jax.experimental import pallas as pl
from jax.experimental.pallas import tpu as pltpu

F32 = jnp.float32
MXU_DTYPE = jnp.bfloat16

V7X_VMEM_BYTES = 64 * 1024 * 1024
VMEM_LIMIT = V7X_VMEM_BYTES - 8 * 1024 * 1024
LANES = 128
SUBLANES = 8

N_MIXERS = 3
GRID_W = 64
LN_EPS = 1e-6
RMS_EPS = 1e-6
HY_TARGET = 1e-2
HY_FAST = 0.3
HY_SLOW = 1.5
HY_MIN_DECAY = math.log(HY_TARGET) / HY_SLOW
HY_MAX_DECAY = math.log(HY_TARGET) / HY_FAST
QK_NOPE = 128
QK_ROPE = 64
V_HEAD = 128
ROPE_PAIRS = QK_ROPE // 4
ROPE_THETA = 10000.0
ATTN_SCALE = (QK_NOPE + QK_ROPE) ** -0.5
QUERY_SCALE = ATTN_SCALE * math.log2(math.e)
ATTN_CHUNK = 512
V_ROWS = V_HEAD + 16
POOL_WINDOWS = (2, 4, 8, 16)
POOL_HALO = 8

DFT_N2 = 128
DFT_SMALL_MAX_L = 512
DFT_MID_ROWS = 4


def _params(sem, vmem=VMEM_LIMIT, flags=None):
    return pltpu.CompilerParams(dimension_semantics=sem, vmem_limit_bytes=vmem, flags=flags)


def _tile(n, t):
    t = min(n, t)
    assert n % t == 0, (n, t)
    return t


def _dot(a, b):
    return jnp.dot(a, b, preferred_element_type=F32)


def _layer_norm_rows(x, g, b):
    mu = jnp.mean(x, axis=-1, keepdims=True)
    xc = x - mu
    var = jnp.mean(xc * xc, axis=-1, keepdims=True)
    return xc * lax.rsqrt(var + LN_EPS) * g + b


def _silu(x):
    return x * (1.0 / (1.0 + jnp.exp(-x)))


def _ada_kernel(c_ref, w_ref, b_ref, o_ref):
    c = c_ref[...]
    a = _silu(c).astype(MXU_DTYPE)
    o_ref[0] = _dot(a, w_ref[0].astype(MXU_DTYPE)) + b_ref[0]


def _ada_mods(cond, ada_w, ada_b):
    depth, d, n = ada_w.shape
    rows = cond.shape[0]
    tn = _tile(n, 1024)
    return pl.pallas_call(
        _ada_kernel,
        out_shape=jax.ShapeDtypeStruct((depth, rows, n), F32),
        grid=(depth, n // tn),
        in_specs=[
            pl.BlockSpec((rows, d), lambda i, j: (0, 0)),
            pl.BlockSpec((1, d, tn), lambda i, j: (i, 0, j)),
            pl.BlockSpec((1, 1, tn), lambda i, j: (i, 0, j)),
        ],
        out_specs=pl.BlockSpec((1, rows, tn), lambda i, j: (i, 0, j)),
        compiler_params=_params(("parallel", "parallel")),
        name="ada_mods",
    )(cond, ada_w, ada_b.reshape(depth, 1, n))


def _mod_proj_kernel(x_ref, sh_ref, sc_ref, w_ref, b_ref, o_ref, u_scr):
    @pl.when(pl.program_id(2) == 0)
    def _():
        u_scr[...] = (x_ref[0] * (1.0 + sc_ref[0]) + sh_ref[0]).astype(MXU_DTYPE)

    o_ref[0, 0] = (_dot(u_scr[...], w_ref[...]) + b_ref[...]).astype(o_ref.dtype)


def _mod_proj(x, shift, scale, w, b, slabs, tm, tn):
    bsz, seq, k = x.shape
    n = w.shape[1]
    n_slab = n // slabs
    tm = _tile(seq, tm)
    tn = _tile(n_slab, tn)
    per = n_slab // tn
    return pl.pallas_call(
        _mod_proj_kernel,
        out_shape=jax.ShapeDtypeStruct((slabs, bsz, seq, n_slab), F32),
        grid=(bsz, seq // tm, n // tn),
        in_specs=[
            pl.BlockSpec((1, tm, k), lambda bi, i, j: (bi, i, 0)),
            pl.BlockSpec((1, 1, k), lambda bi, i, j: (bi, 0, 0)),
            pl.BlockSpec((1, 1, k), lambda bi, i, j: (bi, 0, 0)),
            pl.BlockSpec((k, tn), lambda bi, i, j: (0, j)),
            pl.BlockSpec((1, tn), lambda bi, i, j: (0, j)),
        ],
        out_specs=pl.BlockSpec((1, 1, tm, tn), lambda bi, i, j: (j // per, bi, i, j % per)),
        scratch_shapes=[pltpu.VMEM((tm, k), MXU_DTYPE)],
        compiler_params=_params(("parallel", "parallel", "arbitrary")),
        name="mod_proj",
    )(x, shift, scale, w.astype(MXU_DTYPE), b.reshape(1, n))


def _hyena_in_kernel(x_ref, prev_ref, next_ref, sh_ref, sc_ref, w0_ref, w1_ref, w2_ref, b_ref, cw_ref, cb_ref,
                     x0_ref, v_ref, u_scr):
    i = pl.program_id(1)
    last = pl.num_programs(1) - 1
    tm = x_ref.shape[1]
    ext = tm + 2 * SUBLANES

    @pl.when(pl.program_id(2) == 0)
    def _():
        shift, scale = sh_ref[0], sc_ref[0]
        rows = jnp.concatenate([prev_ref[0], x_ref[0], next_ref[0]], axis=0)
        u_scr[...] = (rows * (1.0 + scale) + shift).astype(MXU_DTYPE)

    r = lax.broadcasted_iota(jnp.int32, (ext, 1), 0)
    inside = jnp.logical_and(jnp.logical_or(i > 0, r >= SUBLANES), jnp.logical_or(i < last, r < SUBLANES + tm))
    u = u_scr[...]
    outs = []
    for s, w_ref in enumerate((w0_ref, w1_ref, w2_ref)):
        proj = jnp.where(inside, _dot(u, w_ref[...]) + b_ref[s], 0.0)
        taps = cw_ref[s]
        conv = pltpu.roll(proj, 1, 0) * taps[0:1] + proj * taps[1:2] + pltpu.roll(proj, ext - 1, 0) * taps[2:3]
        outs.append(conv[SUBLANES:SUBLANES + tm] + cb_ref[s])
    x0_ref[0] = outs[0].astype(x0_ref.dtype)
    v_ref[0] = (outs[2] * outs[1]).astype(v_ref.dtype)


def _hyena_in(x, shift, scale, w_in, b_in, conv_w, conv_b):
    bsz, seq, k = x.shape
    d = w_in.shape[1] // 3
    tm = _tile(seq, 512)
    tn = _tile(d, 512)
    per = d // tn
    hb = tm // SUBLANES
    nhb = seq // SUBLANES
    w = w_in.astype(MXU_DTYPE)
    cw = conv_w.reshape(3, 3, d).transpose(1, 0, 2)
    out = jax.ShapeDtypeStruct((bsz, seq, d), MXU_DTYPE)
    slab = lambda s: pl.BlockSpec((k, tn), lambda bi, i, j: (0, s * per + j))
    return pl.pallas_call(
        _hyena_in_kernel,
        out_shape=(out, out),
        grid=(bsz, seq // tm, per),
        in_specs=[
            pl.BlockSpec((1, tm, k), lambda bi, i, j: (bi, i, 0)),
            pl.BlockSpec((1, SUBLANES, k), lambda bi, i, j: (bi, jnp.maximum(i * hb - 1, 0), 0)),
            pl.BlockSpec((1, SUBLANES, k), lambda bi, i, j: (bi, jnp.minimum((i + 1) * hb, nhb - 1), 0)),
            pl.BlockSpec((1, 1, k), lambda bi, i, j: (bi, 0, 0)),
            pl.BlockSpec((1, 1, k), lambda bi, i, j: (bi, 0, 0)),
            slab(0), slab(1), slab(2),
            pl.BlockSpec((3, 1, tn), lambda bi, i, j: (0, 0, j)),
            pl.BlockSpec((3, 3, tn), lambda bi, i, j: (0, 0, j)),
            pl.BlockSpec((3, 1, tn), lambda bi, i, j: (0, 0, j)),
        ],
        out_specs=(
            pl.BlockSpec((1, tm, tn), lambda bi, i, j: (bi, i, j)),
            pl.BlockSpec((1, tm, tn), lambda bi, i, j: (bi, i, j)),
        ),
        scratch_shapes=[pltpu.VMEM((tm + 2 * SUBLANES, k), MXU_DTYPE)],
        compiler_params=_params(("parallel", "parallel", "arbitrary")),
        name="hyena_in_conv_gate",
    )(x, x, x, shift, scale, w, w, w, b_in.reshape(3, 1, d), cw, conv_b.reshape(3, 1, d))


def _filter_kernel(z_ref, dist_ref, w_in_ref, w_hid_ref, b_ref, fr_ref, w_out_ref, delta_ref, f_ref, norm_ref):
    hp = lax.Precision.HIGHEST
    b = b_ref[...]
    fr = fr_ref[...]
    g = jnp.sin(fr[0:1] * (jnp.dot(z_ref[...], w_in_ref[...], precision=hp, preferred_element_type=F32) + b[0:1]))
    g = jnp.sin(fr[1:2] * (jnp.dot(g, w_hid_ref[0], precision=hp, preferred_element_type=F32) + b[1:2]))
    g = jnp.sin(fr[2:3] * (jnp.dot(g, w_hid_ref[1], precision=hp, preferred_element_type=F32) + b[2:3]))
    filt = jnp.dot(g, w_out_ref[...], precision=hp, preferred_element_type=F32)
    filt = filt * jnp.exp(-dist_ref[...] * delta_ref[...])
    f_ref[...] = filt

    @pl.when(pl.program_id(0) == 0)
    def _():
        norm_ref[...] = jnp.zeros_like(norm_ref)

    norm_ref[...] += jnp.sum(jnp.abs(filt), axis=0, keepdims=True)


def _implicit_filter(seq, f_w_in, f_w_hid, f_b, f_freq, f_w_out):
    emb, width = f_w_in.shape
    bands_n = (emb - 1) // 2
    d = f_w_out.shape[1]
    emb_pad = -(-emb // LANES) * LANES
    filt_w = -(-width // LANES) * LANES
    pad_to = lambda a, shape: jnp.pad(a.astype(F32), [(0, s - n) for s, n in zip(shape, a.shape)])
    pos = jnp.arange(seq, dtype=F32)
    t = pos / (seq - 1)
    bands = jnp.linspace(1e-4, bands_n - 1, bands_n, dtype=F32)
    ang = (2.0 * math.pi / seq) * pos[:, None] * bands[None, :]
    z = pad_to(jnp.concatenate([t[:, None], jnp.cos(ang), -jnp.sin(ang)], axis=-1), (seq, emb_pad))
    w_in = pad_to(f_w_in, (emb_pad, filt_w))
    f_w_hid = pad_to(f_w_hid, (2, filt_w, filt_w))
    f_b = pad_to(f_b, (3, filt_w))
    f_freq = pad_to(f_freq, (3, filt_w))
    f_w_out = pad_to(f_w_out, (filt_w, d))
    dist = (jnp.abs(pos - seq // 2) / (seq // 2))[:, None]
    deltas = jnp.abs(jnp.linspace(HY_MIN_DECAY, HY_MAX_DECAY, d, dtype=F32))[None, :]
    tl = _tile(seq, 512)
    full = lambda shape: pl.BlockSpec(shape, lambda i: (0,) * len(shape))
    return pl.pallas_call(
        _filter_kernel,
        out_shape=(jax.ShapeDtypeStruct((seq, d), F32), jax.ShapeDtypeStruct((1, d), F32)),
        grid=(seq // tl,),
        in_specs=[
            pl.BlockSpec((tl, emb_pad), lambda i: (i, 0)),
            pl.BlockSpec((tl, 1), lambda i: (i, 0)),
            full((emb_pad, filt_w)),
            full((2, filt_w, filt_w)),
            full((3, filt_w)),
            full((3, filt_w)),
            full((filt_w, d)),
            full((1, d)),
        ],
        out_specs=(pl.BlockSpec((tl, d), lambda i: (i, 0)), full((1, d))),
        compiler_params=_params(("arbitrary",)),
        name="hyena_filter",
    )(z, dist, w_in, f_w_hid.astype(F32), f_b.astype(F32), f_freq.astype(F32), f_w_out.astype(F32), deltas)


def _real_form(mr, mi):
    top = jnp.concatenate([mr, -mi], axis=-1)
    bot = jnp.concatenate([mi, mr], axis=-1)
    return jnp.concatenate([top, bot], axis=-2)


def _unit_roots(idx, n):
    ang = (2.0 * math.pi / n) * (idx % n).astype(F32)
    return jnp.cos(ang), -jnp.sin(ang)


def _dft_tables(seq):
    n = 2 * seq
    n2 = DFT_N2
    n1 = n // n2
    i32 = jnp.int32
    p = jnp.arange(n1, dtype=i32)
    c = jnp.arange(n2, dtype=i32)
    tr, ti = _unit_roots(c[:, None] * p[None, :], n)
    a_in = jnp.arange(n1 // 2, dtype=i32)
    fr, fi = _unit_roots(p[:, None] * a_in[None, :], n1)
    mr = tr[:, :, None] * fr[None] - ti[:, :, None] * fi[None]
    mi = tr[:, :, None] * fi[None] + ti[:, :, None] * fr[None]
    first = _real_form(mr, mi)
    a_out = jnp.arange(n1 // 4, n1 // 4 + n1 // 2, dtype=i32)
    gr, gi = _unit_roots(p[:, None] * a_out[None, :], n1)
    cr = (tr[:, :, None] * gr[None] - ti[:, :, None] * gi[None]) / n
    ci = -(tr[:, :, None] * gi[None] + ti[:, :, None] * gr[None]) / n
    last = _real_form(jnp.swapaxes(cr, 1, 2), jnp.swapaxes(ci, 1, 2))
    q = jnp.arange(n2, dtype=i32)
    hr, hi_ = _unit_roots(q[:, None] * c[None, :], n2)
    mid_f = _real_form(hr, hi_)
    mid_i = _real_form(hr, -hi_)
    return tuple(t.astype(MXU_DTYPE) for t in (first, mid_f, mid_i, last))


def _dft_rows_kernel(x_ref, m_ref, o_ref):
    o_ref[...] = _dot(m_ref[0], x_ref[...].astype(MXU_DTYPE)).astype(o_ref.dtype)


def _dft_first(x, table, d):
    rows, cols = x.shape
    n_out = table.shape[1]
    bd = _tile(d, 2048)
    per = d // bd
    return pl.pallas_call(
        _dft_rows_kernel,
        out_shape=jax.ShapeDtypeStruct((n_out, cols), MXU_DTYPE),
        grid=(cols // bd,),
        in_specs=[
            pl.BlockSpec((rows, bd), lambda j: (0, j)),
            pl.BlockSpec((1, n_out, rows), lambda j: (j // per, 0, 0)),
        ],
        out_specs=pl.BlockSpec((n_out, bd), lambda j: (0, j)),
        compiler_params=_params(("parallel",)),
        name="hyena_dft_first",
    )(x, table)


def _dft_mid_filter_kernel(a_ref, f_ref, norm_ref, o_ref):
    n2 = a_ref.shape[2]
    for p in range(a_ref.shape[1]):
        a = jnp.concatenate([a_ref[0, p], a_ref[1, p]], axis=0)
        z = _dot(f_ref[...], a) / norm_ref[...]
        o_ref[0, p] = z[:n2].astype(o_ref.dtype)
        o_ref[1, p] = z[n2:].astype(o_ref.dtype)


def _dft_mid_conv_kernel(a_ref, h_ref, f_ref, g_ref, o_ref):
    n2 = a_ref.shape[2]
    for p in range(a_ref.shape[1]):
        a = jnp.concatenate([a_ref[0, p], a_ref[1, p]], axis=0)
        z = _dot(f_ref[...], a)
        zr, zi = z[:n2], z[n2:]
        hr, hi_ = h_ref[0, p].astype(F32), h_ref[1, p].astype(F32)
        w = jnp.concatenate([zr * hr - zi * hi_, zr * hi_ + zi * hr], axis=0)
        y = _dot(g_ref[...], w.astype(MXU_DTYPE))
        o_ref[0, p] = y[:n2].astype(o_ref.dtype)
        o_ref[1, p] = y[n2:].astype(o_ref.dtype)


def _dft_mid(a, tables, d, *, spectrum=None, norm=None):
    n2 = DFT_N2
    n1 = a.shape[0] // 2
    a4 = a.reshape(2, n1, n2, d)
    bd = _tile(d, 2048)
    pb = _tile(n1, DFT_MID_ROWS)
    blk = pl.BlockSpec((2, pb, n2, bd), lambda p, j: (0, p, 0, j))
    mat = pl.BlockSpec((2 * n2, 2 * n2), lambda p, j: (0, 0))
    fwd, inv = tables
    if spectrum is None:
        kern, ins, specs = _dft_mid_filter_kernel, (a4, fwd, norm), [blk, mat, pl.BlockSpec((1, bd), lambda p, j: (0, j))]
    else:
        kern, ins, specs = _dft_mid_conv_kernel, (a4, spectrum, fwd, inv), [blk, blk, mat, mat]
    out = pl.pallas_call(
        kern,
        out_shape=jax.ShapeDtypeStruct((2, n1, n2, d), MXU_DTYPE),
        grid=(n1 // pb, d // bd),
        in_specs=specs,
        out_specs=blk,
        compiler_params=_params(("parallel", "parallel")),
        name="hyena_dft_mid",
    )(*ins)
    return out


def _dft_last(b, table, d):
    rows, cols = b.shape
    n1 = rows // 2
    bd = _tile(d, 2048)
    per = d // bd
    return pl.pallas_call(
        _dft_rows_kernel,
        out_shape=jax.ShapeDtypeStruct((n1, cols), MXU_DTYPE),
        grid=(cols // bd,),
        in_specs=[
            pl.BlockSpec((rows, bd), lambda j: (0, j)),
            pl.BlockSpec((1, n1, rows), lambda j: (j // per, 0, 0)),
        ],
        out_specs=pl.BlockSpec((n1, bd), lambda j: (0, j)),
        compiler_params=_params(("parallel",)),
        name="hyena_dft_last",
    )(b, table)


def _long_conv_two_stage(v, filt, norm, tables):
    bsz, seq, d = v.shape
    n1 = 2 * seq // DFT_N2
    first, mid_f, mid_i, last = tables
    cols = DFT_N2 * d
    h_first = _dft_first(filt.reshape(n1 // 2, cols), first[:, :, :n1 // 2], d)
    spectrum = _dft_mid(h_first, (mid_f, mid_i), d, norm=norm)
    a = _dft_first(v.reshape(n1, cols), first, d)
    bmat = _dft_mid(a, (mid_f, mid_i), d, spectrum=spectrum)
    y = _dft_last(bmat.reshape(2 * n1, cols), last, d)
    return y.reshape(bsz, seq, d)


def _dft_small_kernel(v_ref, h_ref, norm_ref, f_ref, g_ref, o_ref):
    seq = v_ref.shape[1]
    n = 2 * seq
    fwd = f_ref[...]
    x = jnp.concatenate([v_ref[0], v_ref[1]], axis=0)
    z = _dot(fwd, x.astype(MXU_DTYPE))
    hx = jnp.concatenate([h_ref[...] / norm_ref[...], jnp.zeros_like(h_ref)], axis=0)
    hs = _dot(fwd, hx.astype(MXU_DTYPE))
    zr, zi, hr, hi_ = z[:n], z[n:], hs[:n], hs[n:]
    w = jnp.concatenate([zr * hr - zi * hi_, zr * hi_ + zi * hr], axis=0)
    y = _dot(g_ref[...], w.astype(MXU_DTYPE))
    o_ref[0] = y[:seq].astype(o_ref.dtype)
    o_ref[1] = y[seq:].astype(o_ref.dtype)


def _long_conv_small(v, filt, norm):
    bsz, seq, d = v.shape
    n = 2 * seq
    k = jnp.arange(n, dtype=jnp.int32)
    t_in = jnp.arange(seq, dtype=jnp.int32)
    fr, fi = _unit_roots(k[:, None] * t_in[None, :], n)
    fwd = _real_form(fr, fi)
    t_out = jnp.arange(seq // 2, seq // 2 + seq, dtype=jnp.int32)
    gr, gi = _unit_roots(t_out[:, None] * k[None, :], n)
    inv = _real_form(gr / n, -gi / n)
    bd = _tile(d, 512)
    full = lambda shape: pl.BlockSpec(shape, lambda j: (0,) * len(shape))
    return pl.pallas_call(
        _dft_small_kernel,
        out_shape=jax.ShapeDtypeStruct((bsz, seq, d), MXU_DTYPE),
        grid=(d // bd,),
        in_specs=[
            pl.BlockSpec((bsz, seq, bd), lambda j: (0, 0, j)),
            pl.BlockSpec((seq, bd), lambda j: (0, j)),
            pl.BlockSpec((1, bd), lambda j: (0, j)),
            full(fwd.shape), full(inv.shape),
        ],
        out_specs=pl.BlockSpec((bsz, seq, bd), lambda j: (0, 0, j)),
        compiler_params=_params(("parallel",)),
        name="hyena_dft_small",
    )(v, filt, norm, fwd.astype(MXU_DTYPE), inv.astype(MXU_DTYPE))


def _hyena_out_kernel(alpha, x0_ref, cv_ref, v_ref, bias_ref, w_ref, b_ref, h_ref, g_ref, lg_ref, lb_ref, o_ref):
    y = x0_ref[0].astype(F32) * (cv_ref[0].astype(F32) + v_ref[0].astype(F32) * bias_ref[...])
    y = _dot(y.astype(MXU_DTYPE), w_ref[...]) + b_ref[...]
    o_ref[0] = _layer_norm_rows(alpha * h_ref[0] + g_ref[0] * y, lg_ref[...], lb_ref[...])


def _attn_out_kernel(alpha, o_in_ref, w_ref, h_ref, g_ref, lg_ref, lb_ref, o_ref):
    y = _dot(o_in_ref[0], w_ref[...])
    o_ref[0] = _layer_norm_rows(alpha * h_ref[0] + g_ref[0] * y, lg_ref[...], lb_ref[...])


def _row_spec(tm, d):
    return pl.BlockSpec((1, tm, d), lambda bi, i: (bi, i, 0))


def _vec_spec(d):
    return pl.BlockSpec((1, d), lambda bi, i: (0, 0))


def _bvec_spec(d):
    return pl.BlockSpec((1, 1, d), lambda bi, i: (bi, 0, 0))


def _const_spec(shape):
    return pl.BlockSpec(shape, lambda bi, i: (0,) * len(shape), pipeline_mode=pl.Buffered(1))


def _hyena_out(alpha, x0, cv, v, bias, w, b, h, gate, ln_g, ln_b):
    bsz, seq, d = h.shape
    tm = _tile(seq, 512)
    return pl.pallas_call(
        functools.partial(_hyena_out_kernel, alpha),
        out_shape=jax.ShapeDtypeStruct(h.shape, F32),
        grid=(bsz, seq // tm),
        in_specs=[_row_spec(tm, d), _row_spec(tm, d), _row_spec(tm, d), _vec_spec(d), _const_spec((d, d)),
                  _vec_spec(d), _row_spec(tm, d), _bvec_spec(d), _vec_spec(d), _vec_spec(d)],
        out_specs=_row_spec(tm, d),
        compiler_params=_params(("parallel", "parallel")),
        name="hyena_out_norm",
    )(x0, cv, v, bias.reshape(1, d), w.astype(MXU_DTYPE), b.reshape(1, d), h, gate, ln_g.reshape(1, d), ln_b.reshape(1, d))


def _attn_out(alpha, o, w, h, gate, ln_g, ln_b):
    bsz, seq, d = h.shape
    k = o.shape[2]
    tm = _tile(seq, 512)
    return pl.pallas_call(
        functools.partial(_attn_out_kernel, alpha),
        out_shape=jax.ShapeDtypeStruct(h.shape, F32),
        grid=(bsz, seq // tm),
        in_specs=[_row_spec(tm, k), _const_spec((k, d)), _row_spec(tm, d), _bvec_spec(d), _vec_spec(d), _vec_spec(d)],
        out_specs=_row_spec(tm, d),
        compiler_params=_params(("parallel", "parallel")),
        name="attn_out_norm",
    )(o, w.astype(MXU_DTYPE), h, gate, ln_g.reshape(1, d), ln_b.reshape(1, d))


def _ffn_kernel(alpha, h_ref, sh_ref, sc_ref, g_ref, wg_ref, wu_ref, wd_ref, lg_ref, lb_ref, o_ref, u_scr, acc_scr):
    f = pl.program_id(2)

    @pl.when(f == 0)
    def _():
        u_scr[...] = (h_ref[0] * (1.0 + sc_ref[0]) + sh_ref[0]).astype(MXU_DTYPE)
        acc_scr[...] = jnp.zeros_like(acc_scr)

    u = u_scr[...]
    gate = _dot(u, wg_ref[...])
    up = _dot(u, wu_ref[...])
    act = (_silu(gate) * up).astype(MXU_DTYPE)
    acc_scr[...] += _dot(act, wd_ref[...])

    @pl.when(f == pl.num_programs(2) - 1)
    def _():
        o_ref[0] = _layer_norm_rows(alpha * h_ref[0] + g_ref[0] * acc_scr[...], lg_ref[...], lb_ref[...])


def _ffn(alpha, h, shift, scale, gate, w_gate, w_up, w_down, ln_g, ln_b):
    bsz, seq, d = h.shape
    ff = w_gate.shape[1]
    tm = _tile(seq, 512)
    tf = _tile(ff, 512)
    return pl.pallas_call(
        functools.partial(_ffn_kernel, alpha),
        out_shape=jax.ShapeDtypeStruct(h.shape, F32),
        grid=(bsz, seq // tm, ff // tf),
        in_specs=[
            pl.BlockSpec((1, tm, d), lambda bi, i, f: (bi, i, 0)),
            pl.BlockSpec((1, 1, d), lambda bi, i, f: (bi, 0, 0)),
            pl.BlockSpec((1, 1, d), lambda bi, i, f: (bi, 0, 0)),
            pl.BlockSpec((1, 1, d), lambda bi, i, f: (bi, 0, 0)),
            pl.BlockSpec((d, tf), lambda bi, i, f: (0, f)),
            pl.BlockSpec((d, tf), lambda bi, i, f: (0, f)),
            pl.BlockSpec((tf, d), lambda bi, i, f: (f, 0)),
            pl.BlockSpec((1, d), lambda bi, i, f: (0, 0)),
            pl.BlockSpec((1, d), lambda bi, i, f: (0, 0)),
        ],
        out_specs=pl.BlockSpec((1, tm, d), lambda bi, i, f: (bi, i, 0)),
        scratch_shapes=[pltpu.VMEM((tm, d), MXU_DTYPE), pltpu.VMEM((tm, d), F32)],
        compiler_params=_params(("parallel", "parallel", "arbitrary")),
        name="ffn_swiglu_norm",
    )(h, shift, scale, gate, w_gate.astype(MXU_DTYPE), w_up.astype(MXU_DTYPE), w_down.astype(MXU_DTYPE),
      ln_g.reshape(1, d), ln_b.reshape(1, d))


def _rotate_half_cols(w):
    ws = w.reshape(w.shape[:-1] + (2, 2, ROPE_PAIRS))
    return jnp.stack([-ws[..., 1, :], ws[..., 0, :]], axis=-2).reshape(w.shape)


def _rope_table(seq):
    rows = seq // GRID_W
    row = jnp.repeat(jnp.arange(rows, dtype=F32), GRID_W)
    col = jnp.tile(jnp.arange(GRID_W, dtype=F32), rows)
    inv = ROPE_THETA ** (-jnp.arange(ROPE_PAIRS, dtype=F32) / ROPE_PAIRS)
    ang = jnp.stack([row[:, None] * inv, col[:, None] * inv], axis=1)
    ang = jnp.broadcast_to(ang[:, :, None, :], (seq, 2, 2, ROPE_PAIRS)).reshape(seq, QK_ROPE)
    return jnp.concatenate([jnp.cos(ang), jnp.sin(ang)], axis=-1)


def _rms_rows(x, g):
    return x * lax.rsqrt(jnp.mean(x * x, axis=-1, keepdims=True) + RMS_EPS) * g


def _rope_pair(x, cs):
    t = x * cs
    return t + pltpu.roll(t, QK_ROPE, 1)


def _mla_q_kernel(cq_ref, g_ref, w_ref, cs_ref, q_ref):
    xn = _rms_rows(cq_ref[0], g_ref[...]).astype(MXU_DTYPE)
    cs = cs_ref[...]

    def head(h, carry):
        a = _dot(xn, w_ref[h])
        r = _rope_pair(a[:, QK_NOPE:], cs)
        q_ref[0, h] = (jnp.concatenate([a[:, :QK_NOPE], r], axis=1) * QUERY_SCALE).astype(q_ref.dtype)
        return carry

    lax.fori_loop(0, w_ref.shape[0], head, 0, unroll=2)


def _mla_queries(t, q_norm, wq_b, cs, heads):
    bsz, seq, _ = t.shape
    rank = q_norm.shape[0]
    tm = _tile(seq, 512)
    w = wq_b.reshape(rank, heads, QK_NOPE + QK_ROPE)
    w = jnp.concatenate([w, _rotate_half_cols(w[..., QK_NOPE:])], axis=-1).transpose(1, 0, 2).astype(MXU_DTYPE)
    return pl.pallas_call(
        _mla_q_kernel,
        out_shape=jax.ShapeDtypeStruct((bsz, heads, seq, 2 * LANES), MXU_DTYPE),
        grid=(bsz, seq // tm),
        in_specs=[
            pl.BlockSpec((1, tm, rank), lambda bi, i: (bi, i, 0)),
            pl.BlockSpec((1, rank), lambda bi, i: (0, 0)),
            pl.BlockSpec((heads, rank, 2 * LANES), lambda bi, i: (0, 0, 0)),
            pl.BlockSpec((tm, LANES), lambda bi, i: (i, 0)),
        ],
        out_specs=pl.BlockSpec((1, heads, tm, 2 * LANES), lambda bi, i: (bi, 0, i, 0)),
        compiler_params=_params(("parallel", "parallel")),
        name="mla_queries",
    )(t, q_norm.reshape(1, rank), w, cs)


def _mla_kv_kernel(use_rope, ckv_ref, kr_ref, g_ref, w_ref, cs_ref, k_ref, vt_ref):
    xn = _rms_rows(ckv_ref[0], g_ref[...]).astype(MXU_DTYPE)
    kr = kr_ref[0]
    if use_rope:
        kr = _rope_pair(kr, cs_ref[...])
    lane = lax.broadcasted_iota(jnp.int32, kr.shape, 1)
    kr = jnp.where(lane < QK_ROPE, kr, 0.0).astype(k_ref.dtype)
    ones = jnp.ones((V_ROWS - V_HEAD, xn.shape[0]), vt_ref.dtype)

    def head(h, carry):
        kv = _dot(xn, w_ref[h])
        k_ref[0, h] = jnp.concatenate([kv[:, :QK_NOPE].astype(k_ref.dtype), kr], axis=1)
        vt_ref[0, h] = jnp.concatenate([kv[:, QK_NOPE:].T.astype(vt_ref.dtype), ones], axis=0)
        return carry

    lax.fori_loop(0, w_ref.shape[0], head, 0, unroll=2)


def _mla_keys_values(t, kv_norm, wkv_b, cs, heads, use_rope):
    bsz, seq, _ = t.shape
    rank = kv_norm.shape[0]
    tm = _tile(seq, 512)
    w = wkv_b.reshape(rank, heads, QK_NOPE + V_HEAD).transpose(1, 0, 2).astype(MXU_DTYPE)
    return pl.pallas_call(
        functools.partial(_mla_kv_kernel, use_rope),
        out_shape=(jax.ShapeDtypeStruct((bsz, heads, seq, 2 * LANES), MXU_DTYPE),
                   jax.ShapeDtypeStruct((bsz, heads, V_ROWS, seq), MXU_DTYPE)),
        grid=(bsz, seq // tm),
        in_specs=[
            pl.BlockSpec((1, tm, rank), lambda bi, i: (bi, i, 1)),
            pl.BlockSpec((1, tm, LANES), lambda bi, i: (bi, i, 2 * rank // LANES)),
            pl.BlockSpec((1, rank), lambda bi, i: (0, 0)),
            pl.BlockSpec((heads, rank, QK_NOPE + V_HEAD), lambda bi, i: (0, 0, 0)),
            pl.BlockSpec((tm, LANES), lambda bi, i: (i, 0)),
        ],
        out_specs=(pl.BlockSpec((1, heads, tm, 2 * LANES), lambda bi, i: (bi, 0, i, 0)),
                   pl.BlockSpec((1, heads, V_ROWS, tm), lambda bi, i: (bi, 0, 0, i))),
        compiler_params=_params(("parallel", "parallel")),
        name="mla_keys_values",
    )(t, t, kv_norm.reshape(1, rank), w, cs)


def _attn_kernel(q_ref, k_ref, vt_ref, kc_ref, vtc_ref, o_ref, qt_scr, s_scr, p_scr, smax_scr, m_scr, alpha_scr, acc_scr):
    kv = pl.program_id(3)
    n_main = pl.num_programs(3) - 1

    n_chunks, _, cw = qt_scr.shape

    @pl.when(kv == 0)
    def _():
        for c in range(n_chunks):
            qt_scr[c] = q_ref[0, 0, c * cw:(c + 1) * cw, :].T
        m_scr[...] = jnp.full_like(m_scr, -jnp.inf)
        acc_scr[...] = jnp.zeros_like(acc_scr)

    def step(k, vt):
        nk = k.shape[0]

        def scores(c, slot):
            s = _dot(k, qt_scr[c])
            s_scr[slot, :nk, :] = s
            smax_scr[slot] = jnp.max(s, axis=0, keepdims=True)

        def exponent(c, slot):
            m_prev = m_scr[c]
            m_new = jnp.maximum(m_prev, smax_scr[slot])
            alpha_scr[c] = jnp.exp2(m_prev - m_new)
            p_scr[slot, :nk, :] = jnp.exp2((s_scr[slot, :nk, :] - m_new).astype(MXU_DTYPE))
            m_scr[c] = m_new

        def values(c, slot):
            acc_scr[c] = alpha_scr[c] * acc_scr[c] + _dot(vt, p_scr[slot, :nk, :])

        for i in range(n_chunks + 2):
            if i < n_chunks:
                scores(i, i % 2)
            if 1 <= i <= n_chunks:
                exponent(i - 1, (i - 1) % 2)
            if i >= 2:
                values(i - 2, i % 2)

    @pl.when(kv < n_main)
    def _():
        step(k_ref[0, 0], vt_ref[0, 0])

    @pl.when(kv == n_main)
    def _():
        step(kc_ref[0, 0], vtc_ref[0, 0])
        for c in range(n_chunks):
            acc = acc_scr[c]
            o_ref[0, c * cw:(c + 1) * cw, :] = (acc[:V_HEAD] / acc[V_HEAD:V_HEAD + 1]).T.astype(o_ref.dtype)


def _mla_attend(q, k, vt, kc, vtc):
    bsz, heads, seq, dq = q.shape
    lc = kc.shape[2]
    tq = _tile(seq, 2048)
    tk = _tile(seq, 2048)
    cw = min(tq, ATTN_CHUNK)
    assert lc <= tk
    n_main = seq // tk
    return pl.pallas_call(
        _attn_kernel,
        out_shape=jax.ShapeDtypeStruct((bsz, seq, heads * V_HEAD), MXU_DTYPE),
        grid=(bsz, heads, seq // tq, n_main + 1),
        in_specs=[
            pl.BlockSpec((1, 1, tq, dq), lambda bi, h, i, j: (bi, h, i, 0)),
            pl.BlockSpec((1, 1, tk, dq), lambda bi, h, i, j: (bi, h, jnp.minimum(j, n_main - 1), 0)),
            pl.BlockSpec((1, 1, V_ROWS, tk), lambda bi, h, i, j: (bi, h, 0, jnp.minimum(j, n_main - 1))),
            pl.BlockSpec((1, 1, lc, dq), lambda bi, h, i, j: (bi, h, 0, 0)),
            pl.BlockSpec((1, 1, V_ROWS, lc), lambda bi, h, i, j: (bi, h, 0, 0)),
        ],
        out_specs=pl.BlockSpec((1, tq, V_HEAD), lambda bi, h, i, j: (bi, i, h)),
        scratch_shapes=[pltpu.VMEM((tq // cw, dq, cw), MXU_DTYPE), pltpu.VMEM((2, tk, cw), F32),
                        pltpu.VMEM((2, tk, cw), MXU_DTYPE), pltpu.VMEM((2, 1, cw), F32),
                        pltpu.VMEM((tq // cw, 1, cw), F32), pltpu.VMEM((tq // cw, 1, cw), F32), pltpu.VMEM((tq // cw, V_ROWS, cw), F32)],
        compiler_params=_params(("parallel", "parallel", "parallel", "arbitrary")),
        name="mla_attention",
    )(q, k, vt, kc, vtc)


def _pool_kernel(alpha, seq, h_ref, prev_ref, next_ref, sh_ref, sc_ref, g_ref, w_ref, ps_ref, lg_ref, lb_ref, o_ref):
    i = pl.program_id(1)
    last = pl.num_programs(1) - 1
    tm = h_ref.shape[1]
    d = h_ref.shape[2]
    groups = len(POOL_WINDOWS)
    ch = d // groups
    ext = tm + 2 * POOL_HALO
    shift, scale = sh_ref[0], sc_ref[0]
    h = h_ref[0]
    u = h * (1.0 + scale) + shift
    u_prev = jnp.where(i == 0, 0.0, prev_ref[0] * (1.0 + scale) + shift)
    u_next = jnp.where(i == last, 0.0, next_ref[0] * (1.0 + scale) + shift)
    e = jnp.concatenate([u_prev, u, u_next], axis=0)
    t = i * tm + lax.broadcasted_iota(jnp.int32, (tm, 1), 0)
    ys = []
    for g, win in enumerate(POOL_WINDOWS):
        a = e[:, g * ch:(g + 1) * ch]
        span = 1
        while span < win:
            a = a + pltpu.roll(a, ext - span, 0)
            span *= 2
        half = win // 2
        a = pltpu.roll(a, half, 0)
        s = a[POOL_HALO:POOL_HALO + tm]
        cnt = (jnp.minimum(t + half, seq) - jnp.maximum(t - half, 0)).astype(F32)
        dg = s / cnt - u[:, g * ch:(g + 1) * ch]
        ys.append(_dot(dg.astype(MXU_DTYPE), w_ref[g]))
    y = jnp.concatenate(ys, axis=1) * ps_ref[...]
    o_ref[0] = _layer_norm_rows(alpha * h + g_ref[0] * y, lg_ref[...], lb_ref[...])


def _pool_mixer(alpha, h, shift, scale, gate, w_grp, pool_scale, ln_g, ln_b):
    bsz, seq, d = h.shape
    groups, ch, _ = w_grp.shape
    tm = _tile(seq, 512)
    hb = tm // POOL_HALO
    nhb = seq // POOL_HALO
    return pl.pallas_call(
        functools.partial(_pool_kernel, alpha, seq),
        out_shape=jax.ShapeDtypeStruct(h.shape, F32),
        grid=(bsz, seq // tm),
        in_specs=[
            _row_spec(tm, d),
            pl.BlockSpec((1, POOL_HALO, d), lambda bi, i: (bi, jnp.maximum(i * hb - 1, 0), 0)),
            pl.BlockSpec((1, POOL_HALO, d), lambda bi, i: (bi, jnp.minimum((i + 1) * hb, nhb - 1), 0)),
            _bvec_spec(d), _bvec_spec(d), _bvec_spec(d),
            pl.BlockSpec((groups, ch, ch), lambda bi, i: (0, 0, 0)),
            _vec_spec(d), _vec_spec(d), _vec_spec(d),
        ],
        out_specs=_row_spec(tm, d),
        compiler_params=_params(("parallel", "parallel")),
        name="pool_mixer_norm",
    )(h, h, h, shift, scale, gate, w_grp.astype(MXU_DTYPE), pool_scale.reshape(1, d), ln_g.reshape(1, d), ln_b.reshape(1, d))


def _hyena_mixer_norm(alpha, h, mod, hy, filt, norm, tables, ln_g, ln_b):
    shift, scale, gate = mod
    w_in, b_in, conv_w, conv_b, bias, w_out, b_out = hy
    x0, v = _hyena_in(h, shift, scale, w_in, b_in, conv_w, conv_b)
    if h.shape[1] <= DFT_SMALL_MAX_L:
        cv = _long_conv_small(v, filt, norm)
    else:
        cv = _long_conv_two_stage(v, filt, norm, tables)
    return _hyena_out(alpha, x0, cv, v, bias, w_out, b_out, h, gate, ln_g, ln_b)


def kernel(x, c, ctx, c_ctx, ada_w, ada_b, ln_g, ln_b, ffn_w_gate, ffn_w_up, ffn_w_down, hy_w_in, hy_b_in, hy_conv_w, hy_conv_b, hy_f_w_in, hy_f_w_hid, hy_f_b, hy_f_freq, hy_f_w_out, hy_bias, hy_w_out, hy_b_out, mla_w_in, mla_q_norm, mla_kv_norm, mla_wq_b, mla_wkv_b, mla_w_out, pool_w, pool_scale):
    bsz, seq, d = x.shape
    depth = ada_w.shape[0]
    assert bsz == 2, "the long convolution packs exactly two batch rows into one complex signal"
    assert ctx.shape[0] == bsz and seq % GRID_W == 0
    alpha = (2.0 * depth) ** 0.25
    heads = d // V_HEAD
    mla_layers = [i for i in range(depth) if i % N_MIXERS == 1]
    last_ctx_read = mla_layers[-1] if mla_layers else -1

    cond = jnp.concatenate([c, jnp.broadcast_to(c_ctx[None], (SUBLANES - bsz, d))], axis=0)
    mods = _ada_mods(cond, ada_w, ada_b)

    def mod_vecs(i, ctx_stream):
        m = jnp.broadcast_to(mods[i, bsz][None], (bsz, 6 * d)) if ctx_stream else mods[i, :bsz]
        return [m[:, None, k * d:(k + 1) * d] for k in range(6)]

    cs = _rope_table(seq)
    tables = _dft_tables(seq) if seq > DFT_SMALL_MAX_L else None

    h, hc = x, ctx
    for i in range(depth):
        kind, j = i % N_MIXERS, i // N_MIXERS
        ctx_update = i < last_ctx_read
        sh1, sc1, g1, sh2, sc2, g2 = mod_vecs(i, False)
        if kind == 1 or ctx_update:
            csh1, csc1, cg1, csh2, csc2, cg2 = mod_vecs(i, True)
        lg, lb = ln_g[i], ln_b[i]
        if kind == 0:
            hy = (hy_w_in[j], hy_b_in[j], hy_conv_w[j], hy_conv_b[j], hy_bias[j], hy_w_out[j], hy_b_out[j])
            fp = (hy_f_w_in[j], hy_f_w_hid[j], hy_f_b[j], hy_f_freq[j], hy_f_w_out[j])
            filt, norm = _implicit_filter(seq, *fp)
            h_mid = _hyena_mixer_norm(alpha, h, (sh1, sc1, g1), hy, filt, norm, tables, lg[0], lb[0])
            if ctx_update:
                filt_c, norm_c = _implicit_filter(hc.shape[1], *fp)
                hc_mid = _hyena_mixer_norm(alpha, hc, (csh1, csc1, cg1), hy, filt_c, norm_c, None, lg[0], lb[0])
        elif kind == 1:
            assert not ctx_update, "context queries are only needed when a later layer reads the context"
            rank = mla_q_norm.shape[1]
            w_in = mla_w_in[j]
            w_in = jnp.concatenate([w_in, _rotate_half_cols(w_in[:, 2 * rank:])], axis=1)
            zeros = jnp.zeros((w_in.shape[1],), F32)
            t = _mod_proj(h, sh1, sc1, w_in, zeros, slabs=1, tm=512, tn=w_in.shape[1])[0]
            tc = _mod_proj(hc, csh1, csc1, w_in, zeros, slabs=1, tm=512, tn=w_in.shape[1])[0]
            q = _mla_queries(t, mla_q_norm[j], mla_wq_b[j], cs, heads)
            k, v = _mla_keys_values(t, mla_kv_norm[j], mla_wkv_b[j], cs, heads, True)
            kc, vc = _mla_keys_values(tc, mla_kv_norm[j], mla_wkv_b[j], cs, heads, False)
            o = _mla_attend(q, k, v, kc, vc)
            h_mid = _attn_out(alpha, o, mla_w_out[j], h, g1, lg[0], lb[0])
        else:
            h_mid = _pool_mixer(alpha, h, sh1, sc1, g1, pool_w[j], pool_scale[j], lg[0], lb[0])
            if ctx_update:
                hc_mid = _pool_mixer(alpha, hc, csh1, csc1, cg1, pool_w[j], pool_scale[j], lg[0], lb[0])
        ffn = (ffn_w_gate[i], ffn_w_up[i], ffn_w_down[i])
        h = _ffn(alpha, h_mid, sh2, sc2, g2, *ffn, lg[1], lb[1])
        if ctx_update:
            hc = _ffn(alpha, hc_mid, csh2, csc2, cg2, *ffn, lg[1], lb[1])
    return h
```

```python
import functools
import math

import jax
import jax.numpy as jnp
from jax import lax
from jax.experimental import pallas as pl
from jax.experimental.pallas import tpu as pltpu

F32 = jnp.float32
MXU_DTYPE = jnp.bfloat16

V7X_VMEM_BYTES = 64 * 1024 * 1024
VMEM_LIMIT = V7X_VMEM_BYTES - 8 * 1024 * 1024
LANES = 128
SUBLANES = 8

N_MIXERS = 3
GRID_W = 64
LN_EPS = 1e-6
RMS_EPS = 1e-6
HY_TARGET = 1e-2
HY_FAST = 0.3
HY_SLOW = 1.5
HY_MIN_DECAY = math.log(HY_TARGET) / HY_SLOW
HY_MAX_DECAY = math.log(HY_TARGET) / HY_FAST
QK_NOPE = 128
QK_ROPE = 64
V_HEAD = 128
ROPE_PAIRS = QK_ROPE // 4
ROPE_THETA = 10000.0
ATTN_SCALE = (QK_NOPE + QK_ROPE) ** -0.5
QUERY_SCALE = ATTN_SCALE * math.log2(math.e)
ATTN_CHUNK = 512
V_ROWS = V_HEAD + 16
POOL_WINDOWS = (2, 4, 8, 16)
POOL_HALO = 8

DFT_N2 = 128
DFT_SMALL_MAX_L = 512
DFT_MID_ROWS = 4


def _params(sem, vmem=VMEM_LIMIT, flags=None):
    return pltpu.CompilerParams(dimension_semantics=sem, vmem_limit_bytes=vmem, flags=flags)


def _tile(n, t):
    t = min(n, t)
    assert n % t == 0, (n, t)
    return t


def _dot(a, b):
    return jnp.dot(a, b, preferred_element_type=F32)


def _layer_norm_rows(x, g, b):
    mu = jnp.mean(x, axis=-1, keepdims=True)
    xc = x - mu
    var = jnp.mean(xc * xc, axis=-1, keepdims=True)
    return xc * lax.rsqrt(var + LN_EPS) * g + b


def _silu(x):
    return x * (1.0 / (1.0 + jnp.exp(-x)))


def _ada_kernel(c_ref, w_ref, b_ref, o_ref):
    c = c_ref[...]
    a = _silu(c).astype(MXU_DTYPE)
    o_ref[0] = _dot(a, w_ref[0].astype(MXU_DTYPE)) + b_ref[0]


def _ada_mods(cond, ada_w, ada_b):
    depth, d, n = ada_w.shape
    rows = cond.shape[0]
    tn = _tile(n, 1024)
    return pl.pallas_call(
        _ada_kernel,
        out_shape=jax.ShapeDtypeStruct((depth, rows, n), F32),
        grid=(depth, n // tn),
        in_specs=[
            pl.BlockSpec((rows, d), lambda i, j: (0, 0)),
            pl.BlockSpec((1, d, tn), lambda i, j: (i, 0, j)),
            pl.BlockSpec((1, 1, tn), lambda i, j: (i, 0, j)),
        ],
        out_specs=pl.BlockSpec((1, rows, tn), lambda i, j: (i, 0, j)),
        compiler_params=_params(("parallel", "parallel")),
        name="ada_mods",
    )(cond, ada_w, ada_b.reshape(depth, 1, n))


def _mod_proj_kernel(x_ref, sh_ref, sc_ref, w_ref, b_ref, o_ref, u_scr):
    @pl.when(pl.program_id(2) == 0)
    def _():
        u_scr[...] = (x_ref[0] * (1.0 + sc_ref[0]) + sh_ref[0]).astype(MXU_DTYPE)

    o_ref[0, 0] = (_dot(u_scr[...], w_ref[...]) + b_ref[...]).astype(o_ref.dtype)


def _mod_proj(x, shift, scale, w, b, slabs, tm, tn):
    bsz, seq, k = x.shape
    n = w.shape[1]
    n_slab = n // slabs
    tm = _tile(seq, tm)
    tn = _tile(n_slab, tn)
    per = n_slab // tn
    return pl.pallas_call(
        _mod_proj_kernel,
        out_shape=jax.ShapeDtypeStruct((slabs, bsz, seq, n_slab), MXU_DTYPE),
        grid=(bsz, seq // tm, n // tn),
        in_specs=[
            pl.BlockSpec((1, tm, k), lambda bi, i, j: (bi, i, 0)),
            pl.BlockSpec((1, 1, k), lambda bi, i, j: (bi, 0, 0)),
            pl.BlockSpec((1, 1, k), lambda bi, i, j: (bi, 0, 0)),
            pl.BlockSpec((k, tn), lambda bi, i, j: (0, j)),
            pl.BlockSpec((1, tn), lambda bi, i, j: (0, j)),
        ],
        out_specs=pl.BlockSpec((1, 1, tm, tn), lambda bi, i, j: (j // per, bi, i, j % per)),
        scratch_shapes=[pltpu.VMEM((tm, k), MXU_DTYPE)],
        compiler_params=_params(("parallel", "parallel", "arbitrary")),
        name="mod_proj",
    )(x, shift, scale, w.astype(MXU_DTYPE), b.reshape(1, n))


def _hyena_in_kernel(x_ref, prev_ref, next_ref, sh_ref, sc_ref, w0_ref, w1_ref, w2_ref, b_ref, cw_ref, cb_ref,
                     x0_ref, v_ref, u_scr):
    i = pl.program_id(1)
    last = pl.num_programs(1) - 1
    tm = x_ref.shape[1]
    ext = tm + 2 * SUBLANES

    @pl.when(pl.program_id(2) == 0)
    def _():
        shift, scale = sh_ref[0], sc_ref[0]
        rows = jnp.concatenate([prev_ref[0], x_ref[0], next_ref[0]], axis=0)
        u_scr[...] = (rows * (1.0 + scale) + shift).astype(MXU_DTYPE)

    u = u_scr[...]
    outs = []
    for s, w_ref in enumerate((w0_ref, w1_ref, w2_ref)):
        raw = _dot(u, w_ref[...])
        b, taps = b_ref[s], cw_ref[s]
        head = jnp.where(i == 0, -b, raw[:SUBLANES])
        tail = jnp.where(i == last, -b, raw[SUBLANES + tm:])
        raw = jnp.concatenate([head, raw[SUBLANES:SUBLANES + tm], tail], axis=0)
        conv = pltpu.roll(raw, 1, 0) * taps[0:1] + raw * taps[1:2] + pltpu.roll(raw, ext - 1, 0) * taps[2:3]
        const = b * (taps[0:1] + taps[1:2] + taps[2:3]) + cb_ref[s]
        outs.append(conv[SUBLANES:SUBLANES + tm] + const)
    x0_ref[0] = outs[0].astype(x0_ref.dtype)
    v_ref[0] = (outs[2] * outs[1]).astype(v_ref.dtype)


def _hyena_in(x, shift, scale, w_in, b_in, conv_w, conv_b):
    bsz, seq, k = x.shape
    d = w_in.shape[1] // 3
    tm = _tile(seq, 512)
    tn = _tile(d, 512)
    per = d // tn
    hb = tm // SUBLANES
    nhb = seq // SUBLANES
    w = w_in.astype(MXU_DTYPE)
    cw = conv_w.reshape(3, 3, d).transpose(1, 0, 2)
    out = jax.ShapeDtypeStruct((bsz, seq, d), MXU_DTYPE)
    slab = lambda s: pl.BlockSpec((k, tn), lambda bi, i, j: (0, s * per + j))
    return pl.pallas_call(
        _hyena_in_kernel,
        out_shape=(out, out),
        grid=(bsz, seq // tm, per),
        in_specs=[
            pl.BlockSpec((1, tm, k), lambda bi, i, j: (bi, i, 0)),
            pl.BlockSpec((1, SUBLANES, k), lambda bi, i, j: (bi, jnp.maximum(i * hb - 1, 0), 0)),
            pl.BlockSpec((1, SUBLANES, k), lambda bi, i, j: (bi, jnp.minimum((i + 1) * hb, nhb - 1), 0)),
            pl.BlockSpec((1, 1, k), lambda bi, i, j: (bi, 0, 0)),
            pl.BlockSpec((1, 1, k), lambda bi, i, j: (bi, 0, 0)),
            slab(0), slab(1), slab(2),
            pl.BlockSpec((3, 1, tn), lambda bi, i, j: (0, 0, j)),
            pl.BlockSpec((3, 3, tn), lambda bi, i, j: (0, 0, j)),
            pl.BlockSpec((3, 1, tn), lambda bi, i, j: (0, 0, j)),
        ],
        out_specs=(
            pl.BlockSpec((1, tm, tn), lambda bi, i, j: (bi, i, j)),
            pl.BlockSpec((1, tm, tn), lambda bi, i, j: (bi, i, j)),
        ),
        scratch_shapes=[pltpu.VMEM((tm + 2 * SUBLANES, k), MXU_DTYPE)],
        compiler_params=_params(("parallel", "parallel", "arbitrary")),
        name="hyena_in_conv_gate",
    )(x, x, x, shift, scale, w, w, w, b_in.reshape(3, 1, d), cw, conv_b.reshape(3, 1, d))


def _filter_kernel(z_ref, dist_ref, w_in_ref, w_hid_ref, b_ref, fr_ref, w_out_ref, delta_ref, f_ref, norm_ref):
    hp = lax.Precision.HIGHEST
    b = b_ref[...]
    fr = fr_ref[...]
    g = jnp.sin(fr[0:1] * (jnp.dot(z_ref[...], w_in_ref[...], precision=hp, preferred_element_type=F32) + b[0:1]))
    g = jnp.sin(fr[1:2] * (jnp.dot(g, w_hid_ref[0], precision=hp, preferred_element_type=F32) + b[1:2]))
    g = jnp.sin(fr[2:3] * (jnp.dot(g, w_hid_ref[1], precision=hp, preferred_element_type=F32) + b[2:3]))
    filt = _dot(g.astype(MXU_DTYPE), w_out_ref[...])
    filt = filt * jnp.exp(-dist_ref[...] * delta_ref[...])
    f_ref[...] = filt

    @pl.when(pl.program_id(0) == 0)
    def _():
        norm_ref[...] = jnp.zeros_like(norm_ref)

    norm_ref[...] += jnp.sum(jnp.abs(filt), axis=0, keepdims=True)


def _implicit_filter(seq, f_w_in, f_w_hid, f_b, f_freq, f_w_out):
    emb, width = f_w_in.shape
    bands_n = (emb - 1) // 2
    d = f_w_out.shape[1]
    emb_pad = -(-emb // LANES) * LANES
    filt_w = -(-width // LANES) * LANES
    pad_to = lambda a, shape: jnp.pad(a.astype(F32), [(0, s - n) for s, n in zip(shape, a.shape)])
    pos = jnp.arange(seq, dtype=F32)
    t = pos / (seq - 1)
    bands = jnp.linspace(1e-4, bands_n - 1, bands_n, dtype=F32)
    ang = (2.0 * math.pi / seq) * pos[:, None] * bands[None, :]
    z = pad_to(jnp.concatenate([t[:, None], jnp.cos(ang), -jnp.sin(ang)], axis=-1), (seq, emb_pad))
    w_in = pad_to(f_w_in, (emb_pad, filt_w))
    f_w_hid = pad_to(f_w_hid, (2, filt_w, filt_w))
    f_b = pad_to(f_b, (3, filt_w))
    f_freq = pad_to(f_freq, (3, filt_w))
    f_w_out = pad_to(f_w_out, (filt_w, d))
    dist = (jnp.abs(pos - seq // 2) / (seq // 2))[:, None]
    deltas = jnp.abs(jnp.linspace(HY_MIN_DECAY, HY_MAX_DECAY, d, dtype=F32))[None, :]
    tl = _tile(seq, 512)
    full = lambda shape: pl.BlockSpec(shape, lambda i: (0,) * len(shape))
    return pl.pallas_call(
        _filter_kernel,
        out_shape=(jax.ShapeDtypeStruct((seq, d), F32), jax.ShapeDtypeStruct((1, d), F32)),
        grid=(seq // tl,),
        in_specs=[
            pl.BlockSpec((tl, emb_pad), lambda i: (i, 0)),
            pl.BlockSpec((tl, 1), lambda i: (i, 0)),
            full((emb_pad, filt_w)),
            full((2, filt_w, filt_w)),
            full((3, filt_w)),
            full((3, filt_w)),
            full((filt_w, d)),
            full((1, d)),
        ],
        out_specs=(pl.BlockSpec((tl, d), lambda i: (i, 0)), full((1, d))),
        compiler_params=_params(("arbitrary",)),
        name="hyena_filter",
    )(z, dist, w_in, f_w_hid, f_b, f_freq, f_w_out.astype(MXU_DTYPE), deltas)


def _real_form(mr, mi):
    top = jnp.concatenate([mr, -mi], axis=-1)
    bot = jnp.concatenate([mi, mr], axis=-1)
    return jnp.concatenate([top, bot], axis=-2)


def _unit_roots(idx, n):
    ang = (2.0 * math.pi / n) * (idx % n).astype(F32)
    return jnp.cos(ang), -jnp.sin(ang)


def _dft_tables(seq):
    n = 2 * seq
    n2 = DFT_N2
    n1 = n // n2
    i32 = jnp.int32
    p = jnp.arange(n1, dtype=i32)
    c = jnp.arange(n2, dtype=i32)
    tr, ti = _unit_roots(c[:, None] * p[None, :], n)
    a_in = jnp.arange(n1 // 2, dtype=i32)
    fr, fi = _unit_roots(p[:, None] * a_in[None, :], n1)
    mr = tr[:, :, None] * fr[None] - ti[:, :, None] * fi[None]
    mi = tr[:, :, None] * fi[None] + ti[:, :, None] * fr[None]
    first = _real_form(mr, mi)
    a_out = jnp.arange(n1 // 4, n1 // 4 + n1 // 2, dtype=i32)
    gr, gi = _unit_roots(p[:, None] * a_out[None, :], n1)
    cr = (tr[:, :, None] * gr[None] - ti[:, :, None] * gi[None]) / n
    ci = -(tr[:, :, None] * gi[None] + ti[:, :, None] * gr[None]) / n
    last = _real_form(jnp.swapaxes(cr, 1, 2), jnp.swapaxes(ci, 1, 2))
    q = jnp.arange(n2, dtype=i32)
    hr, hi_ = _unit_roots(q[:, None] * c[None, :], n2)
    mid_f = _real_form(hr, hi_)
    mid_i = _real_form(hr, -hi_)
    return tuple(t.astype(MXU_DTYPE) for t in (first, mid_f, mid_i, last))


def _dft_rows_kernel(x_ref, m_ref, o_ref):
    o_ref[...] = _dot(m_ref[0], x_ref[...].astype(MXU_DTYPE)).astype(o_ref.dtype)


def _dft_first(x, table, d):
    rows, cols = x.shape
    n_out = table.shape[1]
    if rows % LANES:
        table = table[:, :, :rows]
    bd = _tile(d, 2048)
    per = d // bd
    return pl.pallas_call(
        _dft_rows_kernel,
        out_shape=jax.ShapeDtypeStruct((n_out, cols), MXU_DTYPE),
        grid=(cols // bd,),
        in_specs=[
            pl.BlockSpec((rows, bd), lambda j: (0, j)),
            pl.BlockSpec((1, n_out, rows), lambda j: (j // per, 0, 0)),
        ],
        out_specs=pl.BlockSpec((n_out, bd), lambda j: (0, j)),
        compiler_params=_params(("parallel",)),
        name="hyena_dft_first",
    )(x, table)


def _dft_mid_filter_kernel(a_ref, f_ref, norm_ref, o_ref):
    n2 = a_ref.shape[2]
    for p in range(a_ref.shape[1]):
        a = jnp.concatenate([a_ref[0, p], a_ref[1, p]], axis=0)
        z = _dot(f_ref[...], a) / norm_ref[...]
        o_ref[0, p] = z[:n2].astype(o_ref.dtype)
        o_ref[1, p] = z[n2:].astype(o_ref.dtype)


def _dft_mid_conv_kernel(a_ref, h_ref, f_ref, g_ref, o_ref):
    n2 = a_ref.shape[2]
    for p in range(a_ref.shape[1]):
        a = jnp.concatenate([a_ref[0, p], a_ref[1, p]], axis=0)
        z = _dot(f_ref[...], a)
        zr, zi = z[:n2], z[n2:]
        hr, hi_ = h_ref[0, p].astype(F32), h_ref[1, p].astype(F32)
        w = jnp.concatenate([zr * hr - zi * hi_, zr * hi_ + zi * hr], axis=0)
        y = _dot(g_ref[...], w.astype(MXU_DTYPE))
        o_ref[0, p] = y[:n2].astype(o_ref.dtype)
        o_ref[1, p] = y[n2:].astype(o_ref.dtype)


def _dft_mid(a, tables, d, *, spectrum=None, norm=None):
    n2 = DFT_N2
    n1 = a.shape[0] // 2
    a4 = a.reshape(2, n1, n2, d)
    bd = _tile(d, 2048)
    pb = _tile(n1, DFT_MID_ROWS)
    blk = pl.BlockSpec((2, pb, n2, bd), lambda p, j: (0, p, 0, j))
    mat = pl.BlockSpec((2 * n2, 2 * n2), lambda p, j: (0, 0))
    fwd, inv = tables
    if spectrum is None:
        kern, ins, specs = _dft_mid_filter_kernel, (a4, fwd, norm), [blk, mat, pl.BlockSpec((1, bd), lambda p, j: (0, j))]
    else:
        kern, ins, specs = _dft_mid_conv_kernel, (a4, spectrum, fwd, inv), [blk, blk, mat, mat]
    out = pl.pallas_call(
        kern,
        out_shape=jax.ShapeDtypeStruct((2, n1, n2, d), MXU_DTYPE),
        grid=(n1 // pb, d // bd),
        in_specs=specs,
        out_specs=blk,
        compiler_params=_params(("parallel", "parallel")),
        name="hyena_dft_mid",
    )(*ins)
    return out


def _dft_last(b, table, d):
    rows, cols = b.shape
    n1 = rows // 2
    bd = _tile(d, 2048)
    per = d // bd
    return pl.pallas_call(
        _dft_rows_kernel,
        out_shape=jax.ShapeDtypeStruct((n1, cols), MXU_DTYPE),
        grid=(cols // bd,),
        in_specs=[
            pl.BlockSpec((rows, bd), lambda j: (0, j)),
            pl.BlockSpec((1, n1, rows), lambda j: (j // per, 0, 0)),
        ],
        out_specs=pl.BlockSpec((n1, bd), lambda j: (0, j)),
        compiler_params=_params(("parallel",)),
        name="hyena_dft_last",
    )(b, table)


def _long_conv_two_stage(v, filt, norm, tables):
    bsz, seq, d = v.shape
    n1 = 2 * seq // DFT_N2
    first, mid_f, mid_i, last = tables
    cols = DFT_N2 * d
    h_first = _dft_first(filt.reshape(n1 // 2, cols), first, d)
    spectrum = _dft_mid(h_first, (mid_f, mid_i), d, norm=norm)
    a = _dft_first(v.reshape(n1, cols), first, d)
    bmat = _dft_mid(a, (mid_f, mid_i), d, spectrum=spectrum)
    y = _dft_last(bmat.reshape(2 * n1, cols), last, d)
    return y.reshape(bsz, seq, d)


def _dft_small_kernel(v_ref, h_ref, norm_ref, f_ref, g_ref, o_ref):
    seq = v_ref.shape[1]
    n = 2 * seq
    fwd = f_ref[...]
    x = jnp.concatenate([v_ref[0], v_ref[1]], axis=0)
    z = _dot(fwd, x.astype(MXU_DTYPE))
    hx = jnp.concatenate([h_ref[...] / norm_ref[...], jnp.zeros_like(h_ref)], axis=0)
    hs = _dot(fwd, hx.astype(MXU_DTYPE))
    zr, zi, hr, hi_ = z[:n], z[n:], hs[:n], hs[n:]
    w = jnp.concatenate([zr * hr - zi * hi_, zr * hi_ + zi * hr], axis=0)
    y = _dot(g_ref[...], w.astype(MXU_DTYPE))
    o_ref[0] = y[:seq].astype(o_ref.dtype)
    o_ref[1] = y[seq:].astype(o_ref.dtype)


def _long_conv_small(v, filt, norm):
    bsz, seq, d = v.shape
    n = 2 * seq
    k = jnp.arange(n, dtype=jnp.int32)
    t_in = jnp.arange(seq, dtype=jnp.int32)
    fr, fi = _unit_roots(k[:, None] * t_in[None, :], n)
    fwd = _real_form(fr, fi)
    t_out = jnp.arange(seq // 2, seq // 2 + seq, dtype=jnp.int32)
    gr, gi = _unit_roots(t_out[:, None] * k[None, :], n)
    inv = _real_form(gr / n, -gi / n)
    bd = _tile(d, 512)
    full = lambda shape: pl.BlockSpec(shape, lambda j: (0,) * len(shape))
    return pl.pallas_call(
        _dft_small_kernel,
        out_shape=jax.ShapeDtypeStruct((bsz, seq, d), MXU_DTYPE),
        grid=(d // bd,),
        in_specs=[
            pl.BlockSpec((bsz, seq, bd), lambda j: (0, 0, j)),
            pl.BlockSpec((seq, bd), lambda j: (0, j)),
            pl.BlockSpec((1, bd), lambda j: (0, j)),
            full(fwd.shape), full(inv.shape),
        ],
        out_specs=pl.BlockSpec((bsz, seq, bd), lambda j: (0, 0, j)),
        compiler_params=_params(("parallel",)),
        name="hyena_dft_small",
    )(v, filt, norm, fwd.astype(MXU_DTYPE), inv.astype(MXU_DTYPE))


def _hyena_out_kernel(alpha, x0_ref, cv_ref, v_ref, bias_ref, w_ref, b_ref, h_ref, g_ref, lg_ref, lb_ref, o_ref):
    y = x0_ref[0].astype(F32) * (cv_ref[0].astype(F32) + v_ref[0].astype(F32) * bias_ref[...])
    y = _dot(y.astype(MXU_DTYPE), w_ref[...]) + b_ref[...]
    o_ref[0] = _layer_norm_rows(alpha * h_ref[0] + g_ref[0] * y, lg_ref[...], lb_ref[...])


def _attn_out_kernel(alpha, o_in_ref, w_ref, h_ref, g_ref, lg_ref, lb_ref, o_ref):
    y = _dot(o_in_ref[0], w_ref[...])
    o_ref[0] = _layer_norm_rows(alpha * h_ref[0] + g_ref[0] * y, lg_ref[...], lb_ref[...])


def _row_spec(tm, d):
    return pl.BlockSpec((1, tm, d), lambda bi, i: (bi, i, 0))


def _vec_spec(d):
    return pl.BlockSpec((1, d), lambda bi, i: (0, 0))


def _bvec_spec(d):
    return pl.BlockSpec((1, 1, d), lambda bi, i: (bi, 0, 0))


def _const_spec(shape):
    return pl.BlockSpec(shape, lambda bi, i: (0,) * len(shape), pipeline_mode=pl.Buffered(1))


def _hyena_out(alpha, x0, cv, v, bias, w, b, h, gate, ln_g, ln_b):
    bsz, seq, d = h.shape
    tm = _tile(seq, 512)
    return pl.pallas_call(
        functools.partial(_hyena_out_kernel, alpha),
        out_shape=jax.ShapeDtypeStruct(h.shape, F32),
        grid=(bsz, seq // tm),
        in_specs=[_row_spec(tm, d), _row_spec(tm, d), _row_spec(tm, d), _vec_spec(d), _const_spec((d, d)),
                  _vec_spec(d), _row_spec(tm, d), _bvec_spec(d), _vec_spec(d), _vec_spec(d)],
        out_specs=_row_spec(tm, d),
        compiler_params=_params(("parallel", "parallel")),
        name="hyena_out_norm",
    )(x0, cv, v, bias.reshape(1, d), w.astype(MXU_DTYPE), b.reshape(1, d), h, gate, ln_g.reshape(1, d), ln_b.reshape(1, d))


def _attn_out(alpha, o, w, h, gate, ln_g, ln_b):
    bsz, seq, d = h.shape
    k = o.shape[2]
    tm = _tile(seq, 512)
    return pl.pallas_call(
        functools.partial(_attn_out_kernel, alpha),
        out_shape=jax.ShapeDtypeStruct(h.shape, F32),
        grid=(bsz, seq // tm),
        in_specs=[_row_spec(tm, k), _const_spec((k, d)), _row_spec(tm, d), _bvec_spec(d), _vec_spec(d), _vec_spec(d)],
        out_specs=_row_spec(tm, d),
        compiler_params=_params(("parallel", "parallel")),
        name="attn_out_norm",
    )(o, w.astype(MXU_DTYPE), h, gate, ln_g.reshape(1, d), ln_b.reshape(1, d))


def _ffn_kernel(alpha, h_ref, sh_ref, sc_ref, g_ref, wg_ref, wu_ref, wd_ref, lg_ref, lb_ref, o_ref, u_scr, acc_scr):
    f = pl.program_id(2)

    @pl.when(f == 0)
    def _():
        u_scr[...] = (h_ref[0] * (1.0 + sc_ref[0]) + sh_ref[0]).astype(MXU_DTYPE)
        acc_scr[...] = jnp.zeros_like(acc_scr)

    u = u_scr[...]
    gate = _dot(u, wg_ref[...])
    up = _dot(u, wu_ref[...])
    act = (_silu(gate) * up).astype(MXU_DTYPE)
    acc_scr[...] += _dot(act, wd_ref[...])

    @pl.when(f == pl.num_programs(2) - 1)
    def _():
        o_ref[0] = _layer_norm_rows(alpha * h_ref[0] + g_ref[0] * acc_scr[...], lg_ref[...], lb_ref[...])


def _ffn(alpha, h, shift, scale, gate, w_gate, w_up, w_down, ln_g, ln_b):
    bsz, seq, d = h.shape
    ff = w_gate.shape[1]
    tm = _tile(seq, 512)
    tf = _tile(ff, 512)
    return pl.pallas_call(
        functools.partial(_ffn_kernel, alpha),
        out_shape=jax.ShapeDtypeStruct(h.shape, F32),
        grid=(bsz, seq // tm, ff // tf),
        in_specs=[
            pl.BlockSpec((1, tm, d), lambda bi, i, f: (bi, i, 0)),
            pl.BlockSpec((1, 1, d), lambda bi, i, f: (bi, 0, 0)),
            pl.BlockSpec((1, 1, d), lambda bi, i, f: (bi, 0, 0)),
            pl.BlockSpec((1, 1, d), lambda bi, i, f: (bi, 0, 0)),
            pl.BlockSpec((d, tf), lambda bi, i, f: (0, f)),
            pl.BlockSpec((d, tf), lambda bi, i, f: (0, f)),
            pl.BlockSpec((tf, d), lambda bi, i, f: (f, 0)),
            pl.BlockSpec((1, d), lambda bi, i, f: (0, 0)),
            pl.BlockSpec((1, d), lambda bi, i, f: (0, 0)),
        ],
        out_specs=pl.BlockSpec((1, tm, d), lambda bi, i, f: (bi, i, 0)),
        scratch_shapes=[pltpu.VMEM((tm, d), MXU_DTYPE), pltpu.VMEM((tm, d), F32)],
        compiler_params=_params(("parallel", "parallel", "arbitrary")),
        name="ffn_swiglu_norm",
    )(h, shift, scale, gate, w_gate.astype(MXU_DTYPE), w_up.astype(MXU_DTYPE), w_down.astype(MXU_DTYPE),
      ln_g.reshape(1, d), ln_b.reshape(1, d))


def _rotate_half_cols(w):
    ws = w.reshape(w.shape[:-1] + (2, 2, ROPE_PAIRS))
    return jnp.stack([-ws[..., 1, :], ws[..., 0, :]], axis=-2).reshape(w.shape)


def _rope_table(seq):
    rows = seq // GRID_W
    row = jnp.repeat(jnp.arange(rows, dtype=F32), GRID_W)
    col = jnp.tile(jnp.arange(GRID_W, dtype=F32), rows)
    inv = ROPE_THETA ** (-jnp.arange(ROPE_PAIRS, dtype=F32) / ROPE_PAIRS)
    ang = jnp.stack([row[:, None] * inv, col[:, None] * inv], axis=1)
    ang = jnp.broadcast_to(ang[:, :, None, :], (seq, 2, 2, ROPE_PAIRS)).reshape(seq, QK_ROPE)
    return jnp.concatenate([jnp.cos(ang), jnp.sin(ang)], axis=-1)


def _rms_rows(x, g):
    return x * lax.rsqrt(jnp.mean(x * x, axis=-1, keepdims=True) + RMS_EPS) * g


def _rope_pair(x, cs):
    t = x * cs
    return t + pltpu.roll(t, QK_ROPE, 1)


def _mla_q_kernel(cq_ref, g_ref, w_ref, cs_ref, q_ref):
    xn = _rms_rows(cq_ref[0].astype(F32), g_ref[...]).astype(MXU_DTYPE)
    cs = cs_ref[...]

    def head(h, carry):
        a = _dot(xn, w_ref[h])
        r = _rope_pair(a[:, QK_NOPE:], cs)
        q_ref[0, h] = (jnp.concatenate([a[:, :QK_NOPE], r], axis=1) * QUERY_SCALE).astype(q_ref.dtype)
        return carry

    lax.fori_loop(0, w_ref.shape[0], head, 0, unroll=2)


def _mla_queries(t, q_norm, wq_b, cs, heads):
    bsz, seq, _ = t.shape
    rank = q_norm.shape[0]
    tm = _tile(seq, 512)
    w = wq_b.reshape(rank, heads, QK_NOPE + QK_ROPE)
    w = jnp.concatenate([w, _rotate_half_cols(w[..., QK_NOPE:])], axis=-1).transpose(1, 0, 2).astype(MXU_DTYPE)
    return pl.pallas_call(
        _mla_q_kernel,
        out_shape=jax.ShapeDtypeStruct((bsz, heads, seq, 2 * LANES), MXU_DTYPE),
        grid=(bsz, seq // tm),
        in_specs=[
            pl.BlockSpec((1, tm, rank), lambda bi, i: (bi, i, 0)),
            pl.BlockSpec((1, rank), lambda bi, i: (0, 0)),
            pl.BlockSpec((heads, rank, 2 * LANES), lambda bi, i: (0, 0, 0)),
            pl.BlockSpec((tm, LANES), lambda bi, i: (i, 0)),
        ],
        out_specs=pl.BlockSpec((1, heads, tm, 2 * LANES), lambda bi, i: (bi, 0, i, 0)),
        compiler_params=_params(("parallel", "parallel")),
        name="mla_queries",
    )(t, q_norm.reshape(1, rank), w, cs)


def _mla_kv_kernel(use_rope, ckv_ref, kr_ref, g_ref, w_ref, cs_ref, k_ref, vt_ref):
    xn = _rms_rows(ckv_ref[0].astype(F32), g_ref[...]).astype(MXU_DTYPE)
    kr = kr_ref[0].astype(F32)
    if use_rope:
        kr = _rope_pair(kr, cs_ref[...])
    lane = lax.broadcasted_iota(jnp.int32, kr.shape, 1)
    kr = jnp.where(lane < QK_ROPE, kr, 0.0).astype(k_ref.dtype)
    ones = jnp.ones((V_ROWS - V_HEAD, xn.shape[0]), vt_ref.dtype)

    def head(h, carry):
        kv = _dot(xn, w_ref[h])
        k_ref[0, h] = jnp.concatenate([kv[:, :QK_NOPE].astype(k_ref.dtype), kr], axis=1)
        vt_ref[0, h] = jnp.concatenate([kv[:, QK_NOPE:].T.astype(vt_ref.dtype), ones], axis=0)
        return carry

    lax.fori_loop(0, w_ref.shape[0], head, 0, unroll=2)


def _mla_keys_values(t, kv_norm, wkv_b, cs, heads, use_rope):
    bsz, seq, _ = t.shape
    rank = kv_norm.shape[0]
    tm = _tile(seq, 512)
    w = wkv_b.reshape(rank, heads, QK_NOPE + V_HEAD).transpose(1, 0, 2).astype(MXU_DTYPE)
    return pl.pallas_call(
        functools.partial(_mla_kv_kernel, use_rope),
        out_shape=(jax.ShapeDtypeStruct((bsz, heads, seq, 2 * LANES), MXU_DTYPE),
                   jax.ShapeDtypeStruct((bsz, heads, V_ROWS, seq), MXU_DTYPE)),
        grid=(bsz, seq // tm),
        in_specs=[
            pl.BlockSpec((1, tm, rank), lambda bi, i: (bi, i, 1)),
            pl.BlockSpec((1, tm, LANES), lambda bi, i: (bi, i, 2 * rank // LANES)),
            pl.BlockSpec((1, rank), lambda bi, i: (0, 0)),
            pl.BlockSpec((heads, rank, QK_NOPE + V_HEAD), lambda bi, i: (0, 0, 0)),
            pl.BlockSpec((tm, LANES), lambda bi, i: (i, 0)),
        ],
        out_specs=(pl.BlockSpec((1, heads, tm, 2 * LANES), lambda bi, i: (bi, 0, i, 0)),
                   pl.BlockSpec((1, heads, V_ROWS, tm), lambda bi, i: (bi, 0, 0, i))),
        compiler_params=_params(("parallel", "parallel")),
        name="mla_keys_values",
    )(t, t, kv_norm.reshape(1, rank), w, cs)


def _attn_kernel(q_ref, k_ref, vt_ref, kc_ref, vtc_ref, o_ref, qt_scr, s_scr, p_scr, smax_scr, m_scr, alpha_scr, acc_scr):
    kv = pl.program_id(3)
    n_main = pl.num_programs(3) - 1

    n_chunks, _, cw = qt_scr.shape

    @pl.when(kv == 0)
    def _():
        for c in range(n_chunks):
            qt_scr[c] = q_ref[0, 0, c * cw:(c + 1) * cw, :].T
        m_scr[...] = jnp.full_like(m_scr, -jnp.inf)
        acc_scr[...] = jnp.zeros_like(acc_scr)

    def step(k, vt):
        nk = k.shape[0]

        def scores(c, slot):
            s = _dot(k, qt_scr[c])
            s_scr[slot, :nk, :] = s
            smax_scr[slot] = jnp.max(s, axis=0, keepdims=True)

        def exponent(c, slot):
            m_prev = m_scr[c]
            m_new = jnp.maximum(m_prev, smax_scr[slot])
            alpha_scr[c] = jnp.exp2(m_prev - m_new)
            p_scr[slot, :nk, :] = jnp.exp2((s_scr[slot, :nk, :] - m_new).astype(MXU_DTYPE))
            m_scr[c] = m_new

        def values(c, slot):
            acc_scr[c] = alpha_scr[c] * acc_scr[c] + _dot(vt, p_scr[slot, :nk, :])

        for i in range(n_chunks + 2):
            if i < n_chunks:
                scores(i, i % 2)
            if 1 <= i <= n_chunks:
                exponent(i - 1, (i - 1) % 2)
            if i >= 2:
                values(i - 2, i % 2)

    @pl.when(kv < n_main)
    def _():
        step(k_ref[0, 0], vt_ref[0, 0])

    @pl.when(kv == n_main)
    def _():
        step(kc_ref[0, 0], vtc_ref[0, 0])
        for c in range(n_chunks):
            acc = acc_scr[c]
            o_ref[0, c * cw:(c + 1) * cw, :] = (acc[:V_HEAD] / acc[V_HEAD:V_HEAD + 1]).T.astype(o_ref.dtype)


def _mla_attend(q, k, vt, kc, vtc):
    bsz, heads, seq, dq = q.shape
    lc = kc.shape[2]
    tq = _tile(seq, 2048)
    tk = _tile(seq, 2048)
    cw = min(tq, ATTN_CHUNK)
    assert lc <= tk
    n_main = seq // tk
    return pl.pallas_call(
        _attn_kernel,
        out_shape=jax.ShapeDtypeStruct((bsz, seq, heads * V_HEAD), MXU_DTYPE),
        grid=(bsz, heads, seq // tq, n_main + 1),
        in_specs=[
            pl.BlockSpec((1, 1, tq, dq), lambda bi, h, i, j: (bi, h, i, 0)),
            pl.BlockSpec((1, 1, tk, dq), lambda bi, h, i, j: (bi, h, jnp.minimum(j, n_main - 1), 0)),
            pl.BlockSpec((1, 1, V_ROWS, tk), lambda bi, h, i, j: (bi, h, 0, jnp.minimum(j, n_main - 1))),
            pl.BlockSpec((1, 1, lc, dq), lambda bi, h, i, j: (bi, h, 0, 0)),
            pl.BlockSpec((1, 1, V_ROWS, lc), lambda bi, h, i, j: (bi, h, 0, 0)),
        ],
        out_specs=pl.BlockSpec((1, tq, V_HEAD), lambda bi, h, i, j: (bi, i, h)),
        scratch_shapes=[pltpu.VMEM((tq // cw, dq, cw), MXU_DTYPE), pltpu.VMEM((2, tk, cw), F32),
                        pltpu.VMEM((2, tk, cw), MXU_DTYPE), pltpu.VMEM((2, 1, cw), F32),
                        pltpu.VMEM((tq // cw, 1, cw), F32), pltpu.VMEM((tq // cw, 1, cw), F32), pltpu.VMEM((tq // cw, V_ROWS, cw), F32)],
        compiler_params=_params(("parallel", "parallel", "parallel", "arbitrary")),
        name="mla_attention",
    )(q, k, vt, kc, vtc)


def _pool_kernel(alpha, seq, h_ref, prev_ref, next_ref, sh_ref, sc_ref, g_ref, w_ref, ps_ref, lg_ref, lb_ref, o_ref):
    i = pl.program_id(1)
    last = pl.num_programs(1) - 1
    tm = h_ref.shape[1]
    d = h_ref.shape[2]
    groups = len(POOL_WINDOWS)
    ch = d // groups
    ext = tm + 2 * POOL_HALO
    shift, scale = sh_ref[0], sc_ref[0]
    h = h_ref[0]
    u = h * (1.0 + scale) + shift
    u_prev = jnp.where(i == 0, 0.0, prev_ref[0] * (1.0 + scale) + shift)
    u_next = jnp.where(i == last, 0.0, next_ref[0] * (1.0 + scale) + shift)
    e = jnp.concatenate([u_prev, u, u_next], axis=0)
    t = i * tm + lax.broadcasted_iota(jnp.int32, (tm, 1), 0)
    ys = []
    for g, win in enumerate(POOL_WINDOWS):
        a = e[:, g * ch:(g + 1) * ch]
        span = 1
        while span < win:
            a = a + pltpu.roll(a, ext - span, 0)
            span *= 2
        half = win // 2
        a = pltpu.roll(a, half, 0)
        s = a[POOL_HALO:POOL_HALO + tm]
        cnt = (jnp.minimum(t + half, seq) - jnp.maximum(t - half, 0)).astype(F32)
        dg = s / cnt - u[:, g * ch:(g + 1) * ch]
        ys.append(_dot(dg.astype(MXU_DTYPE), w_ref[g]))
    y = jnp.concatenate(ys, axis=1) * ps_ref[...]
    o_ref[0] = _layer_norm_rows(alpha * h + g_ref[0] * y, lg_ref[...], lb_ref[...])


def _pool_mixer(alpha, h, shift, scale, gate, w_grp, pool_scale, ln_g, ln_b):
    bsz, seq, d = h.shape
    groups, ch, _ = w_grp.shape
    tm = _tile(seq, 512)
    hb = tm // POOL_HALO
    nhb = seq // POOL_HALO
    return pl.pallas_call(
        functools.partial(_pool_kernel, alpha, seq),
        out_shape=jax.ShapeDtypeStruct(h.shape, F32),
        grid=(bsz, seq // tm),
        in_specs=[
            _row_spec(tm, d),
            pl.BlockSpec((1, POOL_HALO, d), lambda bi, i: (bi, jnp.maximum(i * hb - 1, 0), 0)),
            pl.BlockSpec((1, POOL_HALO, d), lambda bi, i: (bi, jnp.minimum((i + 1) * hb, nhb - 1), 0)),
            _bvec_spec(d), _bvec_spec(d), _bvec_spec(d),
            pl.BlockSpec((groups, ch, ch), lambda bi, i: (0, 0, 0)),
            _vec_spec(d), _vec_spec(d), _vec_spec(d),
        ],
        out_specs=_row_spec(tm, d),
        compiler_params=_params(("parallel", "parallel")),
        name="pool_mixer_norm",
    )(h, h, h, shift, scale, gate, w_grp.astype(MXU_DTYPE), pool_scale.reshape(1, d), ln_g.reshape(1, d), ln_b.reshape(1, d))


def _hyena_mixer_norm(alpha, h, mod, hy, filt, norm, tables, ln_g, ln_b):
    shift, scale, gate = mod
    w_in, b_in, conv_w, conv_b, bias, w_out, b_out = hy
    x0, v = _hyena_in(h, shift, scale, w_in, b_in, conv_w, conv_b)
    if h.shape[1] <= DFT_SMALL_MAX_L:
        cv = _long_conv_small(v, filt, norm)
    else:
        cv = _long_conv_two_stage(v, filt, norm, tables)
    return _hyena_out(alpha, x0, cv, v, bias, w_out, b_out, h, gate, ln_g, ln_b)


def kernel(x, c, ctx, c_ctx, ada_w, ada_b, ln_g, ln_b, ffn_w_gate, ffn_w_up, ffn_w_down, hy_w_in, hy_b_in, hy_conv_w, hy_conv_b, hy_f_w_in, hy_f_w_hid, hy_f_b, hy_f_freq, hy_f_w_out, hy_bias, hy_w_out, hy_b_out, mla_w_in, mla_q_norm, mla_kv_norm, mla_wq_b, mla_wkv_b, mla_w_out, pool_w, pool_scale):
    bsz, seq, d = x.shape
    depth = ada_w.shape[0]
    assert bsz == 2, "the long convolution packs exactly two batch rows into one complex signal"
    assert ctx.shape[0] == bsz and seq % GRID_W == 0
    alpha = (2.0 * depth) ** 0.25
    heads = d // V_HEAD
    mla_layers = [i for i in range(depth) if i % N_MIXERS == 1]
    last_ctx_read = mla_layers[-1] if mla_layers else -1

    cond = jnp.concatenate([c, jnp.broadcast_to(c_ctx[None], (SUBLANES - bsz, d))], axis=0)
    mods = _ada_mods(cond, ada_w, ada_b)

    def mod_vecs(i, ctx_stream):
        m = jnp.broadcast_to(mods[i, bsz][None], (bsz, 6 * d)) if ctx_stream else mods[i, :bsz]
        return [m[:, None, k * d:(k + 1) * d] for k in range(6)]

    cs = _rope_table(seq)
    tables = _dft_tables(seq) if seq > DFT_SMALL_MAX_L else None

    h, hc = x, ctx
    for i in range(depth):
        kind, j = i % N_MIXERS, i // N_MIXERS
        ctx_update = i < last_ctx_read
        sh1, sc1, g1, sh2, sc2, g2 = mod_vecs(i, False)
        if kind == 1 or ctx_update:
            csh1, csc1, cg1, csh2, csc2, cg2 = mod_vecs(i, True)
        lg, lb = ln_g[i], ln_b[i]
        if kind == 0:
            hy = (hy_w_in[j], hy_b_in[j], hy_conv_w[j], hy_conv_b[j], hy_bias[j], hy_w_out[j], hy_b_out[j])
            fp = (hy_f_w_in[j], hy_f_w_hid[j], hy_f_b[j], hy_f_freq[j], hy_f_w_out[j])
            filt, norm = _implicit_filter(seq, *fp)
            h_mid = _hyena_mixer_norm(alpha, h, (sh1, sc1, g1), hy, filt, norm, tables, lg[0], lb[0])
            if ctx_update:
                filt_c, norm_c = _implicit_filter(hc.shape[1], *fp)
                hc_mid = _hyena_mixer_norm(alpha, hc, (csh1, csc1, cg1), hy, filt_c, norm_c, None, lg[0], lb[0])
        elif kind == 1:
            assert not ctx_update, "context queries are only needed when a later layer reads the context"
            rank = mla_q_norm.shape[1]
            w_in = mla_w_in[j]
            w_in = jnp.concatenate([w_in, _rotate_half_cols(w_in[:, 2 * rank:])], axis=1)
            zeros = jnp.zeros((w_in.shape[1],), F32)
            t = _mod_proj(h, sh1, sc1, w_in, zeros, slabs=1, tm=512, tn=w_in.shape[1])[0]
            tc = _mod_proj(hc, csh1, csc1, w_in, zeros, slabs=1, tm=512, tn=w_in.shape[1])[0]
            q = _mla_queries(t, mla_q_norm[j], mla_wq_b[j], cs, heads)
            k, v = _mla_keys_values(t, mla_kv_norm[j], mla_wkv_b[j], cs, heads, True)
            kc, vc = _mla_keys_values(tc, mla_kv_norm[j], mla_wkv_b[j], cs, heads, False)
            o = _mla_attend(q, k, v, kc, vc)
            h_mid = _attn_out(alpha, o, mla_w_out[j], h, g1, lg[0], lb[0])
        else:
            h_mid = _pool_mixer(alpha, h, sh1, sc1, g1, pool_w[j], pool_scale[j], lg[0], lb[0])
            if ctx_update:
                hc_mid = _pool_mixer(alpha, hc, csh1, csc1, cg1, pool_w[j], pool_scale[j], lg[0], lb[0])
        ffn = (ffn_w_gate[i], ffn_w_up[i], ffn_w_down[i])
        h = _ffn(alpha, h_mid, sh2, sc2, g2, *ffn, lg[1], lb[1])
        if ctx_update:
            hc = _ffn(alpha, hc_mid, csh2, csc2, cg2, *ffn, lg[1], lb[1])
    return h
```

```python
import functools
import math

import jax
import jax.numpy as jnp
from jax import lax
from jax.experimental import pallas as pl
from jax.experimental.pallas import tpu as pltpu

F32 = jnp.float32
MXU_DTYPE = jnp.bfloat16

V7X_VMEM_BYTES = 64 * 1024 * 1024
VMEM_LIMIT = V7X_VMEM_BYTES - 8 * 1024 * 1024
LANES = 128
SUBLANES = 8

N_MIXERS = 3
GRID_W = 64
LN_EPS = 1e-6
RMS_EPS = 1e-6
HY_TARGET = 1e-2
HY_FAST = 0.3
HY_SLOW = 1.5
HY_MIN_DECAY = math.log(HY_TARGET) / HY_SLOW
HY_MAX_DECAY = math.log(HY_TARGET) / HY_FAST
QK_NOPE = 128
QK_ROPE = 64
V_HEAD = 128
ROPE_PAIRS = QK_ROPE // 4
ROPE_THETA = 10000.0
ATTN_SCALE = (QK_NOPE + QK_ROPE) ** -0.5
QUERY_SCALE = ATTN_SCALE * math.log2(math.e)
ATTN_CHUNK = 512
V_ROWS = V_HEAD + 16
POOL_WINDOWS = (2, 4, 8, 16)
POOL_HALO = 8

DFT_N2 = 128
DFT_SMALL_MAX_L = 512
DFT_MID_ROWS = 4


def _params(sem, vmem=VMEM_LIMIT, flags=None):
    return pltpu.CompilerParams(dimension_semantics=sem, vmem_limit_bytes=vmem, flags=flags)


def _tile(n, t):
    t = min(n, t)
    assert n % t == 0, (n, t)
    return t


def _dot(a, b):
    return jnp.dot(a, b, preferred_element_type=F32)


def _layer_norm_rows(x, g, b):
    mu = jnp.mean(x, axis=-1, keepdims=True)
    xc = x - mu
    var = jnp.mean(xc * xc, axis=-1, keepdims=True)
    return xc * lax.rsqrt(var + LN_EPS) * g + b


def _silu(x):
    return x * (1.0 / (1.0 + jnp.exp(-x)))


def _ada_kernel(c_ref, w_ref, b_ref, o_ref):
    c = c_ref[...]
    a = _silu(c).astype(MXU_DTYPE)
    o_ref[0] = _dot(a, w_ref[0].astype(MXU_DTYPE)) + b_ref[0]


def _ada_mods(cond, ada_w, ada_b):
    depth, d, n = ada_w.shape
    rows = cond.shape[0]
    tn = _tile(n, 1024)
    return pl.pallas_call(
        _ada_kernel,
        out_shape=jax.ShapeDtypeStruct((depth, rows, n), F32),
        grid=(depth, n // tn),
        in_specs=[
            pl.BlockSpec((rows, d), lambda i, j: (0, 0)),
            pl.BlockSpec((1, d, tn), lambda i, j: (i, 0, j)),
            pl.BlockSpec((1, 1, tn), lambda i, j: (i, 0, j)),
        ],
        out_specs=pl.BlockSpec((1, rows, tn), lambda i, j: (i, 0, j)),
        compiler_params=_params(("parallel", "parallel")),
        name="ada_mods",
    )(cond, ada_w, ada_b.reshape(depth, 1, n))


def _mod_proj_kernel(x_ref, sh_ref, sc_ref, w_ref, b_ref, o_ref, u_scr):
    @pl.when(pl.program_id(2) == 0)
    def _():
        u_scr[...] = (x_ref[0] * (1.0 + sc_ref[0]) + sh_ref[0]).astype(MXU_DTYPE)

    o_ref[0, 0] = (_dot(u_scr[...], w_ref[...]) + b_ref[...]).astype(o_ref.dtype)


def _mod_proj(x, shift, scale, w, b, slabs, tm, tn):
    bsz, seq, k = x.shape
    n = w.shape[1]
    n_slab = n // slabs
    tm = _tile(seq, tm)
    tn = _tile(n_slab, tn)
    per = n_slab // tn
    return pl.pallas_call(
        _mod_proj_kernel,
        out_shape=jax.ShapeDtypeStruct((slabs, bsz, seq, n_slab), MXU_DTYPE),
        grid=(bsz, seq // tm, n // tn),
        in_specs=[
            pl.BlockSpec((1, tm, k), lambda bi, i, j: (bi, i, 0)),
            pl.BlockSpec((1, 1, k), lambda bi, i, j: (bi, 0, 0)),
            pl.BlockSpec((1, 1, k), lambda bi, i, j: (bi, 0, 0)),
            pl.BlockSpec((k, tn), lambda bi, i, j: (0, j)),
            pl.BlockSpec((1, tn), lambda bi, i, j: (0, j)),
        ],
        out_specs=pl.BlockSpec((1, 1, tm, tn), lambda bi, i, j: (j // per, bi, i, j % per)),
        scratch_shapes=[pltpu.VMEM((tm, k), MXU_DTYPE)],
        compiler_params=_params(("parallel", "parallel", "arbitrary")),
        name="mod_proj",
    )(x, shift, scale, w.astype(MXU_DTYPE), b.reshape(1, n))


def _hyena_in_kernel(x_ref, prev_ref, next_ref, sh_ref, sc_ref, w0_ref, w1_ref, w2_ref, b_ref, cw_ref, cb_ref,
                     x0_ref, v_ref, u_scr):
    i = pl.program_id(1)
    last = pl.num_programs(1) - 1
    tm = x_ref.shape[1]
    ext = tm + 2 * SUBLANES

    @pl.when(pl.program_id(2) == 0)
    def _():
        shift, scale = sh_ref[0], sc_ref[0]
        rows = jnp.concatenate([prev_ref[0], x_ref[0], next_ref[0]], axis=0)
        u_scr[...] = (rows * (1.0 + scale) + shift).astype(MXU_DTYPE)

    u = u_scr[...]
    outs = []
    for s, w_ref in enumerate((w0_ref, w1_ref, w2_ref)):
        raw = _dot(u, w_ref[...])
        b, taps = b_ref[s], cw_ref[s]
        head = jnp.where(i == 0, -b, raw[:SUBLANES])
        tail = jnp.where(i == last, -b, raw[SUBLANES + tm:])
        raw = jnp.concatenate([head, raw[SUBLANES:SUBLANES + tm], tail], axis=0)
        conv = pltpu.roll(raw, 1, 0) * taps[0:1] + raw * taps[1:2] + pltpu.roll(raw, ext - 1, 0) * taps[2:3]
        const = b * (taps[0:1] + taps[1:2] + taps[2:3]) + cb_ref[s]
        outs.append(conv[SUBLANES:SUBLANES + tm] + const)
    x0_ref[0] = outs[0].astype(x0_ref.dtype)
    v_ref[0] = (outs[2] * outs[1]).astype(v_ref.dtype)


def _hyena_in(x, shift, scale, w_in, b_in, conv_w, conv_b):
    bsz, seq, k = x.shape
    d = w_in.shape[1] // 3
    tm = _tile(seq, 512)
    tn = _tile(d, 512)
    per = d // tn
    hb = tm // SUBLANES
    nhb = seq // SUBLANES
    w = w_in.astype(MXU_DTYPE)
    cw = conv_w.reshape(3, 3, d).transpose(1, 0, 2)
    out = jax.ShapeDtypeStruct((bsz, seq, d), MXU_DTYPE)
    slab = lambda s: pl.BlockSpec((k, tn), lambda bi, i, j: (0, s * per + j))
    return pl.pallas_call(
        _hyena_in_kernel,
        out_shape=(out, out),
        grid=(bsz, seq // tm, per),
        in_specs=[
            pl.BlockSpec((1, tm, k), lambda bi, i, j: (bi, i, 0)),
            pl.BlockSpec((1, SUBLANES, k), lambda bi, i, j: (bi, jnp.maximum(i * hb - 1, 0), 0)),
            pl.BlockSpec((1, SUBLANES, k), lambda bi, i, j: (bi, jnp.minimum((i + 1) * hb, nhb - 1), 0)),
            pl.BlockSpec((1, 1, k), lambda bi, i, j: (bi, 0, 0)),
            pl.BlockSpec((1, 1, k), lambda bi, i, j: (bi, 0, 0)),
            slab(0), slab(1), slab(2),
            pl.BlockSpec((3, 1, tn), lambda bi, i, j: (0, 0, j)),
            pl.BlockSpec((3, 3, tn), lambda bi, i, j: (0, 0, j)),
            pl.BlockSpec((3, 1, tn), lambda bi, i, j: (0, 0, j)),
        ],
        out_specs=(
            pl.BlockSpec((1, tm, tn), lambda bi, i, j: (bi, i, j)),
            pl.BlockSpec((1, tm, tn), lambda bi, i, j: (bi, i, j)),
        ),
        scratch_shapes=[pltpu.VMEM((tm + 2 * SUBLANES, k), MXU_DTYPE)],
        compiler_params=_params(("parallel", "parallel", "arbitrary")),
        name="hyena_in_conv_gate",
    )(x, x, x, shift, scale, w, w, w, b_in.reshape(3, 1, d), cw, conv_b.reshape(3, 1, d))


def _filter_kernel(z_ref, dist_ref, w_in_ref, w_hid_ref, b_ref, fr_ref, w_out_ref, delta_ref, f_ref, norm_ref):
    hp = lax.Precision.HIGHEST
    b = b_ref[...]
    fr = fr_ref[...]
    g = jnp.sin(fr[0:1] * (jnp.dot(z_ref[...], w_in_ref[...], precision=hp, preferred_element_type=F32) + b[0:1]))
    g = jnp.sin(fr[1:2] * (jnp.dot(g, w_hid_ref[0], precision=hp, preferred_element_type=F32) + b[1:2]))
    g = jnp.sin(fr[2:3] * (jnp.dot(g, w_hid_ref[1], precision=hp, preferred_element_type=F32) + b[2:3]))
    filt = _dot(g.astype(MXU_DTYPE), w_out_ref[...])
    filt = filt * jnp.exp(-dist_ref[...] * delta_ref[...])
    f_ref[...] = filt

    @pl.when(pl.program_id(0) == 0)
    def _():
        norm_ref[...] = jnp.zeros_like(norm_ref)

    norm_ref[...] += jnp.sum(jnp.abs(filt), axis=0, keepdims=True)


def _first_stage_row_order(seq):
    a_hi, c, a_lo = jnp.meshgrid(jnp.arange(seq // (DFT_N2 * SUBLANES)), jnp.arange(DFT_N2), jnp.arange(SUBLANES),
                                 indexing="ij")
    return (DFT_N2 * (SUBLANES * a_hi + a_lo) + c).reshape(seq)


def _implicit_filter(seq, f_w_in, f_w_hid, f_b, f_freq, f_w_out, positions=None):
    emb, width = f_w_in.shape
    bands_n = (emb - 1) // 2
    d = f_w_out.shape[1]
    emb_pad = -(-emb // LANES) * LANES
    filt_w = -(-width // LANES) * LANES
    pad_to = lambda a, shape: jnp.pad(a.astype(F32), [(0, s - n) for s, n in zip(shape, a.shape)])
    pos = (jnp.arange(seq) if positions is None else positions).astype(F32)
    t = pos / (seq - 1)
    bands = jnp.linspace(1e-4, bands_n - 1, bands_n, dtype=F32)
    ang = (2.0 * math.pi / seq) * pos[:, None] * bands[None, :]
    z = pad_to(jnp.concatenate([t[:, None], jnp.cos(ang), -jnp.sin(ang)], axis=-1), (seq, emb_pad))
    w_in = pad_to(f_w_in, (emb_pad, filt_w))
    f_w_hid = pad_to(f_w_hid, (2, filt_w, filt_w))
    f_b = pad_to(f_b, (3, filt_w))
    f_freq = pad_to(f_freq, (3, filt_w))
    f_w_out = pad_to(f_w_out, (filt_w, d))
    dist = (jnp.abs(pos - seq // 2) / (seq // 2))[:, None]
    deltas = jnp.abs(jnp.linspace(HY_MIN_DECAY, HY_MAX_DECAY, d, dtype=F32))[None, :]
    tl = _tile(seq, 512)
    full = lambda shape: pl.BlockSpec(shape, lambda i: (0,) * len(shape))
    return pl.pallas_call(
        _filter_kernel,
        out_shape=(jax.ShapeDtypeStruct((seq, d), F32), jax.ShapeDtypeStruct((1, d), F32)),
        grid=(seq // tl,),
        in_specs=[
            pl.BlockSpec((tl, emb_pad), lambda i: (i, 0)),
            pl.BlockSpec((tl, 1), lambda i: (i, 0)),
            full((emb_pad, filt_w)),
            full((2, filt_w, filt_w)),
            full((3, filt_w)),
            full((3, filt_w)),
            full((filt_w, d)),
            full((1, d)),
        ],
        out_specs=(pl.BlockSpec((tl, d), lambda i: (i, 0)), full((1, d))),
        compiler_params=_params(("arbitrary",)),
        name="hyena_filter",
    )(z, dist, w_in, f_w_hid, f_b, f_freq, f_w_out.astype(MXU_DTYPE), deltas)


def _real_form(mr, mi):
    top = jnp.concatenate([mr, -mi], axis=-1)
    bot = jnp.concatenate([mi, mr], axis=-1)
    return jnp.concatenate([top, bot], axis=-2)


def _unit_roots(idx, n):
    ang = (2.0 * math.pi / n) * (idx % n).astype(F32)
    return jnp.cos(ang), -jnp.sin(ang)


def _dft_tables(seq):
    n = 2 * seq
    n2 = DFT_N2
    n1 = n // n2
    i32 = jnp.int32
    p = jnp.arange(n1, dtype=i32)
    c = jnp.arange(n2, dtype=i32)
    tr, ti = _unit_roots(c[:, None] * p[None, :], n)
    a_in = jnp.arange(n1 // 2, dtype=i32)
    fr, fi = _unit_roots(p[:, None] * a_in[None, :], n1)
    mr = tr[:, :, None] * fr[None] - ti[:, :, None] * fi[None]
    mi = tr[:, :, None] * fi[None] + ti[:, :, None] * fr[None]
    first = _real_form(mr, mi)
    a_out = jnp.arange(n1 // 4, n1 // 4 + n1 // 2, dtype=i32)
    gr, gi = _unit_roots(p[:, None] * a_out[None, :], n1)
    cr = (tr[:, :, None] * gr[None] - ti[:, :, None] * gi[None]) / n
    ci = -(tr[:, :, None] * gi[None] + ti[:, :, None] * gr[None]) / n
    last = _real_form(jnp.swapaxes(cr, 1, 2), jnp.swapaxes(ci, 1, 2))
    q = jnp.arange(n2, dtype=i32)
    hr, hi_ = _unit_roots(q[:, None] * c[None, :], n2)
    mid_f = _real_form(hr, hi_)
    mid_i = _real_form(hr, -hi_)
    return tuple(t.astype(MXU_DTYPE) for t in (first, mid_f, mid_i, last))


def _dft_first_kernel(x_ref, m_ref, o_ref):
    x = x_ref[...]
    x = x.reshape(x.shape[0] * x.shape[1] * SUBLANES, x.shape[4])
    o_ref[...] = _dot(m_ref[0], x.astype(MXU_DTYPE)).astype(o_ref.dtype)


def _dft_first(x5, table):
    r, a_hi, n2, _, d = x5.shape
    rows = r * a_hi * SUBLANES
    n_out = table.shape[1]
    if rows % LANES:
        table = table[:, :, :rows]
    bd = _tile(d, 2048)
    per = d // bd
    return pl.pallas_call(
        _dft_first_kernel,
        out_shape=jax.ShapeDtypeStruct((n_out, n2 * d), MXU_DTYPE),
        grid=(n2 * per,),
        in_specs=[
            pl.BlockSpec((r, a_hi, 1, SUBLANES, bd), lambda j: (0, 0, j // per, 0, j % per)),
            pl.BlockSpec((1, n_out, rows), lambda j: (j // per, 0, 0)),
        ],
        out_specs=pl.BlockSpec((n_out, bd), lambda j: (0, j)),
        compiler_params=_params(("parallel",)),
        name="hyena_dft_first",
    )(x5, table)


def _dft_mid_filter_kernel(a_ref, f_ref, norm_ref, o_ref):
    n2 = a_ref.shape[2]
    for p in range(a_ref.shape[1]):
        a = jnp.concatenate([a_ref[0, p], a_ref[1, p]], axis=0)
        z = _dot(f_ref[...], a) / norm_ref[...]
        o_ref[0, p] = z[:n2].astype(o_ref.dtype)
        o_ref[1, p] = z[n2:].astype(o_ref.dtype)


def _dft_mid_conv_kernel(a_ref, h_ref, f_ref, g_ref, o_ref):
    n2 = a_ref.shape[2]
    for p in range(a_ref.shape[1]):
        a = jnp.concatenate([a_ref[0, p], a_ref[1, p]], axis=0)
        z = _dot(f_ref[...], a)
        zr, zi = z[:n2], z[n2:]
        hr, hi_ = h_ref[0, p].astype(F32), h_ref[1, p].astype(F32)
        w = jnp.concatenate([zr * hr - zi * hi_, zr * hi_ + zi * hr], axis=0)
        y = _dot(g_ref[...], w.astype(MXU_DTYPE))
        o_ref[0, p] = y[:n2].astype(o_ref.dtype)
        o_ref[1, p] = y[n2:].astype(o_ref.dtype)


def _dft_mid(a, tables, d, *, spectrum=None, norm=None):
    n2 = DFT_N2
    n1 = a.shape[0] // 2
    a4 = a.reshape(2, n1, n2, d)
    bd = _tile(d, 2048)
    pb = _tile(n1, DFT_MID_ROWS)
    blk = pl.BlockSpec((2, pb, n2, bd), lambda p, j: (0, p, 0, j))
    mat = pl.BlockSpec((2 * n2, 2 * n2), lambda p, j: (0, 0))
    fwd, inv = tables
    if spectrum is None:
        kern, ins, specs = _dft_mid_filter_kernel, (a4, fwd, norm), [blk, mat, pl.BlockSpec((1, bd), lambda p, j: (0, j))]
    else:
        kern, ins, specs = _dft_mid_conv_kernel, (a4, spectrum, fwd, inv), [blk, blk, mat, mat]
    out = pl.pallas_call(
        kern,
        out_shape=jax.ShapeDtypeStruct((2, n1, n2, d), MXU_DTYPE),
        grid=(n1 // pb, d // bd),
        in_specs=specs,
        out_specs=blk,
        compiler_params=_params(("parallel", "parallel")),
        name="hyena_dft_mid",
    )(*ins)
    return out


def _dft_last_kernel(x_ref, m_ref, o_ref):
    y = _dot(m_ref[0], x_ref[...])
    o_ref[...] = y.reshape(o_ref.shape).astype(o_ref.dtype)


def _dft_last(b, table, d):
    rows, cols = b.shape
    n1 = rows // 2
    n2 = cols // d
    a_hi = n1 // 2 // SUBLANES
    bd = _tile(d, 2048)
    per = d // bd
    return pl.pallas_call(
        _dft_last_kernel,
        out_shape=jax.ShapeDtypeStruct((2, a_hi, n2, SUBLANES, d), MXU_DTYPE),
        grid=(cols // bd,),
        in_specs=[
            pl.BlockSpec((rows, bd), lambda j: (0, j)),
            pl.BlockSpec((1, n1, rows), lambda j: (j // per, 0, 0)),
        ],
        out_specs=pl.BlockSpec((2, a_hi, 1, SUBLANES, bd), lambda j: (0, 0, j // per, 0, j % per)),
        compiler_params=_params(("parallel",)),
        name="hyena_dft_last",
    )(b, table)


def _long_conv_two_stage(v, filt, norm, tables):
    bsz, seq, d = v.shape
    n1 = 2 * seq // DFT_N2
    a_hi = n1 // 2 // SUBLANES
    first, mid_f, mid_i, last = tables
    cols = DFT_N2 * d
    h_first = _dft_first(filt.reshape(1, a_hi, DFT_N2, SUBLANES, d), first)
    spectrum = _dft_mid(h_first, (mid_f, mid_i), d, norm=norm)
    v5 = v.reshape(bsz, a_hi, SUBLANES, DFT_N2, d).transpose(0, 1, 3, 2, 4)
    a = _dft_first(v5, first)
    bmat = _dft_mid(a, (mid_f, mid_i), d, spectrum=spectrum)
    y5 = _dft_last(bmat.reshape(2 * n1, cols), last, d)
    return y5.transpose(0, 1, 3, 2, 4).reshape(bsz, seq, d)


def _dft_small_kernel(v_ref, h_ref, norm_ref, f_ref, g_ref, o_ref):
    seq = v_ref.shape[1]
    n = 2 * seq
    fwd = f_ref[...]
    x = jnp.concatenate([v_ref[0], v_ref[1]], axis=0)
    z = _dot(fwd, x.astype(MXU_DTYPE))
    hx = jnp.concatenate([h_ref[...] / norm_ref[...], jnp.zeros_like(h_ref)], axis=0)
    hs = _dot(fwd, hx.astype(MXU_DTYPE))
    zr, zi, hr, hi_ = z[:n], z[n:], hs[:n], hs[n:]
    w = jnp.concatenate([zr * hr - zi * hi_, zr * hi_ + zi * hr], axis=0)
    y = _dot(g_ref[...], w.astype(MXU_DTYPE))
    o_ref[0] = y[:seq].astype(o_ref.dtype)
    o_ref[1] = y[seq:].astype(o_ref.dtype)


def _long_conv_small(v, filt, norm):
    bsz, seq, d = v.shape
    n = 2 * seq
    k = jnp.arange(n, dtype=jnp.int32)
    t_in = jnp.arange(seq, dtype=jnp.int32)
    fr, fi = _unit_roots(k[:, None] * t_in[None, :], n)
    fwd = _real_form(fr, fi)
    t_out = jnp.arange(seq // 2, seq // 2 + seq, dtype=jnp.int32)
    gr, gi = _unit_roots(t_out[:, None] * k[None, :], n)
    inv = _real_form(gr / n, -gi / n)
    bd = _tile(d, 512)
    full = lambda shape: pl.BlockSpec(shape, lambda j: (0,) * len(shape))
    return pl.pallas_call(
        _dft_small_kernel,
        out_shape=jax.ShapeDtypeStruct((bsz, seq, d), MXU_DTYPE),
        grid=(d // bd,),
        in_specs=[
            pl.BlockSpec((bsz, seq, bd), lambda j: (0, 0, j)),
            pl.BlockSpec((seq, bd), lambda j: (0, j)),
            pl.BlockSpec((1, bd), lambda j: (0, j)),
            full(fwd.shape), full(inv.shape),
        ],
        out_specs=pl.BlockSpec((bsz, seq, bd), lambda j: (0, 0, j)),
        compiler_params=_params(("parallel",)),
        name="hyena_dft_small",
    )(v, filt, norm, fwd.astype(MXU_DTYPE), inv.astype(MXU_DTYPE))


def _hyena_out_kernel(alpha, x0_ref, cv_ref, v_ref, bias_ref, w_ref, b_ref, h_ref, g_ref, lg_ref, lb_ref, o_ref):
    y = x0_ref[0].astype(F32) * (cv_ref[0].astype(F32) + v_ref[0].astype(F32) * bias_ref[...])
    y = _dot(y.astype(MXU_DTYPE), w_ref[...]) + b_ref[...]
    o_ref[0] = _layer_norm_rows(alpha * h_ref[0] + g_ref[0] * y, lg_ref[...], lb_ref[...])


def _attn_out_kernel(alpha, o_in_ref, w_ref, h_ref, g_ref, lg_ref, lb_ref, o_ref):
    y = _dot(o_in_ref[0], w_ref[...])
    o_ref[0] = _layer_norm_rows(alpha * h_ref[0] + g_ref[0] * y, lg_ref[...], lb_ref[...])


def _row_spec(tm, d):
    return pl.BlockSpec((1, tm, d), lambda bi, i: (bi, i, 0))


def _vec_spec(d):
    return pl.BlockSpec((1, d), lambda bi, i: (0, 0))


def _bvec_spec(d):
    return pl.BlockSpec((1, 1, d), lambda bi, i: (bi, 0, 0))


def _const_spec(shape):
    return pl.BlockSpec(shape, lambda bi, i: (0,) * len(shape), pipeline_mode=pl.Buffered(1))


def _hyena_out(alpha, x0, cv, v, bias, w, b, h, gate, ln_g, ln_b):
    bsz, seq, d = h.shape
    tm = _tile(seq, 512)
    return pl.pallas_call(
        functools.partial(_hyena_out_kernel, alpha),
        out_shape=jax.ShapeDtypeStruct(h.shape, F32),
        grid=(bsz, seq // tm),
        in_specs=[_row_spec(tm, d), _row_spec(tm, d), _row_spec(tm, d), _vec_spec(d), _const_spec((d, d)),
                  _vec_spec(d), _row_spec(tm, d), _bvec_spec(d), _vec_spec(d), _vec_spec(d)],
        out_specs=_row_spec(tm, d),
        compiler_params=_params(("parallel", "parallel")),
        name="hyena_out_norm",
    )(x0, cv, v, bias.reshape(1, d), w.astype(MXU_DTYPE), b.reshape(1, d), h, gate, ln_g.reshape(1, d), ln_b.reshape(1, d))


def _attn_out(alpha, o, w, h, gate, ln_g, ln_b):
    bsz, seq, d = h.shape
    k = o.shape[2]
    tm = _tile(seq, 512)
    return pl.pallas_call(
        functools.partial(_attn_out_kernel, alpha),
        out_shape=jax.ShapeDtypeStruct(h.shape, F32),
        grid=(bsz, seq // tm),
        in_specs=[_row_spec(tm, k), _const_spec((k, d)), _row_spec(tm, d), _bvec_spec(d), _vec_spec(d), _vec_spec(d)],
        out_specs=_row_spec(tm, d),
        compiler_params=_params(("parallel", "parallel")),
        name="attn_out_norm",
    )(o, w.astype(MXU_DTYPE), h, gate, ln_g.reshape(1, d), ln_b.reshape(1, d))


def _ffn_kernel(alpha, h_ref, sh_ref, sc_ref, g_ref, wg_ref, wu_ref, wd_ref, lg_ref, lb_ref, o_ref, u_scr, acc_scr):
    f = pl.program_id(2)

    @pl.when(f == 0)
    def _():
        u_scr[...] = (h_ref[0] * (1.0 + sc_ref[0]) + sh_ref[0]).astype(MXU_DTYPE)
        acc_scr[...] = jnp.zeros_like(acc_scr)

    u = u_scr[...]
    gate = _dot(u, wg_ref[...])
    up = _dot(u, wu_ref[...])
    act = (_silu(gate) * up).astype(MXU_DTYPE)
    acc_scr[...] += _dot(act, wd_ref[...])

    @pl.when(f == pl.num_programs(2) - 1)
    def _():
        o_ref[0] = _layer_norm_rows(alpha * h_ref[0] + g_ref[0] * acc_scr[...], lg_ref[...], lb_ref[...])


def _ffn(alpha, h, shift, scale, gate, w_gate, w_up, w_down, ln_g, ln_b):
    bsz, seq, d = h.shape
    ff = w_gate.shape[1]
    tm = _tile(seq, 512)
    tf = _tile(ff, 512)
    return pl.pallas_call(
        functools.partial(_ffn_kernel, alpha),
        out_shape=jax.ShapeDtypeStruct(h.shape, F32),
        grid=(bsz, seq // tm, ff // tf),
        in_specs=[
            pl.BlockSpec((1, tm, d), lambda bi, i, f: (bi, i, 0)),
            pl.BlockSpec((1, 1, d), lambda bi, i, f: (bi, 0, 0)),
            pl.BlockSpec((1, 1, d), lambda bi, i, f: (bi, 0, 0)),
            pl.BlockSpec((1, 1, d), lambda bi, i, f: (bi, 0, 0)),
            pl.BlockSpec((d, tf), lambda bi, i, f: (0, f)),
            pl.BlockSpec((d, tf), lambda bi, i, f: (0, f)),
            pl.BlockSpec((tf, d), lambda bi, i, f: (f, 0)),
            pl.BlockSpec((1, d), lambda bi, i, f: (0, 0)),
            pl.BlockSpec((1, d), lambda bi, i, f: (0, 0)),
        ],
        out_specs=pl.BlockSpec((1, tm, d), lambda bi, i, f: (bi, i, 0)),
        scratch_shapes=[pltpu.VMEM((tm, d), MXU_DTYPE), pltpu.VMEM((tm, d), F32)],
        compiler_params=_params(("parallel", "parallel", "arbitrary")),
        name="ffn_swiglu_norm",
    )(h, shift, scale, gate, w_gate.astype(MXU_DTYPE), w_up.astype(MXU_DTYPE), w_down.astype(MXU_DTYPE),
      ln_g.reshape(1, d), ln_b.reshape(1, d))


def _rotate_half_cols(w):
    ws = w.reshape(w.shape[:-1] + (2, 2, ROPE_PAIRS))
    return jnp.stack([-ws[..., 1, :], ws[..., 0, :]], axis=-2).reshape(w.shape)


def _rope_table(seq):
    rows = seq // GRID_W
    row = jnp.repeat(jnp.arange(rows, dtype=F32), GRID_W)
    col = jnp.tile(jnp.arange(GRID_W, dtype=F32), rows)
    inv = ROPE_THETA ** (-jnp.arange(ROPE_PAIRS, dtype=F32) / ROPE_PAIRS)
    ang = jnp.stack([row[:, None] * inv, col[:, None] * inv], axis=1)
    ang = jnp.broadcast_to(ang[:, :, None, :], (seq, 2, 2, ROPE_PAIRS)).reshape(seq, QK_ROPE)
    return jnp.concatenate([jnp.cos(ang), jnp.sin(ang)], axis=-1)


def _rms_rows(x, g):
    return x * lax.rsqrt(jnp.mean(x * x, axis=-1, keepdims=True) + RMS_EPS) * g


def _rope_pair(x, cs):
    t = x * cs
    return t + pltpu.roll(t, QK_ROPE, 1)


def _mla_q_kernel(cq_ref, g_ref, w_ref, cs_ref, q_ref):
    xn = _rms_rows(cq_ref[0].astype(F32), g_ref[...]).astype(MXU_DTYPE)
    cs = cs_ref[...]

    def head(h, carry):
        a = _dot(xn, w_ref[h])
        r = _rope_pair(a[:, QK_NOPE:], cs)
        q_ref[0, h] = (jnp.concatenate([a[:, :QK_NOPE], r], axis=1) * QUERY_SCALE).astype(q_ref.dtype)
        return carry

    lax.fori_loop(0, w_ref.shape[0], head, 0, unroll=2)


def _mla_queries(t, q_norm, wq_b, cs, heads):
    bsz, seq, _ = t.shape
    rank = q_norm.shape[0]
    tm = _tile(seq, 512)
    w = wq_b.reshape(rank, heads, QK_NOPE + QK_ROPE)
    w = jnp.concatenate([w, _rotate_half_cols(w[..., QK_NOPE:])], axis=-1).transpose(1, 0, 2).astype(MXU_DTYPE)
    return pl.pallas_call(
        _mla_q_kernel,
        out_shape=jax.ShapeDtypeStruct((bsz, heads, seq, 2 * LANES), MXU_DTYPE),
        grid=(bsz, seq // tm),
        in_specs=[
            pl.BlockSpec((1, tm, rank), lambda bi, i: (bi, i, 0)),
            pl.BlockSpec((1, rank), lambda bi, i: (0, 0)),
            pl.BlockSpec((heads, rank, 2 * LANES), lambda bi, i: (0, 0, 0)),
            pl.BlockSpec((tm, LANES), lambda bi, i: (i, 0)),
        ],
        out_specs=pl.BlockSpec((1, heads, tm, 2 * LANES), lambda bi, i: (bi, 0, i, 0)),
        compiler_params=_params(("parallel", "parallel")),
        name="mla_queries",
    )(t, q_norm.reshape(1, rank), w, cs)


def _mla_kv_kernel(use_rope, ckv_ref, kr_ref, g_ref, w_ref, cs_ref, k_ref, vt_ref):
    xn = _rms_rows(ckv_ref[0].astype(F32), g_ref[...]).astype(MXU_DTYPE)
    kr = kr_ref[0].astype(F32)
    if use_rope:
        kr = _rope_pair(kr, cs_ref[...])
    lane = lax.broadcasted_iota(jnp.int32, kr.shape, 1)
    kr = jnp.where(lane < QK_ROPE, kr, 0.0).astype(k_ref.dtype)
    ones = jnp.ones((V_ROWS - V_HEAD, xn.shape[0]), vt_ref.dtype)

    def head(h, carry):
        kv = _dot(xn, w_ref[h])
        k_ref[0, h] = jnp.concatenate([kv[:, :QK_NOPE].astype(k_ref.dtype), kr], axis=1)
        vt_ref[0, h] = jnp.concatenate([kv[:, QK_NOPE:].T.astype(vt_ref.dtype), ones], axis=0)
        return carry

    lax.fori_loop(0, w_ref.shape[0], head, 0, unroll=2)


def _mla_keys_values(t, kv_norm, wkv_b, cs, heads, use_rope):
    bsz, seq, _ = t.shape
    rank = kv_norm.shape[0]
    tm = _tile(seq, 512)
    w = wkv_b.reshape(rank, heads, QK_NOPE + V_HEAD).transpose(1, 0, 2).astype(MXU_DTYPE)
    return pl.pallas_call(
        functools.partial(_mla_kv_kernel, use_rope),
        out_shape=(jax.ShapeDtypeStruct((bsz, heads, seq, 2 * LANES), MXU_DTYPE),
                   jax.ShapeDtypeStruct((bsz, heads, V_ROWS, seq), MXU_DTYPE)),
        grid=(bsz, seq // tm),
        in_specs=[
            pl.BlockSpec((1, tm, rank), lambda bi, i: (bi, i, 1)),
            pl.BlockSpec((1, tm, LANES), lambda bi, i: (bi, i, 2 * rank // LANES)),
            pl.BlockSpec((1, rank), lambda bi, i: (0, 0)),
            pl.BlockSpec((heads, rank, QK_NOPE + V_HEAD), lambda bi, i: (0, 0, 0)),
            pl.BlockSpec((tm, LANES), lambda bi, i: (i, 0)),
        ],
        out_specs=(pl.BlockSpec((1, heads, tm, 2 * LANES), lambda bi, i: (bi, 0, i, 0)),
                   pl.BlockSpec((1, heads, V_ROWS, tm), lambda bi, i: (bi, 0, 0, i))),
        compiler_params=_params(("parallel", "parallel")),
        name="mla_keys_values",
    )(t, t, kv_norm.reshape(1, rank), w, cs)


def _attn_kernel(q_ref, k_ref, vt_ref, kc_ref, vtc_ref, o_ref, qt_scr, s_scr, p_scr, smax_scr, m_scr, alpha_scr, acc_scr):
    kv = pl.program_id(3)
    n_main = pl.num_programs(3) - 1

    n_chunks, _, cw = qt_scr.shape

    @pl.when(kv == 0)
    def _():
        for c in range(n_chunks):
            qt_scr[c] = q_ref[0, 0, c * cw:(c + 1) * cw, :].T
        m_scr[...] = jnp.full_like(m_scr, -jnp.inf)
        acc_scr[...] = jnp.zeros_like(acc_scr)

    def step(k, vt):
        nk = k.shape[0]

        def scores(c, slot):
            s = _dot(k, qt_scr[c])
            s_scr[slot, :nk, :] = s
            smax_scr[slot] = jnp.max(s, axis=0, keepdims=True)

        def exponent(c, slot):
            m_prev = m_scr[c]
            m_new = jnp.maximum(m_prev, smax_scr[slot])
            alpha_scr[c] = jnp.exp2(m_prev - m_new)
            p_scr[slot, :nk, :] = jnp.exp2((s_scr[slot, :nk, :] - m_new).astype(MXU_DTYPE))
            m_scr[c] = m_new

        def values(c, slot):
            acc_scr[c] = alpha_scr[c] * acc_scr[c] + _dot(vt, p_scr[slot, :nk, :])

        for i in range(n_chunks + 2):
            if i < n_chunks:
                scores(i, i % 2)
            if 1 <= i <= n_chunks:
                exponent(i - 1, (i - 1) % 2)
            if i >= 2:
                values(i - 2, i % 2)

    @pl.when(kv < n_main)
    def _():
        step(k_ref[0, 0], vt_ref[0, 0])

    @pl.when(kv == n_main)
    def _():
        step(kc_ref[0, 0], vtc_ref[0, 0])
        for c in range(n_chunks):
            acc = acc_scr[c]
            o_ref[0, c * cw:(c + 1) * cw, :] = (acc[:V_HEAD] / acc[V_HEAD:V_HEAD + 1]).T.astype(o_ref.dtype)


def _mla_attend(q, k, vt, kc, vtc):
    bsz, heads, seq, dq = q.shape
    lc = kc.shape[2]
    tq = _tile(seq, 2048)
    tk = _tile(seq, 2048)
    cw = min(tq, ATTN_CHUNK)
    assert lc <= tk
    n_main = seq // tk
    return pl.pallas_call(
        _attn_kernel,
        out_shape=jax.ShapeDtypeStruct((bsz, seq, heads * V_HEAD), MXU_DTYPE),
        grid=(bsz, heads, seq // tq, n_main + 1),
        in_specs=[
            pl.BlockSpec((1, 1, tq, dq), lambda bi, h, i, j: (bi, h, i, 0)),
            pl.BlockSpec((1, 1, tk, dq), lambda bi, h, i, j: (bi, h, jnp.minimum(j, n_main - 1), 0)),
            pl.BlockSpec((1, 1, V_ROWS, tk), lambda bi, h, i, j: (bi, h, 0, jnp.minimum(j, n_main - 1))),
            pl.BlockSpec((1, 1, lc, dq), lambda bi, h, i, j: (bi, h, 0, 0)),
            pl.BlockSpec((1, 1, V_ROWS, lc), lambda bi, h, i, j: (bi, h, 0, 0)),
        ],
        out_specs=pl.BlockSpec((1, tq, V_HEAD), lambda bi, h, i, j: (bi, i, h)),
        scratch_shapes=[pltpu.VMEM((tq // cw, dq, cw), MXU_DTYPE), pltpu.VMEM((2, tk, cw), F32),
                        pltpu.VMEM((2, tk, cw), MXU_DTYPE), pltpu.VMEM((2, 1, cw), F32),
                        pltpu.VMEM((tq // cw, 1, cw), F32), pltpu.VMEM((tq // cw, 1, cw), F32), pltpu.VMEM((tq // cw, V_ROWS, cw), F32)],
        compiler_params=_params(("parallel", "parallel", "parallel", "arbitrary")),
        name="mla_attention",
    )(q, k, vt, kc, vtc)


def _pool_kernel(alpha, seq, h_ref, prev_ref, next_ref, sh_ref, sc_ref, g_ref, w_ref, ps_ref, lg_ref, lb_ref, o_ref):
    i = pl.program_id(1)
    last = pl.num_programs(1) - 1
    tm = h_ref.shape[1]
    d = h_ref.shape[2]
    groups = len(POOL_WINDOWS)
    ch = d // groups
    ext = tm + 2 * POOL_HALO
    shift, scale = sh_ref[0], sc_ref[0]
    h = h_ref[0]
    u = h * (1.0 + scale) + shift
    u_prev = jnp.where(i == 0, 0.0, prev_ref[0] * (1.0 + scale) + shift)
    u_next = jnp.where(i == last, 0.0, next_ref[0] * (1.0 + scale) + shift)
    e = jnp.concatenate([u_prev, u, u_next], axis=0)
    t = i * tm + lax.broadcasted_iota(jnp.int32, (tm, 1), 0)
    ys = []
    for g, win in enumerate(POOL_WINDOWS):
        a = e[:, g * ch:(g + 1) * ch]
        span = 1
        while span < win:
            a = a + pltpu.roll(a, ext - span, 0)
            span *= 2
        half = win // 2
        a = pltpu.roll(a, half, 0)
        s = a[POOL_HALO:POOL_HALO + tm]
        cnt = (jnp.minimum(t + half, seq) - jnp.maximum(t - half, 0)).astype(F32)
        dg = s / cnt - u[:, g * ch:(g + 1) * ch]
        ys.append(_dot(dg.astype(MXU_DTYPE), w_ref[g]))
    y = jnp.concatenate(ys, axis=1) * ps_ref[...]
    o_ref[0] = _layer_norm_rows(alpha * h + g_ref[0] * y, lg_ref[...], lb_ref[...])


def _pool_mixer(alpha, h, shift, scale, gate, w_grp, pool_scale, ln_g, ln_b):
    bsz, seq, d = h.shape
    groups, ch, _ = w_grp.shape
    tm = _tile(seq, 512)
    hb = tm // POOL_HALO
    nhb = seq // POOL_HALO
    return pl.pallas_call(
        functools.partial(_pool_kernel, alpha, seq),
        out_shape=jax.ShapeDtypeStruct(h.shape, F32),
        grid=(bsz, seq // tm),
        in_specs=[
            _row_spec(tm, d),
            pl.BlockSpec((1, POOL_HALO, d), lambda bi, i: (bi, jnp.maximum(i * hb - 1, 0), 0)),
            pl.BlockSpec((1, POOL_HALO, d), lambda bi, i: (bi, jnp.minimum((i + 1) * hb, nhb - 1), 0)),
            _bvec_spec(d), _bvec_spec(d), _bvec_spec(d),
            pl.BlockSpec((groups, ch, ch), lambda bi, i: (0, 0, 0)),
            _vec_spec(d), _vec_spec(d), _vec_spec(d),
        ],
        out_specs=_row_spec(tm, d),
        compiler_params=_params(("parallel", "parallel")),
        name="pool_mixer_norm",
    )(h, h, h, shift, scale, gate, w_grp.astype(MXU_DTYPE), pool_scale.reshape(1, d), ln_g.reshape(1, d), ln_b.reshape(1, d))


def _hyena_mixer_norm(alpha, h, mod, hy, filt, norm, tables, ln_g, ln_b):
    shift, scale, gate = mod
    w_in, b_in, conv_w, conv_b, bias, w_out, b_out = hy
    x0, v = _hyena_in(h, shift, scale, w_in, b_in, conv_w, conv_b)
    if h.shape[1] <= DFT_SMALL_MAX_L:
        cv = _long_conv_small(v, filt, norm)
    else:
        cv = _long_conv_two_stage(v, filt, norm, tables)
    return _hyena_out(alpha, x0, cv, v, bias, w_out, b_out, h, gate, ln_g, ln_b)


def kernel(x, c, ctx, c_ctx, ada_w, ada_b, ln_g, ln_b, ffn_w_gate, ffn_w_up, ffn_w_down, hy_w_in, hy_b_in, hy_conv_w, hy_conv_b, hy_f_w_in, hy_f_w_hid, hy_f_b, hy_f_freq, hy_f_w_out, hy_bias, hy_w_out, hy_b_out, mla_w_in, mla_q_norm, mla_kv_norm, mla_wq_b, mla_wkv_b, mla_w_out, pool_w, pool_scale):
    bsz, seq, d = x.shape
    depth = ada_w.shape[0]
    assert bsz == 2, "the long convolution packs exactly two batch rows into one complex signal"
    assert ctx.shape[0] == bsz and seq % GRID_W == 0
    alpha = (2.0 * depth) ** 0.25
    heads = d // V_HEAD
    mla_layers = [i for i in range(depth) if i % N_MIXERS == 1]
    last_ctx_read = mla_layers[-1] if mla_layers else -1

    cond = jnp.concatenate([c, jnp.broadcast_to(c_ctx[None], (SUBLANES - bsz, d))], axis=0)
    mods = _ada_mods(cond, ada_w, ada_b)

    def mod_vecs(i, ctx_stream):
        m = jnp.broadcast_to(mods[i, bsz][None], (bsz, 6 * d)) if ctx_stream else mods[i, :bsz]
        return [m[:, None, k * d:(k + 1) * d] for k in range(6)]

    cs = _rope_table(seq)
    two_stage = seq > DFT_SMALL_MAX_L
    tables = _dft_tables(seq) if two_stage else None
    row_order = _first_stage_row_order(seq) if two_stage else None

    h, hc = x, ctx
    for i in range(depth):
        kind, j = i % N_MIXERS, i // N_MIXERS
        ctx_update = i < last_ctx_read
        sh1, sc1, g1, sh2, sc2, g2 = mod_vecs(i, False)
        if kind == 1 or ctx_update:
            csh1, csc1, cg1, csh2, csc2, cg2 = mod_vecs(i, True)
        lg, lb = ln_g[i], ln_b[i]
        if kind == 0:
            hy = (hy_w_in[j], hy_b_in[j], hy_conv_w[j], hy_conv_b[j], hy_bias[j], hy_w_out[j], hy_b_out[j])
            fp = (hy_f_w_in[j], hy_f_w_hid[j], hy_f_b[j], hy_f_freq[j], hy_f_w_out[j])
            filt, norm = _implicit_filter(seq, *fp, positions=row_order)
            h_mid = _hyena_mixer_norm(alpha, h, (sh1, sc1, g1), hy, filt, norm, tables, lg[0], lb[0])
            if ctx_update:
                filt_c, norm_c = _implicit_filter(hc.shape[1], *fp)
                hc_mid = _hyena_mixer_norm(alpha, hc, (csh1, csc1, cg1), hy, filt_c, norm_c, None, lg[0], lb[0])
        elif kind == 1:
            assert not ctx_update, "context queries are only needed when a later layer reads the context"
            rank = mla_q_norm.shape[1]
            w_in = mla_w_in[j]
            w_in = jnp.concatenate([w_in, _rotate_half_cols(w_in[:, 2 * rank:])], axis=1)
            zeros = jnp.zeros((w_in.shape[1],), F32)
            t = _mod_proj(h, sh1, sc1, w_in, zeros, slabs=1, tm=512, tn=w_in.shape[1])[0]
            tc = _mod_proj(hc, csh1, csc1, w_in, zeros, slabs=1, tm=512, tn=w_in.shape[1])[0]
            q = _mla_queries(t, mla_q_norm[j], mla_wq_b[j], cs, heads)
            k, v = _mla_keys_values(t, mla_kv_norm[j], mla_wkv_b[j], cs, heads, True)
            kc, vc = _mla_keys_values(tc, mla_kv_norm[j], mla_wkv_b[j], cs, heads, False)
            o = _mla_attend(q, k, v, kc, vc)
            h_mid = _attn_out(alpha, o, mla_w_out[j], h, g1, lg[0], lb[0])
        else:
            h_mid = _pool_mixer(alpha, h, sh1, sc1, g1, pool_w[j], pool_scale[j], lg[0], lb[0])
            if ctx_update:
                hc_mid = _pool_mixer(alpha, hc, csh1, csc1, cg1, pool_w[j], pool_scale[j], lg[0], lb[0])
        ffn = (ffn_w_gate[i], ffn_w_up[i], ffn_w_down[i])
        h = _ffn(alpha, h_mid, sh2, sc2, g2, *ffn, lg[1], lb[1])
        if ctx_update:
            hc = _ffn(alpha, hc_mid, csh2, csc2, cg2, *ffn, lg[1], lb[1])
    return h
```

```python
import functools
import math

import jax
import jax.numpy as jnp
from jax import lax
from jax.experimental import pallas as pl
from jax.experimental.pallas import tpu as pltpu

F32 = jnp.float32
MXU_DTYPE = jnp.bfloat16

V7X_VMEM_BYTES = 64 * 1024 * 1024
VMEM_LIMIT = V7X_VMEM_BYTES - 8 * 1024 * 1024
LANES = 128
SUBLANES = 8

N_MIXERS = 3
GRID_W = 64
LN_EPS = 1e-6
RMS_EPS = 1e-6
HY_TARGET = 1e-2
HY_FAST = 0.3
HY_SLOW = 1.5
HY_MIN_DECAY = math.log(HY_TARGET) / HY_SLOW
HY_MAX_DECAY = math.log(HY_TARGET) / HY_FAST
QK_NOPE = 128
QK_ROPE = 64
V_HEAD = 128
ROPE_PAIRS = QK_ROPE // 4
ROPE_THETA = 10000.0
ATTN_SCALE = (QK_NOPE + QK_ROPE) ** -0.5
QUERY_SCALE = ATTN_SCALE * math.log2(math.e)
ATTN_CHUNK = 512
V_ROWS = V_HEAD + 16
POOL_WINDOWS = (2, 4, 8, 16)
POOL_HALO = 8

DFT_N2 = 128
DFT_SMALL_MAX_L = 512
DFT_MID_ROWS = 4


def _params(sem, vmem=VMEM_LIMIT, flags=None):
    return pltpu.CompilerParams(dimension_semantics=sem, vmem_limit_bytes=vmem, flags=flags)


def _tile(n, t):
    t = min(n, t)
    assert n % t == 0, (n, t)
    return t


def _dot(a, b):
    return jnp.dot(a, b, preferred_element_type=F32)


def _layer_norm_rows(x, g, b):
    mu = jnp.mean(x, axis=-1, keepdims=True)
    xc = x - mu
    var = jnp.mean(xc * xc, axis=-1, keepdims=True)
    return xc * lax.rsqrt(var + LN_EPS) * g + b


def _silu(x):
    return x * (1.0 / (1.0 + jnp.exp(-x)))


def _ada_kernel(c_ref, w_ref, b_ref, o_ref):
    c = c_ref[...]
    a = _silu(c).astype(MXU_DTYPE)
    o_ref[0] = _dot(a, w_ref[0].astype(MXU_DTYPE)) + b_ref[0]


def _ada_mods(cond, ada_w, ada_b):
    depth, d, n = ada_w.shape
    rows = cond.shape[0]
    tn = _tile(n, 1024)
    return pl.pallas_call(
        _ada_kernel,
        out_shape=jax.ShapeDtypeStruct((depth, rows, n), F32),
        grid=(depth, n // tn),
        in_specs=[
            pl.BlockSpec((rows, d), lambda i, j: (0, 0)),
            pl.BlockSpec((1, d, tn), lambda i, j: (i, 0, j)),
            pl.BlockSpec((1, 1, tn), lambda i, j: (i, 0, j)),
        ],
        out_specs=pl.BlockSpec((1, rows, tn), lambda i, j: (i, 0, j)),
        compiler_params=_params(("parallel", "parallel")),
        name="ada_mods",
    )(cond, ada_w, ada_b.reshape(depth, 1, n))


def _mod_proj_kernel(x_ref, sh_ref, sc_ref, w_ref, b_ref, o_ref, u_scr):
    @pl.when(pl.program_id(2) == 0)
    def _():
        u_scr[...] = (x_ref[0] * (1.0 + sc_ref[0]) + sh_ref[0]).astype(MXU_DTYPE)

    o_ref[0, 0] = (_dot(u_scr[...], w_ref[...]) + b_ref[...]).astype(o_ref.dtype)


def _mod_proj(x, shift, scale, w, b, slabs, tm, tn):
    bsz, seq, k = x.shape
    n = w.shape[1]
    n_slab = n // slabs
    tm = _tile(seq, tm)
    tn = _tile(n_slab, tn)
    per = n_slab // tn
    return pl.pallas_call(
        _mod_proj_kernel,
        out_shape=jax.ShapeDtypeStruct((slabs, bsz, seq, n_slab), MXU_DTYPE),
        grid=(bsz, seq // tm, n // tn),
        in_specs=[
            pl.BlockSpec((1, tm, k), lambda bi, i, j: (bi, i, 0)),
            pl.BlockSpec((1, 1, k), lambda bi, i, j: (bi, 0, 0)),
            pl.BlockSpec((1, 1, k), lambda bi, i, j: (bi, 0, 0)),
            pl.BlockSpec((k, tn), lambda bi, i, j: (0, j)),
            pl.BlockSpec((1, tn), lambda bi, i, j: (0, j)),
        ],
        out_specs=pl.BlockSpec((1, 1, tm, tn), lambda bi, i, j: (j // per, bi, i, j % per)),
        scratch_shapes=[pltpu.VMEM((tm, k), MXU_DTYPE)],
        compiler_params=_params(("parallel", "parallel", "arbitrary")),
        name="mod_proj",
    )(x, shift, scale, w.astype(MXU_DTYPE), b.reshape(1, n))


def _hyena_in_kernel(x_ref, prev_ref, next_ref, sh_ref, sc_ref, w0_ref, w1_ref, w2_ref, b_ref, cw_ref, cb_ref,
                     x0_ref, v_ref, u_scr):
    i = pl.program_id(1)
    last = pl.num_programs(1) - 1
    tm = x_ref.shape[1]
    ext = tm + 2 * SUBLANES

    @pl.when(pl.program_id(2) == 0)
    def _():
        shift, scale = sh_ref[0], sc_ref[0]
        rows = jnp.concatenate([prev_ref[0], x_ref[0], next_ref[0]], axis=0)
        u_scr[...] = (rows * (1.0 + scale) + shift).astype(MXU_DTYPE)

    u = u_scr[...]
    outs = []
    for s, w_ref in enumerate((w0_ref, w1_ref, w2_ref)):
        raw = _dot(u, w_ref[...])
        b, taps = b_ref[s], cw_ref[s]
        head = jnp.where(i == 0, -b, raw[:SUBLANES])
        tail = jnp.where(i == last, -b, raw[SUBLANES + tm:])
        raw = jnp.concatenate([head, raw[SUBLANES:SUBLANES + tm], tail], axis=0)
        conv = pltpu.roll(raw, 1, 0) * taps[0:1] + raw * taps[1:2] + pltpu.roll(raw, ext - 1, 0) * taps[2:3]
        const = b * (taps[0:1] + taps[1:2] + taps[2:3]) + cb_ref[s]
        outs.append(conv[SUBLANES:SUBLANES + tm] + const)
    x0_ref[0] = outs[0].astype(x0_ref.dtype)
    v_ref[0] = (outs[2] * outs[1]).astype(v_ref.dtype)


def _hyena_in(x, shift, scale, w_in, b_in, conv_w, conv_b):
    bsz, seq, k = x.shape
    d = w_in.shape[1] // 3
    tm = _tile(seq, 512)
    tn = _tile(d, 512)
    per = d // tn
    hb = tm // SUBLANES
    nhb = seq // SUBLANES
    w = w_in.astype(MXU_DTYPE)
    cw = conv_w.reshape(3, 3, d).transpose(1, 0, 2)
    out = jax.ShapeDtypeStruct((bsz, seq, d), MXU_DTYPE)
    slab = lambda s: pl.BlockSpec((k, tn), lambda bi, i, j: (0, s * per + j))
    return pl.pallas_call(
        _hyena_in_kernel,
        out_shape=(out, out),
        grid=(bsz, seq // tm, per),
        in_specs=[
            pl.BlockSpec((1, tm, k), lambda bi, i, j: (bi, i, 0)),
            pl.BlockSpec((1, SUBLANES, k), lambda bi, i, j: (bi, jnp.maximum(i * hb - 1, 0), 0)),
            pl.BlockSpec((1, SUBLANES, k), lambda bi, i, j: (bi, jnp.minimum((i + 1) * hb, nhb - 1), 0)),
            pl.BlockSpec((1, 1, k), lambda bi, i, j: (bi, 0, 0)),
            pl.BlockSpec((1, 1, k), lambda bi, i, j: (bi, 0, 0)),
            slab(0), slab(1), slab(2),
            pl.BlockSpec((3, 1, tn), lambda bi, i, j: (0, 0, j)),
            pl.BlockSpec((3, 3, tn), lambda bi, i, j: (0, 0, j)),
            pl.BlockSpec((3, 1, tn), lambda bi, i, j: (0, 0, j)),
        ],
        out_specs=(
            pl.BlockSpec((1, tm, tn), lambda bi, i, j: (bi, i, j)),
            pl.BlockSpec((1, tm, tn), lambda bi, i, j: (bi, i, j)),
        ),
        scratch_shapes=[pltpu.VMEM((tm + 2 * SUBLANES, k), MXU_DTYPE)],
        compiler_params=_params(("parallel", "parallel", "arbitrary")),
        name="hyena_in_conv_gate",
    )(x, x, x, shift, scale, w, w, w, b_in.reshape(3, 1, d), cw, conv_b.reshape(3, 1, d))


def _filter_kernel(z_ref, dist_ref, w_in_ref, w_hid_ref, b_ref, fr_ref, w_out_ref, delta_ref, f_ref, norm_ref):
    hp = lax.Precision.HIGHEST
    b = b_ref[...]
    fr = fr_ref[...]
    g = jnp.sin(fr[0:1] * (jnp.dot(z_ref[...], w_in_ref[...], precision=hp, preferred_element_type=F32) + b[0:1]))
    g = jnp.sin(fr[1:2] * (jnp.dot(g, w_hid_ref[0], precision=hp, preferred_element_type=F32) + b[1:2]))
    g = jnp.sin(fr[2:3] * (jnp.dot(g, w_hid_ref[1], precision=hp, preferred_element_type=F32) + b[2:3]))
    filt = _dot(g.astype(MXU_DTYPE), w_out_ref[...])
    filt = filt * jnp.exp(-dist_ref[...] * delta_ref[...])
    f_ref[...] = filt

    @pl.when(pl.program_id(0) == 0)
    def _():
        norm_ref[...] = jnp.zeros_like(norm_ref)

    norm_ref[...] += jnp.sum(jnp.abs(filt), axis=0, keepdims=True)


def _first_stage_row_order(seq):
    a_hi, c, a_lo = jnp.meshgrid(jnp.arange(seq // (DFT_N2 * SUBLANES)), jnp.arange(DFT_N2), jnp.arange(SUBLANES),
                                 indexing="ij")
    return (DFT_N2 * (SUBLANES * a_hi + a_lo) + c).reshape(seq)


def _implicit_filter(seq, f_w_in, f_w_hid, f_b, f_freq, f_w_out, positions=None):
    emb, width = f_w_in.shape
    bands_n = (emb - 1) // 2
    d = f_w_out.shape[1]
    emb_pad = -(-emb // LANES) * LANES
    filt_w = -(-width // LANES) * LANES
    pad_to = lambda a, shape: jnp.pad(a.astype(F32), [(0, s - n) for s, n in zip(shape, a.shape)])
    pos = (jnp.arange(seq) if positions is None else positions).astype(F32)
    t = pos / (seq - 1)
    bands = jnp.linspace(1e-4, bands_n - 1, bands_n, dtype=F32)
    ang = (2.0 * math.pi / seq) * pos[:, None] * bands[None, :]
    z = pad_to(jnp.concatenate([t[:, None], jnp.cos(ang), -jnp.sin(ang)], axis=-1), (seq, emb_pad))
    w_in = pad_to(f_w_in, (emb_pad, filt_w))
    f_w_hid = pad_to(f_w_hid, (2, filt_w, filt_w))
    f_b = pad_to(f_b, (3, filt_w))
    f_freq = pad_to(f_freq, (3, filt_w))
    f_w_out = pad_to(f_w_out, (filt_w, d))
    dist = (jnp.abs(pos - seq // 2) / (seq // 2))[:, None]
    deltas = jnp.abs(jnp.linspace(HY_MIN_DECAY, HY_MAX_DECAY, d, dtype=F32))[None, :]
    tl = _tile(seq, 512)
    full = lambda shape: pl.BlockSpec(shape, lambda i: (0,) * len(shape))
    return pl.pallas_call(
        _filter_kernel,
        out_shape=(jax.ShapeDtypeStruct((seq, d), F32), jax.ShapeDtypeStruct((1, d), F32)),
        grid=(seq // tl,),
        in_specs=[
            pl.BlockSpec((tl, emb_pad), lambda i: (i, 0)),
            pl.BlockSpec((tl, 1), lambda i: (i, 0)),
            full((emb_pad, filt_w)),
            full((2, filt_w, filt_w)),
            full((3, filt_w)),
            full((3, filt_w)),
            full((filt_w, d)),
            full((1, d)),
        ],
        out_specs=(pl.BlockSpec((tl, d), lambda i: (i, 0)), full((1, d))),
        compiler_params=_params(("arbitrary",)),
        name="hyena_filter",
    )(z, dist, w_in, f_w_hid, f_b, f_freq, f_w_out.astype(MXU_DTYPE), deltas)


def _real_form(mr, mi):
    top = jnp.concatenate([mr, -mi], axis=-1)
    bot = jnp.concatenate([mi, mr], axis=-1)
    return jnp.concatenate([top, bot], axis=-2)


def _unit_roots(idx, n):
    ang = (2.0 * math.pi / n) * (idx % n).astype(F32)
    return jnp.cos(ang), -jnp.sin(ang)


def _dft_tables(seq):
    n = 2 * seq
    n2 = DFT_N2
    n1 = n // n2
    i32 = jnp.int32
    p = jnp.arange(n1, dtype=i32)
    c = jnp.arange(n2, dtype=i32)
    tr, ti = _unit_roots(c[:, None] * p[None, :], n)
    a_in = jnp.arange(n1 // 2, dtype=i32)
    fr, fi = _unit_roots(p[:, None] * a_in[None, :], n1)
    mr = tr[:, :, None] * fr[None] - ti[:, :, None] * fi[None]
    mi = tr[:, :, None] * fi[None] + ti[:, :, None] * fr[None]
    first = _real_form(mr, mi)
    a_out = jnp.arange(n1 // 4, n1 // 4 + n1 // 2, dtype=i32)
    gr, gi = _unit_roots(p[:, None] * a_out[None, :], n1)
    cr = (tr[:, :, None] * gr[None] - ti[:, :, None] * gi[None]) / n
    ci = -(tr[:, :, None] * gi[None] + ti[:, :, None] * gr[None]) / n
    last = _real_form(jnp.swapaxes(cr, 1, 2), jnp.swapaxes(ci, 1, 2))
    q = jnp.arange(n2, dtype=i32)
    hr, hi_ = _unit_roots(q[:, None] * c[None, :], n2)
    mid_f = _real_form(hr, hi_)
    mid_i = _real_form(hr, -hi_)
    return tuple(t.astype(MXU_DTYPE) for t in (first, mid_f, mid_i, last))


def _dft_first_kernel(x_ref, m_ref, o_ref):
    x = x_ref[...]
    x = x.reshape(x.shape[0] * x.shape[1] * SUBLANES, x.shape[4])
    o_ref[...] = _dot(m_ref[0], x.astype(MXU_DTYPE)).astype(o_ref.dtype)


def _dft_first(x5, table):
    r, a_hi, n2, _, d = x5.shape
    rows = r * a_hi * SUBLANES
    n_out = table.shape[1]
    if rows % LANES:
        table = table[:, :, :rows]
    bd = _tile(d, 2048)
    per = d // bd
    return pl.pallas_call(
        _dft_first_kernel,
        out_shape=jax.ShapeDtypeStruct((n_out, n2 * d), MXU_DTYPE),
        grid=(n2 * per,),
        in_specs=[
            pl.BlockSpec((r, a_hi, 1, SUBLANES, bd), lambda j: (0, 0, j // per, 0, j % per)),
            pl.BlockSpec((1, n_out, rows), lambda j: (j // per, 0, 0)),
        ],
        out_specs=pl.BlockSpec((n_out, bd), lambda j: (0, j)),
        compiler_params=_params(("parallel",)),
        name="hyena_dft_first",
    )(x5, table)


def _dft_mid_filter_kernel(a_ref, f_ref, norm_ref, o_ref):
    n2 = a_ref.shape[2]
    for p in range(a_ref.shape[1]):
        a = jnp.concatenate([a_ref[0, p], a_ref[1, p]], axis=0)
        z = _dot(f_ref[...], a) / norm_ref[...]
        o_ref[0, p] = z[:n2].astype(o_ref.dtype)
        o_ref[1, p] = z[n2:].astype(o_ref.dtype)


def _dft_mid_conv_kernel(a_ref, h_ref, f_ref, g_ref, o_ref):
    n2 = a_ref.shape[2]
    for p in range(a_ref.shape[1]):
        a = jnp.concatenate([a_ref[0, p], a_ref[1, p]], axis=0)
        z = _dot(f_ref[...], a)
        zr, zi = z[:n2], z[n2:]
        hr, hi_ = h_ref[0, p].astype(F32), h_ref[1, p].astype(F32)
        w = jnp.concatenate([zr * hr - zi * hi_, zr * hi_ + zi * hr], axis=0)
        y = _dot(g_ref[...], w.astype(MXU_DTYPE))
        o_ref[0, p] = y[:n2].astype(o_ref.dtype)
        o_ref[1, p] = y[n2:].astype(o_ref.dtype)


def _dft_mid(a, tables, d, *, spectrum=None, norm=None):
    n2 = DFT_N2
    n1 = a.shape[0] // 2
    a4 = a.reshape(2, n1, n2, d)
    bd = _tile(d, 2048)
    pb = _tile(n1, DFT_MID_ROWS)
    blk = pl.BlockSpec((2, pb, n2, bd), lambda p, j: (0, p, 0, j))
    mat = pl.BlockSpec((2 * n2, 2 * n2), lambda p, j: (0, 0))
    fwd, inv = tables
    if spectrum is None:
        kern, ins, specs = _dft_mid_filter_kernel, (a4, fwd, norm), [blk, mat, pl.BlockSpec((1, bd), lambda p, j: (0, j))]
    else:
        kern, ins, specs = _dft_mid_conv_kernel, (a4, spectrum, fwd, inv), [blk, blk, mat, mat]
    out = pl.pallas_call(
        kern,
        out_shape=jax.ShapeDtypeStruct((2, n1, n2, d), MXU_DTYPE),
        grid=(n1 // pb, d // bd),
        in_specs=specs,
        out_specs=blk,
        compiler_params=_params(("parallel", "parallel")),
        name="hyena_dft_mid",
    )(*ins)
    return out


def _dft_last_kernel(x_ref, m_ref, o_ref):
    y = _dot(m_ref[0], x_ref[...])
    o_ref[...] = y.reshape(o_ref.shape).astype(o_ref.dtype)


def _dft_last(b, table, d):
    rows, cols = b.shape
    n1 = rows // 2
    n2 = cols // d
    a_hi = n1 // 2 // SUBLANES
    bd = _tile(d, 2048)
    per = d // bd
    return pl.pallas_call(
        _dft_last_kernel,
        out_shape=jax.ShapeDtypeStruct((2, a_hi, n2, SUBLANES, d), MXU_DTYPE),
        grid=(cols // bd,),
        in_specs=[
            pl.BlockSpec((rows, bd), lambda j: (0, j)),
            pl.BlockSpec((1, n1, rows), lambda j: (j // per, 0, 0)),
        ],
        out_specs=pl.BlockSpec((2, a_hi, 1, SUBLANES, bd), lambda j: (0, 0, j // per, 0, j % per)),
        compiler_params=_params(("parallel",)),
        name="hyena_dft_last",
    )(b, table)


def _long_conv_two_stage(v, filt, norm, tables):
    bsz, seq, d = v.shape
    n1 = 2 * seq // DFT_N2
    a_hi = n1 // 2 // SUBLANES
    first, mid_f, mid_i, last = tables
    cols = DFT_N2 * d
    h_first = _dft_first(filt.reshape(1, a_hi, DFT_N2, SUBLANES, d), first)
    spectrum = _dft_mid(h_first, (mid_f, mid_i), d, norm=norm)
    v5 = v.reshape(bsz, a_hi, SUBLANES, DFT_N2, d).transpose(0, 1, 3, 2, 4)
    a = _dft_first(v5, first)
    bmat = _dft_mid(a, (mid_f, mid_i), d, spectrum=spectrum)
    y5 = _dft_last(bmat.reshape(2 * n1, cols), last, d)
    return y5.transpose(0, 1, 3, 2, 4).reshape(bsz, seq, d)


def _dft_small_kernel(v_ref, h_ref, norm_ref, f_ref, g_ref, o_ref):
    seq = v_ref.shape[1]
    n = 2 * seq
    fwd = f_ref[...]
    x = jnp.concatenate([v_ref[0], v_ref[1]], axis=0)
    z = _dot(fwd, x.astype(MXU_DTYPE))
    hx = jnp.concatenate([h_ref[...] / norm_ref[...], jnp.zeros_like(h_ref)], axis=0)
    hs = _dot(fwd, hx.astype(MXU_DTYPE))
    zr, zi, hr, hi_ = z[:n], z[n:], hs[:n], hs[n:]
    w = jnp.concatenate([zr * hr - zi * hi_, zr * hi_ + zi * hr], axis=0)
    y = _dot(g_ref[...], w.astype(MXU_DTYPE))
    o_ref[0] = y[:seq].astype(o_ref.dtype)
    o_ref[1] = y[seq:].astype(o_ref.dtype)


def _long_conv_small(v, filt, norm):
    bsz, seq, d = v.shape
    n = 2 * seq
    k = jnp.arange(n, dtype=jnp.int32)
    t_in = jnp.arange(seq, dtype=jnp.int32)
    fr, fi = _unit_roots(k[:, None] * t_in[None, :], n)
    fwd = _real_form(fr, fi)
    t_out = jnp.arange(seq // 2, seq // 2 + seq, dtype=jnp.int32)
    gr, gi = _unit_roots(t_out[:, None] * k[None, :], n)
    inv = _real_form(gr / n, -gi / n)
    bd = _tile(d, 512)
    full = lambda shape: pl.BlockSpec(shape, lambda j: (0,) * len(shape))
    return pl.pallas_call(
        _dft_small_kernel,
        out_shape=jax.ShapeDtypeStruct((bsz, seq, d), MXU_DTYPE),
        grid=(d // bd,),
        in_specs=[
            pl.BlockSpec((bsz, seq, bd), lambda j: (0, 0, j)),
            pl.BlockSpec((seq, bd), lambda j: (0, j)),
            pl.BlockSpec((1, bd), lambda j: (0, j)),
            full(fwd.shape), full(inv.shape),
        ],
        out_specs=pl.BlockSpec((bsz, seq, bd), lambda j: (0, 0, j)),
        compiler_params=_params(("parallel",)),
        name="hyena_dft_small",
    )(v, filt, norm, fwd.astype(MXU_DTYPE), inv.astype(MXU_DTYPE))


def _hyena_out_kernel(alpha, x0_ref, cv_ref, v_ref, bias_ref, w_ref, b_ref, h_ref, g_ref, lg_ref, lb_ref, o_ref):
    y = x0_ref[0].astype(F32) * (cv_ref[0].astype(F32) + v_ref[0].astype(F32) * bias_ref[...])
    y = _dot(y.astype(MXU_DTYPE), w_ref[...]) + b_ref[...]
    o_ref[0] = _layer_norm_rows(alpha * h_ref[0] + g_ref[0] * y, lg_ref[...], lb_ref[...])


def _attn_out_kernel(alpha, o_in_ref, w_ref, h_ref, g_ref, lg_ref, lb_ref, o_ref):
    y = _dot(o_in_ref[0], w_ref[...])
    o_ref[0] = _layer_norm_rows(alpha * h_ref[0] + g_ref[0] * y, lg_ref[...], lb_ref[...])


def _row_spec(tm, d):
    return pl.BlockSpec((1, tm, d), lambda bi, i: (bi, i, 0))


def _vec_spec(d):
    return pl.BlockSpec((1, d), lambda bi, i: (0, 0))


def _bvec_spec(d):
    return pl.BlockSpec((1, 1, d), lambda bi, i: (bi, 0, 0))


def _const_spec(shape):
    return pl.BlockSpec(shape, lambda bi, i: (0,) * len(shape), pipeline_mode=pl.Buffered(1))


def _hyena_out(alpha, x0, cv, v, bias, w, b, h, gate, ln_g, ln_b):
    bsz, seq, d = h.shape
    tm = _tile(seq, 512)
    return pl.pallas_call(
        functools.partial(_hyena_out_kernel, alpha),
        out_shape=jax.ShapeDtypeStruct(h.shape, F32),
        grid=(bsz, seq // tm),
        in_specs=[_row_spec(tm, d), _row_spec(tm, d), _row_spec(tm, d), _vec_spec(d), _const_spec((d, d)),
                  _vec_spec(d), _row_spec(tm, d), _bvec_spec(d), _vec_spec(d), _vec_spec(d)],
        out_specs=_row_spec(tm, d),
        compiler_params=_params(("parallel", "parallel")),
        name="hyena_out_norm",
    )(x0, cv, v, bias.reshape(1, d), w.astype(MXU_DTYPE), b.reshape(1, d), h, gate, ln_g.reshape(1, d), ln_b.reshape(1, d))


def _attn_out(alpha, o, w, h, gate, ln_g, ln_b):
    bsz, seq, d = h.shape
    k = o.shape[2]
    tm = _tile(seq, 512)
    return pl.pallas_call(
        functools.partial(_attn_out_kernel, alpha),
        out_shape=jax.ShapeDtypeStruct(h.shape, F32),
        grid=(bsz, seq // tm),
        in_specs=[_row_spec(tm, k), _const_spec((k, d)), _row_spec(tm, d), _bvec_spec(d), _vec_spec(d), _vec_spec(d)],
        out_specs=_row_spec(tm, d),
        compiler_params=_params(("parallel", "parallel")),
        name="attn_out_norm",
    )(o, w.astype(MXU_DTYPE), h, gate, ln_g.reshape(1, d), ln_b.reshape(1, d))


def _ffn_kernel(alpha, h_ref, sh_ref, sc_ref, g_ref, wg_ref, wu_ref, wd_ref, lg_ref, lb_ref, o_ref, u_scr, act_scr, acc_scr):
    f = pl.program_id(2)
    last = pl.num_programs(2) - 1

    def activation():
        u = u_scr[...]
        return (_silu(_dot(u, wg_ref[...])) * _dot(u, wu_ref[...])).astype(MXU_DTYPE)

    @pl.when(f == 0)
    def _():
        u_scr[...] = (h_ref[0] * (1.0 + sc_ref[0]) + sh_ref[0]).astype(MXU_DTYPE)
        act_scr[...] = activation()

    @pl.when(f == 1)
    def _():
        acc_scr[...] = _dot(act_scr[...], wd_ref[...])
        act_scr[...] = activation()

    @pl.when(jnp.logical_and(f > 1, f < last))
    def _():
        acc_scr[...] += _dot(act_scr[...], wd_ref[...])
        act_scr[...] = activation()

    @pl.when(f == last)
    def _():
        y = acc_scr[...] + _dot(act_scr[...], wd_ref[...])
        o_ref[0] = _layer_norm_rows(alpha * h_ref[0] + g_ref[0] * y, lg_ref[...], lb_ref[...])


def _ffn(alpha, h, shift, scale, gate, w_gate, w_up, w_down, ln_g, ln_b):
    bsz, seq, d = h.shape
    ff = w_gate.shape[1]
    tm = _tile(seq, 512)
    tf = _tile(ff, 512)
    nf = ff // tf
    assert nf >= 2, "the lagged down-projection needs at least two hidden chunks"
    return pl.pallas_call(
        functools.partial(_ffn_kernel, alpha),
        out_shape=jax.ShapeDtypeStruct(h.shape, F32),
        grid=(bsz, seq // tm, nf + 1),
        in_specs=[
            pl.BlockSpec((1, tm, d), lambda bi, i, f: (bi, i, 0)),
            pl.BlockSpec((1, 1, d), lambda bi, i, f: (bi, 0, 0)),
            pl.BlockSpec((1, 1, d), lambda bi, i, f: (bi, 0, 0)),
            pl.BlockSpec((1, 1, d), lambda bi, i, f: (bi, 0, 0)),
            pl.BlockSpec((d, tf), lambda bi, i, f: (0, jnp.minimum(f, nf - 1))),
            pl.BlockSpec((d, tf), lambda bi, i, f: (0, jnp.minimum(f, nf - 1))),
            pl.BlockSpec((tf, d), lambda bi, i, f: (jnp.maximum(f - 1, 0), 0)),
            pl.BlockSpec((1, d), lambda bi, i, f: (0, 0)),
            pl.BlockSpec((1, d), lambda bi, i, f: (0, 0)),
        ],
        out_specs=pl.BlockSpec((1, tm, d), lambda bi, i, f: (bi, i, 0)),
        scratch_shapes=[pltpu.VMEM((tm, d), MXU_DTYPE), pltpu.VMEM((tm, tf), MXU_DTYPE), pltpu.VMEM((tm, d), F32)],
        compiler_params=_params(("parallel", "parallel", "arbitrary")),
        name="ffn_swiglu_norm",
    )(h, shift, scale, gate, w_gate.astype(MXU_DTYPE), w_up.astype(MXU_DTYPE), w_down.astype(MXU_DTYPE),
      ln_g.reshape(1, d), ln_b.reshape(1, d))


def _rotate_half_cols(w):
    ws = w.reshape(w.shape[:-1] + (2, 2, ROPE_PAIRS))
    return jnp.stack([-ws[..., 1, :], ws[..., 0, :]], axis=-2).reshape(w.shape)


def _rope_table(seq):
    rows = seq // GRID_W
    row = jnp.repeat(jnp.arange(rows, dtype=F32), GRID_W)
    col = jnp.tile(jnp.arange(GRID_W, dtype=F32), rows)
    inv = ROPE_THETA ** (-jnp.arange(ROPE_PAIRS, dtype=F32) / ROPE_PAIRS)
    ang = jnp.stack([row[:, None] * inv, col[:, None] * inv], axis=1)
    ang = jnp.broadcast_to(ang[:, :, None, :], (seq, 2, 2, ROPE_PAIRS)).reshape(seq, QK_ROPE)
    return jnp.concatenate([jnp.cos(ang), jnp.sin(ang)], axis=-1)


def _rms_rows(x, g):
    return x * lax.rsqrt(jnp.mean(x * x, axis=-1, keepdims=True) + RMS_EPS) * g


def _rope_pair(x, cs):
    t = x * cs
    return t + pltpu.roll(t, QK_ROPE, 1)


def _mla_q_kernel(cq_ref, g_ref, w_ref, cs_ref, q_ref):
    xn = _rms_rows(cq_ref[0].astype(F32), g_ref[...]).astype(MXU_DTYPE)
    cs = cs_ref[...]

    def head(h, carry):
        a = _dot(xn, w_ref[h])
        r = _rope_pair(a[:, QK_NOPE:], cs)
        q_ref[0, h] = (jnp.concatenate([a[:, :QK_NOPE], r], axis=1) * QUERY_SCALE).astype(q_ref.dtype)
        return carry

    lax.fori_loop(0, w_ref.shape[0], head, 0, unroll=2)


def _mla_queries(t, q_norm, wq_b, cs, heads):
    bsz, seq, _ = t.shape
    rank = q_norm.shape[0]
    tm = _tile(seq, 512)
    w = wq_b.reshape(rank, heads, QK_NOPE + QK_ROPE)
    w = jnp.concatenate([w, _rotate_half_cols(w[..., QK_NOPE:])], axis=-1).transpose(1, 0, 2).astype(MXU_DTYPE)
    return pl.pallas_call(
        _mla_q_kernel,
        out_shape=jax.ShapeDtypeStruct((bsz, heads, seq, 2 * LANES), MXU_DTYPE),
        grid=(bsz, seq // tm),
        in_specs=[
            pl.BlockSpec((1, tm, rank), lambda bi, i: (bi, i, 0)),
            pl.BlockSpec((1, rank), lambda bi, i: (0, 0)),
            pl.BlockSpec((heads, rank, 2 * LANES), lambda bi, i: (0, 0, 0)),
            pl.BlockSpec((tm, LANES), lambda bi, i: (i, 0)),
        ],
        out_specs=pl.BlockSpec((1, heads, tm, 2 * LANES), lambda bi, i: (bi, 0, i, 0)),
        compiler_params=_params(("parallel", "parallel")),
        name="mla_queries",
    )(t, q_norm.reshape(1, rank), w, cs)


def _mla_kv_kernel(use_rope, ckv_ref, kr_ref, g_ref, w_ref, cs_ref, k_ref, vt_ref):
    xn = _rms_rows(ckv_ref[0].astype(F32), g_ref[...]).astype(MXU_DTYPE)
    kr = kr_ref[0].astype(F32)
    if use_rope:
        kr = _rope_pair(kr, cs_ref[...])
    lane = lax.broadcasted_iota(jnp.int32, kr.shape, 1)
    kr = jnp.where(lane < QK_ROPE, kr, 0.0).astype(k_ref.dtype)
    ones = jnp.ones((V_ROWS - V_HEAD, xn.shape[0]), vt_ref.dtype)

    def head(h, carry):
        kv = _dot(xn, w_ref[h])
        k_ref[0, h] = jnp.concatenate([kv[:, :QK_NOPE].astype(k_ref.dtype), kr], axis=1)
        vt_ref[0, h] = jnp.concatenate([kv[:, QK_NOPE:].T.astype(vt_ref.dtype), ones], axis=0)
        return carry

    lax.fori_loop(0, w_ref.shape[0], head, 0, unroll=2)


def _mla_keys_values(t, kv_norm, wkv_b, cs, heads, use_rope):
    bsz, seq, _ = t.shape
    rank = kv_norm.shape[0]
    tm = _tile(seq, 512)
    w = wkv_b.reshape(rank, heads, QK_NOPE + V_HEAD).transpose(1, 0, 2).astype(MXU_DTYPE)
    return pl.pallas_call(
        functools.partial(_mla_kv_kernel, use_rope),
        out_shape=(jax.ShapeDtypeStruct((bsz, heads, seq, 2 * LANES), MXU_DTYPE),
                   jax.ShapeDtypeStruct((bsz, heads, V_ROWS, seq), MXU_DTYPE)),
        grid=(bsz, seq // tm),
        in_specs=[
            pl.BlockSpec((1, tm, rank), lambda bi, i: (bi, i, 1)),
            pl.BlockSpec((1, tm, LANES), lambda bi, i: (bi, i, 2 * rank // LANES)),
            pl.BlockSpec((1, rank), lambda bi, i: (0, 0)),
            pl.BlockSpec((heads, rank, QK_NOPE + V_HEAD), lambda bi, i: (0, 0, 0)),
            pl.BlockSpec((tm, LANES), lambda bi, i: (i, 0)),
        ],
        out_specs=(pl.BlockSpec((1, heads, tm, 2 * LANES), lambda bi, i: (bi, 0, i, 0)),
                   pl.BlockSpec((1, heads, V_ROWS, tm), lambda bi, i: (bi, 0, 0, i))),
        compiler_params=_params(("parallel", "parallel")),
        name="mla_keys_values",
    )(t, t, kv_norm.reshape(1, rank), w, cs)


def _attn_kernel(q_ref, k_ref, vt_ref, kc_ref, vtc_ref, o_ref, qt_scr, s_scr, p_scr, smax_scr, m_scr, alpha_scr, acc_scr):
    kv = pl.program_id(3)
    n_main = pl.num_programs(3) - 1

    n_chunks, _, cw = qt_scr.shape

    @pl.when(kv == 0)
    def _():
        for c in range(n_chunks):
            qt_scr[c] = q_ref[0, 0, c * cw:(c + 1) * cw, :].T
        m_scr[...] = jnp.full_like(m_scr, -jnp.inf)
        acc_scr[...] = jnp.zeros_like(acc_scr)

    def step(k, vt):
        nk = k.shape[0]

        def scores(c, slot):
            s = _dot(k, qt_scr[c])
            s_scr[slot, :nk, :] = s
            smax_scr[slot] = jnp.max(s, axis=0, keepdims=True)

        def exponent(c, slot):
            m_prev = m_scr[c]
            m_new = jnp.maximum(m_prev, smax_scr[slot])
            alpha_scr[c] = jnp.exp2(m_prev - m_new)
            p_scr[slot, :nk, :] = jnp.exp2((s_scr[slot, :nk, :] - m_new).astype(MXU_DTYPE))
            m_scr[c] = m_new

        def values(c, slot):
            acc_scr[c] = alpha_scr[c] * acc_scr[c] + _dot(vt, p_scr[slot, :nk, :])

        for i in range(n_chunks + 2):
            if i < n_chunks:
                scores(i, i % 2)
            if 1 <= i <= n_chunks:
                exponent(i - 1, (i - 1) % 2)
            if i >= 2:
                values(i - 2, i % 2)

    @pl.when(kv < n_main)
    def _():
        step(k_ref[0, 0], vt_ref[0, 0])

    @pl.when(kv == n_main)
    def _():
        step(kc_ref[0, 0], vtc_ref[0, 0])
        for c in range(n_chunks):
            acc = acc_scr[c]
            o_ref[0, c * cw:(c + 1) * cw, :] = (acc[:V_HEAD] / acc[V_HEAD:V_HEAD + 1]).T.astype(o_ref.dtype)


def _mla_attend(q, k, vt, kc, vtc):
    bsz, heads, seq, dq = q.shape
    lc = kc.shape[2]
    tq = _tile(seq, 2048)
    tk = _tile(seq, 2048)
    cw = min(tq, ATTN_CHUNK)
    assert lc <= tk
    n_main = seq // tk
    return pl.pallas_call(
        _attn_kernel,
        out_shape=jax.ShapeDtypeStruct((bsz, seq, heads * V_HEAD), MXU_DTYPE),
        grid=(bsz, heads, seq // tq, n_main + 1),
        in_specs=[
            pl.BlockSpec((1, 1, tq, dq), lambda bi, h, i, j: (bi, h, i, 0)),
            pl.BlockSpec((1, 1, tk, dq), lambda bi, h, i, j: (bi, h, jnp.minimum(j, n_main - 1), 0)),
            pl.BlockSpec((1, 1, V_ROWS, tk), lambda bi, h, i, j: (bi, h, 0, jnp.minimum(j, n_main - 1))),
            pl.BlockSpec((1, 1, lc, dq), lambda bi, h, i, j: (bi, h, 0, 0)),
            pl.BlockSpec((1, 1, V_ROWS, lc), lambda bi, h, i, j: (bi, h, 0, 0)),
        ],
        out_specs=pl.BlockSpec((1, tq, V_HEAD), lambda bi, h, i, j: (bi, i, h)),
        scratch_shapes=[pltpu.VMEM((tq // cw, dq, cw), MXU_DTYPE), pltpu.VMEM((2, tk, cw), F32),
                        pltpu.VMEM((2, tk, cw), MXU_DTYPE), pltpu.VMEM((2, 1, cw), F32),
                        pltpu.VMEM((tq // cw, 1, cw), F32), pltpu.VMEM((tq // cw, 1, cw), F32), pltpu.VMEM((tq // cw, V_ROWS, cw), F32)],
        compiler_params=_params(("parallel", "parallel", "parallel", "arbitrary")),
        name="mla_attention",
    )(q, k, vt, kc, vtc)


def _pool_kernel(alpha, seq, h_ref, prev_ref, next_ref, sh_ref, sc_ref, g_ref, w_ref, ps_ref, lg_ref, lb_ref, o_ref):
    i = pl.program_id(1)
    last = pl.num_programs(1) - 1
    tm = h_ref.shape[1]
    d = h_ref.shape[2]
    groups = len(POOL_WINDOWS)
    ch = d // groups
    ext = tm + 2 * POOL_HALO
    shift, scale = sh_ref[0], sc_ref[0]
    h = h_ref[0]
    u = h * (1.0 + scale) + shift
    u_prev = jnp.where(i == 0, 0.0, prev_ref[0] * (1.0 + scale) + shift)
    u_next = jnp.where(i == last, 0.0, next_ref[0] * (1.0 + scale) + shift)
    e = jnp.concatenate([u_prev, u, u_next], axis=0)
    t = i * tm + lax.broadcasted_iota(jnp.int32, (tm, 1), 0)
    ys = []
    for g, win in enumerate(POOL_WINDOWS):
        a = e[:, g * ch:(g + 1) * ch]
        span = 1
        while span < win:
            a = a + pltpu.roll(a, ext - span, 0)
            span *= 2
        half = win // 2
        a = pltpu.roll(a, half, 0)
        s = a[POOL_HALO:POOL_HALO + tm]
        cnt = (jnp.minimum(t + half, seq) - jnp.maximum(t - half, 0)).astype(F32)
        dg = s / cnt - u[:, g * ch:(g + 1) * ch]
        ys.append(_dot(dg.astype(MXU_DTYPE), w_ref[g]))
    y = jnp.concatenate(ys, axis=1) * ps_ref[...]
    o_ref[0] = _layer_norm_rows(alpha * h + g_ref[0] * y, lg_ref[...], lb_ref[...])


def _pool_mixer(alpha, h, shift, scale, gate, w_grp, pool_scale, ln_g, ln_b):
    bsz, seq, d = h.shape
    groups, ch, _ = w_grp.shape
    tm = _tile(seq, 512)
    hb = tm // POOL_HALO
    nhb = seq // POOL_HALO
    return pl.pallas_call(
        functools.partial(_pool_kernel, alpha, seq),
        out_shape=jax.ShapeDtypeStruct(h.shape, F32),
        grid=(bsz, seq // tm),
        in_specs=[
            _row_spec(tm, d),
            pl.BlockSpec((1, POOL_HALO, d), lambda bi, i: (bi, jnp.maximum(i * hb - 1, 0), 0)),
            pl.BlockSpec((1, POOL_HALO, d), lambda bi, i: (bi, jnp.minimum((i + 1) * hb, nhb - 1), 0)),
            _bvec_spec(d), _bvec_spec(d), _bvec_spec(d),
            pl.BlockSpec((groups, ch, ch), lambda bi, i: (0, 0, 0)),
            _vec_spec(d), _vec_spec(d), _vec_spec(d),
        ],
        out_specs=_row_spec(tm, d),
        compiler_params=_params(("parallel", "parallel")),
        name="pool_mixer_norm",
    )(h, h, h, shift, scale, gate, w_grp.astype(MXU_DTYPE), pool_scale.reshape(1, d), ln_g.reshape(1, d), ln_b.reshape(1, d))


def _hyena_mixer_norm(alpha, h, mod, hy, filt, norm, tables, ln_g, ln_b):
    shift, scale, gate = mod
    w_in, b_in, conv_w, conv_b, bias, w_out, b_out = hy
    x0, v = _hyena_in(h, shift, scale, w_in, b_in, conv_w, conv_b)
    if h.shape[1] <= DFT_SMALL_MAX_L:
        cv = _long_conv_small(v, filt, norm)
    else:
        cv = _long_conv_two_stage(v, filt, norm, tables)
    return _hyena_out(alpha, x0, cv, v, bias, w_out, b_out, h, gate, ln_g, ln_b)


def kernel(x, c, ctx, c_ctx, ada_w, ada_b, ln_g, ln_b, ffn_w_gate, ffn_w_up, ffn_w_down, hy_w_in, hy_b_in, hy_conv_w, hy_conv_b, hy_f_w_in, hy_f_w_hid, hy_f_b, hy_f_freq, hy_f_w_out, hy_bias, hy_w_out, hy_b_out, mla_w_in, mla_q_norm, mla_kv_norm, mla_wq_b, mla_wkv_b, mla_w_out, pool_w, pool_scale):
    bsz, seq, d = x.shape
    depth = ada_w.shape[0]
    assert bsz == 2, "the long convolution packs exactly two batch rows into one complex signal"
    assert ctx.shape[0] == bsz and seq % GRID_W == 0
    alpha = (2.0 * depth) ** 0.25
    heads = d // V_HEAD
    mla_layers = [i for i in range(depth) if i % N_MIXERS == 1]
    last_ctx_read = mla_layers[-1] if mla_layers else -1

    cond = jnp.concatenate([c, jnp.broadcast_to(c_ctx[None], (SUBLANES - bsz, d))], axis=0)
    mods = _ada_mods(cond, ada_w, ada_b)

    def mod_vecs(i, ctx_stream):
        m = jnp.broadcast_to(mods[i, bsz][None], (bsz, 6 * d)) if ctx_stream else mods[i, :bsz]
        return [m[:, None, k * d:(k + 1) * d] for k in range(6)]

    cs = _rope_table(seq)
    two_stage = seq > DFT_SMALL_MAX_L
    tables = _dft_tables(seq) if two_stage else None
    row_order = _first_stage_row_order(seq) if two_stage else None

    h, hc = x, ctx
    for i in range(depth):
        kind, j = i % N_MIXERS, i // N_MIXERS
        ctx_update = i < last_ctx_read
        sh1, sc1, g1, sh2, sc2, g2 = mod_vecs(i, False)
        if kind == 1 or ctx_update:
            csh1, csc1, cg1, csh2, csc2, cg2 = mod_vecs(i, True)
        lg, lb = ln_g[i], ln_b[i]
        if kind == 0:
            hy = (hy_w_in[j], hy_b_in[j], hy_conv_w[j], hy_conv_b[j], hy_bias[j], hy_w_out[j], hy_b_out[j])
            fp = (hy_f_w_in[j], hy_f_w_hid[j], hy_f_b[j], hy_f_freq[j], hy_f_w_out[j])
            filt, norm = _implicit_filter(seq, *fp, positions=row_order)
            h_mid = _hyena_mixer_norm(alpha, h, (sh1, sc1, g1), hy, filt, norm, tables, lg[0], lb[0])
            if ctx_update:
                filt_c, norm_c = _implicit_filter(hc.shape[1], *fp)
                hc_mid = _hyena_mixer_norm(alpha, hc, (csh1, csc1, cg1), hy, filt_c, norm_c, None, lg[0], lb[0])
        elif kind == 1:
            assert not ctx_update, "context queries are only needed when a later layer reads the context"
            rank = mla_q_norm.shape[1]
            w_in = mla_w_in[j]
            w_in = jnp.concatenate([w_in, _rotate_half_cols(w_in[:, 2 * rank:])], axis=1)
            zeros = jnp.zeros((w_in.shape[1],), F32)
            t = _mod_proj(h, sh1, sc1, w_in, zeros, slabs=1, tm=512, tn=w_in.shape[1])[0]
            tc = _mod_proj(hc, csh1, csc1, w_in, zeros, slabs=1, tm=512, tn=w_in.shape[1])[0]
            q = _mla_queries(t, mla_q_norm[j], mla_wq_b[j], cs, heads)
            k, v = _mla_keys_values(t, mla_kv_norm[j], mla_wkv_b[j], cs, heads, True)
            kc, vc = _mla_keys_values(tc, mla_kv_norm[j], mla_wkv_b[j], cs, heads, False)
            o = _mla_attend(q, k, v, kc, vc)
            h_mid = _attn_out(alpha, o, mla_w_out[j], h, g1, lg[0], lb[0])
        else:
            h_mid = _pool_mixer(alpha, h, sh1, sc1, g1, pool_w[j], pool_scale[j], lg[0], lb[0])
            if ctx_update:
                hc_mid = _pool_mixer(alpha, hc, csh1, csc1, cg1, pool_w[j], pool_scale[j], lg[0], lb[0])
        ffn = (ffn_w_gate[i], ffn_w_up[i], ffn_w_down[i])
        h = _ffn(alpha, h_mid, sh2, sc2, g2, *ffn, lg[1], lb[1])
        if ctx_update:
            hc = _ffn(alpha, hc_mid, csh2, csc2, cg2, *ffn, lg[1], lb[1])
    return h
```

```python
import functools
import math

import jax
import jax.numpy as jnp
from jax import lax
from jax.experimental import pallas as pl
from jax.experimental.pallas import tpu as pltpu

F32 = jnp.float32
MXU_DTYPE = jnp.bfloat16

V7X_VMEM_BYTES = 64 * 1024 * 1024
VMEM_LIMIT = V7X_VMEM_BYTES - 8 * 1024 * 1024
LANES = 128
SUBLANES = 8

N_MIXERS = 3
GRID_W = 64
LN_EPS = 1e-6
RMS_EPS = 1e-6
HY_TARGET = 1e-2
HY_FAST = 0.3
HY_SLOW = 1.5
HY_MIN_DECAY = math.log(HY_TARGET) / HY_SLOW
HY_MAX_DECAY = math.log(HY_TARGET) / HY_FAST
QK_NOPE = 128
QK_ROPE = 64
V_HEAD = 128
ROPE_PAIRS = QK_ROPE // 4
ROPE_THETA = 10000.0
ATTN_SCALE = (QK_NOPE + QK_ROPE) ** -0.5
QUERY_SCALE = ATTN_SCALE * math.log2(math.e)
ATTN_CHUNK = 512
V_ROWS = V_HEAD + 16
POOL_WINDOWS = (2, 4, 8, 16)
POOL_HALO = 8

DFT_N2 = 128
DFT_SMALL_MAX_L = 512
DFT_MID_ROWS = 4


def _params(sem, vmem=VMEM_LIMIT, flags=None):
    return pltpu.CompilerParams(dimension_semantics=sem, vmem_limit_bytes=vmem, flags=flags)


def _tile(n, t):
    t = min(n, t)
    assert n % t == 0, (n, t)
    return t


def _dot(a, b):
    return jnp.dot(a, b, preferred_element_type=F32)


def _layer_norm_rows(x, g, b):
    mu = jnp.mean(x, axis=-1, keepdims=True)
    xc = x - mu
    var = jnp.mean(xc * xc, axis=-1, keepdims=True)
    return xc * lax.rsqrt(var + LN_EPS) * g + b


def _silu(x):
    return x * (1.0 / (1.0 + jnp.exp(-x)))


def _ada_kernel(c_ref, w_ref, b_ref, o_ref):
    c = c_ref[...]
    a = _silu(c).astype(MXU_DTYPE)
    o_ref[0] = _dot(a, w_ref[0].astype(MXU_DTYPE)) + b_ref[0]


def _ada_mods(cond, ada_w, ada_b):
    depth, d, n = ada_w.shape
    rows = cond.shape[0]
    tn = _tile(n, 1024)
    return pl.pallas_call(
        _ada_kernel,
        out_shape=jax.ShapeDtypeStruct((depth, rows, n), F32),
        grid=(depth, n // tn),
        in_specs=[
            pl.BlockSpec((rows, d), lambda i, j: (0, 0)),
            pl.BlockSpec((1, d, tn), lambda i, j: (i, 0, j)),
            pl.BlockSpec((1, 1, tn), lambda i, j: (i, 0, j)),
        ],
        out_specs=pl.BlockSpec((1, rows, tn), lambda i, j: (i, 0, j)),
        compiler_params=_params(("parallel", "parallel")),
        name="ada_mods",
    )(cond, ada_w, ada_b.reshape(depth, 1, n))


def _mod_proj_kernel(x_ref, sh_ref, sc_ref, w_ref, b_ref, o_ref, u_scr):
    @pl.when(pl.program_id(2) == 0)
    def _():
        u_scr[...] = (x_ref[0] * (1.0 + sc_ref[0]) + sh_ref[0]).astype(MXU_DTYPE)

    o_ref[0, 0] = (_dot(u_scr[...], w_ref[...]) + b_ref[...]).astype(o_ref.dtype)


def _mod_proj(x, shift, scale, w, b, slabs, tm, tn):
    bsz, seq, k = x.shape
    n = w.shape[1]
    n_slab = n // slabs
    tm = _tile(seq, tm)
    tn = _tile(n_slab, tn)
    per = n_slab // tn
    return pl.pallas_call(
        _mod_proj_kernel,
        out_shape=jax.ShapeDtypeStruct((slabs, bsz, seq, n_slab), MXU_DTYPE),
        grid=(bsz, seq // tm, n // tn),
        in_specs=[
            pl.BlockSpec((1, tm, k), lambda bi, i, j: (bi, i, 0)),
            pl.BlockSpec((1, 1, k), lambda bi, i, j: (bi, 0, 0)),
            pl.BlockSpec((1, 1, k), lambda bi, i, j: (bi, 0, 0)),
            pl.BlockSpec((k, tn), lambda bi, i, j: (0, j)),
            pl.BlockSpec((1, tn), lambda bi, i, j: (0, j)),
        ],
        out_specs=pl.BlockSpec((1, 1, tm, tn), lambda bi, i, j: (j // per, bi, i, j % per)),
        scratch_shapes=[pltpu.VMEM((tm, k), MXU_DTYPE)],
        compiler_params=_params(("parallel", "parallel", "arbitrary")),
        name="mod_proj",
    )(x, shift, scale, w.astype(MXU_DTYPE), b.reshape(1, n))


def _hyena_in_kernel(x_ref, prev_ref, next_ref, sh_ref, sc_ref, w0_ref, w1_ref, w2_ref, b_ref, cw_ref, cb_ref,
                     x0_ref, v_ref, u_scr):
    i = pl.program_id(1)
    last = pl.num_programs(1) - 1
    tm = x_ref.shape[1]
    ext = tm + 2 * SUBLANES

    @pl.when(pl.program_id(2) == 0)
    def _():
        shift, scale = sh_ref[0], sc_ref[0]
        rows = jnp.concatenate([prev_ref[0], x_ref[0], next_ref[0]], axis=0)
        u_scr[...] = (rows * (1.0 + scale) + shift).astype(MXU_DTYPE)

    u = u_scr[...]
    outs = []
    for s, w_ref in enumerate((w0_ref, w1_ref, w2_ref)):
        raw = _dot(u, w_ref[...])
        b, taps = b_ref[s], cw_ref[s]
        head = jnp.where(i == 0, -b, raw[:SUBLANES])
        tail = jnp.where(i == last, -b, raw[SUBLANES + tm:])
        raw = jnp.concatenate([head, raw[SUBLANES:SUBLANES + tm], tail], axis=0)
        conv = pltpu.roll(raw, 1, 0) * taps[0:1] + raw * taps[1:2] + pltpu.roll(raw, ext - 1, 0) * taps[2:3]
        const = b * (taps[0:1] + taps[1:2] + taps[2:3]) + cb_ref[s]
        outs.append(conv[SUBLANES:SUBLANES + tm] + const)
    x0_ref[0] = outs[0].astype(x0_ref.dtype)
    v_ref[0] = (outs[2] * outs[1]).astype(v_ref.dtype)


def _hyena_in(x, shift, scale, layer, w_in, b_in, conv_w, conv_b):
    bsz, seq, k = x.shape
    d = w_in.shape[2] // 3
    tm = _tile(seq, 512)
    tn = _tile(d, 512)
    per = d // tn
    hb = tm // SUBLANES
    nhb = seq // SUBLANES
    w = w_in
    cw = conv_w.reshape(3, 3, d).transpose(1, 0, 2)
    out = jax.ShapeDtypeStruct((bsz, seq, d), MXU_DTYPE)
    slab = lambda s: pl.BlockSpec((None, k, tn), lambda bi, i, j: (layer, 0, s * per + j))
    return pl.pallas_call(
        _hyena_in_kernel,
        out_shape=(out, out),
        grid=(bsz, seq // tm, per),
        in_specs=[
            pl.BlockSpec((1, tm, k), lambda bi, i, j: (bi, i, 0)),
            pl.BlockSpec((1, SUBLANES, k), lambda bi, i, j: (bi, jnp.maximum(i * hb - 1, 0), 0)),
            pl.BlockSpec((1, SUBLANES, k), lambda bi, i, j: (bi, jnp.minimum((i + 1) * hb, nhb - 1), 0)),
            pl.BlockSpec((1, 1, k), lambda bi, i, j: (bi, 0, 0)),
            pl.BlockSpec((1, 1, k), lambda bi, i, j: (bi, 0, 0)),
            slab(0), slab(1), slab(2),
            pl.BlockSpec((3, 1, tn), lambda bi, i, j: (0, 0, j)),
            pl.BlockSpec((3, 3, tn), lambda bi, i, j: (0, 0, j)),
            pl.BlockSpec((3, 1, tn), lambda bi, i, j: (0, 0, j)),
        ],
        out_specs=(
            pl.BlockSpec((1, tm, tn), lambda bi, i, j: (bi, i, j)),
            pl.BlockSpec((1, tm, tn), lambda bi, i, j: (bi, i, j)),
        ),
        scratch_shapes=[pltpu.VMEM((tm + 2 * SUBLANES, k), MXU_DTYPE)],
        compiler_params=_params(("parallel", "parallel", "arbitrary")),
        name="hyena_in_conv_gate",
    )(x, x, x, shift, scale, w, w, w, b_in.reshape(3, 1, d), cw, conv_b.reshape(3, 1, d))


def _filter_kernel(z_ref, dist_ref, w_in_ref, w_hid_ref, b_ref, fr_ref, w_out_ref, delta_ref, f_ref, norm_ref):
    hp = lax.Precision.HIGHEST
    b = b_ref[...]
    fr = fr_ref[...]
    g = jnp.sin(fr[0:1] * (jnp.dot(z_ref[...], w_in_ref[...], precision=hp, preferred_element_type=F32) + b[0:1]))
    g = jnp.sin(fr[1:2] * (jnp.dot(g, w_hid_ref[0], precision=hp, preferred_element_type=F32) + b[1:2]))
    g = jnp.sin(fr[2:3] * (jnp.dot(g, w_hid_ref[1], precision=hp, preferred_element_type=F32) + b[2:3]))
    filt = _dot(g.astype(MXU_DTYPE), w_out_ref[...])
    filt = filt * jnp.exp(-dist_ref[...] * delta_ref[...])
    f_ref[...] = filt

    @pl.when(pl.program_id(0) == 0)
    def _():
        norm_ref[...] = jnp.zeros_like(norm_ref)

    norm_ref[...] += jnp.sum(jnp.abs(filt), axis=0, keepdims=True)


def _first_stage_row_order(seq):
    a_hi, c, a_lo = jnp.meshgrid(jnp.arange(seq // (DFT_N2 * SUBLANES)), jnp.arange(DFT_N2), jnp.arange(SUBLANES),
                                 indexing="ij")
    return (DFT_N2 * (SUBLANES * a_hi + a_lo) + c).reshape(seq)


def _implicit_filter(seq, f_w_in, f_w_hid, f_b, f_freq, f_w_out, positions=None):
    emb, width = f_w_in.shape
    bands_n = (emb - 1) // 2
    d = f_w_out.shape[1]
    emb_pad = -(-emb // LANES) * LANES
    filt_w = -(-width // LANES) * LANES
    pad_to = lambda a, shape: jnp.pad(a.astype(F32), [(0, s - n) for s, n in zip(shape, a.shape)])
    pos = (jnp.arange(seq) if positions is None else positions).astype(F32)
    t = pos / (seq - 1)
    bands = jnp.linspace(1e-4, bands_n - 1, bands_n, dtype=F32)
    ang = (2.0 * math.pi / seq) * pos[:, None] * bands[None, :]
    z = pad_to(jnp.concatenate([t[:, None], jnp.cos(ang), -jnp.sin(ang)], axis=-1), (seq, emb_pad))
    w_in = pad_to(f_w_in, (emb_pad, filt_w))
    f_w_hid = pad_to(f_w_hid, (2, filt_w, filt_w))
    f_b = pad_to(f_b, (3, filt_w))
    f_freq = pad_to(f_freq, (3, filt_w))
    f_w_out = pad_to(f_w_out, (filt_w, d))
    dist = (jnp.abs(pos - seq // 2) / (seq // 2))[:, None]
    deltas = jnp.abs(jnp.linspace(HY_MIN_DECAY, HY_MAX_DECAY, d, dtype=F32))[None, :]
    tl = _tile(seq, 512)
    full = lambda shape: pl.BlockSpec(shape, lambda i: (0,) * len(shape))
    return pl.pallas_call(
        _filter_kernel,
        out_shape=(jax.ShapeDtypeStruct((seq, d), F32), jax.ShapeDtypeStruct((1, d), F32)),
        grid=(seq // tl,),
        in_specs=[
            pl.BlockSpec((tl, emb_pad), lambda i: (i, 0)),
            pl.BlockSpec((tl, 1), lambda i: (i, 0)),
            full((emb_pad, filt_w)),
            full((2, filt_w, filt_w)),
            full((3, filt_w)),
            full((3, filt_w)),
            full((filt_w, d)),
            full((1, d)),
        ],
        out_specs=(pl.BlockSpec((tl, d), lambda i: (i, 0)), full((1, d))),
        compiler_params=_params(("arbitrary",)),
        name="hyena_filter",
    )(z, dist, w_in, f_w_hid, f_b, f_freq, f_w_out.astype(MXU_DTYPE), deltas)


def _real_form(mr, mi):
    top = jnp.concatenate([mr, -mi], axis=-1)
    bot = jnp.concatenate([mi, mr], axis=-1)
    return jnp.concatenate([top, bot], axis=-2)


def _unit_roots(idx, n):
    ang = (2.0 * math.pi / n) * (idx % n).astype(F32)
    return jnp.cos(ang), -jnp.sin(ang)


def _dft_tables(seq):
    n = 2 * seq
    n2 = DFT_N2
    n1 = n // n2
    i32 = jnp.int32
    p = jnp.arange(n1, dtype=i32)
    c = jnp.arange(n2, dtype=i32)
    tr, ti = _unit_roots(c[:, None] * p[None, :], n)
    a_in = jnp.arange(n1 // 2, dtype=i32)
    fr, fi = _unit_roots(p[:, None] * a_in[None, :], n1)
    mr = tr[:, :, None] * fr[None] - ti[:, :, None] * fi[None]
    mi = tr[:, :, None] * fi[None] + ti[:, :, None] * fr[None]
    first = _real_form(mr, mi)
    a_out = jnp.arange(n1 // 4, n1 // 4 + n1 // 2, dtype=i32)
    gr, gi = _unit_roots(p[:, None] * a_out[None, :], n1)
    cr = (tr[:, :, None] * gr[None] - ti[:, :, None] * gi[None]) / n
    ci = -(tr[:, :, None] * gi[None] + ti[:, :, None] * gr[None]) / n
    last = _real_form(jnp.swapaxes(cr, 1, 2), jnp.swapaxes(ci, 1, 2))
    q = jnp.arange(n2, dtype=i32)
    hr, hi_ = _unit_roots(q[:, None] * c[None, :], n2)
    mid_f = _real_form(hr, hi_)
    mid_i = _real_form(hr, -hi_)
    return tuple(t.astype(MXU_DTYPE) for t in (first, mid_f, mid_i, last))


def _dft_first_kernel(x_ref, m_ref, o_ref):
    x = x_ref[...]
    x = x.reshape(x.shape[0] * x.shape[1] * SUBLANES, x.shape[4])
    o_ref[...] = _dot(m_ref[0], x.astype(MXU_DTYPE)).astype(o_ref.dtype)


def _dft_first(x5, table):
    r, a_hi, n2, _, d = x5.shape
    rows = r * a_hi * SUBLANES
    n_out = table.shape[1]
    if rows % LANES:
        table = table[:, :, :rows]
    bd = _tile(d, 2048)
    per = d // bd
    return pl.pallas_call(
        _dft_first_kernel,
        out_shape=jax.ShapeDtypeStruct((n_out, n2 * d), MXU_DTYPE),
        grid=(n2 * per,),
        in_specs=[
            pl.BlockSpec((r, a_hi, 1, SUBLANES, bd), lambda j: (0, 0, j // per, 0, j % per)),
            pl.BlockSpec((1, n_out, rows), lambda j: (j // per, 0, 0)),
        ],
        out_specs=pl.BlockSpec((n_out, bd), lambda j: (0, j)),
        compiler_params=_params(("parallel",)),
        name="hyena_dft_first",
    )(x5, table)


def _dft_mid_filter_kernel(a_ref, f_ref, norm_ref, o_ref):
    n2 = a_ref.shape[2]
    for p in range(a_ref.shape[1]):
        a = jnp.concatenate([a_ref[0, p], a_ref[1, p]], axis=0)
        z = _dot(f_ref[...], a) / norm_ref[...]
        o_ref[0, p] = z[:n2].astype(o_ref.dtype)
        o_ref[1, p] = z[n2:].astype(o_ref.dtype)


def _dft_mid_conv_kernel(a_ref, h_ref, f_ref, g_ref, o_ref):
    n2 = a_ref.shape[2]
    for p in range(a_ref.shape[1]):
        a = jnp.concatenate([a_ref[0, p], a_ref[1, p]], axis=0)
        z = _dot(f_ref[...], a)
        zr, zi = z[:n2], z[n2:]
        hr, hi_ = h_ref[0, p].astype(F32), h_ref[1, p].astype(F32)
        w = jnp.concatenate([zr * hr - zi * hi_, zr * hi_ + zi * hr], axis=0)
        y = _dot(g_ref[...], w.astype(MXU_DTYPE))
        o_ref[0, p] = y[:n2].astype(o_ref.dtype)
        o_ref[1, p] = y[n2:].astype(o_ref.dtype)


def _dft_mid(a, tables, d, *, spectrum=None, norm=None):
    n2 = DFT_N2
    n1 = a.shape[0] // 2
    a4 = a.reshape(2, n1, n2, d)
    bd = _tile(d, 2048)
    pb = _tile(n1, DFT_MID_ROWS)
    blk = pl.BlockSpec((2, pb, n2, bd), lambda p, j: (0, p, 0, j))
    mat = pl.BlockSpec((2 * n2, 2 * n2), lambda p, j: (0, 0))
    fwd, inv = tables
    if spectrum is None:
        kern, ins, specs = _dft_mid_filter_kernel, (a4, fwd, norm), [blk, mat, pl.BlockSpec((1, bd), lambda p, j: (0, j))]
    else:
        kern, ins, specs = _dft_mid_conv_kernel, (a4, spectrum, fwd, inv), [blk, blk, mat, mat]
    out = pl.pallas_call(
        kern,
        out_shape=jax.ShapeDtypeStruct((2, n1, n2, d), MXU_DTYPE),
        grid=(n1 // pb, d // bd),
        in_specs=specs,
        out_specs=blk,
        compiler_params=_params(("parallel", "parallel")),
        name="hyena_dft_mid",
    )(*ins)
    return out


def _dft_last_kernel(x_ref, m_ref, o_ref):
    y = _dot(m_ref[0], x_ref[...])
    o_ref[...] = y.reshape(o_ref.shape).astype(o_ref.dtype)


def _dft_last(b, table, d):
    rows, cols = b.shape
    n1 = rows // 2
    n2 = cols // d
    a_hi = n1 // 2 // SUBLANES
    bd = _tile(d, 2048)
    per = d // bd
    return pl.pallas_call(
        _dft_last_kernel,
        out_shape=jax.ShapeDtypeStruct((2, a_hi, n2, SUBLANES, d), MXU_DTYPE),
        grid=(cols // bd,),
        in_specs=[
            pl.BlockSpec((rows, bd), lambda j: (0, j)),
            pl.BlockSpec((1, n1, rows), lambda j: (j // per, 0, 0)),
        ],
        out_specs=pl.BlockSpec((2, a_hi, 1, SUBLANES, bd), lambda j: (0, 0, j // per, 0, j % per)),
        compiler_params=_params(("parallel",)),
        name="hyena_dft_last",
    )(b, table)


def _long_conv_two_stage(v, filt, norm, tables):
    bsz, seq, d = v.shape
    n1 = 2 * seq // DFT_N2
    a_hi = n1 // 2 // SUBLANES
    first, mid_f, mid_i, last = tables
    cols = DFT_N2 * d
    h_first = _dft_first(filt.reshape(1, a_hi, DFT_N2, SUBLANES, d), first)
    spectrum = _dft_mid(h_first, (mid_f, mid_i), d, norm=norm)
    v5 = v.reshape(bsz, a_hi, SUBLANES, DFT_N2, d).transpose(0, 1, 3, 2, 4)
    a = _dft_first(v5, first)
    bmat = _dft_mid(a, (mid_f, mid_i), d, spectrum=spectrum)
    y5 = _dft_last(bmat.reshape(2 * n1, cols), last, d)
    return y5.transpose(0, 1, 3, 2, 4).reshape(bsz, seq, d)


def _dft_small_kernel(v_ref, h_ref, norm_ref, f_ref, g_ref, o_ref):
    seq = v_ref.shape[1]
    n = 2 * seq
    fwd = f_ref[...]
    x = jnp.concatenate([v_ref[0], v_ref[1]], axis=0)
    z = _dot(fwd, x.astype(MXU_DTYPE))
    hx = jnp.concatenate([h_ref[...] / norm_ref[...], jnp.zeros_like(h_ref)], axis=0)
    hs = _dot(fwd, hx.astype(MXU_DTYPE))
    zr, zi, hr, hi_ = z[:n], z[n:], hs[:n], hs[n:]
    w = jnp.concatenate([zr * hr - zi * hi_, zr * hi_ + zi * hr], axis=0)
    y = _dot(g_ref[...], w.astype(MXU_DTYPE))
    o_ref[0] = y[:seq].astype(o_ref.dtype)
    o_ref[1] = y[seq:].astype(o_ref.dtype)


def _long_conv_small(v, filt, norm):
    bsz, seq, d = v.shape
    n = 2 * seq
    k = jnp.arange(n, dtype=jnp.int32)
    t_in = jnp.arange(seq, dtype=jnp.int32)
    fr, fi = _unit_roots(k[:, None] * t_in[None, :], n)
    fwd = _real_form(fr, fi)
    t_out = jnp.arange(seq // 2, seq // 2 + seq, dtype=jnp.int32)
    gr, gi = _unit_roots(t_out[:, None] * k[None, :], n)
    inv = _real_form(gr / n, -gi / n)
    bd = _tile(d, 512)
    full = lambda shape: pl.BlockSpec(shape, lambda j: (0,) * len(shape))
    return pl.pallas_call(
        _dft_small_kernel,
        out_shape=jax.ShapeDtypeStruct((bsz, seq, d), MXU_DTYPE),
        grid=(d // bd,),
        in_specs=[
            pl.BlockSpec((bsz, seq, bd), lambda j: (0, 0, j)),
            pl.BlockSpec((seq, bd), lambda j: (0, j)),
            pl.BlockSpec((1, bd), lambda j: (0, j)),
            full(fwd.shape), full(inv.shape),
        ],
        out_specs=pl.BlockSpec((bsz, seq, bd), lambda j: (0, 0, j)),
        compiler_params=_params(("parallel",)),
        name="hyena_dft_small",
    )(v, filt, norm, fwd.astype(MXU_DTYPE), inv.astype(MXU_DTYPE))


def _hyena_out_kernel(alpha, x0_ref, cv_ref, v_ref, bias_ref, w_ref, b_ref, h_ref, g_ref, lg_ref, lb_ref, o_ref):
    y = x0_ref[0].astype(F32) * (cv_ref[0].astype(F32) + v_ref[0].astype(F32) * bias_ref[...])
    y = _dot(y.astype(MXU_DTYPE), w_ref[...]) + b_ref[...]
    o_ref[0] = _layer_norm_rows(alpha * h_ref[0] + g_ref[0] * y, lg_ref[...], lb_ref[...])


def _attn_out_kernel(alpha, o_in_ref, w_ref, h_ref, g_ref, lg_ref, lb_ref, o_ref):
    y = _dot(o_in_ref[0], w_ref[...])
    o_ref[0] = _layer_norm_rows(alpha * h_ref[0] + g_ref[0] * y, lg_ref[...], lb_ref[...])


def _row_spec(tm, d):
    return pl.BlockSpec((1, tm, d), lambda bi, i: (bi, i, 0))


def _vec_spec(d):
    return pl.BlockSpec((1, d), lambda bi, i: (0, 0))


def _bvec_spec(d):
    return pl.BlockSpec((1, 1, d), lambda bi, i: (bi, 0, 0))


def _const_spec(shape):
    return pl.BlockSpec(shape, lambda bi, i: (0,) * len(shape), pipeline_mode=pl.Buffered(1))


def _hyena_out(alpha, x0, cv, v, bias, layer, w, b, h, gate, ln_g, ln_b):
    bsz, seq, d = h.shape
    tm = _tile(seq, 512)
    return pl.pallas_call(
        functools.partial(_hyena_out_kernel, alpha),
        out_shape=jax.ShapeDtypeStruct(h.shape, F32),
        grid=(bsz, seq // tm),
        in_specs=[_row_spec(tm, d), _row_spec(tm, d), _row_spec(tm, d), _vec_spec(d),
                  pl.BlockSpec((None, d, d), lambda bi, i: (layer, 0, 0), pipeline_mode=pl.Buffered(1)),
                  _vec_spec(d), _row_spec(tm, d), _bvec_spec(d), _vec_spec(d), _vec_spec(d)],
        out_specs=_row_spec(tm, d),
        compiler_params=_params(("parallel", "parallel")),
        name="hyena_out_norm",
    )(x0, cv, v, bias.reshape(1, d), w, b.reshape(1, d), h, gate, ln_g.reshape(1, d), ln_b.reshape(1, d))


def _attn_out(alpha, o, w, h, gate, ln_g, ln_b):
    bsz, seq, d = h.shape
    k = o.shape[2]
    tm = _tile(seq, 512)
    return pl.pallas_call(
        functools.partial(_attn_out_kernel, alpha),
        out_shape=jax.ShapeDtypeStruct(h.shape, F32),
        grid=(bsz, seq // tm),
        in_specs=[_row_spec(tm, k), _const_spec((k, d)), _row_spec(tm, d), _bvec_spec(d), _vec_spec(d), _vec_spec(d)],
        out_specs=_row_spec(tm, d),
        compiler_params=_params(("parallel", "parallel")),
        name="attn_out_norm",
    )(o, w.astype(MXU_DTYPE), h, gate, ln_g.reshape(1, d), ln_b.reshape(1, d))


def _ffn_kernel(alpha, h_ref, sh_ref, sc_ref, g_ref, wg_ref, wu_ref, wd_ref, lg_ref, lb_ref, o_ref, u_scr, acc_scr):
    f = pl.program_id(2)

    @pl.when(f == 0)
    def _():
        u_scr[...] = (h_ref[0] * (1.0 + sc_ref[0]) + sh_ref[0]).astype(MXU_DTYPE)
        acc_scr[...] = jnp.zeros_like(acc_scr)

    u = u_scr[...]
    gate = _dot(u, wg_ref[...])
    up = _dot(u, wu_ref[...])
    act = (_silu(gate) * up).astype(MXU_DTYPE)
    acc_scr[...] += _dot(act, wd_ref[...])

    @pl.when(f == pl.num_programs(2) - 1)
    def _():
        o_ref[0] = _layer_norm_rows(alpha * h_ref[0] + g_ref[0] * acc_scr[...], lg_ref[...], lb_ref[...])


def _ffn(alpha, h, shift, scale, gate, layer, w_gate, w_up, w_down, ln_g, ln_b):
    bsz, seq, d = h.shape
    ff = w_gate.shape[2]
    tm = _tile(seq, 512)
    tf = _tile(ff, 512)
    return pl.pallas_call(
        functools.partial(_ffn_kernel, alpha),
        out_shape=jax.ShapeDtypeStruct(h.shape, F32),
        grid=(bsz, seq // tm, ff // tf),
        in_specs=[
            pl.BlockSpec((1, tm, d), lambda bi, i, f: (bi, i, 0)),
            pl.BlockSpec((1, 1, d), lambda bi, i, f: (bi, 0, 0)),
            pl.BlockSpec((1, 1, d), lambda bi, i, f: (bi, 0, 0)),
            pl.BlockSpec((1, 1, d), lambda bi, i, f: (bi, 0, 0)),
            pl.BlockSpec((None, d, tf), lambda bi, i, f: (layer, 0, f)),
            pl.BlockSpec((None, d, tf), lambda bi, i, f: (layer, 0, f)),
            pl.BlockSpec((None, tf, d), lambda bi, i, f: (layer, f, 0)),
            pl.BlockSpec((1, d), lambda bi, i, f: (0, 0)),
            pl.BlockSpec((1, d), lambda bi, i, f: (0, 0)),
        ],
        out_specs=pl.BlockSpec((1, tm, d), lambda bi, i, f: (bi, i, 0)),
        scratch_shapes=[pltpu.VMEM((tm, d), MXU_DTYPE), pltpu.VMEM((tm, d), F32)],
        compiler_params=_params(("parallel", "parallel", "arbitrary")),
        name="ffn_swiglu_norm",
    )(h, shift, scale, gate, w_gate, w_up, w_down,
      ln_g.reshape(1, d), ln_b.reshape(1, d))


def _rotate_half_cols(w):
    ws = w.reshape(w.shape[:-1] + (2, 2, ROPE_PAIRS))
    return jnp.stack([-ws[..., 1, :], ws[..., 0, :]], axis=-2).reshape(w.shape)


def _rope_table(seq):
    rows = seq // GRID_W
    row = jnp.repeat(jnp.arange(rows, dtype=F32), GRID_W)
    col = jnp.tile(jnp.arange(GRID_W, dtype=F32), rows)
    inv = ROPE_THETA ** (-jnp.arange(ROPE_PAIRS, dtype=F32) / ROPE_PAIRS)
    ang = jnp.stack([row[:, None] * inv, col[:, None] * inv], axis=1)
    ang = jnp.broadcast_to(ang[:, :, None, :], (seq, 2, 2, ROPE_PAIRS)).reshape(seq, QK_ROPE)
    return jnp.concatenate([jnp.cos(ang), jnp.sin(ang)], axis=-1)


def _rms_rows(x, g):
    return x * lax.rsqrt(jnp.mean(x * x, axis=-1, keepdims=True) + RMS_EPS) * g


def _rope_pair(x, cs):
    t = x * cs
    return t + pltpu.roll(t, QK_ROPE, 1)


def _mla_q_kernel(cq_ref, g_ref, w_ref, cs_ref, q_ref):
    xn = _rms_rows(cq_ref[0].astype(F32), g_ref[...]).astype(MXU_DTYPE)
    cs = cs_ref[...]

    def head(h, carry):
        a = _dot(xn, w_ref[h])
        r = _rope_pair(a[:, QK_NOPE:], cs)
        q_ref[0, h] = (jnp.concatenate([a[:, :QK_NOPE], r], axis=1) * QUERY_SCALE).astype(q_ref.dtype)
        return carry

    lax.fori_loop(0, w_ref.shape[0], head, 0, unroll=2)


def _mla_queries(t, q_norm, wq_b, cs, heads):
    bsz, seq, _ = t.shape
    rank = q_norm.shape[0]
    tm = _tile(seq, 512)
    w = wq_b.reshape(rank, heads, QK_NOPE + QK_ROPE)
    w = jnp.concatenate([w, _rotate_half_cols(w[..., QK_NOPE:])], axis=-1).transpose(1, 0, 2).astype(MXU_DTYPE)
    return pl.pallas_call(
        _mla_q_kernel,
        out_shape=jax.ShapeDtypeStruct((bsz, heads, seq, 2 * LANES), MXU_DTYPE),
        grid=(bsz, seq // tm),
        in_specs=[
            pl.BlockSpec((1, tm, rank), lambda bi, i: (bi, i, 0)),
            pl.BlockSpec((1, rank), lambda bi, i: (0, 0)),
            pl.BlockSpec((heads, rank, 2 * LANES), lambda bi, i: (0, 0, 0)),
            pl.BlockSpec((tm, LANES), lambda bi, i: (i, 0)),
        ],
        out_specs=pl.BlockSpec((1, heads, tm, 2 * LANES), lambda bi, i: (bi, 0, i, 0)),
        compiler_params=_params(("parallel", "parallel")),
        name="mla_queries",
    )(t, q_norm.reshape(1, rank), w, cs)


def _mla_kv_kernel(use_rope, ckv_ref, kr_ref, g_ref, w_ref, cs_ref, k_ref, vt_ref):
    xn = _rms_rows(ckv_ref[0].astype(F32), g_ref[...]).astype(MXU_DTYPE)
    kr = kr_ref[0].astype(F32)
    if use_rope:
        kr = _rope_pair(kr, cs_ref[...])
    lane = lax.broadcasted_iota(jnp.int32, kr.shape, 1)
    kr = jnp.where(lane < QK_ROPE, kr, 0.0).astype(k_ref.dtype)
    ones = jnp.ones((V_ROWS - V_HEAD, xn.shape[0]), vt_ref.dtype)

    def head(h, carry):
        kv = _dot(xn, w_ref[h])
        k_ref[0, h] = jnp.concatenate([kv[:, :QK_NOPE].astype(k_ref.dtype), kr], axis=1)
        vt_ref[0, h] = jnp.concatenate([kv[:, QK_NOPE:].T.astype(vt_ref.dtype), ones], axis=0)
        return carry

    lax.fori_loop(0, w_ref.shape[0], head, 0, unroll=2)


def _mla_keys_values(t, kv_norm, wkv_b, cs, heads, use_rope):
    bsz, seq, _ = t.shape
    rank = kv_norm.shape[0]
    tm = _tile(seq, 512)
    w = wkv_b.reshape(rank, heads, QK_NOPE + V_HEAD).transpose(1, 0, 2).astype(MXU_DTYPE)
    return pl.pallas_call(
        functools.partial(_mla_kv_kernel, use_rope),
        out_shape=(jax.ShapeDtypeStruct((bsz, heads, seq, 2 * LANES), MXU_DTYPE),
                   jax.ShapeDtypeStruct((bsz, heads, V_ROWS, seq), MXU_DTYPE)),
        grid=(bsz, seq // tm),
        in_specs=[
            pl.BlockSpec((1, tm, rank), lambda bi, i: (bi, i, 1)),
            pl.BlockSpec((1, tm, LANES), lambda bi, i: (bi, i, 2 * rank // LANES)),
            pl.BlockSpec((1, rank), lambda bi, i: (0, 0)),
            pl.BlockSpec((heads, rank, QK_NOPE + V_HEAD), lambda bi, i: (0, 0, 0)),
            pl.BlockSpec((tm, LANES), lambda bi, i: (i, 0)),
        ],
        out_specs=(pl.BlockSpec((1, heads, tm, 2 * LANES), lambda bi, i: (bi, 0, i, 0)),
                   pl.BlockSpec((1, heads, V_ROWS, tm), lambda bi, i: (bi, 0, 0, i))),
        compiler_params=_params(("parallel", "parallel")),
        name="mla_keys_values",
    )(t, t, kv_norm.reshape(1, rank), w, cs)


def _attn_kernel(q_ref, k_ref, vt_ref, kc_ref, vtc_ref, o_ref, qt_scr, s_scr, p_scr, smax_scr, m_scr, alpha_scr, acc_scr):
    kv = pl.program_id(3)
    n_main = pl.num_programs(3) - 1

    n_chunks, _, cw = qt_scr.shape

    @pl.when(kv == 0)
    def _():
        for c in range(n_chunks):
            qt_scr[c] = q_ref[0, 0, c * cw:(c + 1) * cw, :].T
        m_scr[...] = jnp.full_like(m_scr, -jnp.inf)
        acc_scr[...] = jnp.zeros_like(acc_scr)

    def step(k, vt):
        nk = k.shape[0]

        def scores(c, slot):
            s = _dot(k, qt_scr[c])
            s_scr[slot, :nk, :] = s
            smax_scr[slot] = jnp.max(s, axis=0, keepdims=True)

        def exponent(c, slot):
            m_prev = m_scr[c]
            m_new = jnp.maximum(m_prev, smax_scr[slot])
            alpha_scr[c] = jnp.exp2(m_prev - m_new)
            p_scr[slot, :nk, :] = jnp.exp2((s_scr[slot, :nk, :] - m_new).astype(MXU_DTYPE))
            m_scr[c] = m_new

        def values(c, slot):
            acc_scr[c] = alpha_scr[c] * acc_scr[c] + _dot(vt, p_scr[slot, :nk, :])

        for i in range(n_chunks + 2):
            if i < n_chunks:
                scores(i, i % 2)
            if 1 <= i <= n_chunks:
                exponent(i - 1, (i - 1) % 2)
            if i >= 2:
                values(i - 2, i % 2)

    @pl.when(kv < n_main)
    def _():
        step(k_ref[0, 0], vt_ref[0, 0])

    @pl.when(kv == n_main)
    def _():
        step(kc_ref[0, 0], vtc_ref[0, 0])
        for c in range(n_chunks):
            acc = acc_scr[c]
            o_ref[0, c * cw:(c + 1) * cw, :] = (acc[:V_HEAD] / acc[V_HEAD:V_HEAD + 1]).T.astype(o_ref.dtype)


def _mla_attend(q, k, vt, kc, vtc):
    bsz, heads, seq, dq = q.shape
    lc = kc.shape[2]
    tq = _tile(seq, 2048)
    tk = _tile(seq, 2048)
    cw = min(tq, ATTN_CHUNK)
    assert lc <= tk
    n_main = seq // tk
    return pl.pallas_call(
        _attn_kernel,
        out_shape=jax.ShapeDtypeStruct((bsz, seq, heads * V_HEAD), MXU_DTYPE),
        grid=(bsz, heads, seq // tq, n_main + 1),
        in_specs=[
            pl.BlockSpec((1, 1, tq, dq), lambda bi, h, i, j: (bi, h, i, 0)),
            pl.BlockSpec((1, 1, tk, dq), lambda bi, h, i, j: (bi, h, jnp.minimum(j, n_main - 1), 0)),
            pl.BlockSpec((1, 1, V_ROWS, tk), lambda bi, h, i, j: (bi, h, 0, jnp.minimum(j, n_main - 1))),
            pl.BlockSpec((1, 1, lc, dq), lambda bi, h, i, j: (bi, h, 0, 0)),
            pl.BlockSpec((1, 1, V_ROWS, lc), lambda bi, h, i, j: (bi, h, 0, 0)),
        ],
        out_specs=pl.BlockSpec((1, tq, V_HEAD), lambda bi, h, i, j: (bi, i, h)),
        scratch_shapes=[pltpu.VMEM((tq // cw, dq, cw), MXU_DTYPE), pltpu.VMEM((2, tk, cw), F32),
                        pltpu.VMEM((2, tk, cw), MXU_DTYPE), pltpu.VMEM((2, 1, cw), F32),
                        pltpu.VMEM((tq // cw, 1, cw), F32), pltpu.VMEM((tq // cw, 1, cw), F32), pltpu.VMEM((tq // cw, V_ROWS, cw), F32)],
        compiler_params=_params(("parallel", "parallel", "parallel", "arbitrary")),
        name="mla_attention",
    )(q, k, vt, kc, vtc)


def _pool_kernel(alpha, seq, h_ref, prev_ref, next_ref, sh_ref, sc_ref, g_ref, w_ref, ps_ref, lg_ref, lb_ref, o_ref):
    i = pl.program_id(1)
    last = pl.num_programs(1) - 1
    tm = h_ref.shape[1]
    d = h_ref.shape[2]
    groups = len(POOL_WINDOWS)
    ch = d // groups
    ext = tm + 2 * POOL_HALO
    shift, scale = sh_ref[0], sc_ref[0]
    h = h_ref[0]
    u = h * (1.0 + scale) + shift
    u_prev = jnp.where(i == 0, 0.0, prev_ref[0] * (1.0 + scale) + shift)
    u_next = jnp.where(i == last, 0.0, next_ref[0] * (1.0 + scale) + shift)
    e = jnp.concatenate([u_prev, u, u_next], axis=0)
    t = i * tm + lax.broadcasted_iota(jnp.int32, (tm, 1), 0)
    ys = []
    for g, win in enumerate(POOL_WINDOWS):
        a = e[:, g * ch:(g + 1) * ch]
        span = 1
        while span < win:
            a = a + pltpu.roll(a, ext - span, 0)
            span *= 2
        half = win // 2
        a = pltpu.roll(a, half, 0)
        s = a[POOL_HALO:POOL_HALO + tm]
        cnt = (jnp.minimum(t + half, seq) - jnp.maximum(t - half, 0)).astype(F32)
        dg = s / cnt - u[:, g * ch:(g + 1) * ch]
        ys.append(_dot(dg.astype(MXU_DTYPE), w_ref[g]))
    y = jnp.concatenate(ys, axis=1) * ps_ref[...]
    o_ref[0] = _layer_norm_rows(alpha * h + g_ref[0] * y, lg_ref[...], lb_ref[...])


def _pool_mixer(alpha, h, shift, scale, gate, w_grp, pool_scale, ln_g, ln_b):
    bsz, seq, d = h.shape
    groups, ch, _ = w_grp.shape
    tm = _tile(seq, 512)
    hb = tm // POOL_HALO
    nhb = seq // POOL_HALO
    return pl.pallas_call(
        functools.partial(_pool_kernel, alpha, seq),
        out_shape=jax.ShapeDtypeStruct(h.shape, F32),
        grid=(bsz, seq // tm),
        in_specs=[
            _row_spec(tm, d),
            pl.BlockSpec((1, POOL_HALO, d), lambda bi, i: (bi, jnp.maximum(i * hb - 1, 0), 0)),
            pl.BlockSpec((1, POOL_HALO, d), lambda bi, i: (bi, jnp.minimum((i + 1) * hb, nhb - 1), 0)),
            _bvec_spec(d), _bvec_spec(d), _bvec_spec(d),
            pl.BlockSpec((groups, ch, ch), lambda bi, i: (0, 0, 0)),
            _vec_spec(d), _vec_spec(d), _vec_spec(d),
        ],
        out_specs=_row_spec(tm, d),
        compiler_params=_params(("parallel", "parallel")),
        name="pool_mixer_norm",
    )(h, h, h, shift, scale, gate, w_grp.astype(MXU_DTYPE), pool_scale.reshape(1, d), ln_g.reshape(1, d), ln_b.reshape(1, d))


def _hyena_mixer_norm(alpha, h, mod, hy, filt, norm, tables, ln_g, ln_b):
    shift, scale, gate = mod
    layer, w_in, b_in, conv_w, conv_b, bias, w_out, b_out = hy
    x0, v = _hyena_in(h, shift, scale, layer, w_in, b_in, conv_w, conv_b)
    if h.shape[1] <= DFT_SMALL_MAX_L:
        cv = _long_conv_small(v, filt, norm)
    else:
        cv = _long_conv_two_stage(v, filt, norm, tables)
    return _hyena_out(alpha, x0, cv, v, bias, layer, w_out, b_out, h, gate, ln_g, ln_b)


def kernel(x, c, ctx, c_ctx, ada_w, ada_b, ln_g, ln_b, ffn_w_gate, ffn_w_up, ffn_w_down, hy_w_in, hy_b_in, hy_conv_w, hy_conv_b, hy_f_w_in, hy_f_w_hid, hy_f_b, hy_f_freq, hy_f_w_out, hy_bias, hy_w_out, hy_b_out, mla_w_in, mla_q_norm, mla_kv_norm, mla_wq_b, mla_wkv_b, mla_w_out, pool_w, pool_scale):
    bsz, seq, d = x.shape
    depth = ada_w.shape[0]
    assert bsz == 2, "the long convolution packs exactly two batch rows into one complex signal"
    assert ctx.shape[0] == bsz and seq % GRID_W == 0
    alpha = (2.0 * depth) ** 0.25
    heads = d // V_HEAD
    mla_layers = [i for i in range(depth) if i % N_MIXERS == 1]
    last_ctx_read = mla_layers[-1] if mla_layers else -1

    cond = jnp.concatenate([c, jnp.broadcast_to(c_ctx[None], (SUBLANES - bsz, d))], axis=0)
    mods = _ada_mods(cond, ada_w, ada_b)

    def mod_vecs(i, ctx_stream):
        m = jnp.broadcast_to(mods[i, bsz][None], (bsz, 6 * d)) if ctx_stream else mods[i, :bsz]
        return [m[:, None, k * d:(k + 1) * d] for k in range(6)]

    cs = _rope_table(seq)
    two_stage = seq > DFT_SMALL_MAX_L
    tables = _dft_tables(seq) if two_stage else None
    row_order = _first_stage_row_order(seq) if two_stage else None

    ffn_w = tuple(w.astype(MXU_DTYPE) for w in (ffn_w_gate, ffn_w_up, ffn_w_down))
    hy_w = (hy_w_in.astype(MXU_DTYPE), hy_w_out.astype(MXU_DTYPE))
    h, hc = x, ctx
    for i in range(depth):
        kind, j = i % N_MIXERS, i // N_MIXERS
        ctx_update = i < last_ctx_read
        sh1, sc1, g1, sh2, sc2, g2 = mod_vecs(i, False)
        if kind == 1 or ctx_update:
            csh1, csc1, cg1, csh2, csc2, cg2 = mod_vecs(i, True)
        lg, lb = ln_g[i], ln_b[i]
        if kind == 0:
            hy = (j, hy_w[0], hy_b_in[j], hy_conv_w[j], hy_conv_b[j], hy_bias[j], hy_w[1], hy_b_out[j])
            fp = (hy_f_w_in[j], hy_f_w_hid[j], hy_f_b[j], hy_f_freq[j], hy_f_w_out[j])
            filt, norm = _implicit_filter(seq, *fp, positions=row_order)
            h_mid = _hyena_mixer_norm(alpha, h, (sh1, sc1, g1), hy, filt, norm, tables, lg[0], lb[0])
            if ctx_update:
                filt_c, norm_c = _implicit_filter(hc.shape[1], *fp)
                hc_mid = _hyena_mixer_norm(alpha, hc, (csh1, csc1, cg1), hy, filt_c, norm_c, None, lg[0], lb[0])
        elif kind == 1:
            assert not ctx_update, "context queries are only needed when a later layer reads the context"
            rank = mla_q_norm.shape[1]
            w_in = mla_w_in[j]
            w_in = jnp.concatenate([w_in, _rotate_half_cols(w_in[:, 2 * rank:])], axis=1)
            zeros = jnp.zeros((w_in.shape[1],), F32)
            t = _mod_proj(h, sh1, sc1, w_in, zeros, slabs=1, tm=512, tn=w_in.shape[1])[0]
            tc = _mod_proj(hc, csh1, csc1, w_in, zeros, slabs=1, tm=512, tn=w_in.shape[1])[0]
            q = _mla_queries(t, mla_q_norm[j], mla_wq_b[j], cs, heads)
            k, v = _mla_keys_values(t, mla_kv_norm[j], mla_wkv_b[j], cs, heads, True)
            kc, vc = _mla_keys_values(tc, mla_kv_norm[j], mla_wkv_b[j], cs, heads, False)
            o = _mla_attend(q, k, v, kc, vc)
            h_mid = _attn_out(alpha, o, mla_w_out[j], h, g1, lg[0], lb[0])
        else:
            h_mid = _pool_mixer(alpha, h, sh1, sc1, g1, pool_w[j], pool_scale[j], lg[0], lb[0])
            if ctx_update:
                hc_mid = _pool_mixer(alpha, hc, csh1, csc1, cg1, pool_w[j], pool_scale[j], lg[0], lb[0])
        h = _ffn(alpha, h_mid, sh2, sc2, g2, i, *ffn_w, lg[1], lb[1])
        if ctx_update:
            hc = _ffn(alpha, hc_mid, csh2, csc2, cg2, i, *ffn_w, lg[1], lb[1])
    return h
```

```python
import functools
import math

import jax
import jax.numpy as jnp
from jax import lax
from jax.experimental import pallas as pl
from jax.experimental.pallas import tpu as pltpu

F32 = jnp.float32
MXU_DTYPE = jnp.bfloat16

V7X_VMEM_BYTES = 64 * 1024 * 1024
VMEM_LIMIT = V7X_VMEM_BYTES - 8 * 1024 * 1024
LANES = 128
SUBLANES = 8

N_MIXERS = 3
GRID_W = 64
LN_EPS = 1e-6
RMS_EPS = 1e-6
HY_TARGET = 1e-2
HY_FAST = 0.3
HY_SLOW = 1.5
HY_MIN_DECAY = math.log(HY_TARGET) / HY_SLOW
HY_MAX_DECAY = math.log(HY_TARGET) / HY_FAST
QK_NOPE = 128
QK_ROPE = 64
V_HEAD = 128
ROPE_PAIRS = QK_ROPE // 4
ROPE_THETA = 10000.0
ATTN_SCALE = (QK_NOPE + QK_ROPE) ** -0.5
QUERY_SCALE = ATTN_SCALE * math.log2(math.e)
ATTN_CHUNK = 512
V_ROWS = V_HEAD + 16
POOL_WINDOWS = (2, 4, 8, 16)
POOL_HALO = 8

DFT_N2 = 128
DFT_SMALL_MAX_L = 512
DFT_MID_ROWS = 4


def _params(sem, vmem=VMEM_LIMIT, flags=None):
    return pltpu.CompilerParams(dimension_semantics=sem, vmem_limit_bytes=vmem, flags=flags)


def _tile(n, t):
    t = min(n, t)
    assert n % t == 0, (n, t)
    return t


def _dot(a, b):
    return jnp.dot(a, b, preferred_element_type=F32)


def _layer_norm_rows(x, g, b):
    mu = jnp.mean(x, axis=-1, keepdims=True)
    xc = x - mu
    var = jnp.mean(xc * xc, axis=-1, keepdims=True)
    return xc * lax.rsqrt(var + LN_EPS) * g + b


def _silu(x):
    return x * (1.0 / (1.0 + jnp.exp(-x)))


def _ada_kernel(c_ref, w_ref, b_ref, o_ref):
    a = _silu(c_ref[...]).astype(MXU_DTYPE)
    part = _dot(a, w_ref[0].astype(MXU_DTYPE))

    @pl.when(pl.program_id(1) == 0)
    def _():
        o_ref[0] = part + b_ref[0]

    @pl.when(pl.program_id(1) > 0)
    def _():
        o_ref[0] += part


def _ada_mods(cond, ada_w, ada_b):
    depth, d, n = ada_w.shape
    rows = cond.shape[0]
    tk = _tile(d, 256)
    return pl.pallas_call(
        _ada_kernel,
        out_shape=jax.ShapeDtypeStruct((depth, rows, n), F32),
        grid=(depth, d // tk),
        in_specs=[
            pl.BlockSpec((rows, tk), lambda i, k: (0, k)),
            pl.BlockSpec((1, tk, n), lambda i, k: (i, k, 0)),
            pl.BlockSpec((1, 1, n), lambda i, k: (i, 0, 0)),
        ],
        out_specs=pl.BlockSpec((1, rows, n), lambda i, k: (i, 0, 0)),
        compiler_params=_params(("parallel", "arbitrary")),
        name="ada_mods",
    )(cond, ada_w, ada_b.reshape(depth, 1, n))


def _mod_proj_kernel(x_ref, sh_ref, sc_ref, w_ref, b_ref, o_ref, u_scr):
    @pl.when(pl.program_id(2) == 0)
    def _():
        u_scr[...] = (x_ref[0] * (1.0 + sc_ref[0]) + sh_ref[0]).astype(MXU_DTYPE)

    o_ref[0, 0] = (_dot(u_scr[...], w_ref[...]) + b_ref[...]).astype(o_ref.dtype)


def _mod_proj(x, shift, scale, w, b, slabs, tm, tn):
    bsz, seq, k = x.shape
    n = w.shape[1]
    n_slab = n // slabs
    tm = _tile(seq, tm)
    tn = _tile(n_slab, tn)
    per = n_slab // tn
    return pl.pallas_call(
        _mod_proj_kernel,
        out_shape=jax.ShapeDtypeStruct((slabs, bsz, seq, n_slab), MXU_DTYPE),
        grid=(bsz, seq // tm, n // tn),
        in_specs=[
            pl.BlockSpec((1, tm, k), lambda bi, i, j: (bi, i, 0)),
            pl.BlockSpec((1, 1, k), lambda bi, i, j: (bi, 0, 0)),
            pl.BlockSpec((1, 1, k), lambda bi, i, j: (bi, 0, 0)),
            pl.BlockSpec((k, tn), lambda bi, i, j: (0, j)),
            pl.BlockSpec((1, tn), lambda bi, i, j: (0, j)),
        ],
        out_specs=pl.BlockSpec((1, 1, tm, tn), lambda bi, i, j: (j // per, bi, i, j % per)),
        scratch_shapes=[pltpu.VMEM((tm, k), MXU_DTYPE)],
        compiler_params=_params(("parallel", "parallel", "arbitrary")),
        name="mod_proj",
    )(x, shift, scale, w.astype(MXU_DTYPE), b.reshape(1, n))


def _hyena_in_kernel(x_ref, prev_ref, next_ref, sh_ref, sc_ref, w0_ref, w1_ref, w2_ref, b_ref, cw_ref, cb_ref,
                     x0_ref, v_ref, u_scr):
    i = pl.program_id(1)
    last = pl.num_programs(1) - 1
    tm = x_ref.shape[1]
    ext = tm + 2 * SUBLANES

    @pl.when(pl.program_id(2) == 0)
    def _():
        shift, scale = sh_ref[0], sc_ref[0]
        rows = jnp.concatenate([prev_ref[0], x_ref[0], next_ref[0]], axis=0)
        u_scr[...] = (rows * (1.0 + scale) + shift).astype(MXU_DTYPE)

    u = u_scr[...]
    outs = []
    for s, w_ref in enumerate((w0_ref, w1_ref, w2_ref)):
        raw = _dot(u, w_ref[...])
        b, taps = b_ref[s], cw_ref[s]
        head = jnp.where(i == 0, -b, raw[:SUBLANES])
        tail = jnp.where(i == last, -b, raw[SUBLANES + tm:])
        raw = jnp.concatenate([head, raw[SUBLANES:SUBLANES + tm], tail], axis=0)
        conv = pltpu.roll(raw, 1, 0) * taps[0:1] + raw * taps[1:2] + pltpu.roll(raw, ext - 1, 0) * taps[2:3]
        const = b * (taps[0:1] + taps[1:2] + taps[2:3]) + cb_ref[s]
        outs.append(conv[SUBLANES:SUBLANES + tm] + const)
    x0_ref[0] = outs[0].astype(x0_ref.dtype)
    v_ref[0] = (outs[2] * outs[1]).astype(v_ref.dtype)


def _hyena_in(x, shift, scale, layer, w_in, b_in, conv_w, conv_b):
    bsz, seq, k = x.shape
    d = w_in.shape[2] // 3
    tm = _tile(seq, 512)
    tn = _tile(d, 512)
    per = d // tn
    hb = tm // SUBLANES
    nhb = seq // SUBLANES
    w = w_in
    cw = conv_w.reshape(3, 3, d).transpose(1, 0, 2)
    out = jax.ShapeDtypeStruct((bsz, seq, d), MXU_DTYPE)
    slab = lambda s: pl.BlockSpec((None, k, tn), lambda bi, i, j: (layer, 0, s * per + j))
    return pl.pallas_call(
        _hyena_in_kernel,
        out_shape=(out, out),
        grid=(bsz, seq // tm, per),
        in_specs=[
            pl.BlockSpec((1, tm, k), lambda bi, i, j: (bi, i, 0)),
            pl.BlockSpec((1, SUBLANES, k), lambda bi, i, j: (bi, jnp.maximum(i * hb - 1, 0), 0)),
            pl.BlockSpec((1, SUBLANES, k), lambda bi, i, j: (bi, jnp.minimum((i + 1) * hb, nhb - 1), 0)),
            pl.BlockSpec((1, 1, k), lambda bi, i, j: (bi, 0, 0)),
            pl.BlockSpec((1, 1, k), lambda bi, i, j: (bi, 0, 0)),
            slab(0), slab(1), slab(2),
            pl.BlockSpec((3, 1, tn), lambda bi, i, j: (0, 0, j)),
            pl.BlockSpec((3, 3, tn), lambda bi, i, j: (0, 0, j)),
            pl.BlockSpec((3, 1, tn), lambda bi, i, j: (0, 0, j)),
        ],
        out_specs=(
            pl.BlockSpec((1, tm, tn), lambda bi, i, j: (bi, i, j)),
            pl.BlockSpec((1, tm, tn), lambda bi, i, j: (bi, i, j)),
        ),
        scratch_shapes=[pltpu.VMEM((tm + 2 * SUBLANES, k), MXU_DTYPE)],
        compiler_params=_params(("parallel", "parallel", "arbitrary")),
        name="hyena_in_conv_gate",
    )(x, x, x, shift, scale, w, w, w, b_in.reshape(3, 1, d), cw, conv_b.reshape(3, 1, d))


def _filter_kernel(z_ref, dist_ref, w_in_ref, w_hid_ref, b_ref, fr_ref, w_out_ref, delta_ref, f_ref, norm_ref):
    hp = lax.Precision.HIGHEST
    b = b_ref[...]
    fr = fr_ref[...]
    g = jnp.sin(fr[0:1] * (jnp.dot(z_ref[...], w_in_ref[...], precision=hp, preferred_element_type=F32) + b[0:1]))
    g = jnp.sin(fr[1:2] * (jnp.dot(g, w_hid_ref[0], precision=hp, preferred_element_type=F32) + b[1:2]))
    g = jnp.sin(fr[2:3] * (jnp.dot(g, w_hid_ref[1], precision=hp, preferred_element_type=F32) + b[2:3]))
    filt = _dot(g.astype(MXU_DTYPE), w_out_ref[...])
    filt = filt * jnp.exp(-dist_ref[...] * delta_ref[...])
    f_ref[...] = filt

    @pl.when(pl.program_id(0) == 0)
    def _():
        norm_ref[...] = jnp.zeros_like(norm_ref)

    norm_ref[...] += jnp.sum(jnp.abs(filt), axis=0, keepdims=True)


def _first_stage_row_order(seq):
    a_hi, c, a_lo = jnp.meshgrid(jnp.arange(seq // (DFT_N2 * SUBLANES)), jnp.arange(DFT_N2), jnp.arange(SUBLANES),
                                 indexing="ij")
    return (DFT_N2 * (SUBLANES * a_hi + a_lo) + c).reshape(seq)


def _implicit_filter(seq, f_w_in, f_w_hid, f_b, f_freq, f_w_out, positions=None):
    emb, width = f_w_in.shape
    bands_n = (emb - 1) // 2
    d = f_w_out.shape[1]
    emb_pad = -(-emb // LANES) * LANES
    filt_w = -(-width // LANES) * LANES
    pad_to = lambda a, shape: jnp.pad(a.astype(F32), [(0, s - n) for s, n in zip(shape, a.shape)])
    pos = (jnp.arange(seq) if positions is None else positions).astype(F32)
    t = pos / (seq - 1)
    bands = jnp.linspace(1e-4, bands_n - 1, bands_n, dtype=F32)
    ang = (2.0 * math.pi / seq) * pos[:, None] * bands[None, :]
    z = pad_to(jnp.concatenate([t[:, None], jnp.cos(ang), -jnp.sin(ang)], axis=-1), (seq, emb_pad))
    w_in = pad_to(f_w_in, (emb_pad, filt_w))
    f_w_hid = pad_to(f_w_hid, (2, filt_w, filt_w))
    f_b = pad_to(f_b, (3, filt_w))
    f_freq = pad_to(f_freq, (3, filt_w))
    f_w_out = pad_to(f_w_out, (filt_w, d))
    dist = (jnp.abs(pos - seq // 2) / (seq // 2))[:, None]
    deltas = jnp.abs(jnp.linspace(HY_MIN_DECAY, HY_MAX_DECAY, d, dtype=F32))[None, :]
    tl = _tile(seq, 512)
    full = lambda shape: pl.BlockSpec(shape, lambda i: (0,) * len(shape))
    return pl.pallas_call(
        _filter_kernel,
        out_shape=(jax.ShapeDtypeStruct((seq, d), F32), jax.ShapeDtypeStruct((1, d), F32)),
        grid=(seq // tl,),
        in_specs=[
            pl.BlockSpec((tl, emb_pad), lambda i: (i, 0)),
            pl.BlockSpec((tl, 1), lambda i: (i, 0)),
            full((emb_pad, filt_w)),
            full((2, filt_w, filt_w)),
            full((3, filt_w)),
            full((3, filt_w)),
            full((filt_w, d)),
            full((1, d)),
        ],
        out_specs=(pl.BlockSpec((tl, d), lambda i: (i, 0)), full((1, d))),
        compiler_params=_params(("arbitrary",)),
        name="hyena_filter",
    )(z, dist, w_in, f_w_hid, f_b, f_freq, f_w_out.astype(MXU_DTYPE), deltas)


def _real_form(mr, mi):
    top = jnp.concatenate([mr, -mi], axis=-1)
    bot = jnp.concatenate([mi, mr], axis=-1)
    return jnp.concatenate([top, bot], axis=-2)


def _unit_roots(idx, n):
    ang = (2.0 * math.pi / n) * (idx % n).astype(F32)
    return jnp.cos(ang), -jnp.sin(ang)


def _dft_tables(seq):
    n = 2 * seq
    n2 = DFT_N2
    n1 = n // n2
    i32 = jnp.int32
    p = jnp.arange(n1, dtype=i32)
    c = jnp.arange(n2, dtype=i32)
    tr, ti = _unit_roots(c[:, None] * p[None, :], n)
    a_in = jnp.arange(n1 // 2, dtype=i32)
    fr, fi = _unit_roots(p[:, None] * a_in[None, :], n1)
    mr = tr[:, :, None] * fr[None] - ti[:, :, None] * fi[None]
    mi = tr[:, :, None] * fi[None] + ti[:, :, None] * fr[None]
    first = _real_form(mr, mi)
    a_out = jnp.arange(n1 // 4, n1 // 4 + n1 // 2, dtype=i32)
    gr, gi = _unit_roots(p[:, None] * a_out[None, :], n1)
    cr = (tr[:, :, None] * gr[None] - ti[:, :, None] * gi[None]) / n
    ci = -(tr[:, :, None] * gi[None] + ti[:, :, None] * gr[None]) / n
    last = _real_form(jnp.swapaxes(cr, 1, 2), jnp.swapaxes(ci, 1, 2))
    q = jnp.arange(n2, dtype=i32)
    hr, hi_ = _unit_roots(q[:, None] * c[None, :], n2)
    mid_f = _real_form(hr, hi_)
    mid_i = _real_form(hr, -hi_)
    return tuple(t.astype(MXU_DTYPE) for t in (first, mid_f, mid_i, last))


def _dft_first_kernel(x_ref, m_ref, o_ref):
    x = x_ref[...]
    x = x.reshape(x.shape[0] * x.shape[1] * SUBLANES, x.shape[4])
    o_ref[...] = _dot(m_ref[0], x.astype(MXU_DTYPE)).astype(o_ref.dtype)


def _dft_first(x5, table):
    r, a_hi, n2, _, d = x5.shape
    rows = r * a_hi * SUBLANES
    n_out = table.shape[1]
    if rows % LANES:
        table = table[:, :, :rows]
    bd = _tile(d, 2048)
    per = d // bd
    return pl.pallas_call(
        _dft_first_kernel,
        out_shape=jax.ShapeDtypeStruct((n_out, n2 * d), MXU_DTYPE),
        grid=(n2 * per,),
        in_specs=[
            pl.BlockSpec((r, a_hi, 1, SUBLANES, bd), lambda j: (0, 0, j // per, 0, j % per)),
            pl.BlockSpec((1, n_out, rows), lambda j: (j // per, 0, 0)),
        ],
        out_specs=pl.BlockSpec((n_out, bd), lambda j: (0, j)),
        compiler_params=_params(("parallel",)),
        name="hyena_dft_first",
    )(x5, table)


def _dft_mid_filter_kernel(a_ref, f_ref, norm_ref, o_ref):
    n2 = a_ref.shape[2]
    for p in range(a_ref.shape[1]):
        a = jnp.concatenate([a_ref[0, p], a_ref[1, p]], axis=0)
        z = _dot(f_ref[...], a) / norm_ref[...]
        o_ref[0, p] = z[:n2].astype(o_ref.dtype)
        o_ref[1, p] = z[n2:].astype(o_ref.dtype)


def _dft_mid_conv_kernel(a_ref, h_ref, f_ref, g_ref, o_ref):
    n2 = a_ref.shape[2]
    for p in range(a_ref.shape[1]):
        a = jnp.concatenate([a_ref[0, p], a_ref[1, p]], axis=0)
        z = _dot(f_ref[...], a)
        zr, zi = z[:n2], z[n2:]
        hr, hi_ = h_ref[0, p].astype(F32), h_ref[1, p].astype(F32)
        w = jnp.concatenate([zr * hr - zi * hi_, zr * hi_ + zi * hr], axis=0)
        y = _dot(g_ref[...], w.astype(MXU_DTYPE))
        o_ref[0, p] = y[:n2].astype(o_ref.dtype)
        o_ref[1, p] = y[n2:].astype(o_ref.dtype)


def _dft_mid(a, tables, d, *, spectrum=None, norm=None):
    n2 = DFT_N2
    n1 = a.shape[0] // 2
    a4 = a.reshape(2, n1, n2, d)
    bd = _tile(d, 2048)
    pb = _tile(n1, DFT_MID_ROWS)
    blk = pl.BlockSpec((2, pb, n2, bd), lambda p, j: (0, p, 0, j))
    mat = pl.BlockSpec((2 * n2, 2 * n2), lambda p, j: (0, 0))
    fwd, inv = tables
    if spectrum is None:
        kern, ins, specs = _dft_mid_filter_kernel, (a4, fwd, norm), [blk, mat, pl.BlockSpec((1, bd), lambda p, j: (0, j))]
    else:
        kern, ins, specs = _dft_mid_conv_kernel, (a4, spectrum, fwd, inv), [blk, blk, mat, mat]
    out = pl.pallas_call(
        kern,
        out_shape=jax.ShapeDtypeStruct((2, n1, n2, d), MXU_DTYPE),
        grid=(n1 // pb, d // bd),
        in_specs=specs,
        out_specs=blk,
        compiler_params=_params(("parallel", "parallel")),
        name="hyena_dft_mid",
    )(*ins)
    return out


def _dft_last_kernel(x_ref, m_ref, o_ref):
    y = _dot(m_ref[0], x_ref[...])
    o_ref[...] = y.reshape(o_ref.shape).astype(o_ref.dtype)


def _dft_last(b, table, d):
    rows, cols = b.shape
    n1 = rows // 2
    n2 = cols // d
    a_hi = n1 // 2 // SUBLANES
    bd = _tile(d, 2048)
    per = d // bd
    return pl.pallas_call(
        _dft_last_kernel,
        out_shape=jax.ShapeDtypeStruct((2, a_hi, n2, SUBLANES, d), MXU_DTYPE),
        grid=(cols // bd,),
        in_specs=[
            pl.BlockSpec((rows, bd), lambda j: (0, j)),
            pl.BlockSpec((1, n1, rows), lambda j: (j // per, 0, 0)),
        ],
        out_specs=pl.BlockSpec((2, a_hi, 1, SUBLANES, bd), lambda j: (0, 0, j // per, 0, j % per)),
        compiler_params=_params(("parallel",)),
        name="hyena_dft_last",
    )(b, table)


def _long_conv_two_stage(v, filt, norm, tables):
    bsz, seq, d = v.shape
    n1 = 2 * seq // DFT_N2
    a_hi = n1 // 2 // SUBLANES
    first, mid_f, mid_i, last = tables
    cols = DFT_N2 * d
    h_first = _dft_first(filt.reshape(1, a_hi, DFT_N2, SUBLANES, d), first)
    spectrum = _dft_mid(h_first, (mid_f, mid_i), d, norm=norm)
    v5 = v.reshape(bsz, a_hi, SUBLANES, DFT_N2, d).transpose(0, 1, 3, 2, 4)
    a = _dft_first(v5, first)
    bmat = _dft_mid(a, (mid_f, mid_i), d, spectrum=spectrum)
    y5 = _dft_last(bmat.reshape(2 * n1, cols), last, d)
    return y5.transpose(0, 1, 3, 2, 4).reshape(bsz, seq, d)


def _dft_small_kernel(v_ref, h_ref, norm_ref, f_ref, g_ref, o_ref):
    seq = v_ref.shape[1]
    n = 2 * seq
    fwd = f_ref[...]
    x = jnp.concatenate([v_ref[0], v_ref[1]], axis=0)
    z = _dot(fwd, x.astype(MXU_DTYPE))
    hx = jnp.concatenate([h_ref[...] / norm_ref[...], jnp.zeros_like(h_ref)], axis=0)
    hs = _dot(fwd, hx.astype(MXU_DTYPE))
    zr, zi, hr, hi_ = z[:n], z[n:], hs[:n], hs[n:]
    w = jnp.concatenate([zr * hr - zi * hi_, zr * hi_ + zi * hr], axis=0)
    y = _dot(g_ref[...], w.astype(MXU_DTYPE))
    o_ref[0] = y[:seq].astype(o_ref.dtype)
    o_ref[1] = y[seq:].astype(o_ref.dtype)


def _long_conv_small(v, filt, norm):
    bsz, seq, d = v.shape
    n = 2 * seq
    k = jnp.arange(n, dtype=jnp.int32)
    t_in = jnp.arange(seq, dtype=jnp.int32)
    fr, fi = _unit_roots(k[:, None] * t_in[None, :], n)
    fwd = _real_form(fr, fi)
    t_out = jnp.arange(seq // 2, seq // 2 + seq, dtype=jnp.int32)
    gr, gi = _unit_roots(t_out[:, None] * k[None, :], n)
    inv = _real_form(gr / n, -gi / n)
    bd = _tile(d, 512)
    full = lambda shape: pl.BlockSpec(shape, lambda j: (0,) * len(shape))
    return pl.pallas_call(
        _dft_small_kernel,
        out_shape=jax.ShapeDtypeStruct((bsz, seq, d), MXU_DTYPE),
        grid=(d // bd,),
        in_specs=[
            pl.BlockSpec((bsz, seq, bd), lambda j: (0, 0, j)),
            pl.BlockSpec((seq, bd), lambda j: (0, j)),
            pl.BlockSpec((1, bd), lambda j: (0, j)),
            full(fwd.shape), full(inv.shape),
        ],
        out_specs=pl.BlockSpec((bsz, seq, bd), lambda j: (0, 0, j)),
        compiler_params=_params(("parallel",)),
        name="hyena_dft_small",
    )(v, filt, norm, fwd.astype(MXU_DTYPE), inv.astype(MXU_DTYPE))


def _hyena_out_kernel(alpha, x0_ref, cv_ref, v_ref, bias_ref, w_ref, b_ref, h_ref, g_ref, lg_ref, lb_ref, o_ref):
    y = x0_ref[0].astype(F32) * (cv_ref[0].astype(F32) + v_ref[0].astype(F32) * bias_ref[...])
    y = _dot(y.astype(MXU_DTYPE), w_ref[...]) + b_ref[...]
    o_ref[0] = _layer_norm_rows(alpha * h_ref[0] + g_ref[0] * y, lg_ref[...], lb_ref[...])


def _attn_out_kernel(alpha, o_in_ref, w_ref, h_ref, g_ref, lg_ref, lb_ref, o_ref):
    y = _dot(o_in_ref[0], w_ref[...])
    o_ref[0] = _layer_norm_rows(alpha * h_ref[0] + g_ref[0] * y, lg_ref[...], lb_ref[...])


def _row_spec(tm, d):
    return pl.BlockSpec((1, tm, d), lambda bi, i: (bi, i, 0))


def _vec_spec(d):
    return pl.BlockSpec((1, d), lambda bi, i: (0, 0))


def _bvec_spec(d):
    return pl.BlockSpec((1, 1, d), lambda bi, i: (bi, 0, 0))


def _const_spec(shape):
    return pl.BlockSpec(shape, lambda bi, i: (0,) * len(shape), pipeline_mode=pl.Buffered(1))


def _hyena_out(alpha, x0, cv, v, bias, layer, w, b, h, gate, ln_g, ln_b):
    bsz, seq, d = h.shape
    tm = _tile(seq, 512)
    return pl.pallas_call(
        functools.partial(_hyena_out_kernel, alpha),
        out_shape=jax.ShapeDtypeStruct(h.shape, F32),
        grid=(bsz, seq // tm),
        in_specs=[_row_spec(tm, d), _row_spec(tm, d), _row_spec(tm, d), _vec_spec(d),
                  pl.BlockSpec((None, d, d), lambda bi, i: (layer, 0, 0), pipeline_mode=pl.Buffered(1)),
                  _vec_spec(d), _row_spec(tm, d), _bvec_spec(d), _vec_spec(d), _vec_spec(d)],
        out_specs=_row_spec(tm, d),
        compiler_params=_params(("parallel", "parallel")),
        name="hyena_out_norm",
    )(x0, cv, v, bias.reshape(1, d), w, b.reshape(1, d), h, gate, ln_g.reshape(1, d), ln_b.reshape(1, d))


def _attn_out(alpha, o, w, h, gate, ln_g, ln_b):
    bsz, seq, d = h.shape
    k = o.shape[2]
    tm = _tile(seq, 512)
    return pl.pallas_call(
        functools.partial(_attn_out_kernel, alpha),
        out_shape=jax.ShapeDtypeStruct(h.shape, F32),
        grid=(bsz, seq // tm),
        in_specs=[_row_spec(tm, k), _const_spec((k, d)), _row_spec(tm, d), _bvec_spec(d), _vec_spec(d), _vec_spec(d)],
        out_specs=_row_spec(tm, d),
        compiler_params=_params(("parallel", "parallel")),
        name="attn_out_norm",
    )(o, w.astype(MXU_DTYPE), h, gate, ln_g.reshape(1, d), ln_b.reshape(1, d))


def _ffn_kernel(alpha, h_ref, sh_ref, sc_ref, g_ref, wg_ref, wu_ref, wd_ref, lg_ref, lb_ref, o_ref, u_scr, acc_scr):
    f = pl.program_id(2)

    @pl.when(f == 0)
    def _():
        u_scr[...] = (h_ref[0] * (1.0 + sc_ref[0]) + sh_ref[0]).astype(MXU_DTYPE)
        acc_scr[...] = jnp.zeros_like(acc_scr)

    u = u_scr[...]
    gate = _dot(u, wg_ref[...])
    up = _dot(u, wu_ref[...])
    act = (_silu(gate) * up).astype(MXU_DTYPE)
    acc_scr[...] += _dot(act, wd_ref[...])

    @pl.when(f == pl.num_programs(2) - 1)
    def _():
        o_ref[0] = _layer_norm_rows(alpha * h_ref[0] + g_ref[0] * acc_scr[...], lg_ref[...], lb_ref[...])


def _ffn(alpha, h, shift, scale, gate, layer, w_gate, w_up, w_down, ln_g, ln_b):
    bsz, seq, d = h.shape
    ff = w_gate.shape[2]
    tm = _tile(seq, 512)
    tf = _tile(ff, 512)
    return pl.pallas_call(
        functools.partial(_ffn_kernel, alpha),
        out_shape=jax.ShapeDtypeStruct(h.shape, F32),
        grid=(bsz, seq // tm, ff // tf),
        in_specs=[
            pl.BlockSpec((1, tm, d), lambda bi, i, f: (bi, i, 0)),
            pl.BlockSpec((1, 1, d), lambda bi, i, f: (bi, 0, 0)),
            pl.BlockSpec((1, 1, d), lambda bi, i, f: (bi, 0, 0)),
            pl.BlockSpec((1, 1, d), lambda bi, i, f: (bi, 0, 0)),
            pl.BlockSpec((None, d, tf), lambda bi, i, f: (layer, 0, f)),
            pl.BlockSpec((None, d, tf), lambda bi, i, f: (layer, 0, f)),
            pl.BlockSpec((None, tf, d), lambda bi, i, f: (layer, f, 0)),
            pl.BlockSpec((1, d), lambda bi, i, f: (0, 0)),
            pl.BlockSpec((1, d), lambda bi, i, f: (0, 0)),
        ],
        out_specs=pl.BlockSpec((1, tm, d), lambda bi, i, f: (bi, i, 0)),
        scratch_shapes=[pltpu.VMEM((tm, d), MXU_DTYPE), pltpu.VMEM((tm, d), F32)],
        compiler_params=_params(("parallel", "parallel", "arbitrary")),
        name="ffn_swiglu_norm",
    )(h, shift, scale, gate, w_gate, w_up, w_down,
      ln_g.reshape(1, d), ln_b.reshape(1, d))


def _rotate_half_cols(w):
    ws = w.reshape(w.shape[:-1] + (2, 2, ROPE_PAIRS))
    return jnp.stack([-ws[..., 1, :], ws[..., 0, :]], axis=-2).reshape(w.shape)


def _rope_table(seq):
    rows = seq // GRID_W
    row = jnp.repeat(jnp.arange(rows, dtype=F32), GRID_W)
    col = jnp.tile(jnp.arange(GRID_W, dtype=F32), rows)
    inv = ROPE_THETA ** (-jnp.arange(ROPE_PAIRS, dtype=F32) / ROPE_PAIRS)
    ang = jnp.stack([row[:, None] * inv, col[:, None] * inv], axis=1)
    ang = jnp.broadcast_to(ang[:, :, None, :], (seq, 2, 2, ROPE_PAIRS)).reshape(seq, QK_ROPE)
    return jnp.concatenate([jnp.cos(ang), jnp.sin(ang)], axis=-1)


def _rms_rows(x, g):
    return x * lax.rsqrt(jnp.mean(x * x, axis=-1, keepdims=True) + RMS_EPS) * g


def _rope_pair(x, cs):
    t = x * cs
    return t + pltpu.roll(t, QK_ROPE, 1)


def _mla_q_kernel(cq_ref, g_ref, w_ref, cs_ref, q_ref):
    xn = _rms_rows(cq_ref[0].astype(F32), g_ref[...]).astype(MXU_DTYPE)
    cs = cs_ref[...]

    def head(h, carry):
        a = _dot(xn, w_ref[h])
        r = _rope_pair(a[:, QK_NOPE:], cs)
        q_ref[0, h] = (jnp.concatenate([a[:, :QK_NOPE], r], axis=1) * QUERY_SCALE).astype(q_ref.dtype)
        return carry

    lax.fori_loop(0, w_ref.shape[0], head, 0, unroll=2)


def _mla_queries(t, q_norm, wq_b, cs, heads):
    bsz, seq, _ = t.shape
    rank = q_norm.shape[0]
    tm = _tile(seq, 512)
    w = wq_b.reshape(rank, heads, QK_NOPE + QK_ROPE)
    w = jnp.concatenate([w, _rotate_half_cols(w[..., QK_NOPE:])], axis=-1).transpose(1, 0, 2).astype(MXU_DTYPE)
    return pl.pallas_call(
        _mla_q_kernel,
        out_shape=jax.ShapeDtypeStruct((bsz, heads, seq, 2 * LANES), MXU_DTYPE),
        grid=(bsz, seq // tm),
        in_specs=[
            pl.BlockSpec((1, tm, rank), lambda bi, i: (bi, i, 0)),
            pl.BlockSpec((1, rank), lambda bi, i: (0, 0)),
            pl.BlockSpec((heads, rank, 2 * LANES), lambda bi, i: (0, 0, 0)),
            pl.BlockSpec((tm, LANES), lambda bi, i: (i, 0)),
        ],
        out_specs=pl.BlockSpec((1, heads, tm, 2 * LANES), lambda bi, i: (bi, 0, i, 0)),
        compiler_params=_params(("parallel", "parallel")),
        name="mla_queries",
    )(t, q_norm.reshape(1, rank), w, cs)


def _mla_kv_kernel(use_rope, ckv_ref, kr_ref, g_ref, w_ref, cs_ref, k_ref, vt_ref):
    xn = _rms_rows(ckv_ref[0].astype(F32), g_ref[...]).astype(MXU_DTYPE)
    kr = kr_ref[0].astype(F32)
    if use_rope:
        kr = _rope_pair(kr, cs_ref[...])
    lane = lax.broadcasted_iota(jnp.int32, kr.shape, 1)
    kr = jnp.where(lane < QK_ROPE, kr, 0.0).astype(k_ref.dtype)
    ones = jnp.ones((V_ROWS - V_HEAD, xn.shape[0]), vt_ref.dtype)

    def head(h, carry):
        kv = _dot(xn, w_ref[h])
        k_ref[0, h] = jnp.concatenate([kv[:, :QK_NOPE].astype(k_ref.dtype), kr], axis=1)
        vt_ref[0, h] = jnp.concatenate([kv[:, QK_NOPE:].T.astype(vt_ref.dtype), ones], axis=0)
        return carry

    lax.fori_loop(0, w_ref.shape[0], head, 0, unroll=2)


def _mla_keys_values(t, kv_norm, wkv_b, cs, heads, use_rope):
    bsz, seq, _ = t.shape
    rank = kv_norm.shape[0]
    tm = _tile(seq, 512)
    w = wkv_b.reshape(rank, heads, QK_NOPE + V_HEAD).transpose(1, 0, 2).astype(MXU_DTYPE)
    return pl.pallas_call(
        functools.partial(_mla_kv_kernel, use_rope),
        out_shape=(jax.ShapeDtypeStruct((bsz, heads, seq, 2 * LANES), MXU_DTYPE),
                   jax.ShapeDtypeStruct((bsz, heads, V_ROWS, seq), MXU_DTYPE)),
        grid=(bsz, seq // tm),
        in_specs=[
            pl.BlockSpec((1, tm, rank), lambda bi, i: (bi, i, 1)),
            pl.BlockSpec((1, tm, LANES), lambda bi, i: (bi, i, 2 * rank // LANES)),
            pl.BlockSpec((1, rank), lambda bi, i: (0, 0)),
            pl.BlockSpec((heads, rank, QK_NOPE + V_HEAD), lambda bi, i: (0, 0, 0)),
            pl.BlockSpec((tm, LANES), lambda bi, i: (i, 0)),
        ],
        out_specs=(pl.BlockSpec((1, heads, tm, 2 * LANES), lambda bi, i: (bi, 0, i, 0)),
                   pl.BlockSpec((1, heads, V_ROWS, tm), lambda bi, i: (bi, 0, 0, i))),
        compiler_params=_params(("parallel", "parallel")),
        name="mla_keys_values",
    )(t, t, kv_norm.reshape(1, rank), w, cs)


def _attn_kernel(q_ref, k_ref, vt_ref, kc_ref, vtc_ref, o_ref, qt_scr, s_scr, p_scr, smax_scr, m_scr, alpha_scr, acc_scr):
    kv = pl.program_id(3)
    n_main = pl.num_programs(3) - 1

    n_chunks, _, cw = qt_scr.shape

    @pl.when(kv == 0)
    def _():
        for c in range(n_chunks):
            qt_scr[c] = q_ref[0, 0, c * cw:(c + 1) * cw, :].T
        m_scr[...] = jnp.full_like(m_scr, -jnp.inf)
        acc_scr[...] = jnp.zeros_like(acc_scr)

    def step(k, vt):
        nk = k.shape[0]

        def scores(c, slot):
            s = _dot(k, qt_scr[c])
            s_scr[slot, :nk, :] = s
            smax_scr[slot] = jnp.max(s, axis=0, keepdims=True)

        def exponent(c, slot):
            m_prev = m_scr[c]
            m_new = jnp.maximum(m_prev, smax_scr[slot])
            alpha_scr[c] = jnp.exp2(m_prev - m_new)
            p_scr[slot, :nk, :] = jnp.exp2((s_scr[slot, :nk, :] - m_new).astype(MXU_DTYPE))
            m_scr[c] = m_new

        def values(c, slot):
            acc_scr[c] = alpha_scr[c] * acc_scr[c] + _dot(vt, p_scr[slot, :nk, :])

        for i in range(n_chunks + 2):
            if i < n_chunks:
                scores(i, i % 2)
            if 1 <= i <= n_chunks:
                exponent(i - 1, (i - 1) % 2)
            if i >= 2:
                values(i - 2, i % 2)

    @pl.when(kv < n_main)
    def _():
        step(k_ref[0, 0], vt_ref[0, 0])

    @pl.when(kv == n_main)
    def _():
        step(kc_ref[0, 0], vtc_ref[0, 0])
        for c in range(n_chunks):
            acc = acc_scr[c]
            o_ref[0, c * cw:(c + 1) * cw, :] = (acc[:V_HEAD] / acc[V_HEAD:V_HEAD + 1]).T.astype(o_ref.dtype)


def _mla_attend(q, k, vt, kc, vtc):
    bsz, heads, seq, dq = q.shape
    lc = kc.shape[2]
    tq = _tile(seq, 2048)
    tk = _tile(seq, 2048)
    cw = min(tq, ATTN_CHUNK)
    assert lc <= tk
    n_main = seq // tk
    return pl.pallas_call(
        _attn_kernel,
        out_shape=jax.ShapeDtypeStruct((bsz, seq, heads * V_HEAD), MXU_DTYPE),
        grid=(bsz, heads, seq // tq, n_main + 1),
        in_specs=[
            pl.BlockSpec((1, 1, tq, dq), lambda bi, h, i, j: (bi, h, i, 0)),
            pl.BlockSpec((1, 1, tk, dq), lambda bi, h, i, j: (bi, h, jnp.minimum(j, n_main - 1), 0)),
            pl.BlockSpec((1, 1, V_ROWS, tk), lambda bi, h, i, j: (bi, h, 0, jnp.minimum(j, n_main - 1))),
            pl.BlockSpec((1, 1, lc, dq), lambda bi, h, i, j: (bi, h, 0, 0)),
            pl.BlockSpec((1, 1, V_ROWS, lc), lambda bi, h, i, j: (bi, h, 0, 0)),
        ],
        out_specs=pl.BlockSpec((1, tq, V_HEAD), lambda bi, h, i, j: (bi, i, h)),
        scratch_shapes=[pltpu.VMEM((tq // cw, dq, cw), MXU_DTYPE), pltpu.VMEM((2, tk, cw), F32),
                        pltpu.VMEM((2, tk, cw), MXU_DTYPE), pltpu.VMEM((2, 1, cw), F32),
                        pltpu.VMEM((tq // cw, 1, cw), F32), pltpu.VMEM((tq // cw, 1, cw), F32), pltpu.VMEM((tq // cw, V_ROWS, cw), F32)],
        compiler_params=_params(("parallel", "parallel", "parallel", "arbitrary")),
        name="mla_attention",
    )(q, k, vt, kc, vtc)


def _pool_kernel(alpha, seq, h_ref, prev_ref, next_ref, sh_ref, sc_ref, g_ref, w_ref, ps_ref, lg_ref, lb_ref, o_ref):
    i = pl.program_id(1)
    last = pl.num_programs(1) - 1
    tm = h_ref.shape[1]
    d = h_ref.shape[2]
    groups = len(POOL_WINDOWS)
    ch = d // groups
    ext = tm + 2 * POOL_HALO
    shift, scale = sh_ref[0], sc_ref[0]
    h = h_ref[0]
    u = h * (1.0 + scale) + shift
    u_prev = jnp.where(i == 0, 0.0, prev_ref[0] * (1.0 + scale) + shift)
    u_next = jnp.where(i == last, 0.0, next_ref[0] * (1.0 + scale) + shift)
    e = jnp.concatenate([u_prev, u, u_next], axis=0)
    t = i * tm + lax.broadcasted_iota(jnp.int32, (tm, 1), 0)
    ys = []
    for g, win in enumerate(POOL_WINDOWS):
        a = e[:, g * ch:(g + 1) * ch]
        span = 1
        while span < win:
            a = a + pltpu.roll(a, ext - span, 0)
            span *= 2
        half = win // 2
        a = pltpu.roll(a, half, 0)
        s = a[POOL_HALO:POOL_HALO + tm]
        cnt = (jnp.minimum(t + half, seq) - jnp.maximum(t - half, 0)).astype(F32)
        dg = s / cnt - u[:, g * ch:(g + 1) * ch]
        ys.append(_dot(dg.astype(MXU_DTYPE), w_ref[g]))
    y = jnp.concatenate(ys, axis=1) * ps_ref[...]
    o_ref[0] = _layer_norm_rows(alpha * h + g_ref[0] * y, lg_ref[...], lb_ref[...])


def _pool_mixer(alpha, h, shift, scale, gate, w_grp, pool_scale, ln_g, ln_b):
    bsz, seq, d = h.shape
    groups, ch, _ = w_grp.shape
    tm = _tile(seq, 512)
    hb = tm // POOL_HALO
    nhb = seq // POOL_HALO
    return pl.pallas_call(
        functools.partial(_pool_kernel, alpha, seq),
        out_shape=jax.ShapeDtypeStruct(h.shape, F32),
        grid=(bsz, seq // tm),
        in_specs=[
            _row_spec(tm, d),
            pl.BlockSpec((1, POOL_HALO, d), lambda bi, i: (bi, jnp.maximum(i * hb - 1, 0), 0)),
            pl.BlockSpec((1, POOL_HALO, d), lambda bi, i: (bi, jnp.minimum((i + 1) * hb, nhb - 1), 0)),
            _bvec_spec(d), _bvec_spec(d), _bvec_spec(d),
            pl.BlockSpec((groups, ch, ch), lambda bi, i: (0, 0, 0)),
            _vec_spec(d), _vec_spec(d), _vec_spec(d),
        ],
        out_specs=_row_spec(tm, d),
        compiler_params=_params(("parallel", "parallel")),
        name="pool_mixer_norm",
    )(h, h, h, shift, scale, gate, w_grp.astype(MXU_DTYPE), pool_scale.reshape(1, d), ln_g.reshape(1, d), ln_b.reshape(1, d))


def _hyena_mixer_norm(alpha, h, mod, hy, filt, norm, tables, ln_g, ln_b):
    shift, scale, gate = mod
    layer, w_in, b_in, conv_w, conv_b, bias, w_out, b_out = hy
    x0, v = _hyena_in(h, shift, scale, layer, w_in, b_in, conv_w, conv_b)
    if h.shape[1] <= DFT_SMALL_MAX_L:
        cv = _long_conv_small(v, filt, norm)
    else:
        cv = _long_conv_two_stage(v, filt, norm, tables)
    return _hyena_out(alpha, x0, cv, v, bias, layer, w_out, b_out, h, gate, ln_g, ln_b)


def kernel(x, c, ctx, c_ctx, ada_w, ada_b, ln_g, ln_b, ffn_w_gate, ffn_w_up, ffn_w_down, hy_w_in, hy_b_in, hy_conv_w, hy_conv_b, hy_f_w_in, hy_f_w_hid, hy_f_b, hy_f_freq, hy_f_w_out, hy_bias, hy_w_out, hy_b_out, mla_w_in, mla_q_norm, mla_kv_norm, mla_wq_b, mla_wkv_b, mla_w_out, pool_w, pool_scale):
    bsz, seq, d = x.shape
    depth = ada_w.shape[0]
    assert bsz == 2, "the long convolution packs exactly two batch rows into one complex signal"
    assert ctx.shape[0] == bsz and seq % GRID_W == 0
    alpha = (2.0 * depth) ** 0.25
    heads = d // V_HEAD
    mla_layers = [i for i in range(depth) if i % N_MIXERS == 1]
    last_ctx_read = mla_layers[-1] if mla_layers else -1

    cond = jnp.concatenate([c, jnp.broadcast_to(c_ctx[None], (SUBLANES - bsz, d))], axis=0)
    mods = _ada_mods(cond, ada_w, ada_b)

    def mod_vecs(i, ctx_stream):
        m = jnp.broadcast_to(mods[i, bsz][None], (bsz, 6 * d)) if ctx_stream else mods[i, :bsz]
        return [m[:, None, k * d:(k + 1) * d] for k in range(6)]

    cs = _rope_table(seq)
    two_stage = seq > DFT_SMALL_MAX_L
    tables = _dft_tables(seq) if two_stage else None
    row_order = _first_stage_row_order(seq) if two_stage else None

    ffn_w = tuple(w.astype(MXU_DTYPE) for w in (ffn_w_gate, ffn_w_up, ffn_w_down))
    hy_w = (hy_w_in.astype(MXU_DTYPE), hy_w_out.astype(MXU_DTYPE))
    h, hc = x, ctx
    for i in range(depth):
        kind, j = i % N_MIXERS, i // N_MIXERS
        ctx_update = i < last_ctx_read
        sh1, sc1, g1, sh2, sc2, g2 = mod_vecs(i, False)
        if kind == 1 or ctx_update:
            csh1, csc1, cg1, csh2, csc2, cg2 = mod_vecs(i, True)
        lg, lb = ln_g[i], ln_b[i]
        if kind == 0:
            hy = (j, hy_w[0], hy_b_in[j], hy_conv_w[j], hy_conv_b[j], hy_bias[j], hy_w[1], hy_b_out[j])
            fp = (hy_f_w_in[j], hy_f_w_hid[j], hy_f_b[j], hy_f_freq[j], hy_f_w_out[j])
            filt, norm = _implicit_filter(seq, *fp, positions=row_order)
            h_mid = _hyena_mixer_norm(alpha, h, (sh1, sc1, g1), hy, filt, norm, tables, lg[0], lb[0])
            if ctx_update:
                filt_c, norm_c = _implicit_filter(hc.shape[1], *fp)
                hc_mid = _hyena_mixer_norm(alpha, hc, (csh1, csc1, cg1), hy, filt_c, norm_c, None, lg[0], lb[0])
        elif kind == 1:
            assert not ctx_update, "context queries are only needed when a later layer reads the context"
            rank = mla_q_norm.shape[1]
            w_in = mla_w_in[j]
            w_in = jnp.concatenate([w_in, _rotate_half_cols(w_in[:, 2 * rank:])], axis=1)
            zeros = jnp.zeros((w_in.shape[1],), F32)
            t = _mod_proj(h, sh1, sc1, w_in, zeros, slabs=1, tm=512, tn=w_in.shape[1])[0]
            tc = _mod_proj(hc, csh1, csc1, w_in, zeros, slabs=1, tm=512, tn=w_in.shape[1])[0]
            q = _mla_queries(t, mla_q_norm[j], mla_wq_b[j], cs, heads)
            k, v = _mla_keys_values(t, mla_kv_norm[j], mla_wkv_b[j], cs, heads, True)
            kc, vc = _mla_keys_values(tc, mla_kv_norm[j], mla_wkv_b[j], cs, heads, False)
            o = _mla_attend(q, k, v, kc, vc)
            h_mid = _attn_out(alpha, o, mla_w_out[j], h, g1, lg[0], lb[0])
        else:
            h_mid = _pool_mixer(alpha, h, sh1, sc1, g1, pool_w[j], pool_scale[j], lg[0], lb[0])
            if ctx_update:
                hc_mid = _pool_mixer(alpha, hc, csh1, csc1, cg1, pool_w[j], pool_scale[j], lg[0], lb[0])
        h = _ffn(alpha, h_mid, sh2, sc2, g2, i, *ffn_w, lg[1], lb[1])
        if ctx_update:
            hc = _ffn(alpha, hc_mid, csh2, csc2, cg2, i, *ffn_w, lg[1], lb[1])
    return h
```

```python
import functools
import math

import jax
import jax.numpy as jnp
from jax import lax
from jax.experimental import pallas as pl
from jax.experimental.pallas import tpu as pltpu

F32 = jnp.float32
MXU_DTYPE = jnp.bfloat16

V7X_VMEM_BYTES = 64 * 1024 * 1024
VMEM_LIMIT = V7X_VMEM_BYTES - 8 * 1024 * 1024
LANES = 128
SUBLANES = 8

N_MIXERS = 3
GRID_W = 64
LN_EPS = 1e-6
RMS_EPS = 1e-6
HY_TARGET = 1e-2
HY_FAST = 0.3
HY_SLOW = 1.5
HY_MIN_DECAY = math.log(HY_TARGET) / HY_SLOW
HY_MAX_DECAY = math.log(HY_TARGET) / HY_FAST
QK_NOPE = 128
QK_ROPE = 64
V_HEAD = 128
ROPE_PAIRS = QK_ROPE // 4
ROPE_THETA = 10000.0
ATTN_SCALE = (QK_NOPE + QK_ROPE) ** -0.5
QUERY_SCALE = ATTN_SCALE * math.log2(math.e)
ATTN_CHUNK = 512
V_ROWS = V_HEAD + 16
ADA_STREAMS = 4
POOL_WINDOWS = (2, 4, 8, 16)
POOL_HALO = 8

DFT_N2 = 128
DFT_SMALL_MAX_L = 512
DFT_MID_ROWS = 4


def _params(sem, vmem=VMEM_LIMIT, flags=None):
    return pltpu.CompilerParams(dimension_semantics=sem, vmem_limit_bytes=vmem, flags=flags)


def _tile(n, t):
    t = min(n, t)
    assert n % t == 0, (n, t)
    return t


def _dot(a, b):
    return jnp.dot(a, b, preferred_element_type=F32)


def _layer_norm_rows(x, g, b):
    mu = jnp.mean(x, axis=-1, keepdims=True)
    xc = x - mu
    var = jnp.mean(xc * xc, axis=-1, keepdims=True)
    return xc * lax.rsqrt(var + LN_EPS) * g + b


def _silu(x):
    return x * (1.0 / (1.0 + jnp.exp(-x)))


def _ada_kernel(c_ref, *refs):
    w_refs, b_ref, o_ref = refs[:ADA_STREAMS], refs[ADA_STREAMS], refs[ADA_STREAMS + 1]
    a = _silu(c_ref[...]).astype(MXU_DTYPE)
    part = jnp.concatenate([_dot(a, w_ref[0].astype(MXU_DTYPE)) for w_ref in w_refs], axis=1)

    @pl.when(pl.program_id(1) == 0)
    def _():
        o_ref[0] = part + b_ref[0]

    @pl.when(pl.program_id(1) > 0)
    def _():
        o_ref[0] += part


def _ada_mods(cond, ada_w, ada_b):
    depth, d, n = ada_w.shape
    rows = cond.shape[0]
    tk = _tile(d, 256)
    part = n // ADA_STREAMS
    w_spec = lambda s: pl.BlockSpec((1, tk, part), lambda i, k: (i, k, s))
    return pl.pallas_call(
        _ada_kernel,
        out_shape=jax.ShapeDtypeStruct((depth, rows, n), F32),
        grid=(depth, d // tk),
        in_specs=[pl.BlockSpec((rows, tk), lambda i, k: (0, k))] + [w_spec(s) for s in range(ADA_STREAMS)]
        + [pl.BlockSpec((1, 1, n), lambda i, k: (i, 0, 0))],
        out_specs=pl.BlockSpec((1, rows, n), lambda i, k: (i, 0, 0)),
        compiler_params=_params(("parallel", "arbitrary")),
        name="ada_mods",
    )(cond, *([ada_w] * ADA_STREAMS), ada_b.reshape(depth, 1, n))


def _mod_proj_kernel(x_ref, sh_ref, sc_ref, w_ref, b_ref, o_ref, u_scr):
    @pl.when(pl.program_id(2) == 0)
    def _():
        u_scr[...] = (x_ref[0] * (1.0 + sc_ref[0]) + sh_ref[0]).astype(MXU_DTYPE)

    o_ref[0, 0] = (_dot(u_scr[...], w_ref[...]) + b_ref[...]).astype(o_ref.dtype)


def _mod_proj(x, shift, scale, w, b, slabs, tm, tn):
    bsz, seq, k = x.shape
    n = w.shape[1]
    n_slab = n // slabs
    tm = _tile(seq, tm)
    tn = _tile(n_slab, tn)
    per = n_slab // tn
    return pl.pallas_call(
        _mod_proj_kernel,
        out_shape=jax.ShapeDtypeStruct((slabs, bsz, seq, n_slab), MXU_DTYPE),
        grid=(bsz, seq // tm, n // tn),
        in_specs=[
            pl.BlockSpec((1, tm, k), lambda bi, i, j: (bi, i, 0)),
            pl.BlockSpec((1, 1, k), lambda bi, i, j: (bi, 0, 0)),
            pl.BlockSpec((1, 1, k), lambda bi, i, j: (bi, 0, 0)),
            pl.BlockSpec((k, tn), lambda bi, i, j: (0, j)),
            pl.BlockSpec((1, tn), lambda bi, i, j: (0, j)),
        ],
        out_specs=pl.BlockSpec((1, 1, tm, tn), lambda bi, i, j: (j // per, bi, i, j % per)),
        scratch_shapes=[pltpu.VMEM((tm, k), MXU_DTYPE)],
        compiler_params=_params(("parallel", "parallel", "arbitrary")),
        name="mod_proj",
    )(x, shift, scale, w.astype(MXU_DTYPE), b.reshape(1, n))


def _hyena_in_kernel(x_ref, prev_ref, next_ref, sh_ref, sc_ref, w0_ref, w1_ref, w2_ref, b_ref, cw_ref, cb_ref,
                     x0_ref, v_ref, u_scr):
    i = pl.program_id(1)
    last = pl.num_programs(1) - 1
    tm = x_ref.shape[1]
    ext = tm + 2 * SUBLANES

    @pl.when(pl.program_id(2) == 0)
    def _():
        shift, scale = sh_ref[0], sc_ref[0]
        rows = jnp.concatenate([prev_ref[0], x_ref[0], next_ref[0]], axis=0)
        u_scr[...] = (rows * (1.0 + scale) + shift).astype(MXU_DTYPE)

    u = u_scr[...]
    outs = []
    for s, w_ref in enumerate((w0_ref, w1_ref, w2_ref)):
        raw = _dot(u, w_ref[...])
        b, taps = b_ref[s], cw_ref[s]
        head = jnp.where(i == 0, -b, raw[:SUBLANES])
        tail = jnp.where(i == last, -b, raw[SUBLANES + tm:])
        raw = jnp.concatenate([head, raw[SUBLANES:SUBLANES + tm], tail], axis=0)
        conv = pltpu.roll(raw, 1, 0) * taps[0:1] + raw * taps[1:2] + pltpu.roll(raw, ext - 1, 0) * taps[2:3]
        const = b * (taps[0:1] + taps[1:2] + taps[2:3]) + cb_ref[s]
        outs.append(conv[SUBLANES:SUBLANES + tm] + const)
    x0_ref[0] = outs[0].astype(x0_ref.dtype)
    v_ref[0] = (outs[2] * outs[1]).astype(v_ref.dtype)


def _hyena_in(x, shift, scale, layer, w_in, b_in, conv_w, conv_b):
    bsz, seq, k = x.shape
    d = w_in.shape[2] // 3
    tm = _tile(seq, 512)
    tn = _tile(d, 512)
    per = d // tn
    hb = tm // SUBLANES
    nhb = seq // SUBLANES
    w = w_in
    cw = conv_w.reshape(3, 3, d).transpose(1, 0, 2)
    out = jax.ShapeDtypeStruct((bsz, seq, d), MXU_DTYPE)
    slab = lambda s: pl.BlockSpec((None, k, tn), lambda bi, i, j: (layer, 0, s * per + j))
    return pl.pallas_call(
        _hyena_in_kernel,
        out_shape=(out, out),
        grid=(bsz, seq // tm, per),
        in_specs=[
            pl.BlockSpec((1, tm, k), lambda bi, i, j: (bi, i, 0)),
            pl.BlockSpec((1, SUBLANES, k), lambda bi, i, j: (bi, jnp.maximum(i * hb - 1, 0), 0)),
            pl.BlockSpec((1, SUBLANES, k), lambda bi, i, j: (bi, jnp.minimum((i + 1) * hb, nhb - 1), 0)),
            pl.BlockSpec((1, 1, k), lambda bi, i, j: (bi, 0, 0)),
            pl.BlockSpec((1, 1, k), lambda bi, i, j: (bi, 0, 0)),
            slab(0), slab(1), slab(2),
            pl.BlockSpec((3, 1, tn), lambda bi, i, j: (0, 0, j)),
            pl.BlockSpec((3, 3, tn), lambda bi, i, j: (0, 0, j)),
            pl.BlockSpec((3, 1, tn), lambda bi, i, j: (0, 0, j)),
        ],
        out_specs=(
            pl.BlockSpec((1, tm, tn), lambda bi, i, j: (bi, i, j)),
            pl.BlockSpec((1, tm, tn), lambda bi, i, j: (bi, i, j)),
        ),
        scratch_shapes=[pltpu.VMEM((tm + 2 * SUBLANES, k), MXU_DTYPE)],
        compiler_params=_params(("parallel", "parallel", "arbitrary")),
        name="hyena_in_conv_gate",
    )(x, x, x, shift, scale, w, w, w, b_in.reshape(3, 1, d), cw, conv_b.reshape(3, 1, d))


def _filter_kernel(z_ref, dist_ref, w_in_ref, w_hid_ref, b_ref, fr_ref, w_out_ref, delta_ref, f_ref, norm_ref):
    hp = lax.Precision.HIGHEST
    b = b_ref[...]
    fr = fr_ref[...]
    g = jnp.sin(fr[0:1] * (jnp.dot(z_ref[...], w_in_ref[...], precision=hp, preferred_element_type=F32) + b[0:1]))
    g = jnp.sin(fr[1:2] * (jnp.dot(g, w_hid_ref[0], precision=hp, preferred_element_type=F32) + b[1:2]))
    g = jnp.sin(fr[2:3] * (jnp.dot(g, w_hid_ref[1], precision=hp, preferred_element_type=F32) + b[2:3]))
    filt = _dot(g.astype(MXU_DTYPE), w_out_ref[...])
    filt = filt * jnp.exp(-dist_ref[...] * delta_ref[...])
    f_ref[...] = filt

    @pl.when(pl.program_id(0) == 0)
    def _():
        norm_ref[...] = jnp.zeros_like(norm_ref)

    norm_ref[...] += jnp.sum(jnp.abs(filt), axis=0, keepdims=True)


def _first_stage_row_order(seq):
    a_hi, c, a_lo = jnp.meshgrid(jnp.arange(seq // (DFT_N2 * SUBLANES)), jnp.arange(DFT_N2), jnp.arange(SUBLANES),
                                 indexing="ij")
    return (DFT_N2 * (SUBLANES * a_hi + a_lo) + c).reshape(seq)


def _implicit_filter(seq, f_w_in, f_w_hid, f_b, f_freq, f_w_out, positions=None):
    emb, width = f_w_in.shape
    bands_n = (emb - 1) // 2
    d = f_w_out.shape[1]
    emb_pad = -(-emb // LANES) * LANES
    filt_w = -(-width // LANES) * LANES
    pad_to = lambda a, shape: jnp.pad(a.astype(F32), [(0, s - n) for s, n in zip(shape, a.shape)])
    pos = (jnp.arange(seq) if positions is None else positions).astype(F32)
    t = pos / (seq - 1)
    bands = jnp.linspace(1e-4, bands_n - 1, bands_n, dtype=F32)
    ang = (2.0 * math.pi / seq) * pos[:, None] * bands[None, :]
    z = pad_to(jnp.concatenate([t[:, None], jnp.cos(ang), -jnp.sin(ang)], axis=-1), (seq, emb_pad))
    w_in = pad_to(f_w_in, (emb_pad, filt_w))
    f_w_hid = pad_to(f_w_hid, (2, filt_w, filt_w))
    f_b = pad_to(f_b, (3, filt_w))
    f_freq = pad_to(f_freq, (3, filt_w))
    f_w_out = pad_to(f_w_out, (filt_w, d))
    dist = (jnp.abs(pos - seq // 2) / (seq // 2))[:, None]
    deltas = jnp.abs(jnp.linspace(HY_MIN_DECAY, HY_MAX_DECAY, d, dtype=F32))[None, :]
    tl = _tile(seq, 512)
    full = lambda shape: pl.BlockSpec(shape, lambda i: (0,) * len(shape))
    return pl.pallas_call(
        _filter_kernel,
        out_shape=(jax.ShapeDtypeStruct((seq, d), F32), jax.ShapeDtypeStruct((1, d), F32)),
        grid=(seq // tl,),
        in_specs=[
            pl.BlockSpec((tl, emb_pad), lambda i: (i, 0)),
            pl.BlockSpec((tl, 1), lambda i: (i, 0)),
            full((emb_pad, filt_w)),
            full((2, filt_w, filt_w)),
            full((3, filt_w)),
            full((3, filt_w)),
            full((filt_w, d)),
            full((1, d)),
        ],
        out_specs=(pl.BlockSpec((tl, d), lambda i: (i, 0)), full((1, d))),
        compiler_params=_params(("arbitrary",)),
        name="hyena_filter",
    )(z, dist, w_in, f_w_hid, f_b, f_freq, f_w_out.astype(MXU_DTYPE), deltas)


def _real_form(mr, mi):
    top = jnp.concatenate([mr, -mi], axis=-1)
    bot = jnp.concatenate([mi, mr], axis=-1)
    return jnp.concatenate([top, bot], axis=-2)


def _unit_roots(idx, n):
    ang = (2.0 * math.pi / n) * (idx % n).astype(F32)
    return jnp.cos(ang), -jnp.sin(ang)


def _dft_tables(seq):
    n = 2 * seq
    n2 = DFT_N2
    n1 = n // n2
    i32 = jnp.int32
    p = jnp.arange(n1, dtype=i32)
    c = jnp.arange(n2, dtype=i32)
    tr, ti = _unit_roots(c[:, None] * p[None, :], n)
    a_in = jnp.arange(n1 // 2, dtype=i32)
    fr, fi = _unit_roots(p[:, None] * a_in[None, :], n1)
    mr = tr[:, :, None] * fr[None] - ti[:, :, None] * fi[None]
    mi = tr[:, :, None] * fi[None] + ti[:, :, None] * fr[None]
    first = _real_form(mr, mi)
    a_out = jnp.arange(n1 // 4, n1 // 4 + n1 // 2, dtype=i32)
    gr, gi = _unit_roots(p[:, None] * a_out[None, :], n1)
    cr = (tr[:, :, None] * gr[None] - ti[:, :, None] * gi[None]) / n
    ci = -(tr[:, :, None] * gi[None] + ti[:, :, None] * gr[None]) / n
    last = _real_form(jnp.swapaxes(cr, 1, 2), jnp.swapaxes(ci, 1, 2))
    q = jnp.arange(n2, dtype=i32)
    hr, hi_ = _unit_roots(q[:, None] * c[None, :], n2)
    mid_f = _real_form(hr, hi_)
    mid_i = _real_form(hr, -hi_)
    return tuple(t.astype(MXU_DTYPE) for t in (first, mid_f, mid_i, last))


def _dft_first_kernel(x_ref, m_ref, o_ref):
    x = x_ref[...]
    x = x.reshape(x.shape[0] * x.shape[1] * SUBLANES, x.shape[4])
    o_ref[...] = _dot(m_ref[0], x.astype(MXU_DTYPE)).astype(o_ref.dtype)


def _dft_first(x5, table):
    r, a_hi, n2, _, d = x5.shape
    rows = r * a_hi * SUBLANES
    n_out = table.shape[1]
    if rows % LANES:
        table = table[:, :, :rows]
    bd = _tile(d, 2048)
    per = d // bd
    return pl.pallas_call(
        _dft_first_kernel,
        out_shape=jax.ShapeDtypeStruct((n_out, n2 * d), MXU_DTYPE),
        grid=(n2 * per,),
        in_specs=[
            pl.BlockSpec((r, a_hi, 1, SUBLANES, bd), lambda j: (0, 0, j // per, 0, j % per)),
            pl.BlockSpec((1, n_out, rows), lambda j: (j // per, 0, 0)),
        ],
        out_specs=pl.BlockSpec((n_out, bd), lambda j: (0, j)),
        compiler_params=_params(("parallel",)),
        name="hyena_dft_first",
    )(x5, table)


def _dft_mid_filter_kernel(a_ref, f_ref, norm_ref, o_ref):
    n2 = a_ref.shape[2]
    for p in range(a_ref.shape[1]):
        a = jnp.concatenate([a_ref[0, p], a_ref[1, p]], axis=0)
        z = _dot(f_ref[...], a) / norm_ref[...]
        o_ref[0, p] = z[:n2].astype(o_ref.dtype)
        o_ref[1, p] = z[n2:].astype(o_ref.dtype)


def _dft_mid_conv_kernel(a_ref, h_ref, f_ref, g_ref, o_ref):
    n2 = a_ref.shape[2]
    for p in range(a_ref.shape[1]):
        a = jnp.concatenate([a_ref[0, p], a_ref[1, p]], axis=0)
        z = _dot(f_ref[...], a)
        zr, zi = z[:n2], z[n2:]
        hr, hi_ = h_ref[0, p].astype(F32), h_ref[1, p].astype(F32)
        w = jnp.concatenate([zr * hr - zi * hi_, zr * hi_ + zi * hr], axis=0)
        y = _dot(g_ref[...], w.astype(MXU_DTYPE))
        o_ref[0, p] = y[:n2].astype(o_ref.dtype)
        o_ref[1, p] = y[n2:].astype(o_ref.dtype)


def _dft_mid(a, tables, d, *, spectrum=None, norm=None):
    n2 = DFT_N2
    n1 = a.shape[0] // 2
    a4 = a.reshape(2, n1, n2, d)
    bd = _tile(d, 2048)
    pb = _tile(n1, DFT_MID_ROWS)
    blk = pl.BlockSpec((2, pb, n2, bd), lambda p, j: (0, p, 0, j))
    mat = pl.BlockSpec((2 * n2, 2 * n2), lambda p, j: (0, 0))
    fwd, inv = tables
    if spectrum is None:
        kern, ins, specs = _dft_mid_filter_kernel, (a4, fwd, norm), [blk, mat, pl.BlockSpec((1, bd), lambda p, j: (0, j))]
    else:
        kern, ins, specs = _dft_mid_conv_kernel, (a4, spectrum, fwd, inv), [blk, blk, mat, mat]
    out = pl.pallas_call(
        kern,
        out_shape=jax.ShapeDtypeStruct((2, n1, n2, d), MXU_DTYPE),
        grid=(n1 // pb, d // bd),
        in_specs=specs,
        out_specs=blk,
        compiler_params=_params(("parallel", "parallel")),
        name="hyena_dft_mid",
    )(*ins)
    return out


def _dft_last_kernel(x_ref, m_ref, o_ref):
    y = _dot(m_ref[0], x_ref[...])
    o_ref[...] = y.reshape(o_ref.shape).astype(o_ref.dtype)


def _dft_last(b, table, d):
    rows, cols = b.shape
    n1 = rows // 2
    n2 = cols // d
    a_hi = n1 // 2 // SUBLANES
    bd = _tile(d, 2048)
    per = d // bd
    return pl.pallas_call(
        _dft_last_kernel,
        out_shape=jax.ShapeDtypeStruct((2, a_hi, n2, SUBLANES, d), MXU_DTYPE),
        grid=(cols // bd,),
        in_specs=[
            pl.BlockSpec((rows, bd), lambda j: (0, j)),
            pl.BlockSpec((1, n1, rows), lambda j: (j // per, 0, 0)),
        ],
        out_specs=pl.BlockSpec((2, a_hi, 1, SUBLANES, bd), lambda j: (0, 0, j // per, 0, j % per)),
        compiler_params=_params(("parallel",)),
        name="hyena_dft_last",
    )(b, table)


def _long_conv_two_stage(v, filt, norm, tables):
    bsz, seq, d = v.shape
    n1 = 2 * seq // DFT_N2
    a_hi = n1 // 2 // SUBLANES
    first, mid_f, mid_i, last = tables
    cols = DFT_N2 * d
    h_first = _dft_first(filt.reshape(1, a_hi, DFT_N2, SUBLANES, d), first)
    spectrum = _dft_mid(h_first, (mid_f, mid_i), d, norm=norm)
    v5 = v.reshape(bsz, a_hi, SUBLANES, DFT_N2, d).transpose(0, 1, 3, 2, 4)
    a = _dft_first(v5, first)
    bmat = _dft_mid(a, (mid_f, mid_i), d, spectrum=spectrum)
    y5 = _dft_last(bmat.reshape(2 * n1, cols), last, d)
    return y5.transpose(0, 1, 3, 2, 4).reshape(bsz, seq, d)


def _dft_small_kernel(v_ref, h_ref, norm_ref, f_ref, g_ref, o_ref):
    seq = v_ref.shape[1]
    n = 2 * seq
    fwd = f_ref[...]
    x = jnp.concatenate([v_ref[0], v_ref[1]], axis=0)
    z = _dot(fwd, x.astype(MXU_DTYPE))
    hx = jnp.concatenate([h_ref[...] / norm_ref[...], jnp.zeros_like(h_ref)], axis=0)
    hs = _dot(fwd, hx.astype(MXU_DTYPE))
    zr, zi, hr, hi_ = z[:n], z[n:], hs[:n], hs[n:]
    w = jnp.concatenate([zr * hr - zi * hi_, zr * hi_ + zi * hr], axis=0)
    y = _dot(g_ref[...], w.astype(MXU_DTYPE))
    o_ref[0] = y[:seq].astype(o_ref.dtype)
    o_ref[1] = y[seq:].astype(o_ref.dtype)


def _long_conv_small(v, filt, norm):
    bsz, seq, d = v.shape
    n = 2 * seq
    k = jnp.arange(n, dtype=jnp.int32)
    t_in = jnp.arange(seq, dtype=jnp.int32)
    fr, fi = _unit_roots(k[:, None] * t_in[None, :], n)
    fwd = _real_form(fr, fi)
    t_out = jnp.arange(seq // 2, seq // 2 + seq, dtype=jnp.int32)
    gr, gi = _unit_roots(t_out[:, None] * k[None, :], n)
    inv = _real_form(gr / n, -gi / n)
    bd = _tile(d, 512)
    full = lambda shape: pl.BlockSpec(shape, lambda j: (0,) * len(shape))
    return pl.pallas_call(
        _dft_small_kernel,
        out_shape=jax.ShapeDtypeStruct((bsz, seq, d), MXU_DTYPE),
        grid=(d // bd,),
        in_specs=[
            pl.BlockSpec((bsz, seq, bd), lambda j: (0, 0, j)),
            pl.BlockSpec((seq, bd), lambda j: (0, j)),
            pl.BlockSpec((1, bd), lambda j: (0, j)),
            full(fwd.shape), full(inv.shape),
        ],
        out_specs=pl.BlockSpec((bsz, seq, bd), lambda j: (0, 0, j)),
        compiler_params=_params(("parallel",)),
        name="hyena_dft_small",
    )(v, filt, norm, fwd.astype(MXU_DTYPE), inv.astype(MXU_DTYPE))


def _hyena_out_kernel(alpha, x0_ref, cv_ref, v_ref, bias_ref, w_ref, b_ref, h_ref, g_ref, lg_ref, lb_ref, o_ref):
    y = x0_ref[0].astype(F32) * (cv_ref[0].astype(F32) + v_ref[0].astype(F32) * bias_ref[...])
    y = _dot(y.astype(MXU_DTYPE), w_ref[...]) + b_ref[...]
    o_ref[0] = _layer_norm_rows(alpha * h_ref[0] + g_ref[0] * y, lg_ref[...], lb_ref[...])


def _attn_out_kernel(alpha, o_in_ref, w_ref, h_ref, g_ref, lg_ref, lb_ref, o_ref):
    y = _dot(o_in_ref[0], w_ref[...])
    o_ref[0] = _layer_norm_rows(alpha * h_ref[0] + g_ref[0] * y, lg_ref[...], lb_ref[...])


def _row_spec(tm, d):
    return pl.BlockSpec((1, tm, d), lambda bi, i: (bi, i, 0))


def _vec_spec(d):
    return pl.BlockSpec((1, d), lambda bi, i: (0, 0))


def _bvec_spec(d):
    return pl.BlockSpec((1, 1, d), lambda bi, i: (bi, 0, 0))


def _const_spec(shape):
    return pl.BlockSpec(shape, lambda bi, i: (0,) * len(shape), pipeline_mode=pl.Buffered(1))


def _hyena_out(alpha, x0, cv, v, bias, layer, w, b, h, gate, ln_g, ln_b):
    bsz, seq, d = h.shape
    tm = _tile(seq, 512)
    return pl.pallas_call(
        functools.partial(_hyena_out_kernel, alpha),
        out_shape=jax.ShapeDtypeStruct(h.shape, F32),
        grid=(bsz, seq // tm),
        in_specs=[_row_spec(tm, d), _row_spec(tm, d), _row_spec(tm, d), _vec_spec(d),
                  pl.BlockSpec((None, d, d), lambda bi, i: (layer, 0, 0), pipeline_mode=pl.Buffered(1)),
                  _vec_spec(d), _row_spec(tm, d), _bvec_spec(d), _vec_spec(d), _vec_spec(d)],
        out_specs=_row_spec(tm, d),
        compiler_params=_params(("parallel", "parallel")),
        name="hyena_out_norm",
    )(x0, cv, v, bias.reshape(1, d), w, b.reshape(1, d), h, gate, ln_g.reshape(1, d), ln_b.reshape(1, d))


def _attn_out(alpha, o, w, h, gate, ln_g, ln_b):
    bsz, seq, d = h.shape
    k = o.shape[2]
    tm = _tile(seq, 512)
    return pl.pallas_call(
        functools.partial(_attn_out_kernel, alpha),
        out_shape=jax.ShapeDtypeStruct(h.shape, F32),
        grid=(bsz, seq // tm),
        in_specs=[_row_spec(tm, k), _const_spec((k, d)), _row_spec(tm, d), _bvec_spec(d), _vec_spec(d), _vec_spec(d)],
        out_specs=_row_spec(tm, d),
        compiler_params=_params(("parallel", "parallel")),
        name="attn_out_norm",
    )(o, w.astype(MXU_DTYPE), h, gate, ln_g.reshape(1, d), ln_b.reshape(1, d))


def _ffn_kernel(alpha, h_ref, sh_ref, sc_ref, g_ref, wg_ref, wu_ref, wd_ref, lg_ref, lb_ref, o_ref, u_scr, acc_scr):
    f = pl.program_id(2)

    @pl.when(f == 0)
    def _():
        u_scr[...] = (h_ref[0] * (1.0 + sc_ref[0]) + sh_ref[0]).astype(MXU_DTYPE)
        acc_scr[...] = jnp.zeros_like(acc_scr)

    u = u_scr[...]
    gate = _dot(u, wg_ref[...])
    up = _dot(u, wu_ref[...])
    act = (_silu(gate) * up).astype(MXU_DTYPE)
    acc_scr[...] += _dot(act, wd_ref[...])

    @pl.when(f == pl.num_programs(2) - 1)
    def _():
        o_ref[0] = _layer_norm_rows(alpha * h_ref[0] + g_ref[0] * acc_scr[...], lg_ref[...], lb_ref[...])


def _ffn(alpha, h, shift, scale, gate, layer, w_gate, w_up, w_down, ln_g, ln_b):
    bsz, seq, d = h.shape
    ff = w_gate.shape[2]
    tm = _tile(seq, 512)
    tf = _tile(ff, 512)
    return pl.pallas_call(
        functools.partial(_ffn_kernel, alpha),
        out_shape=jax.ShapeDtypeStruct(h.shape, F32),
        grid=(bsz, seq // tm, ff // tf),
        in_specs=[
            pl.BlockSpec((1, tm, d), lambda bi, i, f: (bi, i, 0)),
            pl.BlockSpec((1, 1, d), lambda bi, i, f: (bi, 0, 0)),
            pl.BlockSpec((1, 1, d), lambda bi, i, f: (bi, 0, 0)),
            pl.BlockSpec((1, 1, d), lambda bi, i, f: (bi, 0, 0)),
            pl.BlockSpec((None, d, tf), lambda bi, i, f: (layer, 0, f)),
            pl.BlockSpec((None, d, tf), lambda bi, i, f: (layer, 0, f)),
            pl.BlockSpec((None, tf, d), lambda bi, i, f: (layer, f, 0)),
            pl.BlockSpec((1, d), lambda bi, i, f: (0, 0)),
            pl.BlockSpec((1, d), lambda bi, i, f: (0, 0)),
        ],
        out_specs=pl.BlockSpec((1, tm, d), lambda bi, i, f: (bi, i, 0)),
        scratch_shapes=[pltpu.VMEM((tm, d), MXU_DTYPE), pltpu.VMEM((tm, d), F32)],
        compiler_params=_params(("parallel", "parallel", "arbitrary")),
        name="ffn_swiglu_norm",
    )(h, shift, scale, gate, w_gate, w_up, w_down,
      ln_g.reshape(1, d), ln_b.reshape(1, d))


def _rotate_half_cols(w):
    ws = w.reshape(w.shape[:-1] + (2, 2, ROPE_PAIRS))
    return jnp.stack([-ws[..., 1, :], ws[..., 0, :]], axis=-2).reshape(w.shape)


def _rope_table(seq):
    rows = seq // GRID_W
    row = jnp.repeat(jnp.arange(rows, dtype=F32), GRID_W)
    col = jnp.tile(jnp.arange(GRID_W, dtype=F32), rows)
    inv = ROPE_THETA ** (-jnp.arange(ROPE_PAIRS, dtype=F32) / ROPE_PAIRS)
    ang = jnp.stack([row[:, None] * inv, col[:, None] * inv], axis=1)
    ang = jnp.broadcast_to(ang[:, :, None, :], (seq, 2, 2, ROPE_PAIRS)).reshape(seq, QK_ROPE)
    return jnp.concatenate([jnp.cos(ang), jnp.sin(ang)], axis=-1)


def _rms_rows(x, g):
    return x * lax.rsqrt(jnp.mean(x * x, axis=-1, keepdims=True) + RMS_EPS) * g


def _rope_pair(x, cs):
    t = x * cs
    return t + pltpu.roll(t, QK_ROPE, 1)


def _mla_q_kernel(cq_ref, g_ref, w_ref, cs_ref, q_ref):
    xn = _rms_rows(cq_ref[0].astype(F32), g_ref[...]).astype(MXU_DTYPE)
    cs = cs_ref[...]

    def head(h, carry):
        a = _dot(xn, w_ref[h])
        r = _rope_pair(a[:, QK_NOPE:], cs)
        q_ref[0, h] = (jnp.concatenate([a[:, :QK_NOPE], r], axis=1) * QUERY_SCALE).astype(q_ref.dtype)
        return carry

    lax.fori_loop(0, w_ref.shape[0], head, 0, unroll=2)


def _mla_queries(t, q_norm, wq_b, cs, heads):
    bsz, seq, _ = t.shape
    rank = q_norm.shape[0]
    tm = _tile(seq, 512)
    w = wq_b.reshape(rank, heads, QK_NOPE + QK_ROPE)
    w = jnp.concatenate([w, _rotate_half_cols(w[..., QK_NOPE:])], axis=-1).transpose(1, 0, 2).astype(MXU_DTYPE)
    return pl.pallas_call(
        _mla_q_kernel,
        out_shape=jax.ShapeDtypeStruct((bsz, heads, seq, 2 * LANES), MXU_DTYPE),
        grid=(bsz, seq // tm),
        in_specs=[
            pl.BlockSpec((1, tm, rank), lambda bi, i: (bi, i, 0)),
            pl.BlockSpec((1, rank), lambda bi, i: (0, 0)),
            pl.BlockSpec((heads, rank, 2 * LANES), lambda bi, i: (0, 0, 0)),
            pl.BlockSpec((tm, LANES), lambda bi, i: (i, 0)),
        ],
        out_specs=pl.BlockSpec((1, heads, tm, 2 * LANES), lambda bi, i: (bi, 0, i, 0)),
        compiler_params=_params(("parallel", "parallel")),
        name="mla_queries",
    )(t, q_norm.reshape(1, rank), w, cs)


def _mla_kv_kernel(use_rope, ckv_ref, kr_ref, g_ref, w_ref, cs_ref, k_ref, vt_ref):
    xn = _rms_rows(ckv_ref[0].astype(F32), g_ref[...]).astype(MXU_DTYPE)
    kr = kr_ref[0].astype(F32)
    if use_rope:
        kr = _rope_pair(kr, cs_ref[...])
    lane = lax.broadcasted_iota(jnp.int32, kr.shape, 1)
    kr = jnp.where(lane < QK_ROPE, kr, 0.0).astype(k_ref.dtype)
    ones = jnp.ones((V_ROWS - V_HEAD, xn.shape[0]), vt_ref.dtype)

    def head(h, carry):
        kv = _dot(xn, w_ref[h])
        k_ref[0, h] = jnp.concatenate([kv[:, :QK_NOPE].astype(k_ref.dtype), kr], axis=1)
        vt_ref[0, h] = jnp.concatenate([kv[:, QK_NOPE:].T.astype(vt_ref.dtype), ones], axis=0)
        return carry

    lax.fori_loop(0, w_ref.shape[0], head, 0, unroll=2)


def _mla_keys_values(t, kv_norm, wkv_b, cs, heads, use_rope):
    bsz, seq, _ = t.shape
    rank = kv_norm.shape[0]
    tm = _tile(seq, 512)
    w = wkv_b.reshape(rank, heads, QK_NOPE + V_HEAD).transpose(1, 0, 2).astype(MXU_DTYPE)
    return pl.pallas_call(
        functools.partial(_mla_kv_kernel, use_rope),
        out_shape=(jax.ShapeDtypeStruct((bsz, heads, seq, 2 * LANES), MXU_DTYPE),
                   jax.ShapeDtypeStruct((bsz, heads, V_ROWS, seq), MXU_DTYPE)),
        grid=(bsz, seq // tm),
        in_specs=[
            pl.BlockSpec((1, tm, rank), lambda bi, i: (bi, i, 1)),
            pl.BlockSpec((1, tm, LANES), lambda bi, i: (bi, i, 2 * rank // LANES)),
            pl.BlockSpec((1, rank), lambda bi, i: (0, 0)),
            pl.BlockSpec((heads, rank, QK_NOPE + V_HEAD), lambda bi, i: (0, 0, 0)),
            pl.BlockSpec((tm, LANES), lambda bi, i: (i, 0)),
        ],
        out_specs=(pl.BlockSpec((1, heads, tm, 2 * LANES), lambda bi, i: (bi, 0, i, 0)),
                   pl.BlockSpec((1, heads, V_ROWS, tm), lambda bi, i: (bi, 0, 0, i))),
        compiler_params=_params(("parallel", "parallel")),
        name="mla_keys_values",
    )(t, t, kv_norm.reshape(1, rank), w, cs)


def _attn_kernel(q_ref, k_ref, vt_ref, kc_ref, vtc_ref, o_ref, qt_scr, s_scr, p_scr, smax_scr, m_scr, alpha_scr, acc_scr):
    kv = pl.program_id(3)
    n_main = pl.num_programs(3) - 1

    n_chunks, _, cw = qt_scr.shape

    @pl.when(kv == 0)
    def _():
        for c in range(n_chunks):
            qt_scr[c] = q_ref[0, 0, c * cw:(c + 1) * cw, :].T
        m_scr[...] = jnp.full_like(m_scr, -jnp.inf)
        acc_scr[...] = jnp.zeros_like(acc_scr)

    def step(k, vt):
        nk = k.shape[0]

        def scores(c, slot):
            s = _dot(k, qt_scr[c])
            s_scr[slot, :nk, :] = s
            smax_scr[slot] = jnp.max(s, axis=0, keepdims=True)

        def exponent(c, slot):
            m_prev = m_scr[c]
            m_new = jnp.maximum(m_prev, smax_scr[slot])
            alpha_scr[c] = jnp.exp2(m_prev - m_new)
            p_scr[slot, :nk, :] = jnp.exp2((s_scr[slot, :nk, :] - m_new).astype(MXU_DTYPE))
            m_scr[c] = m_new

        def values(c, slot):
            acc_scr[c] = alpha_scr[c] * acc_scr[c] + _dot(vt, p_scr[slot, :nk, :])

        for i in range(n_chunks + 2):
            if i < n_chunks:
                scores(i, i % 2)
            if 1 <= i <= n_chunks:
                exponent(i - 1, (i - 1) % 2)
            if i >= 2:
                values(i - 2, i % 2)

    @pl.when(kv < n_main)
    def _():
        step(k_ref[0, 0], vt_ref[0, 0])

    @pl.when(kv == n_main)
    def _():
        step(kc_ref[0, 0], vtc_ref[0, 0])
        for c in range(n_chunks):
            acc = acc_scr[c]
            o_ref[0, c * cw:(c + 1) * cw, :] = (acc[:V_HEAD] / acc[V_HEAD:V_HEAD + 1]).T.astype(o_ref.dtype)


def _mla_attend(q, k, vt, kc, vtc):
    bsz, heads, seq, dq = q.shape
    lc = kc.shape[2]
    tq = _tile(seq, 2048)
    tk = _tile(seq, 2048)
    cw = min(tq, ATTN_CHUNK)
    assert lc <= tk
    n_main = seq // tk
    return pl.pallas_call(
        _attn_kernel,
        out_shape=jax.ShapeDtypeStruct((bsz, seq, heads * V_HEAD), MXU_DTYPE),
        grid=(bsz, heads, seq // tq, n_main + 1),
        in_specs=[
            pl.BlockSpec((1, 1, tq, dq), lambda bi, h, i, j: (bi, h, i, 0)),
            pl.BlockSpec((1, 1, tk, dq), lambda bi, h, i, j: (bi, h, jnp.minimum(j, n_main - 1), 0)),
            pl.BlockSpec((1, 1, V_ROWS, tk), lambda bi, h, i, j: (bi, h, 0, jnp.minimum(j, n_main - 1))),
            pl.BlockSpec((1, 1, lc, dq), lambda bi, h, i, j: (bi, h, 0, 0)),
            pl.BlockSpec((1, 1, V_ROWS, lc), lambda bi, h, i, j: (bi, h, 0, 0)),
        ],
        out_specs=pl.BlockSpec((1, tq, V_HEAD), lambda bi, h, i, j: (bi, i, h)),
        scratch_shapes=[pltpu.VMEM((tq // cw, dq, cw), MXU_DTYPE), pltpu.VMEM((2, tk, cw), F32),
                        pltpu.VMEM((2, tk, cw), MXU_DTYPE), pltpu.VMEM((2, 1, cw), F32),
                        pltpu.VMEM((tq // cw, 1, cw), F32), pltpu.VMEM((tq // cw, 1, cw), F32), pltpu.VMEM((tq // cw, V_ROWS, cw), F32)],
        compiler_params=_params(("parallel", "parallel", "parallel", "arbitrary")),
        name="mla_attention",
    )(q, k, vt, kc, vtc)


def _pool_kernel(alpha, seq, h_ref, prev_ref, next_ref, sh_ref, sc_ref, g_ref, w_ref, ps_ref, lg_ref, lb_ref, o_ref):
    i = pl.program_id(1)
    last = pl.num_programs(1) - 1
    tm = h_ref.shape[1]
    d = h_ref.shape[2]
    groups = len(POOL_WINDOWS)
    ch = d // groups
    ext = tm + 2 * POOL_HALO
    shift, scale = sh_ref[0], sc_ref[0]
    h = h_ref[0]
    u = h * (1.0 + scale) + shift
    u_prev = jnp.where(i == 0, 0.0, prev_ref[0] * (1.0 + scale) + shift)
    u_next = jnp.where(i == last, 0.0, next_ref[0] * (1.0 + scale) + shift)
    e = jnp.concatenate([u_prev, u, u_next], axis=0)
    t = i * tm + lax.broadcasted_iota(jnp.int32, (tm, 1), 0)
    ys = []
    for g, win in enumerate(POOL_WINDOWS):
        a = e[:, g * ch:(g + 1) * ch]
        span = 1
        while span < win:
            a = a + pltpu.roll(a, ext - span, 0)
            span *= 2
        half = win // 2
        a = pltpu.roll(a, half, 0)
        s = a[POOL_HALO:POOL_HALO + tm]
        cnt = (jnp.minimum(t + half, seq) - jnp.maximum(t - half, 0)).astype(F32)
        dg = s / cnt - u[:, g * ch:(g + 1) * ch]
        ys.append(_dot(dg.astype(MXU_DTYPE), w_ref[g]))
    y = jnp.concatenate(ys, axis=1) * ps_ref[...]
    o_ref[0] = _layer_norm_rows(alpha * h + g_ref[0] * y, lg_ref[...], lb_ref[...])


def _pool_mixer(alpha, h, shift, scale, gate, w_grp, pool_scale, ln_g, ln_b):
    bsz, seq, d = h.shape
    groups, ch, _ = w_grp.shape
    tm = _tile(seq, 512)
    hb = tm // POOL_HALO
    nhb = seq // POOL_HALO
    return pl.pallas_call(
        functools.partial(_pool_kernel, alpha, seq),
        out_shape=jax.ShapeDtypeStruct(h.shape, F32),
        grid=(bsz, seq // tm),
        in_specs=[
            _row_spec(tm, d),
            pl.BlockSpec((1, POOL_HALO, d), lambda bi, i: (bi, jnp.maximum(i * hb - 1, 0), 0)),
            pl.BlockSpec((1, POOL_HALO, d), lambda bi, i: (bi, jnp.minimum((i + 1) * hb, nhb - 1), 0)),
            _bvec_spec(d), _bvec_spec(d), _bvec_spec(d),
            pl.BlockSpec((groups, ch, ch), lambda bi, i: (0, 0, 0)),
            _vec_spec(d), _vec_spec(d), _vec_spec(d),
        ],
        out_specs=_row_spec(tm, d),
        compiler_params=_params(("parallel", "parallel")),
        name="pool_mixer_norm",
    )(h, h, h, shift, scale, gate, w_grp.astype(MXU_DTYPE), pool_scale.reshape(1, d), ln_g.reshape(1, d), ln_b.reshape(1, d))


def _hyena_mixer_norm(alpha, h, mod, hy, filt, norm, tables, ln_g, ln_b):
    shift, scale, gate = mod
    layer, w_in, b_in, conv_w, conv_b, bias, w_out, b_out = hy
    x0, v = _hyena_in(h, shift, scale, layer, w_in, b_in, conv_w, conv_b)
    if h.shape[1] <= DFT_SMALL_MAX_L:
        cv = _long_conv_small(v, filt, norm)
    else:
        cv = _long_conv_two_stage(v, filt, norm, tables)
    return _hyena_out(alpha, x0, cv, v, bias, layer, w_out, b_out, h, gate, ln_g, ln_b)


def kernel(x, c, ctx, c_ctx, ada_w, ada_b, ln_g, ln_b, ffn_w_gate, ffn_w_up, ffn_w_down, hy_w_in, hy_b_in, hy_conv_w, hy_conv_b, hy_f_w_in, hy_f_w_hid, hy_f_b, hy_f_freq, hy_f_w_out, hy_bias, hy_w_out, hy_b_out, mla_w_in, mla_q_norm, mla_kv_norm, mla_wq_b, mla_wkv_b, mla_w_out, pool_w, pool_scale):
    bsz, seq, d = x.shape
    depth = ada_w.shape[0]
    assert bsz == 2, "the long convolution packs exactly two batch rows into one complex signal"
    assert ctx.shape[0] == bsz and seq % GRID_W == 0
    alpha = (2.0 * depth) ** 0.25
    heads = d // V_HEAD
    mla_layers = [i for i in range(depth) if i % N_MIXERS == 1]
    last_ctx_read = mla_layers[-1] if mla_layers else -1

    cond = jnp.concatenate([c, jnp.broadcast_to(c_ctx[None], (SUBLANES - bsz, d))], axis=0)
    mods = _ada_mods(cond, ada_w, ada_b)

    def mod_vecs(i, ctx_stream):
        m = jnp.broadcast_to(mods[i, bsz][None], (bsz, 6 * d)) if ctx_stream else mods[i, :bsz]
        return [m[:, None, k * d:(k + 1) * d] for k in range(6)]

    cs = _rope_table(seq)
    two_stage = seq > DFT_SMALL_MAX_L
    tables = _dft_tables(seq) if two_stage else None
    row_order = _first_stage_row_order(seq) if two_stage else None

    ffn_w = tuple(w.astype(MXU_DTYPE) for w in (ffn_w_gate, ffn_w_up, ffn_w_down))
    hy_w = (hy_w_in.astype(MXU_DTYPE), hy_w_out.astype(MXU_DTYPE))
    h, hc = x, ctx
    for i in range(depth):
        kind, j = i % N_MIXERS, i // N_MIXERS
        ctx_update = i < last_ctx_read
        sh1, sc1, g1, sh2, sc2, g2 = mod_vecs(i, False)
        if kind == 1 or ctx_update:
            csh1, csc1, cg1, csh2, csc2, cg2 = mod_vecs(i, True)
        lg, lb = ln_g[i], ln_b[i]
        if kind == 0:
            hy = (j, hy_w[0], hy_b_in[j], hy_conv_w[j], hy_conv_b[j], hy_bias[j], hy_w[1], hy_b_out[j])
            fp = (hy_f_w_in[j], hy_f_w_hid[j], hy_f_b[j], hy_f_freq[j], hy_f_w_out[j])
            filt, norm = _implicit_filter(seq, *fp, positions=row_order)
            h_mid = _hyena_mixer_norm(alpha, h, (sh1, sc1, g1), hy, filt, norm, tables, lg[0], lb[0])
            if ctx_update:
                filt_c, norm_c = _implicit_filter(hc.shape[1], *fp)
                hc_mid = _hyena_mixer_norm(alpha, hc, (csh1, csc1, cg1), hy, filt_c, norm_c, None, lg[0], lb[0])
        elif kind == 1:
            assert not ctx_update, "context queries are only needed when a later layer reads the context"
            rank = mla_q_norm.shape[1]
            w_in = mla_w_in[j]
            w_in = jnp.concatenate([w_in, _rotate_half_cols(w_in[:, 2 * rank:])], axis=1)
            zeros = jnp.zeros((w_in.shape[1],), F32)
            t = _mod_proj(h, sh1, sc1, w_in, zeros, slabs=1, tm=512, tn=w_in.shape[1])[0]
            tc = _mod_proj(hc, csh1, csc1, w_in, zeros, slabs=1, tm=512, tn=w_in.shape[1])[0]
            q = _mla_queries(t, mla_q_norm[j], mla_wq_b[j], cs, heads)
            k, v = _mla_keys_values(t, mla_kv_norm[j], mla_wkv_b[j], cs, heads, True)
            kc, vc = _mla_keys_values(tc, mla_kv_norm[j], mla_wkv_b[j], cs, heads, False)
            o = _mla_attend(q, k, v, kc, vc)
            h_mid = _attn_out(alpha, o, mla_w_out[j], h, g1, lg[0], lb[0])
        else:
            h_mid = _pool_mixer(alpha, h, sh1, sc1, g1, pool_w[j], pool_scale[j], lg[0], lb[0])
            if ctx_update:
                hc_mid = _pool_mixer(alpha, hc, csh1, csc1, cg1, pool_w[j], pool_scale[j], lg[0], lb[0])
        h = _ffn(alpha, h_mid, sh2, sc2, g2, i, *ffn_w, lg[1], lb[1])
        if ctx_update:
            hc = _ffn(alpha, hc_mid, csh2, csc2, cg2, i, *ffn_w, lg[1], lb[1])
    return h
```

```python
import functools
import math

import jax
import jax.numpy as jnp
from jax import lax
from jax.experimental import pallas as pl
from jax.experimental.pallas import tpu as pltpu

F32 = jnp.float32
MXU_DTYPE = jnp.bfloat16

V7X_VMEM_BYTES = 64 * 1024 * 1024
VMEM_LIMIT = V7X_VMEM_BYTES - 8 * 1024 * 1024
LANES = 128
SUBLANES = 8

N_MIXERS = 3
GRID_W = 64
LN_EPS = 1e-6
RMS_EPS = 1e-6
HY_TARGET = 1e-2
HY_FAST = 0.3
HY_SLOW = 1.5
HY_MIN_DECAY = math.log(HY_TARGET) / HY_SLOW
HY_MAX_DECAY = math.log(HY_TARGET) / HY_FAST
QK_NOPE = 128
QK_ROPE = 64
V_HEAD = 128
ROPE_PAIRS = QK_ROPE // 4
ROPE_THETA = 10000.0
ATTN_SCALE = (QK_NOPE + QK_ROPE) ** -0.5
QUERY_SCALE = ATTN_SCALE * math.log2(math.e)
ATTN_CHUNK = 512
V_ROWS = V_HEAD + 2 * SUBLANES
POOL_WINDOWS = (2, 4, 8, 16)
POOL_HALO = 8

DFT_N2 = 128
DFT_SMALL_MAX_L = 512
DFT_MID_ROWS = 4

ROW_TILE = 512
COL_TILE = 512
ADA_COL_TILE = 1024
DFT_COL_TILE = 2048
ATTN_Q_TILE = 2048
ATTN_KV_TILE = 2048


def _params(sem):
    return pltpu.CompilerParams(dimension_semantics=sem, vmem_limit_bytes=VMEM_LIMIT)


def _tile(n, t):
    t = min(n, t)
    assert n % t == 0, (n, t)
    return t


def _dot(a, b):
    return jnp.dot(a, b, preferred_element_type=F32)


def _layer_norm_rows(x, g, b):
    mu = jnp.mean(x, axis=-1, keepdims=True)
    xc = x - mu
    var = jnp.mean(xc * xc, axis=-1, keepdims=True)
    return xc * lax.rsqrt(var + LN_EPS) * g + b


def _silu(x):
    return x * (1.0 / (1.0 + jnp.exp(-x)))


def _ada_kernel(c_ref, w_ref, b_ref, o_ref):
    c = c_ref[...]
    a = _silu(c).astype(MXU_DTYPE)
    o_ref[0] = _dot(a, w_ref[0].astype(MXU_DTYPE)) + b_ref[0]


def _ada_mods(cond, ada_w, ada_b):
    depth, d, n = ada_w.shape
    rows = cond.shape[0]
    tn = _tile(n, ADA_COL_TILE)
    return pl.pallas_call(
        _ada_kernel,
        out_shape=jax.ShapeDtypeStruct((depth, rows, n), F32),
        grid=(depth, n // tn),
        in_specs=[
            pl.BlockSpec((rows, d), lambda i, j: (0, 0)),
            pl.BlockSpec((1, d, tn), lambda i, j: (i, 0, j)),
            pl.BlockSpec((1, 1, tn), lambda i, j: (i, 0, j)),
        ],
        out_specs=pl.BlockSpec((1, rows, tn), lambda i, j: (i, 0, j)),
        compiler_params=_params(("parallel", "parallel")),
        name="ada_mods",
    )(cond, ada_w, ada_b.reshape(depth, 1, n))


def _mod_proj_kernel(x_ref, sh_ref, sc_ref, w_ref, b_ref, o_ref, u_scr):
    @pl.when(pl.program_id(2) == 0)
    def _():
        u_scr[...] = (x_ref[0] * (1.0 + sc_ref[0]) + sh_ref[0]).astype(MXU_DTYPE)

    o_ref[0, 0] = (_dot(u_scr[...], w_ref[...]) + b_ref[...]).astype(o_ref.dtype)


def _mod_proj(x, shift, scale, w, b, slabs, tm, tn):
    bsz, seq, k = x.shape
    n = w.shape[1]
    n_slab = n // slabs
    tm = _tile(seq, tm)
    tn = _tile(n_slab, tn)
    per = n_slab // tn
    return pl.pallas_call(
        _mod_proj_kernel,
        out_shape=jax.ShapeDtypeStruct((slabs, bsz, seq, n_slab), MXU_DTYPE),
        grid=(bsz, seq // tm, n // tn),
        in_specs=[
            pl.BlockSpec((1, tm, k), lambda bi, i, j: (bi, i, 0)),
            pl.BlockSpec((1, 1, k), lambda bi, i, j: (bi, 0, 0)),
            pl.BlockSpec((1, 1, k), lambda bi, i, j: (bi, 0, 0)),
            pl.BlockSpec((k, tn), lambda bi, i, j: (0, j)),
            pl.BlockSpec((1, tn), lambda bi, i, j: (0, j)),
        ],
        out_specs=pl.BlockSpec((1, 1, tm, tn), lambda bi, i, j: (j // per, bi, i, j % per)),
        scratch_shapes=[pltpu.VMEM((tm, k), MXU_DTYPE)],
        compiler_params=_params(("parallel", "parallel", "arbitrary")),
        name="mod_proj",
    )(x, shift, scale, w.astype(MXU_DTYPE), b.reshape(1, n))


def _hyena_in_kernel(x_ref, prev_ref, next_ref, sh_ref, sc_ref, w0_ref, w1_ref, w2_ref, b_ref, cw_ref, cb_ref,
                     x0_ref, v_ref, u_scr):
    i = pl.program_id(1)
    last = pl.num_programs(1) - 1
    tm = x_ref.shape[1]
    ext = tm + 2 * SUBLANES

    @pl.when(pl.program_id(2) == 0)
    def _():
        shift, scale = sh_ref[0], sc_ref[0]
        rows = jnp.concatenate([prev_ref[0], x_ref[0], next_ref[0]], axis=0)
        u_scr[...] = (rows * (1.0 + scale) + shift).astype(MXU_DTYPE)

    u = u_scr[...]
    outs = []
    for s, w_ref in enumerate((w0_ref, w1_ref, w2_ref)):
        raw = _dot(u, w_ref[...])
        b, taps = b_ref[s], cw_ref[s]
        head = jnp.where(i == 0, -b, raw[:SUBLANES])
        tail = jnp.where(i == last, -b, raw[SUBLANES + tm:])
        raw = jnp.concatenate([head, raw[SUBLANES:SUBLANES + tm], tail], axis=0)
        conv = pltpu.roll(raw, 1, 0) * taps[0:1] + raw * taps[1:2] + pltpu.roll(raw, ext - 1, 0) * taps[2:3]
        const = b * (taps[0:1] + taps[1:2] + taps[2:3]) + cb_ref[s]
        outs.append(conv[SUBLANES:SUBLANES + tm] + const)
    x0_ref[0] = outs[0].astype(x0_ref.dtype)
    v_ref[0] = (outs[2] * outs[1]).astype(v_ref.dtype)


def _hyena_in(x, shift, scale, layer, w_in, b_in, conv_w, conv_b):
    bsz, seq, k = x.shape
    d = w_in.shape[2] // 3
    tm = _tile(seq, ROW_TILE)
    tn = _tile(d, COL_TILE)
    per = d // tn
    hb = tm // SUBLANES
    nhb = seq // SUBLANES
    w = w_in
    cw = conv_w.reshape(3, 3, d).transpose(1, 0, 2)
    out = jax.ShapeDtypeStruct((bsz, seq, d), MXU_DTYPE)
    slab = lambda s: pl.BlockSpec((None, k, tn), lambda bi, i, j: (layer, 0, s * per + j))
    return pl.pallas_call(
        _hyena_in_kernel,
        out_shape=(out, out),
        grid=(bsz, seq // tm, per),
        in_specs=[
            pl.BlockSpec((1, tm, k), lambda bi, i, j: (bi, i, 0)),
            pl.BlockSpec((1, SUBLANES, k), lambda bi, i, j: (bi, jnp.maximum(i * hb - 1, 0), 0)),
            pl.BlockSpec((1, SUBLANES, k), lambda bi, i, j: (bi, jnp.minimum((i + 1) * hb, nhb - 1), 0)),
            pl.BlockSpec((1, 1, k), lambda bi, i, j: (bi, 0, 0)),
            pl.BlockSpec((1, 1, k), lambda bi, i, j: (bi, 0, 0)),
            slab(0), slab(1), slab(2),
            pl.BlockSpec((3, 1, tn), lambda bi, i, j: (0, 0, j)),
            pl.BlockSpec((3, 3, tn), lambda bi, i, j: (0, 0, j)),
            pl.BlockSpec((3, 1, tn), lambda bi, i, j: (0, 0, j)),
        ],
        out_specs=(
            pl.BlockSpec((1, tm, tn), lambda bi, i, j: (bi, i, j)),
            pl.BlockSpec((1, tm, tn), lambda bi, i, j: (bi, i, j)),
        ),
        scratch_shapes=[pltpu.VMEM((tm + 2 * SUBLANES, k), MXU_DTYPE)],
        compiler_params=_params(("parallel", "parallel", "arbitrary")),
        name="hyena_in_conv_gate",
    )(x, x, x, shift, scale, w, w, w, b_in.reshape(3, 1, d), cw, conv_b.reshape(3, 1, d))


def _filter_kernel(z_ref, dist_ref, w_in_ref, w_hid_ref, b_ref, fr_ref, w_out_ref, delta_ref, f_ref, norm_ref):
    hp = lax.Precision.HIGHEST
    b = b_ref[...]
    fr = fr_ref[...]
    g = jnp.sin(fr[0:1] * (jnp.dot(z_ref[...], w_in_ref[...], precision=hp, preferred_element_type=F32) + b[0:1]))
    g = jnp.sin(fr[1:2] * (jnp.dot(g, w_hid_ref[0], precision=hp, preferred_element_type=F32) + b[1:2]))
    g = jnp.sin(fr[2:3] * (jnp.dot(g, w_hid_ref[1], precision=hp, preferred_element_type=F32) + b[2:3]))
    filt = _dot(g.astype(MXU_DTYPE), w_out_ref[...])
    filt = filt * jnp.exp(-dist_ref[...] * delta_ref[...])
    f_ref[...] = filt

    @pl.when(pl.program_id(0) == 0)
    def _():
        norm_ref[...] = jnp.zeros_like(norm_ref)

    norm_ref[...] += jnp.sum(jnp.abs(filt), axis=0, keepdims=True)


def _first_stage_row_order(seq):
    a_hi, c, a_lo = jnp.meshgrid(jnp.arange(seq // (DFT_N2 * SUBLANES)), jnp.arange(DFT_N2), jnp.arange(SUBLANES),
                                 indexing="ij")
    return (DFT_N2 * (SUBLANES * a_hi + a_lo) + c).reshape(seq)


def _implicit_filter(seq, f_w_in, f_w_hid, f_b, f_freq, f_w_out, positions=None):
    emb, width = f_w_in.shape
    bands_n = (emb - 1) // 2
    d = f_w_out.shape[1]
    emb_pad = -(-emb // LANES) * LANES
    filt_w = -(-width // LANES) * LANES
    pad_to = lambda a, shape: jnp.pad(a.astype(F32), [(0, s - n) for s, n in zip(shape, a.shape)])
    pos = (jnp.arange(seq) if positions is None else positions).astype(F32)
    t = pos / (seq - 1)
    bands = jnp.linspace(1e-4, bands_n - 1, bands_n, dtype=F32)
    ang = (2.0 * math.pi / seq) * pos[:, None] * bands[None, :]
    z = pad_to(jnp.concatenate([t[:, None], jnp.cos(ang), -jnp.sin(ang)], axis=-1), (seq, emb_pad))
    w_in = pad_to(f_w_in, (emb_pad, filt_w))
    f_w_hid = pad_to(f_w_hid, (2, filt_w, filt_w))
    f_b = pad_to(f_b, (3, filt_w))
    f_freq = pad_to(f_freq, (3, filt_w))
    f_w_out = pad_to(f_w_out, (filt_w, d))
    dist = (jnp.abs(pos - seq // 2) / (seq // 2))[:, None]
    deltas = jnp.abs(jnp.linspace(HY_MIN_DECAY, HY_MAX_DECAY, d, dtype=F32))[None, :]
    tl = _tile(seq, ROW_TILE)
    full = lambda shape: pl.BlockSpec(shape, lambda i: (0,) * len(shape))
    return pl.pallas_call(
        _filter_kernel,
        out_shape=(jax.ShapeDtypeStruct((seq, d), F32), jax.ShapeDtypeStruct((1, d), F32)),
        grid=(seq // tl,),
        in_specs=[
            pl.BlockSpec((tl, emb_pad), lambda i: (i, 0)),
            pl.BlockSpec((tl, 1), lambda i: (i, 0)),
            full((emb_pad, filt_w)),
            full((2, filt_w, filt_w)),
            full((3, filt_w)),
            full((3, filt_w)),
            full((filt_w, d)),
            full((1, d)),
        ],
        out_specs=(pl.BlockSpec((tl, d), lambda i: (i, 0)), full((1, d))),
        compiler_params=_params(("arbitrary",)),
        name="hyena_filter",
    )(z, dist, w_in, f_w_hid, f_b, f_freq, f_w_out.astype(MXU_DTYPE), deltas)


def _real_form(mr, mi):
    top = jnp.concatenate([mr, -mi], axis=-1)
    bot = jnp.concatenate([mi, mr], axis=-1)
    return jnp.concatenate([top, bot], axis=-2)


def _unit_roots(idx, n):
    ang = (2.0 * math.pi / n) * (idx % n).astype(F32)
    return jnp.cos(ang), -jnp.sin(ang)


def _dft_tables(seq):
    n = 2 * seq
    n2 = DFT_N2
    n1 = n // n2
    i32 = jnp.int32
    p = jnp.arange(n1, dtype=i32)
    c = jnp.arange(n2, dtype=i32)
    tr, ti = _unit_roots(c[:, None] * p[None, :], n)
    a_in = jnp.arange(n1 // 2, dtype=i32)
    fr, fi = _unit_roots(p[:, None] * a_in[None, :], n1)
    mr = tr[:, :, None] * fr[None] - ti[:, :, None] * fi[None]
    mi = tr[:, :, None] * fi[None] + ti[:, :, None] * fr[None]
    first = _real_form(mr, mi)
    a_out = jnp.arange(n1 // 4, n1 // 4 + n1 // 2, dtype=i32)
    gr, gi = _unit_roots(p[:, None] * a_out[None, :], n1)
    cr = (tr[:, :, None] * gr[None] - ti[:, :, None] * gi[None]) / n
    ci = -(tr[:, :, None] * gi[None] + ti[:, :, None] * gr[None]) / n
    last = _real_form(jnp.swapaxes(cr, 1, 2), jnp.swapaxes(ci, 1, 2))
    q = jnp.arange(n2, dtype=i32)
    hr, hi_ = _unit_roots(q[:, None] * c[None, :], n2)
    mid_f = _real_form(hr, hi_)
    mid_i = _real_form(hr, -hi_)
    return tuple(t.astype(MXU_DTYPE) for t in (first, mid_f, mid_i, last))


def _dft_first_kernel(x_ref, m_ref, o_ref):
    x = x_ref[...]
    x = x.reshape(x.shape[0] * x.shape[1] * SUBLANES, x.shape[4])
    o_ref[...] = _dot(m_ref[0], x.astype(MXU_DTYPE)).astype(o_ref.dtype)


def _dft_first(x5, table):
    r, a_hi, n2, _, d = x5.shape
    rows = r * a_hi * SUBLANES
    n_out = table.shape[1]
    if rows % LANES:
        table = table[:, :, :rows]
    bd = _tile(d, DFT_COL_TILE)
    per = d // bd
    return pl.pallas_call(
        _dft_first_kernel,
        out_shape=jax.ShapeDtypeStruct((n_out, n2 * d), MXU_DTYPE),
        grid=(n2 * per,),
        in_specs=[
            pl.BlockSpec((r, a_hi, 1, SUBLANES, bd), lambda j: (0, 0, j // per, 0, j % per)),
            pl.BlockSpec((1, n_out, rows), lambda j: (j // per, 0, 0)),
        ],
        out_specs=pl.BlockSpec((n_out, bd), lambda j: (0, j)),
        compiler_params=_params(("parallel",)),
        name="hyena_dft_first",
    )(x5, table)


def _dft_mid_filter_kernel(a_ref, f_ref, norm_ref, o_ref):
    n2 = a_ref.shape[2]
    for p in range(a_ref.shape[1]):
        a = jnp.concatenate([a_ref[0, p], a_ref[1, p]], axis=0)
        z = _dot(f_ref[...], a) / norm_ref[...]
        o_ref[0, p] = z[:n2].astype(o_ref.dtype)
        o_ref[1, p] = z[n2:].astype(o_ref.dtype)


def _dft_mid_conv_kernel(a_ref, h_ref, f_ref, g_ref, o_ref):
    n2 = a_ref.shape[2]
    for p in range(a_ref.shape[1]):
        a = jnp.concatenate([a_ref[0, p], a_ref[1, p]], axis=0)
        z = _dot(f_ref[...], a)
        zr, zi = z[:n2], z[n2:]
        hr, hi_ = h_ref[0, p].astype(F32), h_ref[1, p].astype(F32)
        w = jnp.concatenate([zr * hr - zi * hi_, zr * hi_ + zi * hr], axis=0)
        y = _dot(g_ref[...], w.astype(MXU_DTYPE))
        o_ref[0, p] = y[:n2].astype(o_ref.dtype)
        o_ref[1, p] = y[n2:].astype(o_ref.dtype)


def _dft_mid(a, tables, d, *, spectrum=None, norm=None):
    n2 = DFT_N2
    n1 = a.shape[0] // 2
    a4 = a.reshape(2, n1, n2, d)
    bd = _tile(d, DFT_COL_TILE)
    pb = _tile(n1, DFT_MID_ROWS)
    blk = pl.BlockSpec((2, pb, n2, bd), lambda p, j: (0, p, 0, j))
    mat = pl.BlockSpec((2 * n2, 2 * n2), lambda p, j: (0, 0))
    fwd, inv = tables
    if spectrum is None:
        kern, ins, specs = _dft_mid_filter_kernel, (a4, fwd, norm), [blk, mat, pl.BlockSpec((1, bd), lambda p, j: (0, j))]
    else:
        kern, ins, specs = _dft_mid_conv_kernel, (a4, spectrum, fwd, inv), [blk, blk, mat, mat]
    out = pl.pallas_call(
        kern,
        out_shape=jax.ShapeDtypeStruct((2, n1, n2, d), MXU_DTYPE),
        grid=(n1 // pb, d // bd),
        in_specs=specs,
        out_specs=blk,
        compiler_params=_params(("parallel", "parallel")),
        name="hyena_dft_mid",
    )(*ins)
    return out


def _dft_last_kernel(x_ref, m_ref, o_ref):
    y = _dot(m_ref[0], x_ref[...])
    o_ref[...] = y.reshape(o_ref.shape).astype(o_ref.dtype)


def _dft_last(b, table, d):
    rows, cols = b.shape
    n1 = rows // 2
    n2 = cols // d
    a_hi = n1 // 2 // SUBLANES
    bd = _tile(d, DFT_COL_TILE)
    per = d // bd
    return pl.pallas_call(
        _dft_last_kernel,
        out_shape=jax.ShapeDtypeStruct((2, a_hi, n2, SUBLANES, d), MXU_DTYPE),
        grid=(cols // bd,),
        in_specs=[
            pl.BlockSpec((rows, bd), lambda j: (0, j)),
            pl.BlockSpec((1, n1, rows), lambda j: (j // per, 0, 0)),
        ],
        out_specs=pl.BlockSpec((2, a_hi, 1, SUBLANES, bd), lambda j: (0, 0, j // per, 0, j % per)),
        compiler_params=_params(("parallel",)),
        name="hyena_dft_last",
    )(b, table)


def _long_conv_two_stage(v, filt, norm, tables):
    bsz, seq, d = v.shape
    n1 = 2 * seq // DFT_N2
    a_hi = n1 // 2 // SUBLANES
    first, mid_f, mid_i, last = tables
    cols = DFT_N2 * d
    h_first = _dft_first(filt.reshape(1, a_hi, DFT_N2, SUBLANES, d), first)
    spectrum = _dft_mid(h_first, (mid_f, mid_i), d, norm=norm)
    v5 = v.reshape(bsz, a_hi, SUBLANES, DFT_N2, d).transpose(0, 1, 3, 2, 4)
    a = _dft_first(v5, first)
    bmat = _dft_mid(a, (mid_f, mid_i), d, spectrum=spectrum)
    y5 = _dft_last(bmat.reshape(2 * n1, cols), last, d)
    return y5.transpose(0, 1, 3, 2, 4).reshape(bsz, seq, d)


def _dft_small_kernel(v_ref, h_ref, norm_ref, f_ref, g_ref, o_ref):
    seq = v_ref.shape[1]
    n = 2 * seq
    fwd = f_ref[...]
    x = jnp.concatenate([v_ref[0], v_ref[1]], axis=0)
    z = _dot(fwd, x.astype(MXU_DTYPE))
    hx = jnp.concatenate([h_ref[...] / norm_ref[...], jnp.zeros_like(h_ref)], axis=0)
    hs = _dot(fwd, hx.astype(MXU_DTYPE))
    zr, zi, hr, hi_ = z[:n], z[n:], hs[:n], hs[n:]
    w = jnp.concatenate([zr * hr - zi * hi_, zr * hi_ + zi * hr], axis=0)
    y = _dot(g_ref[...], w.astype(MXU_DTYPE))
    o_ref[0] = y[:seq].astype(o_ref.dtype)
    o_ref[1] = y[seq:].astype(o_ref.dtype)


def _long_conv_small(v, filt, norm):
    bsz, seq, d = v.shape
    n = 2 * seq
    k = jnp.arange(n, dtype=jnp.int32)
    t_in = jnp.arange(seq, dtype=jnp.int32)
    fr, fi = _unit_roots(k[:, None] * t_in[None, :], n)
    fwd = _real_form(fr, fi)
    t_out = jnp.arange(seq // 2, seq // 2 + seq, dtype=jnp.int32)
    gr, gi = _unit_roots(t_out[:, None] * k[None, :], n)
    inv = _real_form(gr / n, -gi / n)
    bd = _tile(d, COL_TILE)
    full = lambda shape: pl.BlockSpec(shape, lambda j: (0,) * len(shape))
    return pl.pallas_call(
        _dft_small_kernel,
        out_shape=jax.ShapeDtypeStruct((bsz, seq, d), MXU_DTYPE),
        grid=(d // bd,),
        in_specs=[
            pl.BlockSpec((bsz, seq, bd), lambda j: (0, 0, j)),
            pl.BlockSpec((seq, bd), lambda j: (0, j)),
            pl.BlockSpec((1, bd), lambda j: (0, j)),
            full(fwd.shape), full(inv.shape),
        ],
        out_specs=pl.BlockSpec((bsz, seq, bd), lambda j: (0, 0, j)),
        compiler_params=_params(("parallel",)),
        name="hyena_dft_small",
    )(v, filt, norm, fwd.astype(MXU_DTYPE), inv.astype(MXU_DTYPE))


def _hyena_out_kernel(alpha, x0_ref, cv_ref, v_ref, bias_ref, w_ref, b_ref, h_ref, g_ref, lg_ref, lb_ref, o_ref):
    y = x0_ref[0].astype(F32) * (cv_ref[0].astype(F32) + v_ref[0].astype(F32) * bias_ref[...])
    y = _dot(y.astype(MXU_DTYPE), w_ref[...]) + b_ref[...]
    o_ref[0] = _layer_norm_rows(alpha * h_ref[0] + g_ref[0] * y, lg_ref[...], lb_ref[...])


def _attn_out_kernel(alpha, o_in_ref, w_ref, h_ref, g_ref, lg_ref, lb_ref, o_ref):
    y = _dot(o_in_ref[0], w_ref[...])
    o_ref[0] = _layer_norm_rows(alpha * h_ref[0] + g_ref[0] * y, lg_ref[...], lb_ref[...])


def _row_spec(tm, d):
    return pl.BlockSpec((1, tm, d), lambda bi, i: (bi, i, 0))


def _vec_spec(d):
    return pl.BlockSpec((1, d), lambda bi, i: (0, 0))


def _bvec_spec(d):
    return pl.BlockSpec((1, 1, d), lambda bi, i: (bi, 0, 0))


def _const_spec(shape):
    return pl.BlockSpec(shape, lambda bi, i: (0,) * len(shape), pipeline_mode=pl.Buffered(1))


def _hyena_out(alpha, x0, cv, v, bias, layer, w, b, h, gate, ln_g, ln_b):
    bsz, seq, d = h.shape
    tm = _tile(seq, ROW_TILE)
    return pl.pallas_call(
        functools.partial(_hyena_out_kernel, alpha),
        out_shape=jax.ShapeDtypeStruct(h.shape, F32),
        grid=(bsz, seq // tm),
        in_specs=[_row_spec(tm, d), _row_spec(tm, d), _row_spec(tm, d), _vec_spec(d),
                  pl.BlockSpec((None, d, d), lambda bi, i: (layer, 0, 0), pipeline_mode=pl.Buffered(1)),
                  _vec_spec(d), _row_spec(tm, d), _bvec_spec(d), _vec_spec(d), _vec_spec(d)],
        out_specs=_row_spec(tm, d),
        compiler_params=_params(("parallel", "parallel")),
        name="hyena_out_norm",
    )(x0, cv, v, bias.reshape(1, d), w, b.reshape(1, d), h, gate, ln_g.reshape(1, d), ln_b.reshape(1, d))


def _attn_out(alpha, o, w, h, gate, ln_g, ln_b):
    bsz, seq, d = h.shape
    k = o.shape[2]
    tm = _tile(seq, ROW_TILE)
    return pl.pallas_call(
        functools.partial(_attn_out_kernel, alpha),
        out_shape=jax.ShapeDtypeStruct(h.shape, F32),
        grid=(bsz, seq // tm),
        in_specs=[_row_spec(tm, k), _const_spec((k, d)), _row_spec(tm, d), _bvec_spec(d), _vec_spec(d), _vec_spec(d)],
        out_specs=_row_spec(tm, d),
        compiler_params=_params(("parallel", "parallel")),
        name="attn_out_norm",
    )(o, w.astype(MXU_DTYPE), h, gate, ln_g.reshape(1, d), ln_b.reshape(1, d))


def _ffn_kernel(alpha, h_ref, sh_ref, sc_ref, g_ref, wg_ref, wu_ref, wd_ref, lg_ref, lb_ref, o_ref, u_scr, acc_scr):
    f = pl.program_id(2)

    @pl.when(f == 0)
    def _():
        u_scr[...] = (h_ref[0] * (1.0 + sc_ref[0]) + sh_ref[0]).astype(MXU_DTYPE)
        acc_scr[...] = jnp.zeros_like(acc_scr)

    u = u_scr[...]
    gate = _dot(u, wg_ref[...])
    up = _dot(u, wu_ref[...])
    act = (_silu(gate) * up).astype(MXU_DTYPE)
    acc_scr[...] += _dot(act, wd_ref[...])

    @pl.when(f == pl.num_programs(2) - 1)
    def _():
        o_ref[0] = _layer_norm_rows(alpha * h_ref[0] + g_ref[0] * acc_scr[...], lg_ref[...], lb_ref[...])


def _ffn(alpha, h, shift, scale, gate, layer, w_gate, w_up, w_down, ln_g, ln_b):
    bsz, seq, d = h.shape
    ff = w_gate.shape[2]
    tm = _tile(seq, ROW_TILE)
    tf = _tile(ff, COL_TILE)
    return pl.pallas_call(
        functools.partial(_ffn_kernel, alpha),
        out_shape=jax.ShapeDtypeStruct(h.shape, F32),
        grid=(bsz, seq // tm, ff // tf),
        in_specs=[
            pl.BlockSpec((1, tm, d), lambda bi, i, f: (bi, i, 0)),
            pl.BlockSpec((1, 1, d), lambda bi, i, f: (bi, 0, 0)),
            pl.BlockSpec((1, 1, d), lambda bi, i, f: (bi, 0, 0)),
            pl.BlockSpec((1, 1, d), lambda bi, i, f: (bi, 0, 0)),
            pl.BlockSpec((None, d, tf), lambda bi, i, f: (layer, 0, f)),
            pl.BlockSpec((None, d, tf), lambda bi, i, f: (layer, 0, f)),
            pl.BlockSpec((None, tf, d), lambda bi, i, f: (layer, f, 0)),
            pl.BlockSpec((1, d), lambda bi, i, f: (0, 0)),
            pl.BlockSpec((1, d), lambda bi, i, f: (0, 0)),
        ],
        out_specs=pl.BlockSpec((1, tm, d), lambda bi, i, f: (bi, i, 0)),
        scratch_shapes=[pltpu.VMEM((tm, d), MXU_DTYPE), pltpu.VMEM((tm, d), F32)],
        compiler_params=_params(("parallel", "parallel", "arbitrary")),
        name="ffn_swiglu_norm",
    )(h, shift, scale, gate, w_gate, w_up, w_down,
      ln_g.reshape(1, d), ln_b.reshape(1, d))


def _rotate_half_cols(w):
    ws = w.reshape(w.shape[:-1] + (2, 2, ROPE_PAIRS))
    return jnp.stack([-ws[..., 1, :], ws[..., 0, :]], axis=-2).reshape(w.shape)


def _rope_table(seq):
    rows = seq // GRID_W
    row = jnp.repeat(jnp.arange(rows, dtype=F32), GRID_W)
    col = jnp.tile(jnp.arange(GRID_W, dtype=F32), rows)
    inv = ROPE_THETA ** (-jnp.arange(ROPE_PAIRS, dtype=F32) / ROPE_PAIRS)
    ang = jnp.stack([row[:, None] * inv, col[:, None] * inv], axis=1)
    ang = jnp.broadcast_to(ang[:, :, None, :], (seq, 2, 2, ROPE_PAIRS)).reshape(seq, QK_ROPE)
    return jnp.concatenate([jnp.cos(ang), jnp.sin(ang)], axis=-1)


def _rms_rows(x, g):
    return x * lax.rsqrt(jnp.mean(x * x, axis=-1, keepdims=True) + RMS_EPS) * g


def _rope_pair(x, cs):
    t = x * cs
    return t + pltpu.roll(t, QK_ROPE, 1)


def _mla_q_kernel(cq_ref, g_ref, w_ref, cs_ref, q_ref):
    xn = _rms_rows(cq_ref[0].astype(F32), g_ref[...]).astype(MXU_DTYPE)
    cs = cs_ref[...]

    def head(h, carry):
        a = _dot(xn, w_ref[h])
        r = _rope_pair(a[:, QK_NOPE:], cs)
        q_ref[0, h] = (jnp.concatenate([a[:, :QK_NOPE], r], axis=1) * QUERY_SCALE).astype(q_ref.dtype)
        return carry

    lax.fori_loop(0, w_ref.shape[0], head, 0, unroll=2)


def _mla_queries(t, q_norm, wq_b, cs, heads):
    bsz, seq, _ = t.shape
    rank = q_norm.shape[0]
    tm = _tile(seq, ROW_TILE)
    w = wq_b.reshape(rank, heads, QK_NOPE + QK_ROPE)
    w = jnp.concatenate([w, _rotate_half_cols(w[..., QK_NOPE:])], axis=-1).transpose(1, 0, 2).astype(MXU_DTYPE)
    return pl.pallas_call(
        _mla_q_kernel,
        out_shape=jax.ShapeDtypeStruct((bsz, heads, seq, 2 * LANES), MXU_DTYPE),
        grid=(bsz, seq // tm),
        in_specs=[
            pl.BlockSpec((1, tm, rank), lambda bi, i: (bi, i, 0)),
            pl.BlockSpec((1, rank), lambda bi, i: (0, 0)),
            pl.BlockSpec((heads, rank, 2 * LANES), lambda bi, i: (0, 0, 0)),
            pl.BlockSpec((tm, LANES), lambda bi, i: (i, 0)),
        ],
        out_specs=pl.BlockSpec((1, heads, tm, 2 * LANES), lambda bi, i: (bi, 0, i, 0)),
        compiler_params=_params(("parallel", "parallel")),
        name="mla_queries",
    )(t, q_norm.reshape(1, rank), w, cs)


def _mla_kv_kernel(use_rope, ckv_ref, kr_ref, g_ref, w_ref, cs_ref, k_ref, vt_ref):
    xn = _rms_rows(ckv_ref[0].astype(F32), g_ref[...]).astype(MXU_DTYPE)
    kr = kr_ref[0].astype(F32)
    if use_rope:
        kr = _rope_pair(kr, cs_ref[...])
    lane = lax.broadcasted_iota(jnp.int32, kr.shape, 1)
    kr = jnp.where(lane < QK_ROPE, kr, 0.0).astype(k_ref.dtype)
    ones = jnp.ones((V_ROWS - V_HEAD, xn.shape[0]), vt_ref.dtype)

    def head(h, carry):
        kv = _dot(xn, w_ref[h])
        k_ref[0, h] = jnp.concatenate([kv[:, :QK_NOPE].astype(k_ref.dtype), kr], axis=1)
        vt_ref[0, h] = jnp.concatenate([kv[:, QK_NOPE:].T.astype(vt_ref.dtype), ones], axis=0)
        return carry

    lax.fori_loop(0, w_ref.shape[0], head, 0, unroll=2)


def _mla_keys_values(t, kv_norm, wkv_b, cs, heads, use_rope):
    bsz, seq, _ = t.shape
    rank = kv_norm.shape[0]
    tm = _tile(seq, ROW_TILE)
    w = wkv_b.reshape(rank, heads, QK_NOPE + V_HEAD).transpose(1, 0, 2).astype(MXU_DTYPE)
    return pl.pallas_call(
        functools.partial(_mla_kv_kernel, use_rope),
        out_shape=(jax.ShapeDtypeStruct((bsz, heads, seq, 2 * LANES), MXU_DTYPE),
                   jax.ShapeDtypeStruct((bsz, heads, V_ROWS, seq), MXU_DTYPE)),
        grid=(bsz, seq // tm),
        in_specs=[
            pl.BlockSpec((1, tm, rank), lambda bi, i: (bi, i, 1)),
            pl.BlockSpec((1, tm, LANES), lambda bi, i: (bi, i, 2 * rank // LANES)),
            pl.BlockSpec((1, rank), lambda bi, i: (0, 0)),
            pl.BlockSpec((heads, rank, QK_NOPE + V_HEAD), lambda bi, i: (0, 0, 0)),
            pl.BlockSpec((tm, LANES), lambda bi, i: (i, 0)),
        ],
        out_specs=(pl.BlockSpec((1, heads, tm, 2 * LANES), lambda bi, i: (bi, 0, i, 0)),
                   pl.BlockSpec((1, heads, V_ROWS, tm), lambda bi, i: (bi, 0, 0, i))),
        compiler_params=_params(("parallel", "parallel")),
        name="mla_keys_values",
    )(t, t, kv_norm.reshape(1, rank), w, cs)


def _attn_kernel(q_ref, k_ref, vt_ref, kc_ref, vtc_ref, o_ref, qt_scr, s_scr, p_scr, smax_scr, m_scr, alpha_scr, acc_scr):
    kv = pl.program_id(3)
    n_main = pl.num_programs(3) - 1

    n_chunks, _, cw = qt_scr.shape

    @pl.when(kv == 0)
    def _():
        for c in range(n_chunks):
            qt_scr[c] = q_ref[0, 0, c * cw:(c + 1) * cw, :].T
        m_scr[...] = jnp.full_like(m_scr, -jnp.inf)
        acc_scr[...] = jnp.zeros_like(acc_scr)

    def step(k, vt):
        nk = k.shape[0]

        def scores(c, slot):
            s = _dot(k, qt_scr[c])
            s_scr[slot, :nk, :] = s
            smax_scr[slot] = jnp.max(s, axis=0, keepdims=True)

        def exponent(c, slot):
            m_prev = m_scr[c]
            m_new = jnp.maximum(m_prev, smax_scr[slot])
            alpha_scr[c] = jnp.exp2(m_prev - m_new)
            p_scr[slot, :nk, :] = jnp.exp2((s_scr[slot, :nk, :] - m_new).astype(MXU_DTYPE))
            m_scr[c] = m_new

        def values(c, slot):
            acc_scr[c] = alpha_scr[c] * acc_scr[c] + _dot(vt, p_scr[slot, :nk, :])

        for i in range(n_chunks + 2):
            if i < n_chunks:
                scores(i, i % 2)
            if 1 <= i <= n_chunks:
                exponent(i - 1, (i - 1) % 2)
            if i >= 2:
                values(i - 2, i % 2)

    @pl.when(kv < n_main)
    def _():
        step(k_ref[0, 0], vt_ref[0, 0])

    @pl.when(kv == n_main)
    def _():
        step(kc_ref[0, 0], vtc_ref[0, 0])
        for c in range(n_chunks):
            acc = acc_scr[c]
            o_ref[0, c * cw:(c + 1) * cw, :] = (acc[:V_HEAD] / acc[V_HEAD:V_HEAD + 1]).T.astype(o_ref.dtype)


def _mla_attend(q, k, vt, kc, vtc):
    bsz, heads, seq, dq = q.shape
    lc = kc.shape[2]
    tq = _tile(seq, ATTN_Q_TILE)
    tk = _tile(seq, ATTN_KV_TILE)
    cw = min(tq, ATTN_CHUNK)
    assert lc <= tk
    n_main = seq // tk
    return pl.pallas_call(
        _attn_kernel,
        out_shape=jax.ShapeDtypeStruct((bsz, seq, heads * V_HEAD), MXU_DTYPE),
        grid=(bsz, heads, seq // tq, n_main + 1),
        in_specs=[
            pl.BlockSpec((1, 1, tq, dq), lambda bi, h, i, j: (bi, h, i, 0)),
            pl.BlockSpec((1, 1, tk, dq), lambda bi, h, i, j: (bi, h, jnp.minimum(j, n_main - 1), 0)),
            pl.BlockSpec((1, 1, V_ROWS, tk), lambda bi, h, i, j: (bi, h, 0, jnp.minimum(j, n_main - 1))),
            pl.BlockSpec((1, 1, lc, dq), lambda bi, h, i, j: (bi, h, 0, 0)),
            pl.BlockSpec((1, 1, V_ROWS, lc), lambda bi, h, i, j: (bi, h, 0, 0)),
        ],
        out_specs=pl.BlockSpec((1, tq, V_HEAD), lambda bi, h, i, j: (bi, i, h)),
        scratch_shapes=[pltpu.VMEM((tq // cw, dq, cw), MXU_DTYPE), pltpu.VMEM((2, tk, cw), F32),
                        pltpu.VMEM((2, tk, cw), MXU_DTYPE), pltpu.VMEM((2, 1, cw), F32),
                        pltpu.VMEM((tq // cw, 1, cw), F32), pltpu.VMEM((tq // cw, 1, cw), F32), pltpu.VMEM((tq // cw, V_ROWS, cw), F32)],
        compiler_params=_params(("parallel", "parallel", "parallel", "arbitrary")),
        name="mla_attention",
    )(q, k, vt, kc, vtc)


def _pool_kernel(alpha, seq, h_ref, prev_ref, next_ref, sh_ref, sc_ref, g_ref, w_ref, ps_ref, lg_ref, lb_ref, o_ref):
    i = pl.program_id(1)
    last = pl.num_programs(1) - 1
    tm = h_ref.shape[1]
    d = h_ref.shape[2]
    groups = len(POOL_WINDOWS)
    ch = d // groups
    ext = tm + 2 * POOL_HALO
    shift, scale = sh_ref[0], sc_ref[0]
    h = h_ref[0]
    u = h * (1.0 + scale) + shift
    u_prev = jnp.where(i == 0, 0.0, prev_ref[0] * (1.0 + scale) + shift)
    u_next = jnp.where(i == last, 0.0, next_ref[0] * (1.0 + scale) + shift)
    e = jnp.concatenate([u_prev, u, u_next], axis=0)
    t = i * tm + lax.broadcasted_iota(jnp.int32, (tm, 1), 0)
    ys = []
    for g, win in enumerate(POOL_WINDOWS):
        a = e[:, g * ch:(g + 1) * ch]
        span = 1
        while span < win:
            a = a + pltpu.roll(a, ext - span, 0)
            span *= 2
        half = win // 2
        a = pltpu.roll(a, half, 0)
        s = a[POOL_HALO:POOL_HALO + tm]
        cnt = (jnp.minimum(t + half, seq) - jnp.maximum(t - half, 0)).astype(F32)
        dg = s / cnt - u[:, g * ch:(g + 1) * ch]
        ys.append(_dot(dg.astype(MXU_DTYPE), w_ref[g]))
    y = jnp.concatenate(ys, axis=1) * ps_ref[...]
    o_ref[0] = _layer_norm_rows(alpha * h + g_ref[0] * y, lg_ref[...], lb_ref[...])


def _pool_mixer(alpha, h, shift, scale, gate, w_grp, pool_scale, ln_g, ln_b):
    bsz, seq, d = h.shape
    groups, ch, _ = w_grp.shape
    tm = _tile(seq, ROW_TILE)
    hb = tm // POOL_HALO
    nhb = seq // POOL_HALO
    return pl.pallas_call(
        functools.partial(_pool_kernel, alpha, seq),
        out_shape=jax.ShapeDtypeStruct(h.shape, F32),
        grid=(bsz, seq // tm),
        in_specs=[
            _row_spec(tm, d),
            pl.BlockSpec((1, POOL_HALO, d), lambda bi, i: (bi, jnp.maximum(i * hb - 1, 0), 0)),
            pl.BlockSpec((1, POOL_HALO, d), lambda bi, i: (bi, jnp.minimum((i + 1) * hb, nhb - 1), 0)),
            _bvec_spec(d), _bvec_spec(d), _bvec_spec(d),
            pl.BlockSpec((groups, ch, ch), lambda bi, i: (0, 0, 0)),
            _vec_spec(d), _vec_spec(d), _vec_spec(d),
        ],
        out_specs=_row_spec(tm, d),
        compiler_params=_params(("parallel", "parallel")),
        name="pool_mixer_norm",
    )(h, h, h, shift, scale, gate, w_grp.astype(MXU_DTYPE), pool_scale.reshape(1, d), ln_g.reshape(1, d), ln_b.reshape(1, d))


def _hyena_mixer_norm(alpha, h, mod, hy, filt, norm, tables, ln_g, ln_b):
    shift, scale, gate = mod
    layer, w_in, b_in, conv_w, conv_b, bias, w_out, b_out = hy
    x0, v = _hyena_in(h, shift, scale, layer, w_in, b_in, conv_w, conv_b)
    if h.shape[1] <= DFT_SMALL_MAX_L:
        cv = _long_conv_small(v, filt, norm)
    else:
        cv = _long_conv_two_stage(v, filt, norm, tables)
    return _hyena_out(alpha, x0, cv, v, bias, layer, w_out, b_out, h, gate, ln_g, ln_b)


def kernel(x, c, ctx, c_ctx, ada_w, ada_b, ln_g, ln_b, ffn_w_gate, ffn_w_up, ffn_w_down, hy_w_in, hy_b_in, hy_conv_w, hy_conv_b, hy_f_w_in, hy_f_w_hid, hy_f_b, hy_f_freq, hy_f_w_out, hy_bias, hy_w_out, hy_b_out, mla_w_in, mla_q_norm, mla_kv_norm, mla_wq_b, mla_wkv_b, mla_w_out, pool_w, pool_scale):
    bsz, seq, d = x.shape
    depth = ada_w.shape[0]
    assert bsz == 2, "the long convolution packs exactly two batch rows into one complex signal"
    assert ctx.shape[0] == bsz and seq % GRID_W == 0
    alpha = (2.0 * depth) ** 0.25
    heads = d // V_HEAD
    mla_layers = [i for i in range(depth) if i % N_MIXERS == 1]
    last_ctx_read = mla_layers[-1] if mla_layers else -1

    cond = jnp.concatenate([c, jnp.broadcast_to(c_ctx[None], (SUBLANES - bsz, d))], axis=0)
    mods = _ada_mods(cond, ada_w, ada_b)

    def mod_vecs(i, ctx_stream):
        m = jnp.broadcast_to(mods[i, bsz][None], (bsz, 6 * d)) if ctx_stream else mods[i, :bsz]
        return [m[:, None, k * d:(k + 1) * d] for k in range(6)]

    cs = _rope_table(seq)
    two_stage = seq > DFT_SMALL_MAX_L
    tables = _dft_tables(seq) if two_stage else None
    row_order = _first_stage_row_order(seq) if two_stage else None

    ffn_w = tuple(w.astype(MXU_DTYPE) for w in (ffn_w_gate, ffn_w_up, ffn_w_down))
    hy_w = (hy_w_in.astype(MXU_DTYPE), hy_w_out.astype(MXU_DTYPE))
    h, hc = x, ctx
    for i in range(depth):
        kind, j = i % N_MIXERS, i // N_MIXERS
        ctx_update = i < last_ctx_read
        sh1, sc1, g1, sh2, sc2, g2 = mod_vecs(i, False)
        if kind == 1 or ctx_update:
            csh1, csc1, cg1, csh2, csc2, cg2 = mod_vecs(i, True)
        lg, lb = ln_g[i], ln_b[i]
        if kind == 0:
            hy = (j, hy_w[0], hy_b_in[j], hy_conv_w[j], hy_conv_b[j], hy_bias[j], hy_w[1], hy_b_out[j])
            fp = (hy_f_w_in[j], hy_f_w_hid[j], hy_f_b[j], hy_f_freq[j], hy_f_w_out[j])
            filt, norm = _implicit_filter(seq, *fp, positions=row_order)
            h_mid = _hyena_mixer_norm(alpha, h, (sh1, sc1, g1), hy, filt, norm, tables, lg[0], lb[0])
            if ctx_update:
                filt_c, norm_c = _implicit_filter(hc.shape[1], *fp)
                hc_mid = _hyena_mixer_norm(alpha, hc, (csh1, csc1, cg1), hy, filt_c, norm_c, None, lg[0], lb[0])
        elif kind == 1:
            assert not ctx_update, "context queries are only needed when a later layer reads the context"
            rank = mla_q_norm.shape[1]
            w_in = mla_w_in[j]
            w_in = jnp.concatenate([w_in, _rotate_half_cols(w_in[:, 2 * rank:])], axis=1)
            zeros = jnp.zeros((w_in.shape[1],), F32)
            t = _mod_proj(h, sh1, sc1, w_in, zeros, slabs=1, tm=ROW_TILE, tn=w_in.shape[1])[0]
            tc = _mod_proj(hc, csh1, csc1, w_in, zeros, slabs=1, tm=ROW_TILE, tn=w_in.shape[1])[0]
            q = _mla_queries(t, mla_q_norm[j], mla_wq_b[j], cs, heads)
            k, v = _mla_keys_values(t, mla_kv_norm[j], mla_wkv_b[j], cs, heads, True)
            kc, vc = _mla_keys_values(tc, mla_kv_norm[j], mla_wkv_b[j], cs, heads, False)
            o = _mla_attend(q, k, v, kc, vc)
            h_mid = _attn_out(alpha, o, mla_w_out[j], h, g1, lg[0], lb[0])
        else:
            h_mid = _pool_mixer(alpha, h, sh1, sc1, g1, pool_w[j], pool_scale[j], lg[0], lb[0])
            if ctx_update:
                hc_mid = _pool_mixer(alpha, hc, csh1, csc1, cg1, pool_w[j], pool_scale[j], lg[0], lb[0])
        h = _ffn(alpha, h_mid, sh2, sc2, g2, i, *ffn_w, lg[1], lb[1])
        if ctx_update:
            hc = _ffn(alpha, hc_mid, csh2, csc2, cg2, i, *ffn_w, lg[1], lb[1])
    return h
```

```python
import functools
import math

import jax
import jax.numpy as jnp
from jax import lax
from jax.experimental import pallas as pl
from jax.experimental.pallas import tpu as pltpu

F32 = jnp.float32
MXU_DTYPE = jnp.bfloat16

V7X_VMEM_BYTES = 64 * 1024 * 1024
VMEM_LIMIT = V7X_VMEM_BYTES - 8 * 1024 * 1024
LANES = 128
SUBLANES = 8

N_MIXERS = 3
GRID_W = 64
LN_EPS = 1e-6
RMS_EPS = 1e-6
HY_TARGET = 1e-2
HY_FAST = 0.3
HY_SLOW = 1.5
HY_MIN_DECAY = math.log(HY_TARGET) / HY_SLOW
HY_MAX_DECAY = math.log(HY_TARGET) / HY_FAST
QK_NOPE = 128
QK_ROPE = 64
V_HEAD = 128
ROPE_PAIRS = QK_ROPE // 4
ROPE_THETA = 10000.0
ATTN_SCALE = (QK_NOPE + QK_ROPE) ** -0.5
QUERY_SCALE = ATTN_SCALE * math.log2(math.e)
ATTN_CHUNK = 512
V_ROWS = V_HEAD + 2 * SUBLANES
POOL_WINDOWS = (2, 4, 8, 16)
POOL_HALO = 8

DFT_N2 = 128
DFT_SMALL_MAX_L = 512
DFT_MID_ROWS = 4

ROW_TILE = 512
COL_TILE = 512
ADA_COL_TILE = 1024
DFT_COL_TILE = 2048
ATTN_Q_TILE = 2048
ATTN_KV_TILE = 4096


def _params(sem):
    return pltpu.CompilerParams(dimension_semantics=sem, vmem_limit_bytes=VMEM_LIMIT)


def _tile(n, t):
    t = min(n, t)
    assert n % t == 0, (n, t)
    return t


def _dot(a, b):
    return jnp.dot(a, b, preferred_element_type=F32)


def _layer_norm_rows(x, g, b):
    mu = jnp.mean(x, axis=-1, keepdims=True)
    xc = x - mu
    var = jnp.mean(xc * xc, axis=-1, keepdims=True)
    return xc * lax.rsqrt(var + LN_EPS) * g + b


def _silu(x):
    return x * (1.0 / (1.0 + jnp.exp(-x)))


def _ada_kernel(c_ref, w_ref, b_ref, o_ref):
    c = c_ref[...]
    a = _silu(c).astype(MXU_DTYPE)
    o_ref[0] = _dot(a, w_ref[0].astype(MXU_DTYPE)) + b_ref[0]


def _ada_mods(cond, ada_w, ada_b):
    depth, d, n = ada_w.shape
    rows = cond.shape[0]
    tn = _tile(n, ADA_COL_TILE)
    return pl.pallas_call(
        _ada_kernel,
        out_shape=jax.ShapeDtypeStruct((depth, rows, n), F32),
        grid=(depth, n // tn),
        in_specs=[
            pl.BlockSpec((rows, d), lambda i, j: (0, 0)),
            pl.BlockSpec((1, d, tn), lambda i, j: (i, 0, j)),
            pl.BlockSpec((1, 1, tn), lambda i, j: (i, 0, j)),
        ],
        out_specs=pl.BlockSpec((1, rows, tn), lambda i, j: (i, 0, j)),
        compiler_params=_params(("parallel", "parallel")),
        name="ada_mods",
    )(cond, ada_w, ada_b.reshape(depth, 1, n))


def _mod_proj_kernel(x_ref, sh_ref, sc_ref, w_ref, b_ref, o_ref, u_scr):
    @pl.when(pl.program_id(2) == 0)
    def _():
        u_scr[...] = (x_ref[0] * (1.0 + sc_ref[0]) + sh_ref[0]).astype(MXU_DTYPE)

    o_ref[0, 0] = (_dot(u_scr[...], w_ref[...]) + b_ref[...]).astype(o_ref.dtype)


def _mod_proj(x, shift, scale, w, b, slabs, tm, tn):
    bsz, seq, k = x.shape
    n = w.shape[1]
    n_slab = n // slabs
    tm = _tile(seq, tm)
    tn = _tile(n_slab, tn)
    per = n_slab // tn
    return pl.pallas_call(
        _mod_proj_kernel,
        out_shape=jax.ShapeDtypeStruct((slabs, bsz, seq, n_slab), MXU_DTYPE),
        grid=(bsz, seq // tm, n // tn),
        in_specs=[
            pl.BlockSpec((1, tm, k), lambda bi, i, j: (bi, i, 0)),
            pl.BlockSpec((1, 1, k), lambda bi, i, j: (bi, 0, 0)),
            pl.BlockSpec((1, 1, k), lambda bi, i, j: (bi, 0, 0)),
            pl.BlockSpec((k, tn), lambda bi, i, j: (0, j)),
            pl.BlockSpec((1, tn), lambda bi, i, j: (0, j)),
        ],
        out_specs=pl.BlockSpec((1, 1, tm, tn), lambda bi, i, j: (j // per, bi, i, j % per)),
        scratch_shapes=[pltpu.VMEM((tm, k), MXU_DTYPE)],
        compiler_params=_params(("parallel", "parallel", "arbitrary")),
        name="mod_proj",
    )(x, shift, scale, w.astype(MXU_DTYPE), b.reshape(1, n))


def _hyena_in_kernel(x_ref, prev_ref, next_ref, sh_ref, sc_ref, w0_ref, w1_ref, w2_ref, b_ref, cw_ref, cb_ref,
                     x0_ref, v_ref, u_scr):
    i = pl.program_id(1)
    last = pl.num_programs(1) - 1
    tm = x_ref.shape[1]
    ext = tm + 2 * SUBLANES

    @pl.when(pl.program_id(2) == 0)
    def _():
        shift, scale = sh_ref[0], sc_ref[0]
        rows = jnp.concatenate([prev_ref[0], x_ref[0], next_ref[0]], axis=0)
        u_scr[...] = (rows * (1.0 + scale) + shift).astype(MXU_DTYPE)

    u = u_scr[...]
    outs = []
    for s, w_ref in enumerate((w0_ref, w1_ref, w2_ref)):
        raw = _dot(u, w_ref[...])
        b, taps = b_ref[s], cw_ref[s]
        head = jnp.where(i == 0, -b, raw[:SUBLANES])
        tail = jnp.where(i == last, -b, raw[SUBLANES + tm:])
        raw = jnp.concatenate([head, raw[SUBLANES:SUBLANES + tm], tail], axis=0)
        conv = pltpu.roll(raw, 1, 0) * taps[0:1] + raw * taps[1:2] + pltpu.roll(raw, ext - 1, 0) * taps[2:3]
        const = b * (taps[0:1] + taps[1:2] + taps[2:3]) + cb_ref[s]
        outs.append(conv[SUBLANES:SUBLANES + tm] + const)
    x0_ref[0] = outs[0].astype(x0_ref.dtype)
    v_ref[0] = (outs[2] * outs[1]).astype(v_ref.dtype)


def _hyena_in(x, shift, scale, layer, w_in, b_in, conv_w, conv_b):
    bsz, seq, k = x.shape
    d = w_in.shape[2] // 3
    tm = _tile(seq, ROW_TILE)
    tn = _tile(d, COL_TILE)
    per = d // tn
    hb = tm // SUBLANES
    nhb = seq // SUBLANES
    w = w_in
    cw = conv_w.reshape(3, 3, d).transpose(1, 0, 2)
    out = jax.ShapeDtypeStruct((bsz, seq, d), MXU_DTYPE)
    slab = lambda s: pl.BlockSpec((None, k, tn), lambda bi, i, j: (layer, 0, s * per + j))
    return pl.pallas_call(
        _hyena_in_kernel,
        out_shape=(out, out),
        grid=(bsz, seq // tm, per),
        in_specs=[
            pl.BlockSpec((1, tm, k), lambda bi, i, j: (bi, i, 0)),
            pl.BlockSpec((1, SUBLANES, k), lambda bi, i, j: (bi, jnp.maximum(i * hb - 1, 0), 0)),
            pl.BlockSpec((1, SUBLANES, k), lambda bi, i, j: (bi, jnp.minimum((i + 1) * hb, nhb - 1), 0)),
            pl.BlockSpec((1, 1, k), lambda bi, i, j: (bi, 0, 0)),
            pl.BlockSpec((1, 1, k), lambda bi, i, j: (bi, 0, 0)),
            slab(0), slab(1), slab(2),
            pl.BlockSpec((3, 1, tn), lambda bi, i, j: (0, 0, j)),
            pl.BlockSpec((3, 3, tn), lambda bi, i, j: (0, 0, j)),
            pl.BlockSpec((3, 1, tn), lambda bi, i, j: (0, 0, j)),
        ],
        out_specs=(
            pl.BlockSpec((1, tm, tn), lambda bi, i, j: (bi, i, j)),
            pl.BlockSpec((1, tm, tn), lambda bi, i, j: (bi, i, j)),
        ),
        scratch_shapes=[pltpu.VMEM((tm + 2 * SUBLANES, k), MXU_DTYPE)],
        compiler_params=_params(("parallel", "parallel", "arbitrary")),
        name="hyena_in_conv_gate",
    )(x, x, x, shift, scale, w, w, w, b_in.reshape(3, 1, d), cw, conv_b.reshape(3, 1, d))


def _filter_kernel(z_ref, dist_ref, w_in_ref, w_hid_ref, b_ref, fr_ref, w_out_ref, delta_ref, f_ref, norm_ref):
    hp = lax.Precision.HIGHEST
    b = b_ref[...]
    fr = fr_ref[...]
    g = jnp.sin(fr[0:1] * (jnp.dot(z_ref[...], w_in_ref[...], precision=hp, preferred_element_type=F32) + b[0:1]))
    g = jnp.sin(fr[1:2] * (jnp.dot(g, w_hid_ref[0], precision=hp, preferred_element_type=F32) + b[1:2]))
    g = jnp.sin(fr[2:3] * (jnp.dot(g, w_hid_ref[1], precision=hp, preferred_element_type=F32) + b[2:3]))
    filt = _dot(g.astype(MXU_DTYPE), w_out_ref[...])
    filt = filt * jnp.exp(-dist_ref[...] * delta_ref[...])
    f_ref[...] = filt

    @pl.when(pl.program_id(0) == 0)
    def _():
        norm_ref[...] = jnp.zeros_like(norm_ref)

    norm_ref[...] += jnp.sum(jnp.abs(filt), axis=0, keepdims=True)


def _first_stage_row_order(seq):
    a_hi, c, a_lo = jnp.meshgrid(jnp.arange(seq // (DFT_N2 * SUBLANES)), jnp.arange(DFT_N2), jnp.arange(SUBLANES),
                                 indexing="ij")
    return (DFT_N2 * (SUBLANES * a_hi + a_lo) + c).reshape(seq)


def _implicit_filter(seq, f_w_in, f_w_hid, f_b, f_freq, f_w_out, positions=None):
    emb, width = f_w_in.shape
    bands_n = (emb - 1) // 2
    d = f_w_out.shape[1]
    emb_pad = -(-emb // LANES) * LANES
    filt_w = -(-width // LANES) * LANES
    pad_to = lambda a, shape: jnp.pad(a.astype(F32), [(0, s - n) for s, n in zip(shape, a.shape)])
    pos = (jnp.arange(seq) if positions is None else positions).astype(F32)
    t = pos / (seq - 1)
    bands = jnp.linspace(1e-4, bands_n - 1, bands_n, dtype=F32)
    ang = (2.0 * math.pi / seq) * pos[:, None] * bands[None, :]
    z = pad_to(jnp.concatenate([t[:, None], jnp.cos(ang), -jnp.sin(ang)], axis=-1), (seq, emb_pad))
    w_in = pad_to(f_w_in, (emb_pad, filt_w))
    f_w_hid = pad_to(f_w_hid, (2, filt_w, filt_w))
    f_b = pad_to(f_b, (3, filt_w))
    f_freq = pad_to(f_freq, (3, filt_w))
    f_w_out = pad_to(f_w_out, (filt_w, d))
    dist = (jnp.abs(pos - seq // 2) / (seq // 2))[:, None]
    deltas = jnp.abs(jnp.linspace(HY_MIN_DECAY, HY_MAX_DECAY, d, dtype=F32))[None, :]
    tl = _tile(seq, ROW_TILE)
    full = lambda shape: pl.BlockSpec(shape, lambda i: (0,) * len(shape))
    return pl.pallas_call(
        _filter_kernel,
        out_shape=(jax.ShapeDtypeStruct((seq, d), F32), jax.ShapeDtypeStruct((1, d), F32)),
        grid=(seq // tl,),
        in_specs=[
            pl.BlockSpec((tl, emb_pad), lambda i: (i, 0)),
            pl.BlockSpec((tl, 1), lambda i: (i, 0)),
            full((emb_pad, filt_w)),
            full((2, filt_w, filt_w)),
            full((3, filt_w)),
            full((3, filt_w)),
            full((filt_w, d)),
            full((1, d)),
        ],
        out_specs=(pl.BlockSpec((tl, d), lambda i: (i, 0)), full((1, d))),
        compiler_params=_params(("arbitrary",)),
        name="hyena_filter",
    )(z, dist, w_in, f_w_hid, f_b, f_freq, f_w_out.astype(MXU_DTYPE), deltas)


def _real_form(mr, mi):
    top = jnp.concatenate([mr, -mi], axis=-1)
    bot = jnp.concatenate([mi, mr], axis=-1)
    return jnp.concatenate([top, bot], axis=-2)


def _unit_roots(idx, n):
    ang = (2.0 * math.pi / n) * (idx % n).astype(F32)
    return jnp.cos(ang), -jnp.sin(ang)


def _dft_tables(seq):
    n = 2 * seq
    n2 = DFT_N2
    n1 = n // n2
    i32 = jnp.int32
    p = jnp.arange(n1, dtype=i32)
    c = jnp.arange(n2, dtype=i32)
    tr, ti = _unit_roots(c[:, None] * p[None, :], n)
    a_in = jnp.arange(n1 // 2, dtype=i32)
    fr, fi = _unit_roots(p[:, None] * a_in[None, :], n1)
    mr = tr[:, :, None] * fr[None] - ti[:, :, None] * fi[None]
    mi = tr[:, :, None] * fi[None] + ti[:, :, None] * fr[None]
    first = _real_form(mr, mi)
    a_out = jnp.arange(n1 // 4, n1 // 4 + n1 // 2, dtype=i32)
    gr, gi = _unit_roots(p[:, None] * a_out[None, :], n1)
    cr = (tr[:, :, None] * gr[None] - ti[:, :, None] * gi[None]) / n
    ci = -(tr[:, :, None] * gi[None] + ti[:, :, None] * gr[None]) / n
    last = _real_form(jnp.swapaxes(cr, 1, 2), jnp.swapaxes(ci, 1, 2))
    q = jnp.arange(n2, dtype=i32)
    hr, hi_ = _unit_roots(q[:, None] * c[None, :], n2)
    mid_f = _real_form(hr, hi_)
    mid_i = _real_form(hr, -hi_)
    return tuple(t.astype(MXU_DTYPE) for t in (first, mid_f, mid_i, last))


def _dft_first_kernel(x_ref, m_ref, o_ref):
    x = x_ref[...]
    x = x.reshape(x.shape[0] * x.shape[1] * SUBLANES, x.shape[4])
    o_ref[...] = _dot(m_ref[0], x.astype(MXU_DTYPE)).astype(o_ref.dtype)


def _dft_first(x5, table):
    r, a_hi, n2, _, d = x5.shape
    rows = r * a_hi * SUBLANES
    n_out = table.shape[1]
    if rows % LANES:
        table = table[:, :, :rows]
    bd = _tile(d, DFT_COL_TILE)
    per = d // bd
    return pl.pallas_call(
        _dft_first_kernel,
        out_shape=jax.ShapeDtypeStruct((n_out, n2 * d), MXU_DTYPE),
        grid=(n2 * per,),
        in_specs=[
            pl.BlockSpec((r, a_hi, 1, SUBLANES, bd), lambda j: (0, 0, j // per, 0, j % per)),
            pl.BlockSpec((1, n_out, rows), lambda j: (j // per, 0, 0)),
        ],
        out_specs=pl.BlockSpec((n_out, bd), lambda j: (0, j)),
        compiler_params=_params(("parallel",)),
        name="hyena_dft_first",
    )(x5, table)


def _dft_mid_filter_kernel(a_ref, f_ref, norm_ref, o_ref):
    n2 = a_ref.shape[2]
    for p in range(a_ref.shape[1]):
        a = jnp.concatenate([a_ref[0, p], a_ref[1, p]], axis=0)
        z = _dot(f_ref[...], a) / norm_ref[...]
        o_ref[0, p] = z[:n2].astype(o_ref.dtype)
        o_ref[1, p] = z[n2:].astype(o_ref.dtype)


def _dft_mid_conv_kernel(a_ref, h_ref, f_ref, g_ref, o_ref):
    n2 = a_ref.shape[2]
    for p in range(a_ref.shape[1]):
        a = jnp.concatenate([a_ref[0, p], a_ref[1, p]], axis=0)
        z = _dot(f_ref[...], a)
        zr, zi = z[:n2], z[n2:]
        hr, hi_ = h_ref[0, p].astype(F32), h_ref[1, p].astype(F32)
        w = jnp.concatenate([zr * hr - zi * hi_, zr * hi_ + zi * hr], axis=0)
        y = _dot(g_ref[...], w.astype(MXU_DTYPE))
        o_ref[0, p] = y[:n2].astype(o_ref.dtype)
        o_ref[1, p] = y[n2:].astype(o_ref.dtype)


def _dft_mid(a, tables, d, *, spectrum=None, norm=None):
    n2 = DFT_N2
    n1 = a.shape[0] // 2
    a4 = a.reshape(2, n1, n2, d)
    bd = _tile(d, DFT_COL_TILE)
    pb = _tile(n1, DFT_MID_ROWS)
    blk = pl.BlockSpec((2, pb, n2, bd), lambda p, j: (0, p, 0, j))
    mat = pl.BlockSpec((2 * n2, 2 * n2), lambda p, j: (0, 0))
    fwd, inv = tables
    if spectrum is None:
        kern, ins, specs = _dft_mid_filter_kernel, (a4, fwd, norm), [blk, mat, pl.BlockSpec((1, bd), lambda p, j: (0, j))]
    else:
        kern, ins, specs = _dft_mid_conv_kernel, (a4, spectrum, fwd, inv), [blk, blk, mat, mat]
    out = pl.pallas_call(
        kern,
        out_shape=jax.ShapeDtypeStruct((2, n1, n2, d), MXU_DTYPE),
        grid=(n1 // pb, d // bd),
        in_specs=specs,
        out_specs=blk,
        compiler_params=_params(("parallel", "parallel")),
        name="hyena_dft_mid",
    )(*ins)
    return out


def _dft_last_kernel(x_ref, m_ref, o_ref):
    y = _dot(m_ref[0], x_ref[...])
    o_ref[...] = y.reshape(o_ref.shape).astype(o_ref.dtype)


def _dft_last(b, table, d):
    rows, cols = b.shape
    n1 = rows // 2
    n2 = cols // d
    a_hi = n1 // 2 // SUBLANES
    bd = _tile(d, DFT_COL_TILE)
    per = d // bd
    return pl.pallas_call(
        _dft_last_kernel,
        out_shape=jax.ShapeDtypeStruct((2, a_hi, n2, SUBLANES, d), MXU_DTYPE),
        grid=(cols // bd,),
        in_specs=[
            pl.BlockSpec((rows, bd), lambda j: (0, j)),
            pl.BlockSpec((1, n1, rows), lambda j: (j // per, 0, 0)),
        ],
        out_specs=pl.BlockSpec((2, a_hi, 1, SUBLANES, bd), lambda j: (0, 0, j // per, 0, j % per)),
        compiler_params=_params(("parallel",)),
        name="hyena_dft_last",
    )(b, table)


def _long_conv_two_stage(v, filt, norm, tables):
    bsz, seq, d = v.shape
    n1 = 2 * seq // DFT_N2
    a_hi = n1 // 2 // SUBLANES
    first, mid_f, mid_i, last = tables
    cols = DFT_N2 * d
    h_first = _dft_first(filt.reshape(1, a_hi, DFT_N2, SUBLANES, d), first)
    spectrum = _dft_mid(h_first, (mid_f, mid_i), d, norm=norm)
    v5 = v.reshape(bsz, a_hi, SUBLANES, DFT_N2, d).transpose(0, 1, 3, 2, 4)
    a = _dft_first(v5, first)
    bmat = _dft_mid(a, (mid_f, mid_i), d, spectrum=spectrum)
    y5 = _dft_last(bmat.reshape(2 * n1, cols), last, d)
    return y5.transpose(0, 1, 3, 2, 4).reshape(bsz, seq, d)


def _dft_small_kernel(v_ref, h_ref, norm_ref, f_ref, g_ref, o_ref):
    seq = v_ref.shape[1]
    n = 2 * seq
    fwd = f_ref[...]
    x = jnp.concatenate([v_ref[0], v_ref[1]], axis=0)
    z = _dot(fwd, x.astype(MXU_DTYPE))
    hx = jnp.concatenate([h_ref[...] / norm_ref[...], jnp.zeros_like(h_ref)], axis=0)
    hs = _dot(fwd, hx.astype(MXU_DTYPE))
    zr, zi, hr, hi_ = z[:n], z[n:], hs[:n], hs[n:]
    w = jnp.concatenate([zr * hr - zi * hi_, zr * hi_ + zi * hr], axis=0)
    y = _dot(g_ref[...], w.astype(MXU_DTYPE))
    o_ref[0] = y[:seq].astype(o_ref.dtype)
    o_ref[1] = y[seq:].astype(o_ref.dtype)


def _long_conv_small(v, filt, norm):
    bsz, seq, d = v.shape
    n = 2 * seq
    k = jnp.arange(n, dtype=jnp.int32)
    t_in = jnp.arange(seq, dtype=jnp.int32)
    fr, fi = _unit_roots(k[:, None] * t_in[None, :], n)
    fwd = _real_form(fr, fi)
    t_out = jnp.arange(seq // 2, seq // 2 + seq, dtype=jnp.int32)
    gr, gi = _unit_roots(t_out[:, None] * k[None, :], n)
    inv = _real_form(gr / n, -gi / n)
    bd = _tile(d, COL_TILE)
    full = lambda shape: pl.BlockSpec(shape, lambda j: (0,) * len(shape))
    return pl.pallas_call(
        _dft_small_kernel,
        out_shape=jax.ShapeDtypeStruct((bsz, seq, d), MXU_DTYPE),
        grid=(d // bd,),
        in_specs=[
            pl.BlockSpec((bsz, seq, bd), lambda j: (0, 0, j)),
            pl.BlockSpec((seq, bd), lambda j: (0, j)),
            pl.BlockSpec((1, bd), lambda j: (0, j)),
            full(fwd.shape), full(inv.shape),
        ],
        out_specs=pl.BlockSpec((bsz, seq, bd), lambda j: (0, 0, j)),
        compiler_params=_params(("parallel",)),
        name="hyena_dft_small",
    )(v, filt, norm, fwd.astype(MXU_DTYPE), inv.astype(MXU_DTYPE))


def _hyena_out_kernel(alpha, x0_ref, cv_ref, v_ref, bias_ref, w_ref, b_ref, h_ref, g_ref, lg_ref, lb_ref, o_ref):
    y = x0_ref[0].astype(F32) * (cv_ref[0].astype(F32) + v_ref[0].astype(F32) * bias_ref[...])
    y = _dot(y.astype(MXU_DTYPE), w_ref[...]) + b_ref[...]
    o_ref[0] = _layer_norm_rows(alpha * h_ref[0] + g_ref[0] * y, lg_ref[...], lb_ref[...])


def _attn_out_kernel(alpha, o_in_ref, w_ref, h_ref, g_ref, lg_ref, lb_ref, o_ref):
    y = _dot(o_in_ref[0], w_ref[...])
    o_ref[0] = _layer_norm_rows(alpha * h_ref[0] + g_ref[0] * y, lg_ref[...], lb_ref[...])


def _row_spec(tm, d):
    return pl.BlockSpec((1, tm, d), lambda bi, i: (bi, i, 0))


def _vec_spec(d):
    return pl.BlockSpec((1, d), lambda bi, i: (0, 0))


def _bvec_spec(d):
    return pl.BlockSpec((1, 1, d), lambda bi, i: (bi, 0, 0))


def _const_spec(shape):
    return pl.BlockSpec(shape, lambda bi, i: (0,) * len(shape), pipeline_mode=pl.Buffered(1))


def _hyena_out(alpha, x0, cv, v, bias, layer, w, b, h, gate, ln_g, ln_b):
    bsz, seq, d = h.shape
    tm = _tile(seq, ROW_TILE)
    return pl.pallas_call(
        functools.partial(_hyena_out_kernel, alpha),
        out_shape=jax.ShapeDtypeStruct(h.shape, F32),
        grid=(bsz, seq // tm),
        in_specs=[_row_spec(tm, d), _row_spec(tm, d), _row_spec(tm, d), _vec_spec(d),
                  pl.BlockSpec((None, d, d), lambda bi, i: (layer, 0, 0), pipeline_mode=pl.Buffered(1)),
                  _vec_spec(d), _row_spec(tm, d), _bvec_spec(d), _vec_spec(d), _vec_spec(d)],
        out_specs=_row_spec(tm, d),
        compiler_params=_params(("parallel", "parallel")),
        name="hyena_out_norm",
    )(x0, cv, v, bias.reshape(1, d), w, b.reshape(1, d), h, gate, ln_g.reshape(1, d), ln_b.reshape(1, d))


def _attn_out(alpha, o, w, h, gate, ln_g, ln_b):
    bsz, seq, d = h.shape
    k = o.shape[2]
    tm = _tile(seq, ROW_TILE)
    return pl.pallas_call(
        functools.partial(_attn_out_kernel, alpha),
        out_shape=jax.ShapeDtypeStruct(h.shape, F32),
        grid=(bsz, seq // tm),
        in_specs=[_row_spec(tm, k), _const_spec((k, d)), _row_spec(tm, d), _bvec_spec(d), _vec_spec(d), _vec_spec(d)],
        out_specs=_row_spec(tm, d),
        compiler_params=_params(("parallel", "parallel")),
        name="attn_out_norm",
    )(o, w.astype(MXU_DTYPE), h, gate, ln_g.reshape(1, d), ln_b.reshape(1, d))


def _ffn_kernel(alpha, h_ref, sh_ref, sc_ref, g_ref, wg_ref, wu_ref, wd_ref, lg_ref, lb_ref, o_ref, u_scr, acc_scr):
    f = pl.program_id(2)

    @pl.when(f == 0)
    def _():
        u_scr[...] = (h_ref[0] * (1.0 + sc_ref[0]) + sh_ref[0]).astype(MXU_DTYPE)
        acc_scr[...] = jnp.zeros_like(acc_scr)

    u = u_scr[...]
    gate = _dot(u, wg_ref[...])
    up = _dot(u, wu_ref[...])
    act = (_silu(gate) * up).astype(MXU_DTYPE)
    acc_scr[...] += _dot(act, wd_ref[...])

    @pl.when(f == pl.num_programs(2) - 1)
    def _():
        o_ref[0] = _layer_norm_rows(alpha * h_ref[0] + g_ref[0] * acc_scr[...], lg_ref[...], lb_ref[...])


def _ffn(alpha, h, shift, scale, gate, layer, w_gate, w_up, w_down, ln_g, ln_b):
    bsz, seq, d = h.shape
    ff = w_gate.shape[2]
    tm = _tile(seq, ROW_TILE)
    tf = _tile(ff, COL_TILE)
    return pl.pallas_call(
        functools.partial(_ffn_kernel, alpha),
        out_shape=jax.ShapeDtypeStruct(h.shape, F32),
        grid=(bsz, seq // tm, ff // tf),
        in_specs=[
            pl.BlockSpec((1, tm, d), lambda bi, i, f: (bi, i, 0)),
            pl.BlockSpec((1, 1, d), lambda bi, i, f: (bi, 0, 0)),
            pl.BlockSpec((1, 1, d), lambda bi, i, f: (bi, 0, 0)),
            pl.BlockSpec((1, 1, d), lambda bi, i, f: (bi, 0, 0)),
            pl.BlockSpec((None, d, tf), lambda bi, i, f: (layer, 0, f)),
            pl.BlockSpec((None, d, tf), lambda bi, i, f: (layer, 0, f)),
            pl.BlockSpec((None, tf, d), lambda bi, i, f: (layer, f, 0)),
            pl.BlockSpec((1, d), lambda bi, i, f: (0, 0)),
            pl.BlockSpec((1, d), lambda bi, i, f: (0, 0)),
        ],
        out_specs=pl.BlockSpec((1, tm, d), lambda bi, i, f: (bi, i, 0)),
        scratch_shapes=[pltpu.VMEM((tm, d), MXU_DTYPE), pltpu.VMEM((tm, d), F32)],
        compiler_params=_params(("parallel", "parallel", "arbitrary")),
        name="ffn_swiglu_norm",
    )(h, shift, scale, gate, w_gate, w_up, w_down,
      ln_g.reshape(1, d), ln_b.reshape(1, d))


def _rotate_half_cols(w):
    ws = w.reshape(w.shape[:-1] + (2, 2, ROPE_PAIRS))
    return jnp.stack([-ws[..., 1, :], ws[..., 0, :]], axis=-2).reshape(w.shape)


def _rope_table(seq):
    rows = seq // GRID_W
    row = jnp.repeat(jnp.arange(rows, dtype=F32), GRID_W)
    col = jnp.tile(jnp.arange(GRID_W, dtype=F32), rows)
    inv = ROPE_THETA ** (-jnp.arange(ROPE_PAIRS, dtype=F32) / ROPE_PAIRS)
    ang = jnp.stack([row[:, None] * inv, col[:, None] * inv], axis=1)
    ang = jnp.broadcast_to(ang[:, :, None, :], (seq, 2, 2, ROPE_PAIRS)).reshape(seq, QK_ROPE)
    return jnp.concatenate([jnp.cos(ang), jnp.sin(ang)], axis=-1)


def _rms_rows(x, g):
    return x * lax.rsqrt(jnp.mean(x * x, axis=-1, keepdims=True) + RMS_EPS) * g


def _rope_pair(x, cs):
    t = x * cs
    return t + pltpu.roll(t, QK_ROPE, 1)


def _mla_q_kernel(cq_ref, g_ref, w_ref, cs_ref, q_ref):
    xn = _rms_rows(cq_ref[0].astype(F32), g_ref[...]).astype(MXU_DTYPE)
    cs = cs_ref[...]

    def head(h, carry):
        a = _dot(xn, w_ref[h])
        r = _rope_pair(a[:, QK_NOPE:], cs)
        q_ref[0, h] = (jnp.concatenate([a[:, :QK_NOPE], r], axis=1) * QUERY_SCALE).astype(q_ref.dtype)
        return carry

    lax.fori_loop(0, w_ref.shape[0], head, 0, unroll=2)


def _mla_queries(t, q_norm, wq_b, cs, heads):
    bsz, seq, _ = t.shape
    rank = q_norm.shape[0]
    tm = _tile(seq, ROW_TILE)
    w = wq_b.reshape(rank, heads, QK_NOPE + QK_ROPE)
    w = jnp.concatenate([w, _rotate_half_cols(w[..., QK_NOPE:])], axis=-1).transpose(1, 0, 2).astype(MXU_DTYPE)
    return pl.pallas_call(
        _mla_q_kernel,
        out_shape=jax.ShapeDtypeStruct((bsz, heads, seq, 2 * LANES), MXU_DTYPE),
        grid=(bsz, seq // tm),
        in_specs=[
            pl.BlockSpec((1, tm, rank), lambda bi, i: (bi, i, 0)),
            pl.BlockSpec((1, rank), lambda bi, i: (0, 0)),
            pl.BlockSpec((heads, rank, 2 * LANES), lambda bi, i: (0, 0, 0)),
            pl.BlockSpec((tm, LANES), lambda bi, i: (i, 0)),
        ],
        out_specs=pl.BlockSpec((1, heads, tm, 2 * LANES), lambda bi, i: (bi, 0, i, 0)),
        compiler_params=_params(("parallel", "parallel")),
        name="mla_queries",
    )(t, q_norm.reshape(1, rank), w, cs)


def _mla_kv_kernel(use_rope, ckv_ref, kr_ref, g_ref, w_ref, cs_ref, k_ref, vt_ref):
    xn = _rms_rows(ckv_ref[0].astype(F32), g_ref[...]).astype(MXU_DTYPE)
    kr = kr_ref[0].astype(F32)
    if use_rope:
        kr = _rope_pair(kr, cs_ref[...])
    lane = lax.broadcasted_iota(jnp.int32, kr.shape, 1)
    kr = jnp.where(lane < QK_ROPE, kr, 0.0).astype(k_ref.dtype)
    ones = jnp.ones((V_ROWS - V_HEAD, xn.shape[0]), vt_ref.dtype)

    def head(h, carry):
        kv = _dot(xn, w_ref[h])
        k_ref[0, h] = jnp.concatenate([kv[:, :QK_NOPE].astype(k_ref.dtype), kr], axis=1)
        vt_ref[0, h] = jnp.concatenate([kv[:, QK_NOPE:].T.astype(vt_ref.dtype), ones], axis=0)
        return carry

    lax.fori_loop(0, w_ref.shape[0], head, 0, unroll=2)


def _mla_keys_values(t, kv_norm, wkv_b, cs, heads, use_rope):
    bsz, seq, _ = t.shape
    rank = kv_norm.shape[0]
    tm = _tile(seq, ROW_TILE)
    w = wkv_b.reshape(rank, heads, QK_NOPE + V_HEAD).transpose(1, 0, 2).astype(MXU_DTYPE)
    return pl.pallas_call(
        functools.partial(_mla_kv_kernel, use_rope),
        out_shape=(jax.ShapeDtypeStruct((bsz, heads, seq, 2 * LANES), MXU_DTYPE),
                   jax.ShapeDtypeStruct((bsz, heads, V_ROWS, seq), MXU_DTYPE)),
        grid=(bsz, seq // tm),
        in_specs=[
            pl.BlockSpec((1, tm, rank), lambda bi, i: (bi, i, 1)),
            pl.BlockSpec((1, tm, LANES), lambda bi, i: (bi, i, 2 * rank // LANES)),
            pl.BlockSpec((1, rank), lambda bi, i: (0, 0)),
            pl.BlockSpec((heads, rank, QK_NOPE + V_HEAD), lambda bi, i: (0, 0, 0)),
            pl.BlockSpec((tm, LANES), lambda bi, i: (i, 0)),
        ],
        out_specs=(pl.BlockSpec((1, heads, tm, 2 * LANES), lambda bi, i: (bi, 0, i, 0)),
                   pl.BlockSpec((1, heads, V_ROWS, tm), lambda bi, i: (bi, 0, 0, i))),
        compiler_params=_params(("parallel", "parallel")),
        name="mla_keys_values",
    )(t, t, kv_norm.reshape(1, rank), w, cs)


def _attn_kernel(q_ref, k_ref, vt_ref, kc_ref, vtc_ref, o_ref, qt_scr, s_scr, p_scr, smax_scr, m_scr, alpha_scr, acc_scr):
    kv = pl.program_id(3)
    n_main = pl.num_programs(3) - 1

    n_chunks, _, cw = qt_scr.shape

    @pl.when(kv == 0)
    def _():
        for c in range(n_chunks):
            qt_scr[c] = q_ref[0, 0, c * cw:(c + 1) * cw, :].T
        m_scr[...] = jnp.full_like(m_scr, -jnp.inf)
        acc_scr[...] = jnp.zeros_like(acc_scr)

    def step(k, vt):
        nk = k.shape[0]

        def scores(c, slot):
            s = _dot(k, qt_scr[c])
            s_scr[slot, :nk, :] = s
            smax_scr[slot] = jnp.max(s, axis=0, keepdims=True)

        def exponent(c, slot):
            m_prev = m_scr[c]
            m_new = jnp.maximum(m_prev, smax_scr[slot])
            alpha_scr[c] = jnp.exp2(m_prev - m_new)
            p_scr[slot, :nk, :] = jnp.exp2((s_scr[slot, :nk, :] - m_new).astype(MXU_DTYPE))
            m_scr[c] = m_new

        def values(c, slot):
            acc_scr[c] = alpha_scr[c] * acc_scr[c] + _dot(vt, p_scr[slot, :nk, :])

        for i in range(n_chunks + 2):
            if i < n_chunks:
                scores(i, i % 2)
            if 1 <= i <= n_chunks:
                exponent(i - 1, (i - 1) % 2)
            if i >= 2:
                values(i - 2, i % 2)

    @pl.when(kv < n_main)
    def _():
        step(k_ref[0, 0], vt_ref[0, 0])

    @pl.when(kv == n_main)
    def _():
        step(kc_ref[0, 0], vtc_ref[0, 0])
        for c in range(n_chunks):
            acc = acc_scr[c]
            o_ref[0, c * cw:(c + 1) * cw, :] = (acc[:V_HEAD] / acc[V_HEAD:V_HEAD + 1]).T.astype(o_ref.dtype)


def _mla_attend(q, k, vt, kc, vtc):
    bsz, heads, seq, dq = q.shape
    lc = kc.shape[2]
    tq = _tile(seq, ATTN_Q_TILE)
    tk = _tile(seq, ATTN_KV_TILE)
    cw = min(tq, ATTN_CHUNK)
    assert lc <= tk
    n_main = seq // tk
    return pl.pallas_call(
        _attn_kernel,
        out_shape=jax.ShapeDtypeStruct((bsz, seq, heads * V_HEAD), MXU_DTYPE),
        grid=(bsz, heads, seq // tq, n_main + 1),
        in_specs=[
            pl.BlockSpec((1, 1, tq, dq), lambda bi, h, i, j: (bi, h, i, 0)),
            pl.BlockSpec((1, 1, tk, dq), lambda bi, h, i, j: (bi, h, jnp.minimum(j, n_main - 1), 0)),
            pl.BlockSpec((1, 1, V_ROWS, tk), lambda bi, h, i, j: (bi, h, 0, jnp.minimum(j, n_main - 1))),
            pl.BlockSpec((1, 1, lc, dq), lambda bi, h, i, j: (bi, h, 0, 0)),
            pl.BlockSpec((1, 1, V_ROWS, lc), lambda bi, h, i, j: (bi, h, 0, 0)),
        ],
        out_specs=pl.BlockSpec((1, tq, V_HEAD), lambda bi, h, i, j: (bi, i, h)),
        scratch_shapes=[pltpu.VMEM((tq // cw, dq, cw), MXU_DTYPE), pltpu.VMEM((2, tk, cw), F32),
                        pltpu.VMEM((2, tk, cw), MXU_DTYPE), pltpu.VMEM((2, 1, cw), F32),
                        pltpu.VMEM((tq // cw, 1, cw), F32), pltpu.VMEM((tq // cw, 1, cw), F32), pltpu.VMEM((tq // cw, V_ROWS, cw), F32)],
        compiler_params=_params(("parallel", "parallel", "parallel", "arbitrary")),
        name="mla_attention",
    )(q, k, vt, kc, vtc)


def _pool_kernel(alpha, seq, h_ref, prev_ref, next_ref, sh_ref, sc_ref, g_ref, w_ref, ps_ref, lg_ref, lb_ref, o_ref):
    i = pl.program_id(1)
    last = pl.num_programs(1) - 1
    tm = h_ref.shape[1]
    d = h_ref.shape[2]
    groups = len(POOL_WINDOWS)
    ch = d // groups
    ext = tm + 2 * POOL_HALO
    shift, scale = sh_ref[0], sc_ref[0]
    h = h_ref[0]
    u = h * (1.0 + scale) + shift
    u_prev = jnp.where(i == 0, 0.0, prev_ref[0] * (1.0 + scale) + shift)
    u_next = jnp.where(i == last, 0.0, next_ref[0] * (1.0 + scale) + shift)
    e = jnp.concatenate([u_prev, u, u_next], axis=0)
    t = i * tm + lax.broadcasted_iota(jnp.int32, (tm, 1), 0)
    ys = []
    for g, win in enumerate(POOL_WINDOWS):
        a = e[:, g * ch:(g + 1) * ch]
        span = 1
        while span < win:
            a = a + pltpu.roll(a, ext - span, 0)
            span *= 2
        half = win // 2
        a = pltpu.roll(a, half, 0)
        s = a[POOL_HALO:POOL_HALO + tm]
        cnt = (jnp.minimum(t + half, seq) - jnp.maximum(t - half, 0)).astype(F32)
        dg = s / cnt - u[:, g * ch:(g + 1) * ch]
        ys.append(_dot(dg.astype(MXU_DTYPE), w_ref[g]))
    y = jnp.concatenate(ys, axis=1) * ps_ref[...]
    o_ref[0] = _layer_norm_rows(alpha * h + g_ref[0] * y, lg_ref[...], lb_ref[...])


def _pool_mixer(alpha, h, shift, scale, gate, w_grp, pool_scale, ln_g, ln_b):
    bsz, seq, d = h.shape
    groups, ch, _ = w_grp.shape
    tm = _tile(seq, ROW_TILE)
    hb = tm // POOL_HALO
    nhb = seq // POOL_HALO
    return pl.pallas_call(
        functools.partial(_pool_kernel, alpha, seq),
        out_shape=jax.ShapeDtypeStruct(h.shape, F32),
        grid=(bsz, seq // tm),
        in_specs=[
            _row_spec(tm, d),
            pl.BlockSpec((1, POOL_HALO, d), lambda bi, i: (bi, jnp.maximum(i * hb - 1, 0), 0)),
            pl.BlockSpec((1, POOL_HALO, d), lambda bi, i: (bi, jnp.minimum((i + 1) * hb, nhb - 1), 0)),
            _bvec_spec(d), _bvec_spec(d), _bvec_spec(d),
            pl.BlockSpec((groups, ch, ch), lambda bi, i: (0, 0, 0)),
            _vec_spec(d), _vec_spec(d), _vec_spec(d),
        ],
        out_specs=_row_spec(tm, d),
        compiler_params=_params(("parallel", "parallel")),
        name="pool_mixer_norm",
    )(h, h, h, shift, scale, gate, w_grp.astype(MXU_DTYPE), pool_scale.reshape(1, d), ln_g.reshape(1, d), ln_b.reshape(1, d))


def _hyena_mixer_norm(alpha, h, mod, hy, filt, norm, tables, ln_g, ln_b):
    shift, scale, gate = mod
    layer, w_in, b_in, conv_w, conv_b, bias, w_out, b_out = hy
    x0, v = _hyena_in(h, shift, scale, layer, w_in, b_in, conv_w, conv_b)
    if h.shape[1] <= DFT_SMALL_MAX_L:
        cv = _long_conv_small(v, filt, norm)
    else:
        cv = _long_conv_two_stage(v, filt, norm, tables)
    return _hyena_out(alpha, x0, cv, v, bias, layer, w_out, b_out, h, gate, ln_g, ln_b)


def kernel(x, c, ctx, c_ctx, ada_w, ada_b, ln_g, ln_b, ffn_w_gate, ffn_w_up, ffn_w_down, hy_w_in, hy_b_in, hy_conv_w, hy_conv_b, hy_f_w_in, hy_f_w_hid, hy_f_b, hy_f_freq, hy_f_w_out, hy_bias, hy_w_out, hy_b_out, mla_w_in, mla_q_norm, mla_kv_norm, mla_wq_b, mla_wkv_b, mla_w_out, pool_w, pool_scale):
    bsz, seq, d = x.shape
    depth = ada_w.shape[0]
    assert bsz == 2, "the long convolution packs exactly two batch rows into one complex signal"
    assert ctx.shape[0] == bsz and seq % GRID_W == 0
    alpha = (2.0 * depth) ** 0.25
    heads = d // V_HEAD
    mla_layers = [i for i in range(depth) if i % N_MIXERS == 1]
    last_ctx_read = mla_layers[-1] if mla_layers else -1

    cond = jnp.concatenate([c, jnp.broadcast_to(c_ctx[None], (SUBLANES - bsz, d))], axis=0)
    mods = _ada_mods(cond, ada_w, ada_b)

    def mod_vecs(i, ctx_stream):
        m = jnp.broadcast_to(mods[i, bsz][None], (bsz, 6 * d)) if ctx_stream else mods[i, :bsz]
        return [m[:, None, k * d:(k + 1) * d] for k in range(6)]

    cs = _rope_table(seq)
    two_stage = seq > DFT_SMALL_MAX_L
    tables = _dft_tables(seq) if two_stage else None
    row_order = _first_stage_row_order(seq) if two_stage else None

    ffn_w = tuple(w.astype(MXU_DTYPE) for w in (ffn_w_gate, ffn_w_up, ffn_w_down))
    hy_w = (hy_w_in.astype(MXU_DTYPE), hy_w_out.astype(MXU_DTYPE))
    h, hc = x, ctx
    for i in range(depth):
        kind, j = i % N_MIXERS, i // N_MIXERS
        ctx_update = i < last_ctx_read
        sh1, sc1, g1, sh2, sc2, g2 = mod_vecs(i, False)
        if kind == 1 or ctx_update:
            csh1, csc1, cg1, csh2, csc2, cg2 = mod_vecs(i, True)
        lg, lb = ln_g[i], ln_b[i]
        if kind == 0:
            hy = (j, hy_w[0], hy_b_in[j], hy_conv_w[j], hy_conv_b[j], hy_bias[j], hy_w[1], hy_b_out[j])
            fp = (hy_f_w_in[j], hy_f_w_hid[j], hy_f_b[j], hy_f_freq[j], hy_f_w_out[j])
            filt, norm = _implicit_filter(seq, *fp, positions=row_order)
            h_mid = _hyena_mixer_norm(alpha, h, (sh1, sc1, g1), hy, filt, norm, tables, lg[0], lb[0])
            if ctx_update:
                filt_c, norm_c = _implicit_filter(hc.shape[1], *fp)
                hc_mid = _hyena_mixer_norm(alpha, hc, (csh1, csc1, cg1), hy, filt_c, norm_c, None, lg[0], lb[0])
        elif kind == 1:
            assert not ctx_update, "context queries are only needed when a later layer reads the context"
            rank = mla_q_norm.shape[1]
            w_in = mla_w_in[j]
            w_in = jnp.concatenate([w_in, _rotate_half_cols(w_in[:, 2 * rank:])], axis=1)
            zeros = jnp.zeros((w_in.shape[1],), F32)
            t = _mod_proj(h, sh1, sc1, w_in, zeros, slabs=1, tm=ROW_TILE, tn=w_in.shape[1])[0]
            tc = _mod_proj(hc, csh1, csc1, w_in, zeros, slabs=1, tm=ROW_TILE, tn=w_in.shape[1])[0]
            q = _mla_queries(t, mla_q_norm[j], mla_wq_b[j], cs, heads)
            k, v = _mla_keys_values(t, mla_kv_norm[j], mla_wkv_b[j], cs, heads, True)
            kc, vc = _mla_keys_values(tc, mla_kv_norm[j], mla_wkv_b[j], cs, heads, False)
            o = _mla_attend(q, k, v, kc, vc)
            h_mid = _attn_out(alpha, o, mla_w_out[j], h, g1, lg[0], lb[0])
        else:
            h_mid = _pool_mixer(alpha, h, sh1, sc1, g1, pool_w[j], pool_scale[j], lg[0], lb[0])
            if ctx_update:
                hc_mid = _pool_mixer(alpha, hc, csh1, csc1, cg1, pool_w[j], pool_scale[j], lg[0], lb[0])
        h = _ffn(alpha, h_mid, sh2, sc2, g2, i, *ffn_w, lg[1], lb[1])
        if ctx_update:
            hc = _ffn(alpha, hc_mid, csh2, csc2, cg2, i, *ffn_w, lg[1], lb[1])
    return h
```

```python
import functools
import math

import jax
import jax.numpy as jnp
from jax import lax
from jax.experimental import pallas as pl
from jax.experimental.pallas import tpu as pltpu

F32 = jnp.float32
MXU_DTYPE = jnp.bfloat16

V7X_VMEM_BYTES = 64 * 1024 * 1024
VMEM_LIMIT = V7X_VMEM_BYTES - 8 * 1024 * 1024
LANES = 128
SUBLANES = 8

N_MIXERS = 3
GRID_W = 64
LN_EPS = 1e-6
RMS_EPS = 1e-6
HY_TARGET = 1e-2
HY_FAST = 0.3
HY_SLOW = 1.5
HY_MIN_DECAY = math.log(HY_TARGET) / HY_SLOW
HY_MAX_DECAY = math.log(HY_TARGET) / HY_FAST
QK_NOPE = 128
QK_ROPE = 64
V_HEAD = 128
ROPE_PAIRS = QK_ROPE // 4
ROPE_THETA = 10000.0
ATTN_SCALE = (QK_NOPE + QK_ROPE) ** -0.5
QUERY_SCALE = ATTN_SCALE * math.log2(math.e)
ATTN_CHUNK = 512
V_ROWS = V_HEAD + 2 * SUBLANES
POOL_WINDOWS = (2, 4, 8, 16)
POOL_HALO = 8

DFT_N2 = 128
DFT_SMALL_MAX_L = 512
DFT_MID_ROWS = 4

ROW_TILE = 512
COL_TILE = 512
ADA_COL_TILE = 1024
DFT_COL_TILE = 2048
ATTN_Q_TILE = 4096
ATTN_KV_TILE = 4096


def _params(sem):
    return pltpu.CompilerParams(dimension_semantics=sem, vmem_limit_bytes=VMEM_LIMIT)


def _tile(n, t):
    t = min(n, t)
    assert n % t == 0, (n, t)
    return t


def _dot(a, b):
    return jnp.dot(a, b, preferred_element_type=F32)


def _layer_norm_rows(x, g, b):
    mu = jnp.mean(x, axis=-1, keepdims=True)
    xc = x - mu
    var = jnp.mean(xc * xc, axis=-1, keepdims=True)
    return xc * lax.rsqrt(var + LN_EPS) * g + b


def _silu(x):
    return x * (1.0 / (1.0 + jnp.exp(-x)))


def _ada_kernel(c_ref, w_ref, b_ref, o_ref):
    c = c_ref[...]
    a = _silu(c).astype(MXU_DTYPE)
    o_ref[0] = _dot(a, w_ref[0].astype(MXU_DTYPE)) + b_ref[0]


def _ada_mods(cond, ada_w, ada_b):
    depth, d, n = ada_w.shape
    rows = cond.shape[0]
    tn = _tile(n, ADA_COL_TILE)
    return pl.pallas_call(
        _ada_kernel,
        out_shape=jax.ShapeDtypeStruct((depth, rows, n), F32),
        grid=(depth, n // tn),
        in_specs=[
            pl.BlockSpec((rows, d), lambda i, j: (0, 0)),
            pl.BlockSpec((1, d, tn), lambda i, j: (i, 0, j)),
            pl.BlockSpec((1, 1, tn), lambda i, j: (i, 0, j)),
        ],
        out_specs=pl.BlockSpec((1, rows, tn), lambda i, j: (i, 0, j)),
        compiler_params=_params(("parallel", "parallel")),
        name="ada_mods",
    )(cond, ada_w, ada_b.reshape(depth, 1, n))


def _mod_proj_kernel(x_ref, sh_ref, sc_ref, w_ref, b_ref, o_ref, u_scr):
    @pl.when(pl.program_id(2) == 0)
    def _():
        u_scr[...] = (x_ref[0] * (1.0 + sc_ref[0]) + sh_ref[0]).astype(MXU_DTYPE)

    o_ref[0, 0] = (_dot(u_scr[...], w_ref[...]) + b_ref[...]).astype(o_ref.dtype)


def _mod_proj(x, shift, scale, w, b, slabs, tm, tn):
    bsz, seq, k = x.shape
    n = w.shape[1]
    n_slab = n // slabs
    tm = _tile(seq, tm)
    tn = _tile(n_slab, tn)
    per = n_slab // tn
    return pl.pallas_call(
        _mod_proj_kernel,
        out_shape=jax.ShapeDtypeStruct((slabs, bsz, seq, n_slab), MXU_DTYPE),
        grid=(bsz, seq // tm, n // tn),
        in_specs=[
            pl.BlockSpec((1, tm, k), lambda bi, i, j: (bi, i, 0)),
            pl.BlockSpec((1, 1, k), lambda bi, i, j: (bi, 0, 0)),
            pl.BlockSpec((1, 1, k), lambda bi, i, j: (bi, 0, 0)),
            pl.BlockSpec((k, tn), lambda bi, i, j: (0, j)),
            pl.BlockSpec((1, tn), lambda bi, i, j: (0, j)),
        ],
        out_specs=pl.BlockSpec((1, 1, tm, tn), lambda bi, i, j: (j // per, bi, i, j % per)),
        scratch_shapes=[pltpu.VMEM((tm, k), MXU_DTYPE)],
        compiler_params=_params(("parallel", "parallel", "arbitrary")),
        name="mod_proj",
    )(x, shift, scale, w.astype(MXU_DTYPE), b.reshape(1, n))


def _hyena_in_kernel(x_ref, prev_ref, next_ref, sh_ref, sc_ref, w0_ref, w1_ref, w2_ref, b_ref, cw_ref, cb_ref,
                     x0_ref, v_ref, u_scr):
    i = pl.program_id(1)
    last = pl.num_programs(1) - 1
    tm = x_ref.shape[1]
    ext = tm + 2 * SUBLANES

    @pl.when(pl.program_id(2) == 0)
    def _():
        shift, scale = sh_ref[0], sc_ref[0]
        rows = jnp.concatenate([prev_ref[0], x_ref[0], next_ref[0]], axis=0)
        u_scr[...] = (rows * (1.0 + scale) + shift).astype(MXU_DTYPE)

    u = u_scr[...]
    outs = []
    for s, w_ref in enumerate((w0_ref, w1_ref, w2_ref)):
        raw = _dot(u, w_ref[...])
        b, taps = b_ref[s], cw_ref[s]
        head = jnp.where(i == 0, -b, raw[:SUBLANES])
        tail = jnp.where(i == last, -b, raw[SUBLANES + tm:])
        raw = jnp.concatenate([head, raw[SUBLANES:SUBLANES + tm], tail], axis=0)
        conv = pltpu.roll(raw, 1, 0) * taps[0:1] + raw * taps[1:2] + pltpu.roll(raw, ext - 1, 0) * taps[2:3]
        const = b * (taps[0:1] + taps[1:2] + taps[2:3]) + cb_ref[s]
        outs.append(conv[SUBLANES:SUBLANES + tm] + const)
    x0_ref[0] = outs[0].astype(x0_ref.dtype)
    v_ref[0] = (outs[2] * outs[1]).astype(v_ref.dtype)


def _hyena_in(x, shift, scale, layer, w_in, b_in, conv_w, conv_b):
    bsz, seq, k = x.shape
    d = w_in.shape[2] // 3
    tm = _tile(seq, ROW_TILE)
    tn = _tile(d, COL_TILE)
    per = d // tn
    hb = tm // SUBLANES
    nhb = seq // SUBLANES
    w = w_in
    cw = conv_w.reshape(3, 3, d).transpose(1, 0, 2)
    out = jax.ShapeDtypeStruct((bsz, seq, d), MXU_DTYPE)
    slab = lambda s: pl.BlockSpec((None, k, tn), lambda bi, i, j: (layer, 0, s * per + j))
    return pl.pallas_call(
        _hyena_in_kernel,
        out_shape=(out, out),
        grid=(bsz, seq // tm, per),
        in_specs=[
            pl.BlockSpec((1, tm, k), lambda bi, i, j: (bi, i, 0)),
            pl.BlockSpec((1, SUBLANES, k), lambda bi, i, j: (bi, jnp.maximum(i * hb - 1, 0), 0)),
            pl.BlockSpec((1, SUBLANES, k), lambda bi, i, j: (bi, jnp.minimum((i + 1) * hb, nhb - 1), 0)),
            pl.BlockSpec((1, 1, k), lambda bi, i, j: (bi, 0, 0)),
            pl.BlockSpec((1, 1, k), lambda bi, i, j: (bi, 0, 0)),
            slab(0), slab(1), slab(2),
            pl.BlockSpec((3, 1, tn), lambda bi, i, j: (0, 0, j)),
            pl.BlockSpec((3, 3, tn), lambda bi, i, j: (0, 0, j)),
            pl.BlockSpec((3, 1, tn), lambda bi, i, j: (0, 0, j)),
        ],
        out_specs=(
            pl.BlockSpec((1, tm, tn), lambda bi, i, j: (bi, i, j)),
            pl.BlockSpec((1, tm, tn), lambda bi, i, j: (bi, i, j)),
        ),
        scratch_shapes=[pltpu.VMEM((tm + 2 * SUBLANES, k), MXU_DTYPE)],
        compiler_params=_params(("parallel", "parallel", "arbitrary")),
        name="hyena_in_conv_gate",
    )(x, x, x, shift, scale, w, w, w, b_in.reshape(3, 1, d), cw, conv_b.reshape(3, 1, d))


def _filter_kernel(z_ref, dist_ref, w_in_ref, w_hid_ref, b_ref, fr_ref, w_out_ref, delta_ref, f_ref, norm_ref):
    hp = lax.Precision.HIGHEST
    b = b_ref[...]
    fr = fr_ref[...]
    g = jnp.sin(fr[0:1] * (jnp.dot(z_ref[...], w_in_ref[...], precision=hp, preferred_element_type=F32) + b[0:1]))
    g = jnp.sin(fr[1:2] * (jnp.dot(g, w_hid_ref[0], precision=hp, preferred_element_type=F32) + b[1:2]))
    g = jnp.sin(fr[2:3] * (jnp.dot(g, w_hid_ref[1], precision=hp, preferred_element_type=F32) + b[2:3]))
    filt = _dot(g.astype(MXU_DTYPE), w_out_ref[...])
    filt = filt * jnp.exp(-dist_ref[...] * delta_ref[...])
    f_ref[...] = filt

    @pl.when(pl.program_id(0) == 0)
    def _():
        norm_ref[...] = jnp.zeros_like(norm_ref)

    norm_ref[...] += jnp.sum(jnp.abs(filt), axis=0, keepdims=True)


def _first_stage_row_order(seq):
    a_hi, c, a_lo = jnp.meshgrid(jnp.arange(seq // (DFT_N2 * SUBLANES)), jnp.arange(DFT_N2), jnp.arange(SUBLANES),
                                 indexing="ij")
    return (DFT_N2 * (SUBLANES * a_hi + a_lo) + c).reshape(seq)


def _implicit_filter(seq, f_w_in, f_w_hid, f_b, f_freq, f_w_out, positions=None):
    emb, width = f_w_in.shape
    bands_n = (emb - 1) // 2
    d = f_w_out.shape[1]
    emb_pad = -(-emb // LANES) * LANES
    filt_w = -(-width // LANES) * LANES
    pad_to = lambda a, shape: jnp.pad(a.astype(F32), [(0, s - n) for s, n in zip(shape, a.shape)])
    pos = (jnp.arange(seq) if positions is None else positions).astype(F32)
    t = pos / (seq - 1)
    bands = jnp.linspace(1e-4, bands_n - 1, bands_n, dtype=F32)
    ang = (2.0 * math.pi / seq) * pos[:, None] * bands[None, :]
    z = pad_to(jnp.concatenate([t[:, None], jnp.cos(ang), -jnp.sin(ang)], axis=-1), (seq, emb_pad))
    w_in = pad_to(f_w_in, (emb_pad, filt_w))
    f_w_hid = pad_to(f_w_hid, (2, filt_w, filt_w))
    f_b = pad_to(f_b, (3, filt_w))
    f_freq = pad_to(f_freq, (3, filt_w))
    f_w_out = pad_to(f_w_out, (filt_w, d))
    dist = (jnp.abs(pos - seq // 2) / (seq // 2))[:, None]
    deltas = jnp.abs(jnp.linspace(HY_MIN_DECAY, HY_MAX_DECAY, d, dtype=F32))[None, :]
    tl = _tile(seq, ROW_TILE)
    full = lambda shape: pl.BlockSpec(shape, lambda i: (0,) * len(shape))
    return pl.pallas_call(
        _filter_kernel,
        out_shape=(jax.ShapeDtypeStruct((seq, d), F32), jax.ShapeDtypeStruct((1, d), F32)),
        grid=(seq // tl,),
        in_specs=[
            pl.BlockSpec((tl, emb_pad), lambda i: (i, 0)),
            pl.BlockSpec((tl, 1), lambda i: (i, 0)),
            full((emb_pad, filt_w)),
            full((2, filt_w, filt_w)),
            full((3, filt_w)),
            full((3, filt_w)),
            full((filt_w, d)),
            full((1, d)),
        ],
        out_specs=(pl.BlockSpec((tl, d), lambda i: (i, 0)), full((1, d))),
        compiler_params=_params(("arbitrary",)),
        name="hyena_filter",
    )(z, dist, w_in, f_w_hid, f_b, f_freq, f_w_out.astype(MXU_DTYPE), deltas)


def _real_form(mr, mi):
    top = jnp.concatenate([mr, -mi], axis=-1)
    bot = jnp.concatenate([mi, mr], axis=-1)
    return jnp.concatenate([top, bot], axis=-2)


def _unit_roots(idx, n):
    ang = (2.0 * math.pi / n) * (idx % n).astype(F32)
    return jnp.cos(ang), -jnp.sin(ang)


def _dft_tables(seq):
    n = 2 * seq
    n2 = DFT_N2
    n1 = n // n2
    i32 = jnp.int32
    p = jnp.arange(n1, dtype=i32)
    c = jnp.arange(n2, dtype=i32)
    tr, ti = _unit_roots(c[:, None] * p[None, :], n)
    a_in = jnp.arange(n1 // 2, dtype=i32)
    fr, fi = _unit_roots(p[:, None] * a_in[None, :], n1)
    mr = tr[:, :, None] * fr[None] - ti[:, :, None] * fi[None]
    mi = tr[:, :, None] * fi[None] + ti[:, :, None] * fr[None]
    first = _real_form(mr, mi)
    a_out = jnp.arange(n1 // 4, n1 // 4 + n1 // 2, dtype=i32)
    gr, gi = _unit_roots(p[:, None] * a_out[None, :], n1)
    cr = (tr[:, :, None] * gr[None] - ti[:, :, None] * gi[None]) / n
    ci = -(tr[:, :, None] * gi[None] + ti[:, :, None] * gr[None]) / n
    last = _real_form(jnp.swapaxes(cr, 1, 2), jnp.swapaxes(ci, 1, 2))
    q = jnp.arange(n2, dtype=i32)
    hr, hi_ = _unit_roots(q[:, None] * c[None, :], n2)
    mid_f = _real_form(hr, hi_)
    mid_i = _real_form(hr, -hi_)
    return tuple(t.astype(MXU_DTYPE) for t in (first, mid_f, mid_i, last))


def _dft_first_kernel(x_ref, m_ref, o_ref):
    x = x_ref[...]
    x = x.reshape(x.shape[0] * x.shape[1] * SUBLANES, x.shape[4])
    o_ref[...] = _dot(m_ref[0], x.astype(MXU_DTYPE)).astype(o_ref.dtype)


def _dft_first(x5, table):
    r, a_hi, n2, _, d = x5.shape
    rows = r * a_hi * SUBLANES
    n_out = table.shape[1]
    if rows % LANES:
        table = table[:, :, :rows]
    bd = _tile(d, DFT_COL_TILE)
    per = d // bd
    return pl.pallas_call(
        _dft_first_kernel,
        out_shape=jax.ShapeDtypeStruct((n_out, n2 * d), MXU_DTYPE),
        grid=(n2 * per,),
        in_specs=[
            pl.BlockSpec((r, a_hi, 1, SUBLANES, bd), lambda j: (0, 0, j // per, 0, j % per)),
            pl.BlockSpec((1, n_out, rows), lambda j: (j // per, 0, 0)),
        ],
        out_specs=pl.BlockSpec((n_out, bd), lambda j: (0, j)),
        compiler_params=_params(("parallel",)),
        name="hyena_dft_first",
    )(x5, table)


def _dft_mid_filter_kernel(a_ref, f_ref, norm_ref, o_ref):
    n2 = a_ref.shape[2]
    for p in range(a_ref.shape[1]):
        a = jnp.concatenate([a_ref[0, p], a_ref[1, p]], axis=0)
        z = _dot(f_ref[...], a) / norm_ref[...]
        o_ref[0, p] = z[:n2].astype(o_ref.dtype)
        o_ref[1, p] = z[n2:].astype(o_ref.dtype)


def _dft_mid_conv_kernel(a_ref, h_ref, f_ref, g_ref, o_ref):
    n2 = a_ref.shape[2]
    for p in range(a_ref.shape[1]):
        a = jnp.concatenate([a_ref[0, p], a_ref[1, p]], axis=0)
        z = _dot(f_ref[...], a)
        zr, zi = z[:n2], z[n2:]
        hr, hi_ = h_ref[0, p].astype(F32), h_ref[1, p].astype(F32)
        w = jnp.concatenate([zr * hr - zi * hi_, zr * hi_ + zi * hr], axis=0)
        y = _dot(g_ref[...], w.astype(MXU_DTYPE))
        o_ref[0, p] = y[:n2].astype(o_ref.dtype)
        o_ref[1, p] = y[n2:].astype(o_ref.dtype)


def _dft_mid(a, tables, d, *, spectrum=None, norm=None):
    n2 = DFT_N2
    n1 = a.shape[0] // 2
    a4 = a.reshape(2, n1, n2, d)
    bd = _tile(d, DFT_COL_TILE)
    pb = _tile(n1, DFT_MID_ROWS)
    blk = pl.BlockSpec((2, pb, n2, bd), lambda p, j: (0, p, 0, j))
    mat = pl.BlockSpec((2 * n2, 2 * n2), lambda p, j: (0, 0))
    fwd, inv = tables
    if spectrum is None:
        kern, ins, specs = _dft_mid_filter_kernel, (a4, fwd, norm), [blk, mat, pl.BlockSpec((1, bd), lambda p, j: (0, j))]
    else:
        kern, ins, specs = _dft_mid_conv_kernel, (a4, spectrum, fwd, inv), [blk, blk, mat, mat]
    out = pl.pallas_call(
        kern,
        out_shape=jax.ShapeDtypeStruct((2, n1, n2, d), MXU_DTYPE),
        grid=(n1 // pb, d // bd),
        in_specs=specs,
        out_specs=blk,
        compiler_params=_params(("parallel", "parallel")),
        name="hyena_dft_mid",
    )(*ins)
    return out


def _dft_last_kernel(x_ref, m_ref, o_ref):
    y = _dot(m_ref[0], x_ref[...])
    o_ref[...] = y.reshape(o_ref.shape).astype(o_ref.dtype)


def _dft_last(b, table, d):
    rows, cols = b.shape
    n1 = rows // 2
    n2 = cols // d
    a_hi = n1 // 2 // SUBLANES
    bd = _tile(d, DFT_COL_TILE)
    per = d // bd
    return pl.pallas_call(
        _dft_last_kernel,
        out_shape=jax.ShapeDtypeStruct((2, a_hi, n2, SUBLANES, d), MXU_DTYPE),
        grid=(cols // bd,),
        in_specs=[
            pl.BlockSpec((rows, bd), lambda j: (0, j)),
            pl.BlockSpec((1, n1, rows), lambda j: (j // per, 0, 0)),
        ],
        out_specs=pl.BlockSpec((2, a_hi, 1, SUBLANES, bd), lambda j: (0, 0, j // per, 0, j % per)),
        compiler_params=_params(("parallel",)),
        name="hyena_dft_last",
    )(b, table)


def _long_conv_two_stage(v, filt, norm, tables):
    bsz, seq, d = v.shape
    n1 = 2 * seq // DFT_N2
    a_hi = n1 // 2 // SUBLANES
    first, mid_f, mid_i, last = tables
    cols = DFT_N2 * d
    h_first = _dft_first(filt.reshape(1, a_hi, DFT_N2, SUBLANES, d), first)
    spectrum = _dft_mid(h_first, (mid_f, mid_i), d, norm=norm)
    v5 = v.reshape(bsz, a_hi, SUBLANES, DFT_N2, d).transpose(0, 1, 3, 2, 4)
    a = _dft_first(v5, first)
    bmat = _dft_mid(a, (mid_f, mid_i), d, spectrum=spectrum)
    y5 = _dft_last(bmat.reshape(2 * n1, cols), last, d)
    return y5.transpose(0, 1, 3, 2, 4).reshape(bsz, seq, d)


def _dft_small_kernel(v_ref, h_ref, norm_ref, f_ref, g_ref, o_ref):
    seq = v_ref.shape[1]
    n = 2 * seq
    fwd = f_ref[...]
    x = jnp.concatenate([v_ref[0], v_ref[1]], axis=0)
    z = _dot(fwd, x.astype(MXU_DTYPE))
    hx = jnp.concatenate([h_ref[...] / norm_ref[...], jnp.zeros_like(h_ref)], axis=0)
    hs = _dot(fwd, hx.astype(MXU_DTYPE))
    zr, zi, hr, hi_ = z[:n], z[n:], hs[:n], hs[n:]
    w = jnp.concatenate([zr * hr - zi * hi_, zr * hi_ + zi * hr], axis=0)
    y = _dot(g_ref[...], w.astype(MXU_DTYPE))
    o_ref[0] = y[:seq].astype(o_ref.dtype)
    o_ref[1] = y[seq:].astype(o_ref.dtype)


def _long_conv_small(v, filt, norm):
    bsz, seq, d = v.shape
    n = 2 * seq
    k = jnp.arange(n, dtype=jnp.int32)
    t_in = jnp.arange(seq, dtype=jnp.int32)
    fr, fi = _unit_roots(k[:, None] * t_in[None, :], n)
    fwd = _real_form(fr, fi)
    t_out = jnp.arange(seq // 2, seq // 2 + seq, dtype=jnp.int32)
    gr, gi = _unit_roots(t_out[:, None] * k[None, :], n)
    inv = _real_form(gr / n, -gi / n)
    bd = _tile(d, COL_TILE)
    full = lambda shape: pl.BlockSpec(shape, lambda j: (0,) * len(shape))
    return pl.pallas_call(
        _dft_small_kernel,
        out_shape=jax.ShapeDtypeStruct((bsz, seq, d), MXU_DTYPE),
        grid=(d // bd,),
        in_specs=[
            pl.BlockSpec((bsz, seq, bd), lambda j: (0, 0, j)),
            pl.BlockSpec((seq, bd), lambda j: (0, j)),
            pl.BlockSpec((1, bd), lambda j: (0, j)),
            full(fwd.shape), full(inv.shape),
        ],
        out_specs=pl.BlockSpec((bsz, seq, bd), lambda j: (0, 0, j)),
        compiler_params=_params(("parallel",)),
        name="hyena_dft_small",
    )(v, filt, norm, fwd.astype(MXU_DTYPE), inv.astype(MXU_DTYPE))


def _hyena_out_kernel(alpha, x0_ref, cv_ref, v_ref, bias_ref, w_ref, b_ref, h_ref, g_ref, lg_ref, lb_ref, o_ref):
    y = x0_ref[0].astype(F32) * (cv_ref[0].astype(F32) + v_ref[0].astype(F32) * bias_ref[...])
    y = _dot(y.astype(MXU_DTYPE), w_ref[...]) + b_ref[...]
    o_ref[0] = _layer_norm_rows(alpha * h_ref[0] + g_ref[0] * y, lg_ref[...], lb_ref[...])


def _attn_out_kernel(alpha, o_in_ref, w_ref, h_ref, g_ref, lg_ref, lb_ref, o_ref):
    y = _dot(o_in_ref[0], w_ref[...])
    o_ref[0] = _layer_norm_rows(alpha * h_ref[0] + g_ref[0] * y, lg_ref[...], lb_ref[...])


def _row_spec(tm, d):
    return pl.BlockSpec((1, tm, d), lambda bi, i: (bi, i, 0))


def _vec_spec(d):
    return pl.BlockSpec((1, d), lambda bi, i: (0, 0))


def _bvec_spec(d):
    return pl.BlockSpec((1, 1, d), lambda bi, i: (bi, 0, 0))


def _const_spec(shape):
    return pl.BlockSpec(shape, lambda bi, i: (0,) * len(shape), pipeline_mode=pl.Buffered(1))


def _hyena_out(alpha, x0, cv, v, bias, layer, w, b, h, gate, ln_g, ln_b):
    bsz, seq, d = h.shape
    tm = _tile(seq, ROW_TILE)
    return pl.pallas_call(
        functools.partial(_hyena_out_kernel, alpha),
        out_shape=jax.ShapeDtypeStruct(h.shape, F32),
        grid=(bsz, seq // tm),
        in_specs=[_row_spec(tm, d), _row_spec(tm, d), _row_spec(tm, d), _vec_spec(d),
                  pl.BlockSpec((None, d, d), lambda bi, i: (layer, 0, 0), pipeline_mode=pl.Buffered(1)),
                  _vec_spec(d), _row_spec(tm, d), _bvec_spec(d), _vec_spec(d), _vec_spec(d)],
        out_specs=_row_spec(tm, d),
        compiler_params=_params(("parallel", "parallel")),
        name="hyena_out_norm",
    )(x0, cv, v, bias.reshape(1, d), w, b.reshape(1, d), h, gate, ln_g.reshape(1, d), ln_b.reshape(1, d))


def _attn_out(alpha, o, w, h, gate, ln_g, ln_b):
    bsz, seq, d = h.shape
    k = o.shape[2]
    tm = _tile(seq, ROW_TILE)
    return pl.pallas_call(
        functools.partial(_attn_out_kernel, alpha),
        out_shape=jax.ShapeDtypeStruct(h.shape, F32),
        grid=(bsz, seq // tm),
        in_specs=[_row_spec(tm, k), _const_spec((k, d)), _row_spec(tm, d), _bvec_spec(d), _vec_spec(d), _vec_spec(d)],
        out_specs=_row_spec(tm, d),
        compiler_params=_params(("parallel", "parallel")),
        name="attn_out_norm",
    )(o, w.astype(MXU_DTYPE), h, gate, ln_g.reshape(1, d), ln_b.reshape(1, d))


def _ffn_kernel(alpha, h_ref, sh_ref, sc_ref, g_ref, wg_ref, wu_ref, wd_ref, lg_ref, lb_ref, o_ref, u_scr, acc_scr):
    f = pl.program_id(2)

    @pl.when(f == 0)
    def _():
        u_scr[...] = (h_ref[0] * (1.0 + sc_ref[0]) + sh_ref[0]).astype(MXU_DTYPE)
        acc_scr[...] = jnp.zeros_like(acc_scr)

    u = u_scr[...]
    gate = _dot(u, wg_ref[...])
    up = _dot(u, wu_ref[...])
    act = (_silu(gate) * up).astype(MXU_DTYPE)
    acc_scr[...] += _dot(act, wd_ref[...])

    @pl.when(f == pl.num_programs(2) - 1)
    def _():
        o_ref[0] = _layer_norm_rows(alpha * h_ref[0] + g_ref[0] * acc_scr[...], lg_ref[...], lb_ref[...])


def _ffn(alpha, h, shift, scale, gate, layer, w_gate, w_up, w_down, ln_g, ln_b):
    bsz, seq, d = h.shape
    ff = w_gate.shape[2]
    tm = _tile(seq, ROW_TILE)
    tf = _tile(ff, COL_TILE)
    return pl.pallas_call(
        functools.partial(_ffn_kernel, alpha),
        out_shape=jax.ShapeDtypeStruct(h.shape, F32),
        grid=(bsz, seq // tm, ff // tf),
        in_specs=[
            pl.BlockSpec((1, tm, d), lambda bi, i, f: (bi, i, 0)),
            pl.BlockSpec((1, 1, d), lambda bi, i, f: (bi, 0, 0)),
            pl.BlockSpec((1, 1, d), lambda bi, i, f: (bi, 0, 0)),
            pl.BlockSpec((1, 1, d), lambda bi, i, f: (bi, 0, 0)),
            pl.BlockSpec((None, d, tf), lambda bi, i, f: (layer, 0, f)),
            pl.BlockSpec((None, d, tf), lambda bi, i, f: (layer, 0, f)),
            pl.BlockSpec((None, tf, d), lambda bi, i, f: (layer, f, 0)),
            pl.BlockSpec((1, d), lambda bi, i, f: (0, 0)),
            pl.BlockSpec((1, d), lambda bi, i, f: (0, 0)),
        ],
        out_specs=pl.BlockSpec((1, tm, d), lambda bi, i, f: (bi, i, 0)),
        scratch_shapes=[pltpu.VMEM((tm, d), MXU_DTYPE), pltpu.VMEM((tm, d), F32)],
        compiler_params=_params(("parallel", "parallel", "arbitrary")),
        name="ffn_swiglu_norm",
    )(h, shift, scale, gate, w_gate, w_up, w_down,
      ln_g.reshape(1, d), ln_b.reshape(1, d))


def _rotate_half_cols(w):
    ws = w.reshape(w.shape[:-1] + (2, 2, ROPE_PAIRS))
    return jnp.stack([-ws[..., 1, :], ws[..., 0, :]], axis=-2).reshape(w.shape)


def _rope_table(seq):
    rows = seq // GRID_W
    row = jnp.repeat(jnp.arange(rows, dtype=F32), GRID_W)
    col = jnp.tile(jnp.arange(GRID_W, dtype=F32), rows)
    inv = ROPE_THETA ** (-jnp.arange(ROPE_PAIRS, dtype=F32) / ROPE_PAIRS)
    ang = jnp.stack([row[:, None] * inv, col[:, None] * inv], axis=1)
    ang = jnp.broadcast_to(ang[:, :, None, :], (seq, 2, 2, ROPE_PAIRS)).reshape(seq, QK_ROPE)
    return jnp.concatenate([jnp.cos(ang), jnp.sin(ang)], axis=-1)


def _rms_rows(x, g):
    return x * lax.rsqrt(jnp.mean(x * x, axis=-1, keepdims=True) + RMS_EPS) * g


def _rope_pair(x, cs):
    t = x * cs
    return t + pltpu.roll(t, QK_ROPE, 1)


def _mla_q_kernel(cq_ref, g_ref, w_ref, cs_ref, q_ref):
    xn = _rms_rows(cq_ref[0].astype(F32), g_ref[...]).astype(MXU_DTYPE)
    cs = cs_ref[...]

    def head(h, carry):
        a = _dot(xn, w_ref[h])
        r = _rope_pair(a[:, QK_NOPE:], cs)
        q_ref[0, h] = (jnp.concatenate([a[:, :QK_NOPE], r], axis=1) * QUERY_SCALE).astype(q_ref.dtype)
        return carry

    lax.fori_loop(0, w_ref.shape[0], head, 0, unroll=2)


def _mla_queries(t, q_norm, wq_b, cs, heads):
    bsz, seq, _ = t.shape
    rank = q_norm.shape[0]
    tm = _tile(seq, ROW_TILE)
    w = wq_b.reshape(rank, heads, QK_NOPE + QK_ROPE)
    w = jnp.concatenate([w, _rotate_half_cols(w[..., QK_NOPE:])], axis=-1).transpose(1, 0, 2).astype(MXU_DTYPE)
    return pl.pallas_call(
        _mla_q_kernel,
        out_shape=jax.ShapeDtypeStruct((bsz, heads, seq, 2 * LANES), MXU_DTYPE),
        grid=(bsz, seq // tm),
        in_specs=[
            pl.BlockSpec((1, tm, rank), lambda bi, i: (bi, i, 0)),
            pl.BlockSpec((1, rank), lambda bi, i: (0, 0)),
            pl.BlockSpec((heads, rank, 2 * LANES), lambda bi, i: (0, 0, 0)),
            pl.BlockSpec((tm, LANES), lambda bi, i: (i, 0)),
        ],
        out_specs=pl.BlockSpec((1, heads, tm, 2 * LANES), lambda bi, i: (bi, 0, i, 0)),
        compiler_params=_params(("parallel", "parallel")),
        name="mla_queries",
    )(t, q_norm.reshape(1, rank), w, cs)


def _mla_kv_kernel(use_rope, ckv_ref, kr_ref, g_ref, w_ref, cs_ref, k_ref, vt_ref):
    xn = _rms_rows(ckv_ref[0].astype(F32), g_ref[...]).astype(MXU_DTYPE)
    kr = kr_ref[0].astype(F32)
    if use_rope:
        kr = _rope_pair(kr, cs_ref[...])
    lane = lax.broadcasted_iota(jnp.int32, kr.shape, 1)
    kr = jnp.where(lane < QK_ROPE, kr, 0.0).astype(k_ref.dtype)
    ones = jnp.ones((V_ROWS - V_HEAD, xn.shape[0]), vt_ref.dtype)

    def head(h, carry):
        kv = _dot(xn, w_ref[h])
        k_ref[0, h] = jnp.concatenate([kv[:, :QK_NOPE].astype(k_ref.dtype), kr], axis=1)
        vt_ref[0, h] = jnp.concatenate([kv[:, QK_NOPE:].T.astype(vt_ref.dtype), ones], axis=0)
        return carry

    lax.fori_loop(0, w_ref.shape[0], head, 0, unroll=2)


def _mla_keys_values(t, kv_norm, wkv_b, cs, heads, use_rope):
    bsz, seq, _ = t.shape
    rank = kv_norm.shape[0]
    tm = _tile(seq, ROW_TILE)
    w = wkv_b.reshape(rank, heads, QK_NOPE + V_HEAD).transpose(1, 0, 2).astype(MXU_DTYPE)
    return pl.pallas_call(
        functools.partial(_mla_kv_kernel, use_rope),
        out_shape=(jax.ShapeDtypeStruct((bsz, heads, seq, 2 * LANES), MXU_DTYPE),
                   jax.ShapeDtypeStruct((bsz, heads, V_ROWS, seq), MXU_DTYPE)),
        grid=(bsz, seq // tm),
        in_specs=[
            pl.BlockSpec((1, tm, rank), lambda bi, i: (bi, i, 1)),
            pl.BlockSpec((1, tm, LANES), lambda bi, i: (bi, i, 2 * rank // LANES)),
            pl.BlockSpec((1, rank), lambda bi, i: (0, 0)),
            pl.BlockSpec((heads, rank, QK_NOPE + V_HEAD), lambda bi, i: (0, 0, 0)),
            pl.BlockSpec((tm, LANES), lambda bi, i: (i, 0)),
        ],
        out_specs=(pl.BlockSpec((1, heads, tm, 2 * LANES), lambda bi, i: (bi, 0, i, 0)),
                   pl.BlockSpec((1, heads, V_ROWS, tm), lambda bi, i: (bi, 0, 0, i))),
        compiler_params=_params(("parallel", "parallel")),
        name="mla_keys_values",
    )(t, t, kv_norm.reshape(1, rank), w, cs)


def _attn_kernel(q_ref, k_ref, vt_ref, kc_ref, vtc_ref, o_ref, qt_scr, s_scr, p_scr, smax_scr, m_scr, alpha_scr, acc_scr):
    kv = pl.program_id(3)
    n_main = pl.num_programs(3) - 1

    n_chunks, _, cw = qt_scr.shape

    @pl.when(kv == 0)
    def _():
        for c in range(n_chunks):
            qt_scr[c] = q_ref[0, 0, c * cw:(c + 1) * cw, :].T
        m_scr[...] = jnp.full_like(m_scr, -jnp.inf)
        acc_scr[...] = jnp.zeros_like(acc_scr)

    def step(k, vt):
        nk = k.shape[0]

        def scores(c, slot):
            s = _dot(k, qt_scr[c])
            s_scr[slot, :nk, :] = s
            smax_scr[slot] = jnp.max(s, axis=0, keepdims=True)

        def exponent(c, slot):
            m_prev = m_scr[c]
            m_new = jnp.maximum(m_prev, smax_scr[slot])
            alpha_scr[c] = jnp.exp2(m_prev - m_new)
            p_scr[slot, :nk, :] = jnp.exp2((s_scr[slot, :nk, :] - m_new).astype(MXU_DTYPE))
            m_scr[c] = m_new

        def values(c, slot):
            acc_scr[c] = alpha_scr[c] * acc_scr[c] + _dot(vt, p_scr[slot, :nk, :])

        for i in range(n_chunks + 2):
            if i < n_chunks:
                scores(i, i % 2)
            if 1 <= i <= n_chunks:
                exponent(i - 1, (i - 1) % 2)
            if i >= 2:
                values(i - 2, i % 2)

    @pl.when(kv < n_main)
    def _():
        step(k_ref[0, 0], vt_ref[0, 0])

    @pl.when(kv == n_main)
    def _():
        step(kc_ref[0, 0], vtc_ref[0, 0])
        for c in range(n_chunks):
            acc = acc_scr[c]
            o_ref[0, c * cw:(c + 1) * cw, :] = (acc[:V_HEAD] / acc[V_HEAD:V_HEAD + 1]).T.astype(o_ref.dtype)


def _mla_attend(q, k, vt, kc, vtc):
    bsz, heads, seq, dq = q.shape
    lc = kc.shape[2]
    tq = _tile(seq, ATTN_Q_TILE)
    tk = _tile(seq, ATTN_KV_TILE)
    cw = min(tq, ATTN_CHUNK)
    assert lc <= tk
    n_main = seq // tk
    return pl.pallas_call(
        _attn_kernel,
        out_shape=jax.ShapeDtypeStruct((bsz, seq, heads * V_HEAD), MXU_DTYPE),
        grid=(bsz, heads, seq // tq, n_main + 1),
        in_specs=[
            pl.BlockSpec((1, 1, tq, dq), lambda bi, h, i, j: (bi, h, i, 0)),
            pl.BlockSpec((1, 1, tk, dq), lambda bi, h, i, j: (bi, h, jnp.minimum(j, n_main - 1), 0)),
            pl.BlockSpec((1, 1, V_ROWS, tk), lambda bi, h, i, j: (bi, h, 0, jnp.minimum(j, n_main - 1))),
            pl.BlockSpec((1, 1, lc, dq), lambda bi, h, i, j: (bi, h, 0, 0)),
            pl.BlockSpec((1, 1, V_ROWS, lc), lambda bi, h, i, j: (bi, h, 0, 0)),
        ],
        out_specs=pl.BlockSpec((1, tq, V_HEAD), lambda bi, h, i, j: (bi, i, h)),
        scratch_shapes=[pltpu.VMEM((tq // cw, dq, cw), MXU_DTYPE), pltpu.VMEM((2, tk, cw), F32),
                        pltpu.VMEM((2, tk, cw), MXU_DTYPE), pltpu.VMEM((2, 1, cw), F32),
                        pltpu.VMEM((tq // cw, 1, cw), F32), pltpu.VMEM((tq // cw, 1, cw), F32), pltpu.VMEM((tq // cw, V_ROWS, cw), F32)],
        compiler_params=_params(("parallel", "parallel", "parallel", "arbitrary")),
        name="mla_attention",
    )(q, k, vt, kc, vtc)


def _pool_kernel(alpha, seq, h_ref, prev_ref, next_ref, sh_ref, sc_ref, g_ref, w_ref, ps_ref, lg_ref, lb_ref, o_ref):
    i = pl.program_id(1)
    last = pl.num_programs(1) - 1
    tm = h_ref.shape[1]
    d = h_ref.shape[2]
    groups = len(POOL_WINDOWS)
    ch = d // groups
    ext = tm + 2 * POOL_HALO
    shift, scale = sh_ref[0], sc_ref[0]
    h = h_ref[0]
    u = h * (1.0 + scale) + shift
    u_prev = jnp.where(i == 0, 0.0, prev_ref[0] * (1.0 + scale) + shift)
    u_next = jnp.where(i == last, 0.0, next_ref[0] * (1.0 + scale) + shift)
    e = jnp.concatenate([u_prev, u, u_next], axis=0)
    t = i * tm + lax.broadcasted_iota(jnp.int32, (tm, 1), 0)
    ys = []
    for g, win in enumerate(POOL_WINDOWS):
        a = e[:, g * ch:(g + 1) * ch]
        span = 1
        while span < win:
            a = a + pltpu.roll(a, ext - span, 0)
            span *= 2
        half = win // 2
        a = pltpu.roll(a, half, 0)
        s = a[POOL_HALO:POOL_HALO + tm]
        cnt = (jnp.minimum(t + half, seq) - jnp.maximum(t - half, 0)).astype(F32)
        dg = s / cnt - u[:, g * ch:(g + 1) * ch]
        ys.append(_dot(dg.astype(MXU_DTYPE), w_ref[g]))
    y = jnp.concatenate(ys, axis=1) * ps_ref[...]
    o_ref[0] = _layer_norm_rows(alpha * h + g_ref[0] * y, lg_ref[...], lb_ref[...])


def _pool_mixer(alpha, h, shift, scale, gate, w_grp, pool_scale, ln_g, ln_b):
    bsz, seq, d = h.shape
    groups, ch, _ = w_grp.shape
    tm = _tile(seq, ROW_TILE)
    hb = tm // POOL_HALO
    nhb = seq // POOL_HALO
    return pl.pallas_call(
        functools.partial(_pool_kernel, alpha, seq),
        out_shape=jax.ShapeDtypeStruct(h.shape, F32),
        grid=(bsz, seq // tm),
        in_specs=[
            _row_spec(tm, d),
            pl.BlockSpec((1, POOL_HALO, d), lambda bi, i: (bi, jnp.maximum(i * hb - 1, 0), 0)),
            pl.BlockSpec((1, POOL_HALO, d), lambda bi, i: (bi, jnp.minimum((i + 1) * hb, nhb - 1), 0)),
            _bvec_spec(d), _bvec_spec(d), _bvec_spec(d),
            pl.BlockSpec((groups, ch, ch), lambda bi, i: (0, 0, 0)),
            _vec_spec(d), _vec_spec(d), _vec_spec(d),
        ],
        out_specs=_row_spec(tm, d),
        compiler_params=_params(("parallel", "parallel")),
        name="pool_mixer_norm",
    )(h, h, h, shift, scale, gate, w_grp.astype(MXU_DTYPE), pool_scale.reshape(1, d), ln_g.reshape(1, d), ln_b.reshape(1, d))


def _hyena_mixer_norm(alpha, h, mod, hy, filt, norm, tables, ln_g, ln_b):
    shift, scale, gate = mod
    layer, w_in, b_in, conv_w, conv_b, bias, w_out, b_out = hy
    x0, v = _hyena_in(h, shift, scale, layer, w_in, b_in, conv_w, conv_b)
    if h.shape[1] <= DFT_SMALL_MAX_L:
        cv = _long_conv_small(v, filt, norm)
    else:
        cv = _long_conv_two_stage(v, filt, norm, tables)
    return _hyena_out(alpha, x0, cv, v, bias, layer, w_out, b_out, h, gate, ln_g, ln_b)


def kernel(x, c, ctx, c_ctx, ada_w, ada_b, ln_g, ln_b, ffn_w_gate, ffn_w_up, ffn_w_down, hy_w_in, hy_b_in, hy_conv_w, hy_conv_b, hy_f_w_in, hy_f_w_hid, hy_f_b, hy_f_freq, hy_f_w_out, hy_bias, hy_w_out, hy_b_out, mla_w_in, mla_q_norm, mla_kv_norm, mla_wq_b, mla_wkv_b, mla_w_out, pool_w, pool_scale):
    bsz, seq, d = x.shape
    depth = ada_w.shape[0]
    assert bsz == 2, "the long convolution packs exactly two batch rows into one complex signal"
    assert ctx.shape[0] == bsz and seq % GRID_W == 0
    alpha = (2.0 * depth) ** 0.25
    heads = d // V_HEAD
    mla_layers = [i for i in range(depth) if i % N_MIXERS == 1]
    last_ctx_read = mla_layers[-1] if mla_layers else -1

    cond = jnp.concatenate([c, jnp.broadcast_to(c_ctx[None], (SUBLANES - bsz, d))], axis=0)
    mods = _ada_mods(cond, ada_w, ada_b)

    def mod_vecs(i, ctx_stream):
        m = jnp.broadcast_to(mods[i, bsz][None], (bsz, 6 * d)) if ctx_stream else mods[i, :bsz]
        return [m[:, None, k * d:(k + 1) * d] for k in range(6)]

    cs = _rope_table(seq)
    two_stage = seq > DFT_SMALL_MAX_L
    tables = _dft_tables(seq) if two_stage else None
    row_order = _first_stage_row_order(seq) if two_stage else None

    ffn_w = tuple(w.astype(MXU_DTYPE) for w in (ffn_w_gate, ffn_w_up, ffn_w_down))
    hy_w = (hy_w_in.astype(MXU_DTYPE), hy_w_out.astype(MXU_DTYPE))
    h, hc = x, ctx
    for i in range(depth):
        kind, j = i % N_MIXERS, i // N_MIXERS
        ctx_update = i < last_ctx_read
        sh1, sc1, g1, sh2, sc2, g2 = mod_vecs(i, False)
        if kind == 1 or ctx_update:
            csh1, csc1, cg1, csh2, csc2, cg2 = mod_vecs(i, True)
        lg, lb = ln_g[i], ln_b[i]
        if kind == 0:
            hy = (j, hy_w[0], hy_b_in[j], hy_conv_w[j], hy_conv_b[j], hy_bias[j], hy_w[1], hy_b_out[j])
            fp = (hy_f_w_in[j], hy_f_w_hid[j], hy_f_b[j], hy_f_freq[j], hy_f_w_out[j])
            filt, norm = _implicit_filter(seq, *fp, positions=row_order)
            h_mid = _hyena_mixer_norm(alpha, h, (sh1, sc1, g1), hy, filt, norm, tables, lg[0], lb[0])
            if ctx_update:
                filt_c, norm_c = _implicit_filter(hc.shape[1], *fp)
                hc_mid = _hyena_mixer_norm(alpha, hc, (csh1, csc1, cg1), hy, filt_c, norm_c, None, lg[0], lb[0])
        elif kind == 1:
            assert not ctx_update, "context queries are only needed when a later layer reads the context"
            rank = mla_q_norm.shape[1]
            w_in = mla_w_in[j]
            w_in = jnp.concatenate([w_in, _rotate_half_cols(w_in[:, 2 * rank:])], axis=1)
            zeros = jnp.zeros((w_in.shape[1],), F32)
            t = _mod_proj(h, sh1, sc1, w_in, zeros, slabs=1, tm=ROW_TILE, tn=w_in.shape[1])[0]
            tc = _mod_proj(hc, csh1, csc1, w_in, zeros, slabs=1, tm=ROW_TILE, tn=w_in.shape[1])[0]
            q = _mla_queries(t, mla_q_norm[j], mla_wq_b[j], cs, heads)
            k, v = _mla_keys_values(t, mla_kv_norm[j], mla_wkv_b[j], cs, heads, True)
            kc, vc = _mla_keys_values(tc, mla_kv_norm[j], mla_wkv_b[j], cs, heads, False)
            o = _mla_attend(q, k, v, kc, vc)
            h_mid = _attn_out(alpha, o, mla_w_out[j], h, g1, lg[0], lb[0])
        else:
            h_mid = _pool_mixer(alpha, h, sh1, sc1, g1, pool_w[j], pool_scale[j], lg[0], lb[0])
            if ctx_update:
                hc_mid = _pool_mixer(alpha, hc, csh1, csc1, cg1, pool_w[j], pool_scale[j], lg[0], lb[0])
        h = _ffn(alpha, h_mid, sh2, sc2, g2, i, *ffn_w, lg[1], lb[1])
        if ctx_update:
            hc = _ffn(alpha, hc_mid, csh2, csc2, cg2, i, *ffn_w, lg[1], lb[1])
    return h
```
